```python
import math
import jax, jax.numpy as jnp
from jax import lax
import numpy as np

D_MODEL = 1024
BATCH = 16
SEQ = 256
DEPTH = 2
DEC_BATCH = 4
DEC_SEQ = 4096
PAST_LEN = 256

GRID_W = 64
POOL_WIDTH = D_MODEL // 4
POOL_GROUPS = 4
POOL_GROUP_DIM = POOL_WIDTH // POOL_GROUPS
POOL_WINDOWS = (2, 4, 8, 16)
DIFF_V_DIM = 128
DIFF_WIDTH = D_MODEL // 2
DIFF_HEADS = DIFF_WIDTH // DIFF_V_DIM
DIFF_QK_DIM = DIFF_V_DIM // 2
DIFF_QK_WIDTH = DIFF_HEADS * 2 * DIFF_QK_DIM
NA_WIDTH = D_MODEL // 4
NA_HEAD_DIM = 64
NA_HEADS = NA_WIDTH // NA_HEAD_DIM
NA_WIN_H = 8
NA_WIN_W = 16
MIX_WIDTH = POOL_WIDTH + DIFF_WIDTH + NA_WIDTH
IN_WIDTH = POOL_WIDTH + 2 * DIFF_QK_WIDTH + DIFF_WIDTH + 3 * NA_WIDTH
N_EXPERTS = 32
TOP_K = 4
D_EXPERT = D_MODEL
SWIGLU_ALPHA = 1.702
SWIGLU_LIMIT = 7.0
MOE_BLOCK = 256
Q_BLOCK = 128
ROPE_BASE = 10000.0
NORM_EPS = 1e-6
N_MOD = 6

kernel_name = 'hybrid_pool_diffattn_natten_moe_diffusion_step'

F32 = jnp.float32


def rmsnorm(x, g):
    xf = x.astype(F32)
    y = xf * lax.rsqrt(jnp.mean(xf * xf, axis=-1, keepdims=True) + NORM_EPS)
    return (y * g.astype(F32)).astype(x.dtype)


def ada_modulation(cvec, w_ada_l, b_ada_l):
    m = jax.nn.silu(cvec) @ w_ada_l + b_ada_l
    return jnp.split(m[:, None, :], N_MOD, axis=-1)


def modulate(x, shift, scale):
    return x * (1.0 + scale) + shift


def axial_rope_angles(n_tok):
    nf = DIFF_QK_DIM // 4
    inv = ROPE_BASE ** (-jnp.arange(nf, dtype=F32) / nf)
    t = jnp.arange(n_tok)
    row = (t // GRID_W).astype(F32)
    col = (t % GRID_W).astype(F32)
    ang = jnp.stack([row[:, None] * inv, col[:, None] * inv], axis=1)
    return jnp.cos(ang), jnp.sin(ang)


def apply_axial_rope(x, cos, sin):
    nf = cos.shape[-1]
    shp = x.shape
    xr = x.astype(F32).reshape(shp[:-1] + (2, 2, nf))
    x1, x2 = xr[..., 0, :], xr[..., 1, :]
    bshape = (shp[1],) + (1,) * (x.ndim - 3) + (2, nf)
    c = cos.reshape(bshape)
    s = sin.reshape(bshape)
    out = jnp.stack([x1 * c - x2 * s, x1 * s + x2 * c], axis=-2)
    return out.reshape(shp).astype(x.dtype)


def multiscale_pool(u, pool_w_l, pool_scale_l):
    B, T, C = u.shape
    uf = u.astype(F32).reshape(B, T, POOL_GROUPS, POOL_GROUP_DIM)
    csum = jnp.concatenate([jnp.zeros((B, 1, POOL_GROUPS, POOL_GROUP_DIM), F32),
                            jnp.cumsum(uf, axis=1)], axis=1)
    t = np.arange(T)[:, None]
    win = np.array(POOL_WINDOWS)[None, :]
    lo = np.clip(t - win // 2, 0, T)
    hi = np.clip(t - win // 2 + win, 0, T)
    g = np.arange(POOL_GROUPS)
    cnt = jnp.asarray((hi - lo).astype(np.float32))[None, :, :, None]
    mean = (csum[:, hi, g] - csum[:, lo, g]) / cnt
    y = jnp.einsum('btgc,gcd->btgd', mean - uf, pool_w_l.astype(F32))
    return (y.reshape(B, T, C) * pool_scale_l.astype(F32)).astype(u.dtype)


def diff_attention(q, k, v, lam):
    B, T, H = q.shape[:3]
    nb = T // Q_BLOCK
    scale = DIFF_QK_DIM ** -0.5
    kf = k.astype(F32)
    vf = v.astype(F32)
    qb = jnp.moveaxis(q.astype(F32).reshape((B, nb, Q_BLOCK) + q.shape[2:]), 1, 0)

    def block(qi):
        s = jnp.einsum('bqhmd,bkhmd->bmhqk', qi, kf) * scale
        p = jax.nn.softmax(s, axis=-1)
        a = p[:, 0] - lam * p[:, 1]
        return jnp.einsum('bhqk,bkhe->bqhe', a, vf)

    o = lax.map(block, qb)
    return jnp.moveaxis(o, 0, 1).reshape(B, T, H, v.shape[-1])


def dense_attention(q, k, v):
    B, T, H, d = q.shape
    nb = T // Q_BLOCK
    scale = d ** -0.5
    kf = k.astype(F32)
    vf = v.astype(F32)
    qb = jnp.moveaxis(q.astype(F32).reshape(B, nb, Q_BLOCK, H, d), 1, 0)

    def block(qi):
        p = jax.nn.softmax(jnp.einsum('bqhd,bkhd->bhqk', qi, kf) * scale, axis=-1)
        return jnp.einsum('bhqk,bkhd->bqhd', p, vf)

    o = lax.map(block, qb)
    return jnp.moveaxis(o, 0, 1).reshape(B, T, H, d)


def neighbourhood_attention(q, k, v, k_ctx, v_ctx, rpb_l):
    B, T, H, d = q.shape
    rows = T // GRID_W
    wh = min(NA_WIN_H, rows)
    scale = d ** -0.5
    qg = q.astype(F32).reshape(B, rows, GRID_W, H, d)
    kg = k.astype(F32).reshape(B, rows, GRID_W, H, d)
    vg = v.astype(F32).reshape(B, rows, GRID_W, H, d)
    kc = k_ctx.astype(F32)
    vc = v_ctx.astype(F32)
    cols = np.arange(GRID_W)
    col_start = np.clip(cols - NA_WIN_W // 2, 0, GRID_W - NA_WIN_W)
    col_idx = col_start[:, None] + np.arange(NA_WIN_W)[None, :]
    col_off = col_idx - cols[:, None] + (NA_WIN_W - 1)
    rpb_c = rpb_l.astype(F32)[:, :, col_off]
    n_loc = wh * NA_WIN_W

    def row(r):
        rs = jnp.clip(r - wh // 2, 0, rows - wh)
        qr = lax.dynamic_index_in_dim(qg, r, axis=1, keepdims=False)
        kw = lax.dynamic_slice_in_dim(kg, rs, wh, axis=1)[:, :, col_idx]
        vw = lax.dynamic_slice_in_dim(vg, rs, wh, axis=1)[:, :, col_idx]
        row_off = rs + jnp.arange(wh) - r + (NA_WIN_H - 1)
        bias = jnp.transpose(rpb_c[:, row_off], (0, 2, 1, 3))
        s_loc = jnp.einsum('bchd,bwcjhd->bhcwj', qr, kw) * scale + bias[None]
        s_ctx = jnp.einsum('bchd,bkhd->bhck', qr, kc) * scale
        s = jnp.concatenate([s_loc.reshape(B, H, GRID_W, n_loc), s_ctx], axis=-1)
        p = jax.nn.softmax(s, axis=-1)
        p_loc = p[..., :n_loc].reshape(B, H, GRID_W, wh, NA_WIN_W)
        p_ctx = p[..., n_loc:]
        return (jnp.einsum('bhcwj,bwcjhd->bchd', p_loc, vw)
                + jnp.einsum('bhck,bkhd->bchd', p_ctx, vc))

    o = lax.map(row, jnp.arange(rows))
    return jnp.moveaxis(o, 0, 1).reshape(B, T, H, d)


def split_projection(p):
    B, T, _ = p.shape
    sizes = (POOL_WIDTH, DIFF_QK_WIDTH, DIFF_QK_WIDTH, DIFF_WIDTH, NA_WIDTH, NA_WIDTH, NA_WIDTH)
    u, dq, dk, dv, nq, nk, nv = jnp.split(p, np.cumsum(sizes)[:-1].tolist(), axis=-1)
    qk5 = (B, T, DIFF_HEADS, 2, DIFF_QK_DIM)
    na4 = (B, T, NA_HEADS, NA_HEAD_DIM)
    return (u, dq.reshape(qk5), dk.reshape(qk5), dv.reshape(B, T, DIFF_HEADS, DIFF_V_DIM),
            nq.reshape(na4), nk.reshape(na4), nv.reshape(na4))


def merge_heads(pool_o, diff_o, na_o, subln_l, lam_init, w_out_l):
    B, T, _ = pool_o.shape
    dt = pool_o.dtype
    dn = diff_o * lax.rsqrt(jnp.mean(diff_o * diff_o, axis=-1, keepdims=True) + NORM_EPS)
    dn = dn * subln_l.astype(F32) * (1.0 - lam_init)
    cat = jnp.concatenate([pool_o, dn.reshape(B, T, DIFF_WIDTH).astype(dt),
                           na_o.reshape(B, T, NA_WIDTH).astype(dt)], axis=-1)
    return cat @ w_out_l


def moe_ffn(h, router_w_l, router_b_l, w1_l, b1_l, w2_l, b2_l):
    B, T, D = h.shape
    x = h.reshape(B * T, D)
    N = x.shape[0]
    logits = x.astype(F32) @ router_w_l.astype(F32) + router_b_l.astype(F32)
    top_val, top_idx = lax.top_k(logits, TOP_K)
    gates = jax.nn.softmax(top_val, axis=-1)
    n_assign = N * TOP_K
    flat_e = top_idx.reshape(-1).astype(jnp.int32)
    flat_tok = jnp.arange(n_assign, dtype=jnp.int32) // TOP_K
    flat_g = gates.reshape(-1)
    order = jnp.argsort(flat_e)
    sorted_e = flat_e[order]
    counts = jnp.bincount(flat_e, length=N_EXPERTS)
    padded = (counts + MOE_BLOCK - 1) // MOE_BLOCK * MOE_BLOCK
    pad_end = jnp.cumsum(padded)
    pad_start = pad_end - padded
    start = jnp.cumsum(counts) - counts
    dest = pad_start[sorted_e] + jnp.arange(n_assign, dtype=jnp.int32) - start[sorted_e]
    n_blocks = -(-(n_assign + N_EXPERTS * (MOE_BLOCK - 1)) // MOE_BLOCK)
    n_rows = n_blocks * MOE_BLOCK
    row_tok = jnp.full((n_rows,), N, jnp.int32).at[dest].set(flat_tok[order])
    row_gate = jnp.zeros((n_rows,), F32).at[dest].set(flat_g[order])
    block_e = jnp.minimum(jnp.searchsorted(pad_end, jnp.arange(n_blocks) * MOE_BLOCK, side='right'),
                          N_EXPERTS - 1)
    x_pad = jnp.concatenate([x, jnp.zeros((1, D), x.dtype)], axis=0)
    xb = x_pad[row_tok].reshape(n_blocks, MOE_BLOCK, D)

    def expert_block(args):
        xi, e = args
        hh = (xi @ w1_l[e] + b1_l[e]).astype(F32)
        g, u = jnp.split(hh, 2, axis=-1)
        g = jnp.minimum(g, SWIGLU_LIMIT)
        u = jnp.clip(u, -SWIGLU_LIMIT, SWIGLU_LIMIT)
        a = (g * jax.nn.sigmoid(SWIGLU_ALPHA * g)) * (u + 1.0)
        return (a.astype(xi.dtype) @ w2_l[e] + b2_l[e]).astype(F32)

    yb = lax.map(expert_block, (xb, block_e))
    y = jnp.zeros((N + 1, D), F32).at[row_tok].add(yb.reshape(n_rows, D) * row_gate[:, None])
    return y[:N].astype(h.dtype).reshape(B, T, D)


def setup_inputs(seed: int = 0) -> dict:
    key = jax.random.key(seed)
    ks = jax.random.split(key, 24)

    def nrm(k, shape, s):
        return jax.random.normal(k, shape, F32) * s

    L, D = DEPTH, D_MODEL
    return {
        'x_prompt': nrm(ks[0], (BATCH, SEQ, D), 1.0),
        'x_sample': nrm(ks[1], (DEC_BATCH, DEC_SEQ, D), 1.0),
        'cache_diff_k': nrm(ks[2], (DEC_BATCH, DEPTH, PAST_LEN, DIFF_HEADS, 2 * DIFF_QK_DIM), 1.0),
        'cache_diff_v': nrm(ks[3], (DEC_BATCH, DEPTH, PAST_LEN, DIFF_HEADS, DIFF_V_DIM), 1.0),
        'cache_na_k': nrm(ks[4], (DEC_BATCH, DEPTH, PAST_LEN, NA_HEADS, NA_HEAD_DIM), 1.0),
        'cache_na_v': nrm(ks[5], (DEC_BATCH, DEPTH, PAST_LEN, NA_HEADS, NA_HEAD_DIM), 1.0),
        'c': nrm(ks[6], (DEC_BATCH, D), 1.0),
        'c_ctx': nrm(ks[7], (D,), 1.0),
        'w_ada': nrm(ks[8], (L, D, N_MOD * D), 0.5 * D ** -0.5),
        'b_ada': nrm(ks[9], (L, N_MOD * D), 0.01),
        'norm_gain': 1.0 + nrm(ks[10], (L, 4, D), 0.02),
        'w_in': nrm(ks[11], (L, D, IN_WIDTH), D ** -0.5),
        'w_out': nrm(ks[12], (L, MIX_WIDTH, D), MIX_WIDTH ** -0.5),
        'pool_w': nrm(ks[13], (L, POOL_GROUPS, POOL_GROUP_DIM, POOL_GROUP_DIM), POOL_GROUP_DIM ** -0.5),
        'pool_scale': 1.0 + nrm(ks[14], (L, POOL_WIDTH), 0.1),
        'diff_lambda': nrm(ks[15], (L, 4, DIFF_QK_DIM), 0.1),
        'diff_subln': 1.0 + nrm(ks[16], (L, DIFF_V_DIM), 0.02),
        'na_rpb': nrm(ks[17], (L, NA_HEADS, 2 * NA_WIN_H - 1, 2 * NA_WIN_W - 1), 0.1),
        'router_w': nrm(ks[18], (L, D, N_EXPERTS), D ** -0.5),
        'router_b': nrm(ks[19], (L, N_EXPERTS), 0.01),
        'moe_w1': nrm(ks[20], (L, N_EXPERTS, D, 2 * D_EXPERT), D ** -0.5),
        'moe_b1': nrm(ks[21], (L, N_EXPERTS, 2 * D_EXPERT), 0.01),
        'moe_w2': nrm(ks[22], (L, N_EXPERTS, D_EXPERT, D), D_EXPERT ** -0.5),
        'moe_b2': nrm(ks[23], (L, N_EXPERTS, D), 0.01),
    }


def reference(x_prompt, x_sample, cache_diff_k, cache_diff_v, cache_na_k, cache_na_v, c, c_ctx,
              w_ada, b_ada, norm_gain, w_in, w_out, pool_w, pool_scale, diff_lambda, diff_subln,
              na_rpb, router_w, router_b, moe_w1, moe_b1, moe_w2, moe_b2):
    cos, sin = axial_rope_angles(x_sample.shape[1])
    xp, xs = x_prompt, x_sample
    Bp, Tp = xp.shape[:2]
    Bs = xs.shape[0]
    Lc = cache_diff_k.shape[2]
    new_dk, new_dv, new_nk, new_nv = [], [], [], []
    for l in range(DEPTH):
        lam_init = 0.8 - 0.6 * math.exp(-0.3 * l)
        lq1, lk1, lq2, lk2 = diff_lambda[l].astype(F32)
        lam = jnp.exp(jnp.sum(lq1 * lk1)) - jnp.exp(jnp.sum(lq2 * lk2)) + lam_init
        ffn_w = (router_w[l], router_b[l], moe_w1[l], moe_b1[l], moe_w2[l], moe_b2[l])
        g = norm_gain[l]

        m = ada_modulation(c_ctx[None, :], w_ada[l], b_ada[l])
        h = modulate(rmsnorm(xp, g[0]), m[0], m[1])
        u, dq, dk, dv, nq, nk, nv = split_projection(h @ w_in[l])
        mix = merge_heads(multiscale_pool(u, pool_w[l], pool_scale[l]),
                          diff_attention(dq, dk, dv, lam),
                          dense_attention(nq, nk, nv),
                          diff_subln[l], lam_init, w_out[l])
        xp = xp + m[2] * rmsnorm(mix, g[1])
        h = modulate(rmsnorm(xp, g[2]), m[3], m[4])
        xp = xp + m[5] * rmsnorm(moe_ffn(h, *ffn_w), g[3])
        new_dk.append(dk.reshape(Bp, Tp, DIFF_HEADS, 2 * DIFF_QK_DIM))
        new_dv.append(dv)
        new_nk.append(nk)
        new_nv.append(nv)

        m = ada_modulation(c, w_ada[l], b_ada[l])
        h = modulate(rmsnorm(xs, g[0]), m[0], m[1])
        u, dq, dk, dv, nq, nk, nv = split_projection(h @ w_in[l])
        ck = cache_diff_k[:, l].reshape(Bs, Lc, DIFF_HEADS, 2, DIFF_QK_DIM).astype(dk.dtype)
        keys = jnp.concatenate([ck, apply_axial_rope(dk, cos, sin)], axis=1)
        vals = jnp.concatenate([cache_diff_v[:, l].astype(dv.dtype), dv], axis=1)
        diff_o = diff_attention(apply_axial_rope(dq, cos, sin), keys, vals, lam)
        na_o = neighbourhood_attention(nq, nk, nv, cache_na_k[:, l], cache_na_v[:, l], na_rpb[l])
        mix = merge_heads(multiscale_pool(u, pool_w[l], pool_scale[l]), diff_o, na_o,
                          diff_subln[l], lam_init, w_out[l])
        xs = xs + m[2] * rmsnorm(mix, g[1])
        h = modulate(rmsnorm(xs, g[2]), m[3], m[4])
        xs = xs + m[5] * rmsnorm(moe_ffn(h, *ffn_w), g[3])

    new_diff_k = jnp.stack(new_dk, axis=1)
    new_diff_v = jnp.stack(new_dv, axis=1)
    new_na_k = jnp.stack(new_nk, axis=1)
    new_na_v = jnp.stack(new_nv, axis=1)
    return (xp, xs, new_diff_k, new_diff_v, new_na_k, new_na_v)
```

```python
import functools
import math

import numpy as np
import jax
import jax.numpy as jnp
from jax import lax
from jax.experimental import pallas as pl
from jax.experimental.pallas import tpu as pltpu

F32 = jnp.float32
BF16 = jnp.bfloat16
I32 = jnp.int32

GRID_W = 64
POOL_GROUPS = 4
POOL_MAX_HALF = 8
DIFF_HEADS = 4
DIFF_V_DIM = 128
DIFF_QK_DIM = 64
NA_HEADS = 4
NA_HEAD_DIM = 64
NA_WIN_H = 8
NA_WIN_W = 16
NA_Q_ROWS = 4
NA_BAND = 12
TOP_K = 4
SWIGLU_ALPHA = 1.702
SWIGLU_LIMIT = 7.0
ROPE_BASE = 10000.0
NORM_EPS = 1e-6
N_MOD = 6

LOG2E = 1.4426950408889634
MASKED = -1e30
LANES = 128
MOE_ROWS = 256
TOK_CHUNK = 256
VMEM_LIMIT = 56 * 1024 * 1024


def _params(sem, vmem=VMEM_LIMIT):
    return pltpu.CompilerParams(dimension_semantics=sem, vmem_limit_bytes=vmem)


def _dot(a, b):
    return jnp.dot(a, b, preferred_element_type=F32)


def _dot_nt(a, b):
    return lax.dot_general(a, b, (((1,), (1,)), ((), ())), preferred_element_type=F32)


def _split(x):
    hi = x.astype(BF16)
    return hi, (x - hi.astype(F32)).astype(BF16)


def _dot3(a, b):
    ah, al = _split(a)
    bh, bl = _split(b)
    return _dot(ah, bh) + _dot(al, bh) + _dot(ah, bl)


def _rms(x):
    return x * lax.rsqrt(jnp.mean(x * x, axis=-1, keepdims=True) + NORM_EPS)


def _mod_kernel(c_ref, w_ref, b_ref, o_ref):
    c = c_ref[...]
    a = c * (1.0 / (1.0 + jnp.exp(-c)))
    o_ref[...] = _dot3(a, w_ref[...]) + b_ref[...]


def _modulation(cvec, w_ada, b_ada):
    L, D, W = w_ada.shape
    G = cvec.shape[0]
    return pl.pallas_call(
        _mod_kernel,
        grid=(L, W // D),
        in_specs=[pl.BlockSpec((G, D), lambda l, j: (0, 0)),
                  pl.BlockSpec((None, D, D), lambda l, j: (l, 0, j)),
                  pl.BlockSpec((None, 1, D), lambda l, j: (l, 0, j))],
        out_specs=pl.BlockSpec((None, G, D), lambda l, j: (l, 0, j)),
        out_shape=jax.ShapeDtypeStruct((L, G, W), F32),
        compiler_params=_params(("arbitrary", "arbitrary")),
        name="ada_modulation",
    )(cvec, w_ada, b_ada.reshape(L, 1, W))


def _inproj_kernel(x_ref, mod_ref, g_ref, w_ref, cos_ref, sin_ref, o_ref, *, D, rope_lo, rope_hi):
    h = _rms(x_ref[...]) * g_ref[...]
    h = h * (1.0 + mod_ref[:, D:2 * D]) + mod_ref[:, 0:D]
    p = _dot(h.astype(BF16), w_ref[...])
    W = p.shape[1]
    o_ref[:, 0:rope_lo] = p[:, 0:rope_lo]
    o_ref[:, rope_hi:W] = p[:, rope_hi:W]
    cos = cos_ref[...]
    sin = sin_ref[...]
    lane = lax.broadcasted_iota(I32, cos.shape, 1)
    first = (lane % 32) < 16
    for c0 in range(rope_lo, rope_hi, LANES):
        xc = p[:, c0:c0 + LANES]
        partner = jnp.where(first, pltpu.roll(xc, LANES - 16, 1), pltpu.roll(xc, 16, 1))
        o_ref[:, c0:c0 + LANES] = xc * cos + partner * sin


def _in_projection(x, mod4, gain, w_bf, cos_t, sin_t, *, tm, ctx_blocks, blocks_per_seq, l):
    N, D = x.shape
    W = w_bf.shape[1]

    def grp(i):
        return jnp.where(i < ctx_blocks, 0, 1 + (i - ctx_blocks) // blocks_per_seq)

    def rope_blk(i):
        return jnp.where(i < ctx_blocks, 0, 1 + (i - ctx_blocks) % blocks_per_seq)

    kern = functools.partial(_inproj_kernel, D=D, rope_lo=256, rope_hi=256 + 2 * DIFF_HEADS * 2 * DIFF_QK_DIM)
    return pl.pallas_call(
        kern,
        grid=(N // tm,),
        in_specs=[pl.BlockSpec((tm, D), lambda i: (i, 0)),
                  pl.BlockSpec((None, None, 1, N_MOD * D), lambda i: (l, grp(i), 0, 0)),
                  pl.BlockSpec((1, D), lambda i: (0, 0)),
                  pl.BlockSpec((D, W), lambda i: (0, 0)),
                  pl.BlockSpec((tm, LANES), lambda i: (rope_blk(i), 0)),
                  pl.BlockSpec((tm, LANES), lambda i: (rope_blk(i), 0))],
        out_specs=pl.BlockSpec((tm, W), lambda i: (i, 0)),
        out_shape=jax.ShapeDtypeStruct((N, W), F32),
        compiler_params=_params(("arbitrary",)),
        name="in_projection",
    )(x, mod4, gain, w_bf, cos_t, sin_t)


def _pool_kernel(u_ref, w_ref, sc_ref, o_ref, pad_ref, *, T, CH):
    H = 2 * POOL_MAX_HALF
    zeros = jnp.zeros((H, pad_ref.shape[1]), F32)
    pad_ref[0:H, :] = zeros
    pad_ref[H + T:2 * H + T, :] = zeros
    pad_ref[H:H + T, :] = u_ref[...]
    C = pad_ref.shape[1]
    lane = lax.broadcasted_iota(I32, (CH, C), 1)
    half = jnp.left_shift(1, lane // (C // POOL_GROUPS))
    row = lax.broadcasted_iota(I32, (CH, C), 0)

    def body(ci, carry):
        base = pl.multiple_of(ci * CH, CH)
        win = pad_ref[pl.ds(base + POOL_MAX_HALF, CH + H), :]
        acc = jnp.zeros((CH, C), F32)
        for j in range(-POOL_MAX_HALF, POOL_MAX_HALF):
            sl = win[POOL_MAX_HALF + j:POOL_MAX_HALF + j + CH, :]
            inside = (half > j) if j >= 0 else (half >= -j)
            acc = acc + jnp.where(inside, sl, 0.0)
        t = row + base
        cnt = jnp.minimum(t + half, T) - jnp.maximum(t - half, 0)
        d = acc / cnt.astype(F32) - win[POOL_MAX_HALF:POOL_MAX_HALF + CH, :]
        o_ref[pl.ds(base, CH), :] = _dot(d.astype(BF16), w_ref[...]) * sc_ref[...]
        return carry

    lax.fori_loop(0, T // CH, body, 0)


def _pool(p, w_bd, scale, *, row0, n_seq, T, out):
    C = w_bd.shape[0]
    CH = min(T, 256)
    blk0 = row0 // T

    def kern(u_ref, w_ref, sc_ref, prev_ref, o_ref, pad_ref):
        del prev_ref
        _pool_kernel(u_ref, w_ref, sc_ref, o_ref, pad_ref, T=T, CH=CH)

    return pl.pallas_call(
        kern,
        grid=(n_seq,),
        in_specs=[pl.BlockSpec((T, C), lambda s: (blk0 + s, 0)),
                  pl.BlockSpec((C, C), lambda s: (0, 0)),
                  pl.BlockSpec((1, C), lambda s: (0, 0)),
                  pl.BlockSpec(memory_space=pl.ANY)],
        out_specs=pl.BlockSpec((T, C), lambda s: (blk0 + s, 0)),
        out_shape=jax.ShapeDtypeStruct(out.shape, F32),
        scratch_shapes=[pltpu.VMEM((T + 4 * POOL_MAX_HALF, C), F32)],
        input_output_aliases={3: 0},
        compiler_params=_params(("arbitrary",)),
        name="pool_mixer",
    )(p, w_bd, scale, out)


def _softmax_pv(s, v):
    m = jnp.max(s, axis=-1, keepdims=True)
    e = jnp.exp2(s - m)
    l = jnp.sum(e, axis=-1, keepdims=True)
    return _dot(e.astype(BF16), v) * (1.0 / l)


def _diff_kernel(*refs, Ts, Lc, lam_init):
    if Lc:
        lam_ref, sub_ref, q_ref, ks_ref, vs_ref, kc_ref, vc_ref, prev_ref, o_ref, kb, vb = refs
    else:
        lam_ref, sub_ref, q_ref, ks_ref, vs_ref, prev_ref, o_ref, kb, vb = refs
    del prev_ref

    @pl.when(pl.program_id(2) == 0)
    def _():
        kb[0:Ts, :] = ks_ref[...].astype(BF16)
        vb[0:Ts, :] = vs_ref[...].astype(BF16)
        if Lc:
            kb[Ts:Ts + Lc, :] = kc_ref[...].astype(BF16)
            vb[Ts:Ts + Lc, :] = vc_ref[...].astype(BF16)

    lm = lam_ref[...]
    lam = (jnp.exp(jnp.sum(lm[0:1, :] * lm[1:2, :], axis=-1, keepdims=True))
           - jnp.exp(jnp.sum(lm[2:3, :] * lm[3:4, :], axis=-1, keepdims=True)) + lam_init)
    q = q_ref[...] * (DIFF_QK_DIM ** -0.5 * LOG2E)
    lane = lax.broadcasted_iota(I32, q.shape, 1)
    k = kb[...]
    v = vb[...]
    o1 = _softmax_pv(_dot_nt(jnp.where(lane < DIFF_QK_DIM, q, 0.0).astype(BF16), k), v)
    o2 = _softmax_pv(_dot_nt(jnp.where(lane < DIFF_QK_DIM, 0.0, q).astype(BF16), k), v)
    o = o1 - lam * o2
    o_ref[...] = _rms(o) * sub_ref[...] * (1.0 - lam_init)


def _diff_attention(p, lam_l, subln, out, *, row0, n_seq, T, tq, lam_init, cache_k=None, cache_v=None, l=0):
    H = DIFF_HEADS
    Lc = 0 if cache_k is None else cache_k.shape[2]
    nq = T // tq
    qb0 = row0 // tq
    sb0 = row0 // T
    in_specs = [pl.BlockSpec((4, DIFF_QK_DIM), lambda b, h, i: (0, 0)),
                pl.BlockSpec((1, DIFF_V_DIM), lambda b, h, i: (0, 0)),
                pl.BlockSpec((tq, LANES), lambda b, h, i: (qb0 + b * nq + i, 2 + h)),
                pl.BlockSpec((T, LANES), lambda b, h, i: (sb0 + b, 6 + h)),
                pl.BlockSpec((T, LANES), lambda b, h, i: (sb0 + b, 10 + h))]
    args = [lam_l, subln, p, p, p]
    if Lc:
        in_specs += [pl.BlockSpec((None, None, Lc, LANES), lambda b, h, i: (b, l, 0, h)),
                     pl.BlockSpec((None, None, Lc, LANES), lambda b, h, i: (b, l, 0, h))]
        args += [cache_k, cache_v]
    in_specs.append(pl.BlockSpec(memory_space=pl.ANY))
    args.append(out)
    return pl.pallas_call(
        functools.partial(_diff_kernel, Ts=T, Lc=Lc, lam_init=lam_init),
        grid=(n_seq, H, nq),
        in_specs=in_specs,
        out_specs=pl.BlockSpec((tq, LANES), lambda b, h, i: (qb0 + b * nq + i, h)),
        out_shape=jax.ShapeDtypeStruct(out.shape, F32),
        scratch_shapes=[pltpu.VMEM((T + Lc, LANES), BF16), pltpu.VMEM((T + Lc, LANES), BF16)],
        input_output_aliases={len(args) - 1: 0},
        compiler_params=_params(("arbitrary", "arbitrary", "arbitrary")),
        name="diff_attention",
    )(*args)


def _na_heads(q, score_fn, pv_fn):
    lane = lax.broadcasted_iota(I32, q.shape, 1) // NA_HEAD_DIM
    out = jnp.zeros(q.shape, F32)
    for h in range(NA_HEADS):
        qh = jnp.where(lane == h, q, 0.0).astype(BF16)
        out = jnp.where(lane == h, pv_fn(score_fn(qh, h)), out)
    return out


def _dense_kernel(q_ref, k_ref, v_ref, prev_ref, o_ref):
    del prev_ref
    q = q_ref[...] * (NA_HEAD_DIM ** -0.5 * LOG2E)
    k = k_ref[...].astype(BF16)
    v = v_ref[...].astype(BF16)
    o_ref[...] = _na_heads(q, lambda qh, h: _dot_nt(qh, k), lambda s: _softmax_pv(s, v))


def _dense_attention(p, out, *, n_seq, T):
    C = NA_HEADS * NA_HEAD_DIM
    return pl.pallas_call(
        _dense_kernel,
        grid=(n_seq,),
        in_specs=[pl.BlockSpec((T, C), lambda b: (b, 7)),
                  pl.BlockSpec((T, C), lambda b: (b, 8)),
                  pl.BlockSpec((T, C), lambda b: (b, 9)),
                  pl.BlockSpec(memory_space=pl.ANY)],
        out_specs=pl.BlockSpec((T, C), lambda b: (b, 0)),
        out_shape=jax.ShapeDtypeStruct(out.shape, F32),
        input_output_aliases={3: 0},
        compiler_params=_params(("arbitrary",)),
        name="dense_attention",
    )(p, p, p, out)


def _na_kernel(q_ref, ks_ref, vs_ref, kc_ref, vc_ref, bias_ref, prev_ref, o_ref, *, rows):
    del prev_ref
    r0 = pl.program_id(1) * NA_Q_ROWS
    bs = jnp.clip(r0 - NA_WIN_H // 2, 0, rows - NA_BAND)
    start = pl.multiple_of(bs * GRID_W, GRID_W)
    nb = NA_BAND * GRID_W
    kb = ks_ref[pl.ds(start, nb), :].astype(BF16)
    vb = vs_ref[pl.ds(start, nb), :].astype(BF16)
    kc = kc_ref[...].astype(BF16)
    vc = vc_ref[...].astype(BF16)
    q = q_ref[...] * (NA_HEAD_DIM ** -0.5 * LOG2E)

    def scores(qh, h):
        return _dot_nt(qh, kb) + bias_ref[h], _dot_nt(qh, kc)

    def pv(s):
        s_loc, s_ctx = s
        m = jnp.maximum(jnp.max(s_loc, axis=-1, keepdims=True), jnp.max(s_ctx, axis=-1, keepdims=True))
        e_loc = jnp.exp2(s_loc - m)
        e_ctx = jnp.exp2(s_ctx - m)
        l = jnp.sum(e_loc, axis=-1, keepdims=True) + jnp.sum(e_ctx, axis=-1, keepdims=True)
        return (_dot(e_loc.astype(BF16), vb) + _dot(e_ctx.astype(BF16), vc)) * (1.0 / l)

    o_ref[...] = _na_heads(q, scores, pv)


def _na_bias_table(rpb_l, rows):
    wh = NA_WIN_H
    tabs = []
    for r0 in (0, NA_Q_ROWS, rows - NA_Q_ROWS):
        bs = int(np.clip(r0 - wh // 2, 0, rows - NA_BAND))
        r = r0 + np.arange(NA_Q_ROWS)[:, None, None, None]
        c = np.arange(GRID_W)[None, :, None, None]
        kr = bs + np.arange(NA_BAND)[None, None, :, None]
        kc = np.arange(GRID_W)[None, None, None, :]
        rs = np.clip(r - wh // 2, 0, rows - wh)
        cs = np.clip(c - NA_WIN_W // 2, 0, GRID_W - NA_WIN_W)
        valid = (kr >= rs) & (kr < rs + wh) & (kc >= cs) & (kc < cs + NA_WIN_W)
        ro = np.clip(kr - r + (NA_WIN_H - 1), 0, 2 * NA_WIN_H - 2)
        co = np.clip(kc - c + (NA_WIN_W - 1), 0, 2 * NA_WIN_W - 2)
        flat = np.broadcast_to(ro * (2 * NA_WIN_W - 1) + co, valid.shape).reshape(NA_Q_ROWS * GRID_W, -1)
        tabs.append((flat, np.broadcast_to(valid, valid.shape).reshape(NA_Q_ROWS * GRID_W, -1)))
    idx = jnp.asarray(np.stack([t[0] for t in tabs]).astype(np.int32))
    ok = jnp.asarray(np.stack([t[1] for t in tabs]))
    vals = jnp.take(rpb_l.reshape(NA_HEADS, -1).astype(F32) * LOG2E, idx, axis=1)
    return jnp.transpose(jnp.where(ok[None], vals, MASKED), (1, 0, 2, 3))


def _neighbourhood_attention(p, cache_k, cache_v, bias, out, *, row0, n_seq, T, l):
    C = NA_HEADS * NA_HEAD_DIM
    rows = T // GRID_W
    tq = NA_Q_ROWS * GRID_W
    nq = T // tq
    Lc = cache_k.shape[2]
    qb0 = row0 // tq
    sb0 = row0 // T

    def variant(i):
        r0 = i * NA_Q_ROWS
        return (r0 - jnp.clip(r0 - NA_WIN_H // 2, 0, rows - NA_BAND)) // NA_Q_ROWS

    return pl.pallas_call(
        functools.partial(_na_kernel, rows=rows),
        grid=(n_seq, nq),
        in_specs=[pl.BlockSpec((tq, C), lambda b, i: (qb0 + b * nq + i, 7)),
                  pl.BlockSpec((T, C), lambda b, i: (sb0 + b, 8)),
                  pl.BlockSpec((T, C), lambda b, i: (sb0 + b, 9)),
                  pl.BlockSpec((None, None, Lc, C), lambda b, i: (b, l, 0, 0)),
                  pl.BlockSpec((None, None, Lc, C), lambda b, i: (b, l, 0, 0)),
                  pl.BlockSpec((None, NA_HEADS, tq, NA_BAND * GRID_W), lambda b, i: (variant(i), 0, 0, 0)),
                  pl.BlockSpec(memory_space=pl.ANY)],
        out_specs=pl.BlockSpec((tq, C), lambda b, i: (qb0 + b * nq + i, 0)),
        out_shape=jax.ShapeDtypeStruct(out.shape, F32),
        input_output_aliases={6: 0},
        compiler_params=_params(("arbitrary", "arbitrary")),
        name="neighbourhood_attention",
    )(p, p, p, cache_k, cache_v, bias, out)


def _merge_kernel(pool_ref, dn_ref, na_ref, x_ref, mod_ref, g1_ref, g2_ref, w_ref, rw_ref, rb_ref,
                  x1_ref, h2_ref, route_ref, cnt_ref, carry, *, D):
    i = pl.program_id(0)

    @pl.when(i == 0)
    def _():
        carry[...] = jnp.zeros(carry.shape, F32)

    c0 = pool_ref.shape[1]
    c1 = c0 + dn_ref.shape[1]
    mix = (_dot(pool_ref[...].astype(BF16), w_ref[0:c0, :])
           + _dot(dn_ref[...].astype(BF16), w_ref[c0:c1, :])
           + _dot(na_ref[...].astype(BF16), w_ref[c1:, :]))
    x1 = x_ref[...] + mod_ref[:, 2 * D:3 * D] * (_rms(mix) * g1_ref[...])
    x1_ref[...] = x1
    h2 = (_rms(x1) * g2_ref[...]) * (1.0 + mod_ref[:, 4 * D:5 * D]) + mod_ref[:, 3 * D:4 * D]
    h2_ref[...] = h2

    logits = _dot3(h2, rw_ref[...]) + rb_ref[...]
    tm = logits.shape[0]
    lane = lax.broadcasted_iota(I32, (tm, LANES), 1)
    lane_f = lane.astype(F32)
    work = logits
    vals, hots = [], []
    for _ in range(TOP_K):
        mx = jnp.max(work, axis=-1, keepdims=True)
        idx = jnp.min(jnp.where(work == mx, lane_f, float(LANES)), axis=-1, keepdims=True)
        hot = lane_f == idx
        vals.append(mx)
        hots.append(hot)
        work = jnp.where(hot, -jnp.inf, work)
    es = [jnp.exp(v - vals[0]) for v in vals]
    inv = 1.0 / (es[0] + es[1] + es[2] + es[3])

    sel = jnp.zeros((tm, LANES), F32)
    for hot in hots:
        sel = jnp.where(hot, 1.0, sel)
    r = lax.broadcasted_iota(I32, (tm, tm), 0)
    c = lax.broadcasted_iota(I32, (tm, tm), 1)
    before = jnp.where(c < r, 1.0, 0.0).astype(BF16)
    rank = _dot(before, sel.astype(BF16)) + carry[0:1, :]
    total = carry[0:1, :] + jnp.sum(sel, axis=0, keepdims=True)
    carry[0:1, :] = total
    cnt_ref[...] = jnp.broadcast_to(total, cnt_ref.shape)

    route = jnp.zeros((tm, LANES), F32)
    for k in range(TOP_K):
        e_k = jnp.sum(jnp.where(hots[k], lane_f, 0.0), axis=-1, keepdims=True)
        r_k = jnp.sum(jnp.where(hots[k], rank, 0.0), axis=-1, keepdims=True)
        route = jnp.where(lane == k, e_k, route)
        route = jnp.where(lane == TOP_K + k, r_k, route)
        route = jnp.where(lane == 2 * TOP_K + k, es[k] * inv, route)
    route_ref[...] = route


def _merge_route(pool_o, dn, na_o, x, mod4, g1, g2, w_out_bf, rw_pad, rb_pad, *, tm, ctx_blocks,
                 blocks_per_seq, l):
    N, D = x.shape

    def grp(i):
        return jnp.where(i < ctx_blocks, 0, 1 + (i - ctx_blocks) // blocks_per_seq)

    row = lambda w: pl.BlockSpec((tm, w), lambda i: (i, 0))
    full = lambda a: pl.BlockSpec(a.shape, lambda i: (0,) * a.ndim)
    return pl.pallas_call(
        functools.partial(_merge_kernel, D=D),
        grid=(N // tm,),
        in_specs=[row(pool_o.shape[1]), row(dn.shape[1]), row(na_o.shape[1]), row(D),
                  pl.BlockSpec((None, None, 1, N_MOD * D), lambda i: (l, grp(i), 0, 0)),
                  full(g1), full(g2), full(w_out_bf), full(rw_pad), full(rb_pad)],
        out_specs=[row(D), row(D), row(LANES), pl.BlockSpec((8, LANES), lambda i: (0, 0))],
        out_shape=[jax.ShapeDtypeStruct((N, D), F32), jax.ShapeDtypeStruct((N, D), F32),
                   jax.ShapeDtypeStruct((N, LANES), F32), jax.ShapeDtypeStruct((8, LANES), F32)],
        scratch_shapes=[pltpu.VMEM((8, LANES), F32)],
        compiler_params=_params(("arbitrary",)),
        name="merge_route",
    )(pool_o, dn, na_o, x, mod4, g1, g2, w_out_bf, rw_pad, rb_pad)


def _dispatch_kernel(dest_ref, h_ref, xs_in_ref, xs_ref, sem):
    del xs_in_ref
    base = pl.program_id(0) * TOK_CHUNK

    def row_copy(t, k):
        return pltpu.make_async_copy(h_ref.at[pl.ds(base + t, 1)],
                                     xs_ref.at[pl.ds(dest_ref[TOP_K * t + k], 1)], sem)

    def start(t, carry):
        for k in range(TOP_K):
            row_copy(t, k).start()
        return carry

    def wait(t, carry):
        for k in range(TOP_K):
            row_copy(t, k).wait()
        return carry

    lax.fori_loop(0, TOK_CHUNK, start, 0)
    lax.fori_loop(0, TOK_CHUNK, wait, 0)


def _dispatch(dest_flat, h2, n_rows):
    N, D = h2.shape
    xs0 = jnp.zeros((n_rows, D), F32)
    return pl.pallas_call(
        _dispatch_kernel,
        grid=(N // TOK_CHUNK,),
        in_specs=[pl.BlockSpec((TOK_CHUNK * TOP_K,), lambda i: (i,), memory_space=pltpu.SMEM),
                  pl.BlockSpec(memory_space=pl.ANY),
                  pl.BlockSpec(memory_space=pl.ANY)],
        out_specs=pl.BlockSpec(memory_space=pl.ANY),
        out_shape=jax.ShapeDtypeStruct((n_rows, D), F32),
        scratch_shapes=[pltpu.SemaphoreType.DMA(())],
        input_output_aliases={2: 0},
        compiler_params=_params(("arbitrary",)),
        name="moe_dispatch",
    )(dest_flat, h2, xs0)


def _expert_kernel(be_ref, nu_ref, xs_ref, w1_ref, b1_ref, w2_ref, b2_ref, o_ref, w1b, w2b, *, F):
    i = pl.program_id(0)
    e = be_ref[i]
    prev = be_ref[jnp.maximum(i - 1, 0)]

    @pl.when((i == 0) | (e != prev))
    def _():
        w1b[...] = w1_ref[...].astype(BF16)
        w2b[...] = w2_ref[...].astype(BF16)

    @pl.when(i < nu_ref[0])
    def _():
        hh = _dot(xs_ref[...].astype(BF16), w1b[...]) + b1_ref[...]
        g = jnp.minimum(hh[:, 0:F], SWIGLU_LIMIT)
        u = jnp.clip(hh[:, F:2 * F], -SWIGLU_LIMIT, SWIGLU_LIMIT)
        a = (g * (1.0 / (1.0 + jnp.exp(-SWIGLU_ALPHA * g)))) * (u + 1.0)
        o_ref[...] = _dot(a.astype(BF16), w2b[...]) + b2_ref[...]

    @pl.when(i >= nu_ref[0])
    def _():
        o_ref[...] = jnp.zeros(o_ref.shape, F32)


def _experts(block_e, n_used, xs, w1, b1, w2, b2, *, l):
    R, D = xs.shape
    L, E, _, F2 = w1.shape
    F = F2 // 2
    nblk = R // MOE_ROWS
    grid_spec = pltpu.PrefetchScalarGridSpec(
        num_scalar_prefetch=2,
        grid=(nblk,),
        in_specs=[pl.BlockSpec((MOE_ROWS, D), lambda i, be, nu: (i, 0)),
                  pl.BlockSpec((None, None, D, F2), lambda i, be, nu: (l, be[i], 0, 0)),
                  pl.BlockSpec((None, None, 1, F2), lambda i, be, nu: (l, be[i], 0, 0)),
                  pl.BlockSpec((None, None, F, D), lambda i, be, nu: (l, be[i], 0, 0)),
                  pl.BlockSpec((None, None, 1, D), lambda i, be, nu: (l, be[i], 0, 0))],
        out_specs=pl.BlockSpec((MOE_ROWS, D), lambda i, be, nu: (i, 0)),
        scratch_shapes=[pltpu.VMEM((D, F2), BF16), pltpu.VMEM((F, D), BF16)],
    )
    return pl.pallas_call(
        functools.partial(_expert_kernel, F=F),
        grid_spec=grid_spec,
        out_shape=jax.ShapeDtypeStruct((R, D), F32),
        compiler_params=_params(("arbitrary",)),
        name="moe_experts",
    )(block_e, n_used, xs, w1, b1.reshape(L, E, 1, F2), w2, b2.reshape(L, E, 1, D))


def _combine_kernel(dest_ref, yb_ref, route_ref, x1_ref, mod_ref, g_ref, o_ref, buf, sem, *, D):
    def row_copy(t, k):
        return pltpu.make_async_copy(yb_ref.at[pl.ds(dest_ref[TOP_K * t + k], 1)],
                                     buf.at[k, pl.ds(t, 1)], sem)

    def start(t, carry):
        for k in range(TOP_K):
            row_copy(t, k).start()
        return carry

    def wait(t, carry):
        for k in range(TOP_K):
            row_copy(t, k).wait()
        return carry

    lax.fori_loop(0, TOK_CHUNK, start, 0)
    lax.fori_loop(0, TOK_CHUNK, wait, 0)
    route = route_ref[...]
    y = jnp.zeros(o_ref.shape, F32)
    for k in range(TOP_K):
        y = y + buf[k] * route[:, 2 * TOP_K + k:2 * TOP_K + k + 1]
    o_ref[...] = x1_ref[...] + mod_ref[:, 5 * D:6 * D] * (_rms(y) * g_ref[...])


def _combine(dest_flat, yb, route, x1, mod4, g3, *, ctx_blocks, blocks_per_seq, l):
    N, D = x1.shape

    def grp(i):
        return jnp.where(i < ctx_blocks, 0, 1 + (i - ctx_blocks) // blocks_per_seq)

    return pl.pallas_call(
        functools.partial(_combine_kernel, D=D),
        grid=(N // TOK_CHUNK,),
        in_specs=[pl.BlockSpec((TOK_CHUNK * TOP_K,), lambda i: (i,), memory_space=pltpu.SMEM),
                  pl.BlockSpec(memory_space=pl.ANY),
                  pl.BlockSpec((TOK_CHUNK, LANES), lambda i: (i, 0)),
                  pl.BlockSpec((TOK_CHUNK, D), lambda i: (i, 0)),
                  pl.BlockSpec((None, None, 1, N_MOD * D), lambda i: (l, grp(i), 0, 0)),
                  pl.BlockSpec((1, D), lambda i: (0, 0))],
        out_specs=pl.BlockSpec((TOK_CHUNK, D), lambda i: (i, 0)),
        out_shape=jax.ShapeDtypeStruct((N, D), F32),
        scratch_shapes=[pltpu.VMEM((TOP_K, TOK_CHUNK, D), F32), pltpu.SemaphoreType.DMA(())],
        compiler_params=_params(("arbitrary",)),
        name="moe_combine",
    )(dest_flat, yb, route, x1, mod4, g3)


def _rope_tables(Ts, tm):
    nf = DIFF_QK_DIM // 4
    inv = ROPE_BASE ** (-jnp.arange(nf, dtype=F32) / nf)
    t = jnp.arange(Ts)
    pos = jnp.stack([(t // GRID_W).astype(F32), (t % GRID_W).astype(F32)], axis=1)
    ang = pos[:, :, None] * inv[None, None, :]
    cos = jnp.repeat(jnp.cos(ang)[:, :, None, :], 2, axis=2).reshape(Ts, DIFF_QK_DIM)
    sin = jnp.sin(ang)
    sin = jnp.stack([-sin, sin], axis=2).reshape(Ts, DIFF_QK_DIM)
    reps = LANES // DIFF_QK_DIM
    cos = jnp.concatenate([jnp.ones((tm, LANES), F32), jnp.tile(cos, (1, reps))], axis=0)
    sin = jnp.concatenate([jnp.zeros((tm, LANES), F32), jnp.tile(sin, (1, reps))], axis=0)
    return cos, sin


def _block_diag(w):
    G, a, b = w.shape
    out = jnp.zeros((G * a, G * b), w.dtype)
    for g in range(G):
        out = out.at[g * a:(g + 1) * a, g * b:(g + 1) * b].set(w[g])
    return out


def _route_plan(route, counts, n_exp, nblk):
    idx = route[:, 0:TOP_K].astype(I32)
    rank = route[:, TOP_K:2 * TOP_K].astype(I32)
    cnt = counts[0, :n_exp].astype(I32)
    padded = (cnt + MOE_ROWS - 1) // MOE_ROWS * MOE_ROWS
    pad_end = jnp.cumsum(padded)
    pad_start = pad_end - padded
    n_used = (pad_end[-1] // MOE_ROWS).astype(I32)
    blk = jnp.arange(nblk, dtype=I32) * MOE_ROWS
    block_e = jnp.minimum(jnp.sum((blk[:, None] >= pad_end[None, :]).astype(I32), axis=1), n_exp - 1)
    last = jnp.sum(jnp.where(jnp.arange(nblk) == n_used - 1, block_e, 0))
    block_e = jnp.where(jnp.arange(nblk) < n_used, block_e, last).astype(I32)
    dest = (pad_start[idx] + rank).reshape(-1).astype(I32)
    return dest, block_e, n_used.reshape(1)


def kernel(x_prompt, x_sample, cache_diff_k, cache_diff_v, cache_na_k, cache_na_v, c, c_ctx, w_ada, b_ada,
           norm_gain, w_in, w_out, pool_w, pool_scale, diff_lambda, diff_subln, na_rpb, router_w, router_b,
           moe_w1, moe_b1, moe_w2, moe_b2):
    Bp, Tp, D = x_prompt.shape
    Bs, Ts, _ = x_sample.shape
    L = w_ada.shape[0]
    E = router_w.shape[-1]
    Np, Ns = Bp * Tp, Bs * Ts
    N = Np + Ns
    tm = 256
    assert Np % Ts == 0 or Bs == 0, "context rows must be a whole number of latent-sequence blocks"
    assert Np % tm == 0 and Ts % tm == 0 and N % TOK_CHUNK == 0 and Ts % (NA_Q_ROWS * GRID_W) == 0
    assert Ts // GRID_W >= NA_BAND + NA_Q_ROWS
    ctx_blocks, blocks_per_seq = Np // tm, Ts // tm

    G = 16
    cvec = jnp.zeros((G, D), F32).at[0].set(c_ctx).at[1:1 + Bs].set(c)
    mod4 = _modulation(cvec, w_ada, b_ada).reshape(L, G, 1, N_MOD * D)
    cos_t, sin_t = _rope_tables(Ts, tm)
    w_in_bf = w_in.astype(BF16)
    w_out_bf = w_out.astype(BF16)
    rw_pad = jnp.zeros((L, D, LANES), F32).at[:, :, :E].set(router_w)
    rb_pad = jnp.full((L, 1, LANES), MASKED, F32).at[:, 0, :E].set(router_b)
    ck = cache_diff_k.reshape(Bs, L, -1, DIFF_HEADS * 2 * DIFF_QK_DIM)
    cv = cache_diff_v.reshape(Bs, L, -1, DIFF_HEADS * DIFF_V_DIM)
    nk = cache_na_k.reshape(Bs, L, -1, NA_HEADS * NA_HEAD_DIM)
    nv = cache_na_v.reshape(Bs, L, -1, NA_HEADS * NA_HEAD_DIM)
    nblk = -(-(N * TOP_K + E * (MOE_ROWS - 1)) // MOE_ROWS)
    blocks = dict(ctx_blocks=ctx_blocks, blocks_per_seq=blocks_per_seq)

    x = jnp.concatenate([x_prompt.reshape(Np, D), x_sample.reshape(Ns, D)], axis=0)
    new_dk, new_dv, new_nk, new_nv = [], [], [], []
    for l in range(L):
        lam_init = 0.8 - 0.6 * math.exp(-0.3 * l)
        g = norm_gain[l]
        p = _in_projection(x, mod4, g[0:1], w_in_bf[l], cos_t, sin_t, tm=tm, l=l, **blocks)

        pw = _block_diag(pool_w[l]).astype(BF16)
        ps = pool_scale[l].reshape(1, -1)
        pool_o = jnp.zeros((N, pw.shape[0]), F32)
        pool_o = _pool(p, pw, ps, row0=0, n_seq=Bp, T=Tp, out=pool_o)
        pool_o = _pool(p, pw, ps, row0=Np, n_seq=Bs, T=Ts, out=pool_o)

        sub = diff_subln[l].reshape(1, -1)
        dn = jnp.zeros((N, DIFF_HEADS * DIFF_V_DIM), F32)
        dn = _diff_attention(p, diff_lambda[l], sub, dn, row0=0, n_seq=Bp, T=Tp, tq=Tp, lam_init=lam_init)
        dn = _diff_attention(p, diff_lambda[l], sub, dn, row0=Np, n_seq=Bs, T=Ts, tq=256, lam_init=lam_init,
                             cache_k=ck, cache_v=cv, l=l)

        na_o = jnp.zeros((N, NA_HEADS * NA_HEAD_DIM), F32)
        na_o = _dense_attention(p, na_o, n_seq=Bp, T=Tp)
        bias = _na_bias_table(na_rpb[l], Ts // GRID_W)
        na_o = _neighbourhood_attention(p, nk, nv, bias, na_o, row0=Np, n_seq=Bs, T=Ts, l=l)

        x1, h2, route, counts = _merge_route(pool_o, dn, na_o, x, mod4, g[1:2], g[2:3], w_out_bf[l],
                                             rw_pad[l], rb_pad[l], tm=tm, l=l, **blocks)
        dest, block_e, n_used = _route_plan(route, counts, E, nblk)
        xs = _dispatch(dest, h2, nblk * MOE_ROWS)
        yb = _experts(block_e, n_used, xs, moe_w1, moe_b1, moe_w2, moe_b2, l=l)
        x = _combine(dest, yb, route, x1, mod4, g[3:4], l=l, **blocks)

        pc = p[:Np]
        new_dk.append(pc[:, 768:1280].reshape(Bp, Tp, DIFF_HEADS, 2 * DIFF_QK_DIM))
        new_dv.append(pc[:, 1280:1792].reshape(Bp, Tp, DIFF_HEADS, DIFF_V_DIM))
        new_nk.append(pc[:, 2048:2304].reshape(Bp, Tp, NA_HEADS, NA_HEAD_DIM))
        new_nv.append(pc[:, 2304:2560].reshape(Bp, Tp, NA_HEADS, NA_HEAD_DIM))

    return (x[:Np].reshape(Bp, Tp, D), x[Np:].reshape(Bs, Ts, D),
            jnp.stack(new_dk, axis=1), jnp.stack(new_dv, axis=1),
            jnp.stack(new_nk, axis=1), jnp.stack(new_nv, axis=1))
```

```python
import functools
import math

import numpy as np
import jax
import jax.numpy as jnp
from jax import lax
from jax.experimental import pallas as pl
from jax.experimental.pallas import tpu as pltpu

F32 = jnp.float32
BF16 = jnp.bfloat16
I32 = jnp.int32

GRID_W = 64
POOL_GROUPS = 4
POOL_MAX_HALF = 8
DIFF_HEADS = 4
DIFF_V_DIM = 128
DIFF_QK_DIM = 64
NA_HEADS = 4
NA_HEAD_DIM = 64
NA_WIN_H = 8
NA_WIN_W = 16
NA_Q_ROWS = 4
NA_BAND = 12
TOP_K = 4
SWIGLU_ALPHA = 1.702
SWIGLU_LIMIT = 7.0
ROPE_BASE = 10000.0
NORM_EPS = 1e-6
N_MOD = 6

LOG2E = 1.4426950408889634
MASKED = -1e30
LANES = 128
MOE_ROWS = 256
TOK_CHUNK = 256
RUN_ALIGN = 8
VMEM_LIMIT = 56 * 1024 * 1024


def _sorted_rows(n_exp):
    rows = TOK_CHUNK * TOP_K + n_exp * (RUN_ALIGN - 1)
    return -(-rows // MOE_ROWS) * MOE_ROWS


def _params(sem, vmem=VMEM_LIMIT):
    return pltpu.CompilerParams(dimension_semantics=sem, vmem_limit_bytes=vmem)


def _dot(a, b):
    return jnp.dot(a, b, preferred_element_type=F32)


def _dot_nt(a, b):
    return lax.dot_general(a, b, (((1,), (1,)), ((), ())), preferred_element_type=F32)


def _split(x):
    hi = x.astype(BF16)
    return hi, (x - hi.astype(F32)).astype(BF16)


def _dot3(a, b):
    ah, al = _split(a)
    bh, bl = _split(b)
    return _dot(ah, bh) + _dot(al, bh) + _dot(ah, bl)


def _rms(x):
    return x * lax.rsqrt(jnp.mean(x * x, axis=-1, keepdims=True) + NORM_EPS)


def _mod_kernel(c_ref, w_ref, b_ref, o_ref):
    c = c_ref[...]
    a = c * (1.0 / (1.0 + jnp.exp(-c)))
    o_ref[...] = _dot3(a, w_ref[...]) + b_ref[...]


def _modulation(cvec, w_ada, b_ada):
    L, D, W = w_ada.shape
    G = cvec.shape[0]
    return pl.pallas_call(
        _mod_kernel,
        grid=(L, W // D),
        in_specs=[pl.BlockSpec((G, D), lambda l, j: (0, 0)),
                  pl.BlockSpec((None, D, D), lambda l, j: (l, 0, j)),
                  pl.BlockSpec((None, 1, D), lambda l, j: (l, 0, j))],
        out_specs=pl.BlockSpec((None, G, D), lambda l, j: (l, 0, j)),
        out_shape=jax.ShapeDtypeStruct((L, G, W), F32),
        compiler_params=_params(("arbitrary", "arbitrary")),
        name="ada_modulation",
    )(cvec, w_ada, b_ada.reshape(L, 1, W))


def _inproj_kernel(x_ref, mod_ref, g_ref, w_ref, cos_ref, sin_ref, o_ref, *, D, rope_lo, rope_hi):
    h = _rms(x_ref[...]) * g_ref[...]
    h = h * (1.0 + mod_ref[:, D:2 * D]) + mod_ref[:, 0:D]
    p = _dot(h.astype(BF16), w_ref[...])
    W = p.shape[1]
    o_ref[:, 0:rope_lo] = p[:, 0:rope_lo]
    o_ref[:, rope_hi:W] = p[:, rope_hi:W]
    cos = cos_ref[...]
    sin = sin_ref[...]
    lane = lax.broadcasted_iota(I32, cos.shape, 1)
    first = (lane % 32) < 16
    for c0 in range(rope_lo, rope_hi, LANES):
        xc = p[:, c0:c0 + LANES]
        partner = jnp.where(first, pltpu.roll(xc, LANES - 16, 1), pltpu.roll(xc, 16, 1))
        o_ref[:, c0:c0 + LANES] = xc * cos + partner * sin


def _in_projection(x, mod4, gain, w_bf, cos_t, sin_t, *, tm, ctx_blocks, blocks_per_seq, l):
    N, D = x.shape
    W = w_bf.shape[1]

    def grp(i):
        return jnp.where(i < ctx_blocks, 0, 1 + (i - ctx_blocks) // blocks_per_seq)

    def rope_blk(i):
        return jnp.where(i < ctx_blocks, 0, 1 + (i - ctx_blocks) % blocks_per_seq)

    kern = functools.partial(_inproj_kernel, D=D, rope_lo=256, rope_hi=256 + 2 * DIFF_HEADS * 2 * DIFF_QK_DIM)
    return pl.pallas_call(
        kern,
        grid=(N // tm,),
        in_specs=[pl.BlockSpec((tm, D), lambda i: (i, 0)),
                  pl.BlockSpec((None, None, 1, N_MOD * D), lambda i: (l, grp(i), 0, 0)),
                  pl.BlockSpec((1, D), lambda i: (0, 0)),
                  pl.BlockSpec((D, W), lambda i: (0, 0)),
                  pl.BlockSpec((tm, LANES), lambda i: (rope_blk(i), 0)),
                  pl.BlockSpec((tm, LANES), lambda i: (rope_blk(i), 0))],
        out_specs=pl.BlockSpec((tm, W), lambda i: (i, 0)),
        out_shape=jax.ShapeDtypeStruct((N, W), F32),
        compiler_params=_params(("arbitrary",)),
        name="in_projection",
    )(x, mod4, gain, w_bf, cos_t, sin_t)


def _pool_kernel(u_ref, w_ref, sc_ref, o_ref, pad_ref, *, T, CH):
    H = 2 * POOL_MAX_HALF
    zeros = jnp.zeros((H, pad_ref.shape[1]), F32)
    pad_ref[0:H, :] = zeros
    pad_ref[H + T:2 * H + T, :] = zeros
    pad_ref[H:H + T, :] = u_ref[...]
    C = pad_ref.shape[1]
    lane = lax.broadcasted_iota(I32, (CH, C), 1)
    half = jnp.left_shift(1, lane // (C // POOL_GROUPS))
    row = lax.broadcasted_iota(I32, (CH, C), 0)

    def body(ci, carry):
        base = pl.multiple_of(ci * CH, CH)
        win = pad_ref[pl.ds(base + POOL_MAX_HALF, CH + H), :]
        acc = jnp.zeros((CH, C), F32)
        for j in range(-POOL_MAX_HALF, POOL_MAX_HALF):
            sl = win[POOL_MAX_HALF + j:POOL_MAX_HALF + j + CH, :]
            inside = (half > j) if j >= 0 else (half >= -j)
            acc = acc + jnp.where(inside, sl, 0.0)
        t = row + base
        cnt = jnp.minimum(t + half, T) - jnp.maximum(t - half, 0)
        d = acc / cnt.astype(F32) - win[POOL_MAX_HALF:POOL_MAX_HALF + CH, :]
        o_ref[pl.ds(base, CH), :] = _dot(d.astype(BF16), w_ref[...]) * sc_ref[...]
        return carry

    lax.fori_loop(0, T // CH, body, 0)


def _pool(p, w_bd, scale, *, row0, n_seq, T, out):
    C = w_bd.shape[0]
    CH = min(T, 256)
    blk0 = row0 // T

    def kern(u_ref, w_ref, sc_ref, prev_ref, o_ref, pad_ref):
        del prev_ref
        _pool_kernel(u_ref, w_ref, sc_ref, o_ref, pad_ref, T=T, CH=CH)

    return pl.pallas_call(
        kern,
        grid=(n_seq,),
        in_specs=[pl.BlockSpec((T, C), lambda s: (blk0 + s, 0)),
                  pl.BlockSpec((C, C), lambda s: (0, 0)),
                  pl.BlockSpec((1, C), lambda s: (0, 0)),
                  pl.BlockSpec(memory_space=pl.ANY)],
        out_specs=pl.BlockSpec((T, C), lambda s: (blk0 + s, 0)),
        out_shape=jax.ShapeDtypeStruct(out.shape, F32),
        scratch_shapes=[pltpu.VMEM((T + 4 * POOL_MAX_HALF, C), F32)],
        input_output_aliases={3: 0},
        compiler_params=_params(("arbitrary",)),
        name="pool_mixer",
    )(p, w_bd, scale, out)


def _softmax_pv(s, v):
    m = jnp.max(s, axis=-1, keepdims=True)
    e = jnp.exp2(s - m)
    l = jnp.sum(e, axis=-1, keepdims=True)
    return _dot(e.astype(BF16), v) * (1.0 / l)


def _diff_kernel(*refs, Ts, Lc, lam_init):
    if Lc:
        lam_ref, sub_ref, q_ref, ks_ref, vs_ref, kc_ref, vc_ref, prev_ref, o_ref, kb, vb = refs
    else:
        lam_ref, sub_ref, q_ref, ks_ref, vs_ref, prev_ref, o_ref, kb, vb = refs
    del prev_ref

    @pl.when(pl.program_id(2) == 0)
    def _():
        kb[0:Ts, :] = ks_ref[...].astype(BF16)
        vb[0:Ts, :] = vs_ref[...].astype(BF16)
        if Lc:
            kb[Ts:Ts + Lc, :] = kc_ref[...].astype(BF16)
            vb[Ts:Ts + Lc, :] = vc_ref[...].astype(BF16)

    lm = lam_ref[...]
    lam = (jnp.exp(jnp.sum(lm[0:1, :] * lm[1:2, :], axis=-1, keepdims=True))
           - jnp.exp(jnp.sum(lm[2:3, :] * lm[3:4, :], axis=-1, keepdims=True)) + lam_init)
    q = q_ref[...] * (DIFF_QK_DIM ** -0.5 * LOG2E)
    lane = lax.broadcasted_iota(I32, q.shape, 1)
    k = kb[...]
    v = vb[...]
    o1 = _softmax_pv(_dot_nt(jnp.where(lane < DIFF_QK_DIM, q, 0.0).astype(BF16), k), v)
    o2 = _softmax_pv(_dot_nt(jnp.where(lane < DIFF_QK_DIM, 0.0, q).astype(BF16), k), v)
    o = o1 - lam * o2
    o_ref[...] = _rms(o) * sub_ref[...] * (1.0 - lam_init)


def _diff_attention(p, lam_l, subln, out, *, row0, n_seq, T, tq, lam_init, cache_k=None, cache_v=None, l=0):
    H = DIFF_HEADS
    Lc = 0 if cache_k is None else cache_k.shape[2]
    nq = T // tq
    qb0 = row0 // tq
    sb0 = row0 // T
    in_specs = [pl.BlockSpec((4, DIFF_QK_DIM), lambda b, h, i: (0, 0)),
                pl.BlockSpec((1, DIFF_V_DIM), lambda b, h, i: (0, 0)),
                pl.BlockSpec((tq, LANES), lambda b, h, i: (qb0 + b * nq + i, 2 + h)),
                pl.BlockSpec((T, LANES), lambda b, h, i: (sb0 + b, 6 + h)),
                pl.BlockSpec((T, LANES), lambda b, h, i: (sb0 + b, 10 + h))]
    args = [lam_l, subln, p, p, p]
    if Lc:
        in_specs += [pl.BlockSpec((None, None, Lc, LANES), lambda b, h, i: (b, l, 0, h)),
                     pl.BlockSpec((None, None, Lc, LANES), lambda b, h, i: (b, l, 0, h))]
        args += [cache_k, cache_v]
    in_specs.append(pl.BlockSpec(memory_space=pl.ANY))
    args.append(out)
    return pl.pallas_call(
        functools.partial(_diff_kernel, Ts=T, Lc=Lc, lam_init=lam_init),
        grid=(n_seq, H, nq),
        in_specs=in_specs,
        out_specs=pl.BlockSpec((tq, LANES), lambda b, h, i: (qb0 + b * nq + i, h)),
        out_shape=jax.ShapeDtypeStruct(out.shape, F32),
        scratch_shapes=[pltpu.VMEM((T + Lc, LANES), BF16), pltpu.VMEM((T + Lc, LANES), BF16)],
        input_output_aliases={len(args) - 1: 0},
        compiler_params=_params(("arbitrary", "arbitrary", "arbitrary")),
        name="diff_attention",
    )(*args)


def _na_heads(q, score_fn, pv_fn):
    lane = lax.broadcasted_iota(I32, q.shape, 1) // NA_HEAD_DIM
    out = jnp.zeros(q.shape, F32)
    for h in range(NA_HEADS):
        qh = jnp.where(lane == h, q, 0.0).astype(BF16)
        out = jnp.where(lane == h, pv_fn(score_fn(qh, h)), out)
    return out


def _dense_kernel(q_ref, k_ref, v_ref, prev_ref, o_ref):
    del prev_ref
    q = q_ref[...] * (NA_HEAD_DIM ** -0.5 * LOG2E)
    k = k_ref[...].astype(BF16)
    v = v_ref[...].astype(BF16)
    o_ref[...] = _na_heads(q, lambda qh, h: _dot_nt(qh, k), lambda s: _softmax_pv(s, v))


def _dense_attention(p, out, *, n_seq, T):
    C = NA_HEADS * NA_HEAD_DIM
    return pl.pallas_call(
        _dense_kernel,
        grid=(n_seq,),
        in_specs=[pl.BlockSpec((T, C), lambda b: (b, 7)),
                  pl.BlockSpec((T, C), lambda b: (b, 8)),
                  pl.BlockSpec((T, C), lambda b: (b, 9)),
                  pl.BlockSpec(memory_space=pl.ANY)],
        out_specs=pl.BlockSpec((T, C), lambda b: (b, 0)),
        out_shape=jax.ShapeDtypeStruct(out.shape, F32),
        input_output_aliases={3: 0},
        compiler_params=_params(("arbitrary",)),
        name="dense_attention",
    )(p, p, p, out)


def _na_kernel(q_ref, ks_ref, vs_ref, kc_ref, vc_ref, bias_ref, prev_ref, o_ref, *, rows):
    del prev_ref
    r0 = pl.program_id(1) * NA_Q_ROWS
    bs = jnp.clip(r0 - NA_WIN_H // 2, 0, rows - NA_BAND)
    start = pl.multiple_of(bs * GRID_W, GRID_W)
    nb = NA_BAND * GRID_W
    kb = ks_ref[pl.ds(start, nb), :].astype(BF16)
    vb = vs_ref[pl.ds(start, nb), :].astype(BF16)
    kc = kc_ref[...].astype(BF16)
    vc = vc_ref[...].astype(BF16)
    q = q_ref[...] * (NA_HEAD_DIM ** -0.5 * LOG2E)

    def scores(qh, h):
        return _dot_nt(qh, kb) + bias_ref[h], _dot_nt(qh, kc)

    def pv(s):
        s_loc, s_ctx = s
        m = jnp.maximum(jnp.max(s_loc, axis=-1, keepdims=True), jnp.max(s_ctx, axis=-1, keepdims=True))
        e_loc = jnp.exp2(s_loc - m)
        e_ctx = jnp.exp2(s_ctx - m)
        l = jnp.sum(e_loc, axis=-1, keepdims=True) + jnp.sum(e_ctx, axis=-1, keepdims=True)
        return (_dot(e_loc.astype(BF16), vb) + _dot(e_ctx.astype(BF16), vc)) * (1.0 / l)

    o_ref[...] = _na_heads(q, scores, pv)


def _na_bias_table(rpb_l, rows):
    n_ro, n_co = 2 * NA_WIN_H - 1, 2 * NA_WIN_W - 1
    c = np.arange(GRID_W)[:, None]
    kc = np.arange(GRID_W)[None, :]
    cs = np.clip(c - NA_WIN_W // 2, 0, GRID_W - NA_WIN_W)
    col_ok = (kc >= cs) & (kc < cs + NA_WIN_W)
    co = kc - c + (NA_WIN_W - 1)
    pick = ((np.arange(n_co)[:, None, None] == co[None]) & col_ok[None]).astype(np.float32)
    toep = jnp.dot(rpb_l.reshape(NA_HEADS * n_ro, n_co).astype(F32), jnp.asarray(pick.reshape(n_co, -1)),
                   precision=lax.Precision.HIGHEST).reshape(NA_HEADS, n_ro, GRID_W, GRID_W) * LOG2E
    toep = jnp.where(jnp.asarray(col_ok)[None, None], toep, MASKED)
    toep = jnp.concatenate([toep, jnp.full((NA_HEADS, 1, GRID_W, GRID_W), MASKED, F32)], axis=1)
    blk = np.full((3, NA_Q_ROWS, NA_BAND), n_ro, np.int32)
    for v, r0 in enumerate((0, NA_Q_ROWS, rows - NA_Q_ROWS)):
        bs = int(np.clip(r0 - NA_WIN_H // 2, 0, rows - NA_BAND))
        for j in range(NA_Q_ROWS):
            rs = int(np.clip(r0 + j - NA_WIN_H // 2, 0, rows - NA_WIN_H))
            for i in range(NA_BAND):
                if rs <= bs + i < rs + NA_WIN_H:
                    blk[v, j, i] = bs + i - (r0 + j) + (NA_WIN_H - 1)
    tab = toep[:, blk]
    tab = jnp.transpose(tab, (1, 0, 2, 4, 3, 5))
    return tab.reshape(3, NA_HEADS, NA_Q_ROWS * GRID_W, NA_BAND * GRID_W)


def _neighbourhood_attention(p, cache_k, cache_v, bias, out, *, row0, n_seq, T, l):
    C = NA_HEADS * NA_HEAD_DIM
    rows = T // GRID_W
    tq = NA_Q_ROWS * GRID_W
    nq = T // tq
    Lc = cache_k.shape[2]
    qb0 = row0 // tq
    sb0 = row0 // T

    def variant(i):
        r0 = i * NA_Q_ROWS
        return (r0 - jnp.clip(r0 - NA_WIN_H // 2, 0, rows - NA_BAND)) // NA_Q_ROWS

    return pl.pallas_call(
        functools.partial(_na_kernel, rows=rows),
        grid=(n_seq, nq),
        in_specs=[pl.BlockSpec((tq, C), lambda b, i: (qb0 + b * nq + i, 7)),
                  pl.BlockSpec((T, C), lambda b, i: (sb0 + b, 8)),
                  pl.BlockSpec((T, C), lambda b, i: (sb0 + b, 9)),
                  pl.BlockSpec((None, None, Lc, C), lambda b, i: (b, l, 0, 0)),
                  pl.BlockSpec((None, None, Lc, C), lambda b, i: (b, l, 0, 0)),
                  pl.BlockSpec((None, NA_HEADS, tq, NA_BAND * GRID_W), lambda b, i: (variant(i), 0, 0, 0)),
                  pl.BlockSpec(memory_space=pl.ANY)],
        out_specs=pl.BlockSpec((tq, C), lambda b, i: (qb0 + b * nq + i, 0)),
        out_shape=jax.ShapeDtypeStruct(out.shape, F32),
        input_output_aliases={6: 0},
        compiler_params=_params(("arbitrary", "arbitrary")),
        name="neighbourhood_attention",
    )(p, p, p, cache_k, cache_v, bias, out)


def _merge_kernel(pool_ref, dn_ref, na_ref, x_ref, mod_ref, g1_ref, g2_ref, w_ref, rw_ref, rb_ref,
                  x1_ref, h2_ref, route_ref, cnt_ref, *, D):
    c0 = pool_ref.shape[1]
    c1 = c0 + dn_ref.shape[1]
    mix = (_dot(pool_ref[...].astype(BF16), w_ref[0:c0, :])
           + _dot(dn_ref[...].astype(BF16), w_ref[c0:c1, :])
           + _dot(na_ref[...].astype(BF16), w_ref[c1:, :]))
    x1 = x_ref[...] + mod_ref[:, 2 * D:3 * D] * (_rms(mix) * g1_ref[...])
    x1_ref[...] = x1
    h2 = (_rms(x1) * g2_ref[...]) * (1.0 + mod_ref[:, 4 * D:5 * D]) + mod_ref[:, 3 * D:4 * D]
    h2_ref[...] = h2

    logits = _dot3(h2, rw_ref[...]) + rb_ref[...]
    tm = logits.shape[0]
    lane = lax.broadcasted_iota(I32, (tm, LANES), 1)
    lane_f = lane.astype(F32)
    work = logits
    vals, hots = [], []
    for _ in range(TOP_K):
        mx = jnp.max(work, axis=-1, keepdims=True)
        idx = jnp.min(jnp.where(work == mx, lane_f, float(LANES)), axis=-1, keepdims=True)
        hot = lane_f == idx
        vals.append(mx)
        hots.append(hot)
        work = jnp.where(hot, -jnp.inf, work)
    es = [jnp.exp(v - vals[0]) for v in vals]
    inv = 1.0 / (es[0] + es[1] + es[2] + es[3])

    sel = jnp.zeros((tm, LANES), F32)
    for hot in hots:
        sel = jnp.where(hot, 1.0, sel)
    r = lax.broadcasted_iota(I32, (tm, tm), 0)
    c = lax.broadcasted_iota(I32, (tm, tm), 1)
    before = jnp.where(c < r, 1.0, 0.0).astype(BF16)
    rank = _dot(before, sel.astype(BF16))
    cnt = jnp.sum(sel, axis=0, keepdims=True)
    cnt_ref[...] = cnt
    run = jnp.floor((cnt + (RUN_ALIGN - 1)) * (1.0 / RUN_ALIGN)) * RUN_ALIGN
    er = lax.broadcasted_iota(I32, (LANES, LANES), 0)
    ec = lax.broadcasted_iota(I32, (LANES, LANES), 1)
    earlier = jnp.where(er < ec, 1.0, 0.0).astype(BF16)
    run_start = _dot(jnp.broadcast_to(run, (8, LANES)).astype(BF16), earlier)[0:1, :]
    pos = rank + run_start

    route = jnp.zeros((tm, LANES), F32)
    for k in range(TOP_K):
        e_k = jnp.sum(jnp.where(hots[k], lane_f, 0.0), axis=-1, keepdims=True)
        p_k = jnp.sum(jnp.where(hots[k], pos, 0.0), axis=-1, keepdims=True)
        route = jnp.where(lane == k, e_k, route)
        route = jnp.where(lane == TOP_K + k, p_k, route)
        route = jnp.where(lane == 2 * TOP_K + k, es[k] * inv, route)
    route_ref[...] = route


def _merge_route(pool_o, dn, na_o, x, mod4, g1, g2, w_out_bf, rw_pad, rb_pad, *, tm, ctx_blocks,
                 blocks_per_seq, l):
    N, D = x.shape
    nb = N // tm

    def grp(i):
        return jnp.where(i < ctx_blocks, 0, 1 + (i - ctx_blocks) // blocks_per_seq)

    row = lambda w: pl.BlockSpec((tm, w), lambda i: (i, 0))
    full = lambda a: pl.BlockSpec(a.shape, lambda i: (0,) * a.ndim)
    return pl.pallas_call(
        functools.partial(_merge_kernel, D=D),
        grid=(nb,),
        in_specs=[row(pool_o.shape[1]), row(dn.shape[1]), row(na_o.shape[1]), row(D),
                  pl.BlockSpec((None, None, 1, N_MOD * D), lambda i: (l, grp(i), 0, 0)),
                  full(g1), full(g2), full(w_out_bf), full(rw_pad), full(rb_pad)],
        out_specs=[row(D), row(D), row(LANES), pl.BlockSpec((None, 1, LANES), lambda i: (i, 0, 0))],
        out_shape=[jax.ShapeDtypeStruct((N, D), F32), jax.ShapeDtypeStruct((N, D), F32),
                   jax.ShapeDtypeStruct((N, LANES), F32), jax.ShapeDtypeStruct((nb, 1, LANES), F32)],
        compiler_params=_params(("arbitrary",)),
        name="merge_route",
    )(pool_o, dn, na_o, x, mod4, g1, g2, w_out_bf, rw_pad, rb_pad)


def _run_copies(n, src_at, dst_at, sem, wait):
    done = jnp.int32(0)
    size = TOK_CHUNK
    while size >= RUN_ALIGN:
        bit = n & size

        @pl.when(bit != 0)
        def _(size=size, done=done):
            cp = pltpu.make_async_copy(src_at(done, size), dst_at(done, size), sem)
            cp.wait() if wait else cp.start()

        done = done + bit
        size //= 2


def _dispatch_kernel(run_ref, src_ref, dst_ref, tail_ref, taildst_ref, nu_ref, route_ref, h_ref, xs_ref,
                     sorted_buf, zero_buf, sem, *, n_exp):
    b = pl.program_id(0)
    route = route_ref[...]
    col = lax.broadcasted_iota(I32, (TOK_CHUNK, sorted_buf.shape[0]), 1).astype(F32)
    place = jnp.zeros(col.shape, F32)
    for k in range(TOP_K):
        place = jnp.where(col == route[:, TOP_K + k:TOP_K + k + 1], 1.0, place)
    sorted_buf[...] = lax.dot_general(place.astype(BF16), h_ref[...].astype(BF16), (((0,), (0,)), ((), ())),
                                      preferred_element_type=F32)

    def each_run(wait):
        def body(e, carry):
            j = b * n_exp + e
            so = src_ref[j]
            do = dst_ref[j]
            _run_copies(run_ref[j],
                        lambda o, s: sorted_buf.at[pl.ds(pl.multiple_of(so + o, RUN_ALIGN), s)],
                        lambda o, s: xs_ref.at[pl.ds(pl.multiple_of(do + o, RUN_ALIGN), s)], sem, wait)
            return carry
        lax.fori_loop(0, n_exp, body, 0)

    each_run(False)
    each_run(True)

    @pl.when(b == pl.num_programs(0) - 1)
    def _():
        zero_buf[...] = jnp.zeros(zero_buf.shape, F32)

        def each_tail(wait):
            def body(e, carry):
                do = taildst_ref[e]
                _run_copies(tail_ref[e],
                            lambda o, s: zero_buf.at[pl.ds(0, s)],
                            lambda o, s: xs_ref.at[pl.ds(pl.multiple_of(do + o, RUN_ALIGN), s)], sem, wait)
                return carry
            lax.fori_loop(0, n_exp, body, 0)

        each_tail(False)
        each_tail(True)

        def spare_block(wait):
            def body(i, carry):
                cp = pltpu.make_async_copy(zero_buf.at[pl.ds(0, MOE_ROWS)],
                                           xs_ref.at[pl.ds(pl.multiple_of(i * MOE_ROWS, MOE_ROWS), MOE_ROWS)], sem)
                cp.wait() if wait else cp.start()
                return carry
            lax.fori_loop(nu_ref[0], xs_ref.shape[0] // MOE_ROWS, body, 0)

        spare_block(False)
        spare_block(True)


def _dispatch(plan, n_used, route, h2, n_rows, n_exp):
    N, D = h2.shape
    grid_spec = pltpu.PrefetchScalarGridSpec(
        num_scalar_prefetch=6,
        grid=(N // TOK_CHUNK,),
        in_specs=[pl.BlockSpec((TOK_CHUNK, LANES), lambda i, *_: (i, 0)),
                  pl.BlockSpec((TOK_CHUNK, D), lambda i, *_: (i, 0))],
        out_specs=pl.BlockSpec(memory_space=pl.ANY),
        scratch_shapes=[pltpu.VMEM((_sorted_rows(n_exp), D), F32), pltpu.VMEM((max(TOK_CHUNK, MOE_ROWS), D), F32),
                        pltpu.SemaphoreType.DMA(())],
    )
    return pl.pallas_call(
        functools.partial(_dispatch_kernel, n_exp=n_exp),
        grid_spec=grid_spec,
        out_shape=jax.ShapeDtypeStruct((n_rows, D), F32),
        compiler_params=_params(("arbitrary",)),
        name="moe_dispatch",
    )(plan["run"], plan["src"], plan["dst"], plan["tail"], plan["tail_dst"], n_used, route, h2)


def _expert_kernel(be_ref, nu_ref, xs_ref, w1_ref, b1_ref, w2_ref, b2_ref, o_ref, w1b, w2b, *, F):
    i = pl.program_id(0)
    e = be_ref[i]
    prev = be_ref[jnp.maximum(i - 1, 0)]

    @pl.when((i == 0) | (e != prev))
    def _():
        w1b[...] = w1_ref[...].astype(BF16)
        w2b[...] = w2_ref[...].astype(BF16)

    @pl.when(i < nu_ref[0])
    def _():
        hh = _dot(xs_ref[...].astype(BF16), w1b[...]) + b1_ref[...]
        g = jnp.minimum(hh[:, 0:F], SWIGLU_LIMIT)
        u = jnp.clip(hh[:, F:2 * F], -SWIGLU_LIMIT, SWIGLU_LIMIT)
        a = (g * (1.0 / (1.0 + jnp.exp(-SWIGLU_ALPHA * g)))) * (u + 1.0)
        o_ref[...] = _dot(a.astype(BF16), w2b[...]) + b2_ref[...]

    @pl.when(i >= nu_ref[0])
    def _():
        o_ref[...] = jnp.zeros(o_ref.shape, F32)


def _experts(block_e, n_used, xs, w1, b1, w2, b2, *, l):
    R, D = xs.shape
    L, E, _, F2 = w1.shape
    F = F2 // 2
    nblk = R // MOE_ROWS
    grid_spec = pltpu.PrefetchScalarGridSpec(
        num_scalar_prefetch=2,
        grid=(nblk,),
        in_specs=[pl.BlockSpec((MOE_ROWS, D), lambda i, be, nu: (jnp.minimum(i, nu[0] - 1), 0)),
                  pl.BlockSpec((None, None, D, F2), lambda i, be, nu: (l, be[i], 0, 0)),
                  pl.BlockSpec((None, None, 1, F2), lambda i, be, nu: (l, be[i], 0, 0)),
                  pl.BlockSpec((None, None, F, D), lambda i, be, nu: (l, be[i], 0, 0)),
                  pl.BlockSpec((None, None, 1, D), lambda i, be, nu: (l, be[i], 0, 0))],
        out_specs=pl.BlockSpec((MOE_ROWS, D), lambda i, be, nu: (i, 0)),
        scratch_shapes=[pltpu.VMEM((D, F2), BF16), pltpu.VMEM((F, D), BF16)],
    )
    return pl.pallas_call(
        functools.partial(_expert_kernel, F=F),
        grid_spec=grid_spec,
        out_shape=jax.ShapeDtypeStruct((R, D), F32),
        compiler_params=_params(("arbitrary",)),
        name="moe_experts",
    )(block_e, n_used, xs, w1, b1.reshape(L, E, 1, F2), w2, b2.reshape(L, E, 1, D))


def _combine_kernel(run_ref, src_ref, dst_ref, yb_ref, route_ref, x1_ref, mod_ref, g_ref, o_ref,
                    sorted_buf, sem, *, D, n_exp):
    b = pl.program_id(0)

    @pl.when(b == 0)
    def _():
        sorted_buf[...] = jnp.zeros(sorted_buf.shape, F32)

    def each_run(wait):
        def body(e, carry):
            j = b * n_exp + e
            so = src_ref[j]
            do = dst_ref[j]
            _run_copies(run_ref[j],
                        lambda o, s: yb_ref.at[pl.ds(pl.multiple_of(do + o, RUN_ALIGN), s)],
                        lambda o, s: sorted_buf.at[pl.ds(pl.multiple_of(so + o, RUN_ALIGN), s)], sem, wait)
            return carry
        lax.fori_loop(0, n_exp, body, 0)

    each_run(False)
    each_run(True)
    route = route_ref[...]
    col = lax.broadcasted_iota(I32, (TOK_CHUNK, sorted_buf.shape[0]), 1).astype(F32)
    gate = jnp.zeros(col.shape, F32)
    for k in range(TOP_K):
        gate = jnp.where(col == route[:, TOP_K + k:TOP_K + k + 1], route[:, 2 * TOP_K + k:2 * TOP_K + k + 1], gate)
    g_hi, g_lo = _split(gate)
    yb = sorted_buf[...].astype(BF16)
    y = _dot(g_hi, yb) + _dot(g_lo, yb)
    o_ref[...] = x1_ref[...] + mod_ref[:, 5 * D:6 * D] * (_rms(y) * g_ref[...])


def _combine(plan, yb, route, x1, mod4, g3, *, n_exp, ctx_blocks, blocks_per_seq, l):
    N, D = x1.shape

    def grp(i):
        return jnp.where(i < ctx_blocks, 0, 1 + (i - ctx_blocks) // blocks_per_seq)

    grid_spec = pltpu.PrefetchScalarGridSpec(
        num_scalar_prefetch=3,
        grid=(N // TOK_CHUNK,),
        in_specs=[pl.BlockSpec(memory_space=pl.ANY),
                  pl.BlockSpec((TOK_CHUNK, LANES), lambda i, *_: (i, 0)),
                  pl.BlockSpec((TOK_CHUNK, D), lambda i, *_: (i, 0)),
                  pl.BlockSpec((None, None, 1, N_MOD * D), lambda i, *_: (l, grp(i), 0, 0)),
                  pl.BlockSpec((1, D), lambda i, *_: (0, 0))],
        out_specs=pl.BlockSpec((TOK_CHUNK, D), lambda i, *_: (i, 0)),
        scratch_shapes=[pltpu.VMEM((_sorted_rows(n_exp), D), F32), pltpu.SemaphoreType.DMA(())],
    )
    return pl.pallas_call(
        functools.partial(_combine_kernel, D=D, n_exp=n_exp),
        grid_spec=grid_spec,
        out_shape=jax.ShapeDtypeStruct((N, D), F32),
        compiler_params=_params(("arbitrary",)),
        name="moe_combine",
    )(plan["run"], plan["src"], plan["dst"], yb, route, x1, mod4, g3)


def _rope_tables(Ts, tm):
    nf = DIFF_QK_DIM // 4
    inv = ROPE_BASE ** (-jnp.arange(nf, dtype=F32) / nf)
    t = jnp.arange(Ts)
    pos = jnp.stack([(t // GRID_W).astype(F32), (t % GRID_W).astype(F32)], axis=1)
    ang = pos[:, :, None] * inv[None, None, :]
    cos = jnp.repeat(jnp.cos(ang)[:, :, None, :], 2, axis=2).reshape(Ts, DIFF_QK_DIM)
    sin = jnp.sin(ang)
    sin = jnp.stack([-sin, sin], axis=2).reshape(Ts, DIFF_QK_DIM)
    reps = LANES // DIFF_QK_DIM
    cos = jnp.concatenate([jnp.ones((tm, LANES), F32), jnp.tile(cos, (1, reps))], axis=0)
    sin = jnp.concatenate([jnp.zeros((tm, LANES), F32), jnp.tile(sin, (1, reps))], axis=0)
    return cos, sin


def _block_diag(w):
    G, a, b = w.shape
    out = jnp.zeros((G * a, G * b), w.dtype)
    for g in range(G):
        out = out.at[g * a:(g + 1) * a, g * b:(g + 1) * b].set(w[g])
    return out


def _route_plan(counts, n_exp, nblk):
    cnt = counts[:, 0, :n_exp].astype(I32)
    run = (cnt + RUN_ALIGN - 1) // RUN_ALIGN * RUN_ALIGN
    src = jnp.cumsum(run, axis=1) - run
    tot = jnp.sum(run, axis=0)
    region = (tot + MOE_ROWS - 1) // MOE_ROWS * MOE_ROWS
    region_end = jnp.cumsum(region)
    region_start = region_end - region
    dst = region_start[None, :] + jnp.cumsum(run, axis=0) - run
    n_used = (region_end[-1] // MOE_ROWS).astype(I32)
    blk = jnp.arange(nblk, dtype=I32) * MOE_ROWS
    block_e = jnp.minimum(jnp.sum((blk[:, None] >= region_end[None, :]).astype(I32), axis=1), n_exp - 1)
    last = jnp.sum(jnp.where(jnp.arange(nblk) == n_used - 1, block_e, 0))
    block_e = jnp.where(jnp.arange(nblk) < n_used, block_e, last).astype(I32)
    plan = dict(run=run.reshape(-1).astype(I32), src=src.reshape(-1).astype(I32), dst=dst.reshape(-1).astype(I32),
                tail=(region - tot).astype(I32), tail_dst=(region_start + tot).astype(I32))
    return plan, block_e, n_used.reshape(1)


def kernel(x_prompt, x_sample, cache_diff_k, cache_diff_v, cache_na_k, cache_na_v, c, c_ctx, w_ada, b_ada,
           norm_gain, w_in, w_out, pool_w, pool_scale, diff_lambda, diff_subln, na_rpb, router_w, router_b,
           moe_w1, moe_b1, moe_w2, moe_b2):
    Bp, Tp, D = x_prompt.shape
    Bs, Ts, _ = x_sample.shape
    L = w_ada.shape[0]
    E = router_w.shape[-1]
    Np, Ns = Bp * Tp, Bs * Ts
    N = Np + Ns
    tm = TOK_CHUNK
    assert Np % Ts == 0 or Bs == 0, "context rows must be a whole number of latent-sequence blocks"
    assert Np % tm == 0 and Ts % tm == 0 and Ts % (NA_Q_ROWS * GRID_W) == 0
    assert Ts // GRID_W >= NA_BAND + NA_Q_ROWS
    ctx_blocks, blocks_per_seq = Np // tm, Ts // tm

    G = 16
    cvec = jnp.zeros((G, D), F32).at[0].set(c_ctx).at[1:1 + Bs].set(c)
    mod4 = _modulation(cvec, w_ada, b_ada).reshape(L, G, 1, N_MOD * D)
    cos_t, sin_t = _rope_tables(Ts, tm)
    w_in_bf = w_in.astype(BF16)
    w_out_bf = w_out.astype(BF16)
    rw_pad = jnp.zeros((L, D, LANES), F32).at[:, :, :E].set(router_w)
    rb_pad = jnp.full((L, 1, LANES), MASKED, F32).at[:, 0, :E].set(router_b)
    ck = cache_diff_k.reshape(Bs, L, -1, DIFF_HEADS * 2 * DIFF_QK_DIM)
    cv = cache_diff_v.reshape(Bs, L, -1, DIFF_HEADS * DIFF_V_DIM)
    nk = cache_na_k.reshape(Bs, L, -1, NA_HEADS * NA_HEAD_DIM)
    nv = cache_na_v.reshape(Bs, L, -1, NA_HEADS * NA_HEAD_DIM)
    nb = N // TOK_CHUNK
    nblk = -(-(N * TOP_K + nb * E * (RUN_ALIGN - 1) + E * (MOE_ROWS - 1)) // MOE_ROWS)
    blocks = dict(ctx_blocks=ctx_blocks, blocks_per_seq=blocks_per_seq)

    x = jnp.concatenate([x_prompt.reshape(Np, D), x_sample.reshape(Ns, D)], axis=0)
    new_dk, new_dv, new_nk, new_nv = [], [], [], []
    for l in range(L):
        lam_init = 0.8 - 0.6 * math.exp(-0.3 * l)
        g = norm_gain[l]
        p = _in_projection(x, mod4, g[0:1], w_in_bf[l], cos_t, sin_t, tm=tm, l=l, **blocks)

        pw = _block_diag(pool_w[l]).astype(BF16)
        ps = pool_scale[l].reshape(1, -1)
        pool_o = jnp.zeros((N, pw.shape[0]), F32)
        pool_o = _pool(p, pw, ps, row0=0, n_seq=Bp, T=Tp, out=pool_o)
        pool_o = _pool(p, pw, ps, row0=Np, n_seq=Bs, T=Ts, out=pool_o)

        sub = diff_subln[l].reshape(1, -1)
        dn = jnp.zeros((N, DIFF_HEADS * DIFF_V_DIM), F32)
        dn = _diff_attention(p, diff_lambda[l], sub, dn, row0=0, n_seq=Bp, T=Tp, tq=Tp, lam_init=lam_init)
        dn = _diff_attention(p, diff_lambda[l], sub, dn, row0=Np, n_seq=Bs, T=Ts, tq=256, lam_init=lam_init,
                             cache_k=ck, cache_v=cv, l=l)

        na_o = jnp.zeros((N, NA_HEADS * NA_HEAD_DIM), F32)
        na_o = _dense_attention(p, na_o, n_seq=Bp, T=Tp)
        bias = _na_bias_table(na_rpb[l], Ts // GRID_W)
        na_o = _neighbourhood_attention(p, nk, nv, bias, na_o, row0=Np, n_seq=Bs, T=Ts, l=l)

        x1, h2, route, counts = _merge_route(pool_o, dn, na_o, x, mod4, g[1:2], g[2:3], w_out_bf[l],
                                             rw_pad[l], rb_pad[l], tm=tm, l=l, **blocks)
        plan, block_e, n_used = _route_plan(counts, E, nblk)
        xs = _dispatch(plan, n_used, route, h2, nblk * MOE_ROWS, E)
        yb = _experts(block_e, n_used, xs, moe_w1, moe_b1, moe_w2, moe_b2, l=l)
        x = _combine(plan, yb, route, x1, mod4, g[3:4], n_exp=E, l=l, **blocks)

        pc = p[:Np]
        new_dk.append(pc[:, 768:1280].reshape(Bp, Tp, DIFF_HEADS, 2 * DIFF_QK_DIM))
        new_dv.append(pc[:, 1280:1792].reshape(Bp, Tp, DIFF_HEADS, DIFF_V_DIM))
        new_nk.append(pc[:, 2048:2304].reshape(Bp, Tp, NA_HEADS, NA_HEAD_DIM))
        new_nv.append(pc[:, 2304:2560].reshape(Bp, Tp, NA_HEADS, NA_HEAD_DIM))

    return (x[:Np].reshape(Bp, Tp, D), x[Np:].reshape(Bs, Ts, D),
            jnp.stack(new_dk, axis=1), jnp.stack(new_dv, axis=1),
            jnp.stack(new_nk, axis=1), jnp.stack(new_nv, axis=1))
```

```python
import functools
import math

import numpy as np
import jax
import jax.numpy as jnp
from jax import lax
from jax.experimental import pallas as pl
from jax.experimental.pallas import tpu as pltpu

F32 = jnp.float32
BF16 = jnp.bfloat16
I32 = jnp.int32

GRID_W = 64
POOL_GROUPS = 4
POOL_MAX_HALF = 8
DIFF_HEADS = 4
DIFF_V_DIM = 128
DIFF_QK_DIM = 64
NA_HEADS = 4
NA_HEAD_DIM = 64
NA_WIN_H = 8
NA_WIN_W = 16
NA_Q_ROWS = 4
NA_BAND = 12
TOP_K = 4
SWIGLU_ALPHA = 1.702
SWIGLU_LIMIT = 7.0
ROPE_BASE = 10000.0
NORM_EPS = 1e-6
N_MOD = 6

LOG2E = 1.4426950408889634
MASKED = -1e30
LANES = 128
MXU_TILE = 256
MOE_ROWS = 512
COL_DQ, COL_DK, COL_DV = 256, 768, 1280
COL_NQ, COL_NK, COL_NV = 1792, 2048, 2304
DIFF_PAIR = 2
PROJ_ROWS = 512
DIFF_Q_ROWS = 256
TOK_CHUNK = 256
RUN_ALIGN = 8
VMEM_LIMIT = 56 * 1024 * 1024


def _sorted_rows(n_exp):
    rows = TOK_CHUNK * TOP_K + n_exp * (RUN_ALIGN - 1)
    return -(-rows // MXU_TILE) * MXU_TILE


def _params(sem, vmem=VMEM_LIMIT):
    return pltpu.CompilerParams(dimension_semantics=sem, vmem_limit_bytes=vmem)


def _dot(a, b):
    return jnp.dot(a, b, preferred_element_type=F32)


def _dot_nt(a, b):
    return lax.dot_general(a, b, (((1,), (1,)), ((), ())), preferred_element_type=F32)


def _split(x):
    hi = x.astype(BF16)
    return hi, (x - hi.astype(F32)).astype(BF16)


def _dot3(a, b):
    ah, al = _split(a)
    bh, bl = _split(b)
    return _dot(ah, bh) + _dot(al, bh) + _dot(ah, bl)


def _rms(x):
    return x * lax.rsqrt(jnp.mean(x * x, axis=-1, keepdims=True) + NORM_EPS)


def _mod_kernel(c_ref, w_ref, b_ref, o_ref):
    c = c_ref[...]
    a = c * (1.0 / (1.0 + jnp.exp(-c)))
    o_ref[...] = _dot3(a, w_ref[...]) + b_ref[...]


def _modulation(cvec, w_ada, b_ada):
    L, D, W = w_ada.shape
    G = cvec.shape[0]
    return pl.pallas_call(
        _mod_kernel,
        grid=(L, W // D),
        in_specs=[pl.BlockSpec((G, D), lambda l, j: (0, 0)),
                  pl.BlockSpec((None, D, D), lambda l, j: (l, 0, j)),
                  pl.BlockSpec((None, 1, D), lambda l, j: (l, 0, j))],
        out_specs=pl.BlockSpec((None, G, D), lambda l, j: (l, 0, j)),
        out_shape=jax.ShapeDtypeStruct((L, G, W), F32),
        compiler_params=_params(("arbitrary", "arbitrary")),
        name="ada_modulation",
    )(cvec, w_ada, b_ada.reshape(L, 1, W))


def _inproj_kernel(x_ref, mod_ref, g_ref, w_ref, cos_ref, sin_ref, o_ref, *, D, rope_lo, rope_hi):
    h = _rms(x_ref[...]) * g_ref[...]
    h = h * (1.0 + mod_ref[:, D:2 * D]) + mod_ref[:, 0:D]
    p = _dot(h.astype(BF16), w_ref[...])
    W = p.shape[1]
    o_ref[:, 0:rope_lo] = p[:, 0:rope_lo]
    o_ref[:, rope_hi:W] = p[:, rope_hi:W]
    cos = cos_ref[...]
    sin = sin_ref[...]
    lane = lax.broadcasted_iota(I32, cos.shape, 1)
    first = (lane % 32) < 16
    for c0 in range(rope_lo, rope_hi, LANES):
        xc = p[:, c0:c0 + LANES]
        partner = jnp.where(first, pltpu.roll(xc, LANES - 16, 1), pltpu.roll(xc, 16, 1))
        o_ref[:, c0:c0 + LANES] = xc * cos + partner * sin


def _in_projection(x, mod4, gain, w_bf, cos_t, sin_t, *, tm, ctx_blocks, blocks_per_seq, l):
    N, D = x.shape
    W = w_bf.shape[1]

    def grp(i):
        return jnp.where(i < ctx_blocks, 0, 1 + (i - ctx_blocks) // blocks_per_seq)

    def rope_blk(i):
        return jnp.where(i < ctx_blocks, 0, 1 + (i - ctx_blocks) % blocks_per_seq)

    kern = functools.partial(_inproj_kernel, D=D, rope_lo=256, rope_hi=256 + 2 * DIFF_HEADS * 2 * DIFF_QK_DIM)
    return pl.pallas_call(
        kern,
        grid=(N // tm,),
        in_specs=[pl.BlockSpec((tm, D), lambda i: (i, 0)),
                  pl.BlockSpec((None, None, 1, N_MOD * D), lambda i: (l, grp(i), 0, 0)),
                  pl.BlockSpec((1, D), lambda i: (0, 0)),
                  pl.BlockSpec((D, W), lambda i: (0, 0)),
                  pl.BlockSpec((tm, LANES), lambda i: (rope_blk(i), 0)),
                  pl.BlockSpec((tm, LANES), lambda i: (rope_blk(i), 0))],
        out_specs=pl.BlockSpec((tm, W), lambda i: (i, 0)),
        out_shape=jax.ShapeDtypeStruct((N, W), F32),
        compiler_params=_params(("arbitrary",)),
        name="in_projection",
    )(x, mod4, gain, w_bf, cos_t, sin_t)


def _pool_kernel(u_ref, w_ref, sc_ref, o_ref, pad_ref, *, T, CH):
    H = 2 * POOL_MAX_HALF
    zeros = jnp.zeros((H, pad_ref.shape[1]), F32)
    pad_ref[0:H, :] = zeros
    pad_ref[H + T:2 * H + T, :] = zeros
    pad_ref[H:H + T, :] = u_ref[...]
    C = pad_ref.shape[1]
    lane = lax.broadcasted_iota(I32, (CH, C), 1)
    half = jnp.left_shift(1, lane // (C // POOL_GROUPS))
    row = lax.broadcasted_iota(I32, (CH, C), 0)

    def body(ci, carry):
        base = pl.multiple_of(ci * CH, CH)
        win = pad_ref[pl.ds(base + POOL_MAX_HALF, CH + H), :]
        acc = jnp.zeros((CH, C), F32)
        for j in range(-POOL_MAX_HALF, POOL_MAX_HALF):
            sl = win[POOL_MAX_HALF + j:POOL_MAX_HALF + j + CH, :]
            inside = (half > j) if j >= 0 else (half >= -j)
            acc = acc + jnp.where(inside, sl, 0.0)
        t = row + base
        cnt = jnp.minimum(t + half, T) - jnp.maximum(t - half, 0)
        d = acc / cnt.astype(F32) - win[POOL_MAX_HALF:POOL_MAX_HALF + CH, :]
        o_ref[pl.ds(base, CH), :] = _dot(d.astype(BF16), w_ref[...]) * sc_ref[...]
        return carry

    lax.fori_loop(0, T // CH, body, 0)


def _pool(p, w_bd, scale, *, row0, n_seq, T, out):
    C = w_bd.shape[0]
    CH = min(T, 256)
    blk0 = row0 // T

    def kern(u_ref, w_ref, sc_ref, prev_ref, o_ref, pad_ref):
        del prev_ref
        _pool_kernel(u_ref, w_ref, sc_ref, o_ref, pad_ref, T=T, CH=CH)

    return pl.pallas_call(
        kern,
        grid=(n_seq,),
        in_specs=[pl.BlockSpec((T, C), lambda s: (blk0 + s, 0)),
                  pl.BlockSpec((C, C), lambda s: (0, 0)),
                  pl.BlockSpec((1, C), lambda s: (0, 0)),
                  pl.BlockSpec(memory_space=pl.ANY)],
        out_specs=pl.BlockSpec((T, C), lambda s: (blk0 + s, 0)),
        out_shape=jax.ShapeDtypeStruct(out.shape, F32),
        scratch_shapes=[pltpu.VMEM((T + 4 * POOL_MAX_HALF, C), F32)],
        input_output_aliases={3: 0},
        compiler_params=_params(("arbitrary",)),
        name="pool_mixer",
    )(p, w_bd, scale, out)


def _softmax_pv(s, v):
    m = jnp.max(s, axis=-1, keepdims=True)
    e = jnp.exp2(s - m)
    l = jnp.sum(e, axis=-1, keepdims=True)
    return _dot(e.astype(BF16), v) * (1.0 / l)


def _diff_kernel(*refs, Ts, Lc, lam_init):
    if Lc:
        lam_ref, sub_ref, q_ref, ks_ref, vs_ref, kc_ref, vc_ref, prev_ref, o_ref, kb, vb = refs
    else:
        lam_ref, sub_ref, q_ref, ks_ref, vs_ref, prev_ref, o_ref, kb, vb = refs
    del prev_ref

    @pl.when(pl.program_id(2) == 0)
    def _():
        kb[0:Ts, :] = ks_ref[...].astype(BF16)
        vb[0:Ts, :] = vs_ref[...].astype(BF16)
        if Lc:
            kb[Ts:Ts + Lc, :] = kc_ref[...].astype(BF16)
            vb[Ts:Ts + Lc, :] = vc_ref[...].astype(BF16)

    lm = lam_ref[...]
    lam = (jnp.exp(jnp.sum(lm[0:1, :] * lm[1:2, :], axis=-1, keepdims=True))
           - jnp.exp(jnp.sum(lm[2:3, :] * lm[3:4, :], axis=-1, keepdims=True)) + lam_init)
    q = q_ref[...] * (DIFF_QK_DIM ** -0.5 * LOG2E)
    part = lax.broadcasted_iota(I32, q.shape, 1) // DIFF_QK_DIM
    k = kb[...]
    v = vb[...]
    for hh in range(DIFF_PAIR):
        o1 = _softmax_pv(_dot_nt(jnp.where(part == 2 * hh, q, 0.0).astype(BF16), k), v)
        o2 = _softmax_pv(_dot_nt(jnp.where(part == 2 * hh + 1, q, 0.0).astype(BF16), k), v)
        cols = slice(hh * DIFF_V_DIM, (hh + 1) * DIFF_V_DIM)
        o = o1[:, cols] - lam * o2[:, cols]
        o_ref[:, cols] = _rms(o) * sub_ref[...] * (1.0 - lam_init)


def _diff_attention(p, lam_l, subln, out, *, row0, n_seq, T, tq, lam_init, cache_k=None, cache_v=None, l=0):
    W = DIFF_PAIR * DIFF_V_DIM
    assert 2 * DIFF_QK_DIM == DIFF_V_DIM and DIFF_HEADS % DIFF_PAIR == 0
    n_pair = DIFF_HEADS // DIFF_PAIR
    Lc = 0 if cache_k is None else cache_k.shape[2]
    nq = T // tq
    qb0 = row0 // tq
    sb0 = row0 // T
    in_specs = [pl.BlockSpec((4, DIFF_QK_DIM), lambda b, h, i: (0, 0)),
                pl.BlockSpec((1, DIFF_V_DIM), lambda b, h, i: (0, 0)),
                pl.BlockSpec((tq, W), lambda b, h, i: (qb0 + b * nq + i, COL_DQ // W + h)),
                pl.BlockSpec((T, W), lambda b, h, i: (sb0 + b, COL_DK // W + h)),
                pl.BlockSpec((T, W), lambda b, h, i: (sb0 + b, COL_DV // W + h))]
    args = [lam_l, subln, p, p, p]
    if Lc:
        in_specs += [pl.BlockSpec((None, None, Lc, W), lambda b, h, i: (b, l, 0, h)),
                     pl.BlockSpec((None, None, Lc, W), lambda b, h, i: (b, l, 0, h))]
        args += [cache_k, cache_v]
    in_specs.append(pl.BlockSpec(memory_space=pl.ANY))
    args.append(out)
    return pl.pallas_call(
        functools.partial(_diff_kernel, Ts=T, Lc=Lc, lam_init=lam_init),
        grid=(n_seq, n_pair, nq),
        in_specs=in_specs,
        out_specs=pl.BlockSpec((tq, W), lambda b, h, i: (qb0 + b * nq + i, h)),
        out_shape=jax.ShapeDtypeStruct(out.shape, F32),
        scratch_shapes=[pltpu.VMEM((T + Lc, W), BF16), pltpu.VMEM((T + Lc, W), BF16)],
        input_output_aliases={len(args) - 1: 0},
        compiler_params=_params(("arbitrary", "arbitrary", "arbitrary")),
        name="diff_attention",
    )(*args)


def _na_heads(q, score_fn, pv_fn):
    lane = lax.broadcasted_iota(I32, q.shape, 1) // NA_HEAD_DIM
    out = jnp.zeros(q.shape, F32)
    for h in range(NA_HEADS):
        qh = jnp.where(lane == h, q, 0.0).astype(BF16)
        out = jnp.where(lane == h, pv_fn(score_fn(qh, h)), out)
    return out


def _dense_kernel(q_ref, k_ref, v_ref, prev_ref, o_ref):
    del prev_ref
    q = q_ref[...] * (NA_HEAD_DIM ** -0.5 * LOG2E)
    k = k_ref[...].astype(BF16)
    v = v_ref[...].astype(BF16)
    o_ref[...] = _na_heads(q, lambda qh, h: _dot_nt(qh, k), lambda s: _softmax_pv(s, v))


def _dense_attention(p, out, *, n_seq, T):
    C = NA_HEADS * NA_HEAD_DIM
    return pl.pallas_call(
        _dense_kernel,
        grid=(n_seq,),
        in_specs=[pl.BlockSpec((T, C), lambda b: (b, 7)),
                  pl.BlockSpec((T, C), lambda b: (b, 8)),
                  pl.BlockSpec((T, C), lambda b: (b, 9)),
                  pl.BlockSpec(memory_space=pl.ANY)],
        out_specs=pl.BlockSpec((T, C), lambda b: (b, 0)),
        out_shape=jax.ShapeDtypeStruct(out.shape, F32),
        input_output_aliases={3: 0},
        compiler_params=_params(("arbitrary",)),
        name="dense_attention",
    )(p, p, p, out)


def _na_kernel(q_ref, ks_ref, vs_ref, kc_ref, vc_ref, bias_ref, prev_ref, o_ref, *, rows):
    del prev_ref
    r0 = pl.program_id(1) * NA_Q_ROWS
    bs = jnp.clip(r0 - NA_WIN_H // 2, 0, rows - NA_BAND)
    start = pl.multiple_of(bs * GRID_W, GRID_W)
    nb = NA_BAND * GRID_W
    kb = ks_ref[pl.ds(start, nb), :].astype(BF16)
    vb = vs_ref[pl.ds(start, nb), :].astype(BF16)
    kc = kc_ref[...].astype(BF16)
    vc = vc_ref[...].astype(BF16)
    q = q_ref[...] * (NA_HEAD_DIM ** -0.5 * LOG2E)

    def scores(qh, h):
        return _dot_nt(qh, kb) + bias_ref[h], _dot_nt(qh, kc)

    def pv(s):
        s_loc, s_ctx = s
        m = jnp.maximum(jnp.max(s_loc, axis=-1, keepdims=True), jnp.max(s_ctx, axis=-1, keepdims=True))
        e_loc = jnp.exp2(s_loc - m)
        e_ctx = jnp.exp2(s_ctx - m)
        l = jnp.sum(e_loc, axis=-1, keepdims=True) + jnp.sum(e_ctx, axis=-1, keepdims=True)
        return (_dot(e_loc.astype(BF16), vb) + _dot(e_ctx.astype(BF16), vc)) * (1.0 / l)

    o_ref[...] = _na_heads(q, scores, pv)


def _na_bias_table(rpb_l, rows):
    n_ro, n_co = 2 * NA_WIN_H - 1, 2 * NA_WIN_W - 1
    c = np.arange(GRID_W)[:, None]
    kc = np.arange(GRID_W)[None, :]
    cs = np.clip(c - NA_WIN_W // 2, 0, GRID_W - NA_WIN_W)
    col_ok = (kc >= cs) & (kc < cs + NA_WIN_W)
    co = kc - c + (NA_WIN_W - 1)
    pick = ((np.arange(n_co)[:, None, None] == co[None]) & col_ok[None]).astype(np.float32)
    toep = jnp.dot(rpb_l.reshape(NA_HEADS * n_ro, n_co).astype(F32), jnp.asarray(pick.reshape(n_co, -1)),
                   precision=lax.Precision.HIGHEST).reshape(NA_HEADS, n_ro, GRID_W, GRID_W) * LOG2E
    toep = jnp.where(jnp.asarray(col_ok)[None, None], toep, MASKED)
    toep = jnp.concatenate([toep, jnp.full((NA_HEADS, 1, GRID_W, GRID_W), MASKED, F32)], axis=1)
    blk = np.full((3, NA_Q_ROWS, NA_BAND), n_ro, np.int32)
    for v, r0 in enumerate((0, NA_Q_ROWS, rows - NA_Q_ROWS)):
        bs = int(np.clip(r0 - NA_WIN_H // 2, 0, rows - NA_BAND))
        for j in range(NA_Q_ROWS):
            rs = int(np.clip(r0 + j - NA_WIN_H // 2, 0, rows - NA_WIN_H))
            for i in range(NA_BAND):
                if rs <= bs + i < rs + NA_WIN_H:
                    blk[v, j, i] = bs + i - (r0 + j) + (NA_WIN_H - 1)
    tab = toep[:, blk]
    tab = jnp.transpose(tab, (1, 0, 2, 4, 3, 5))
    return tab.reshape(3, NA_HEADS, NA_Q_ROWS * GRID_W, NA_BAND * GRID_W)


def _neighbourhood_attention(p, cache_k, cache_v, bias, out, *, row0, n_seq, T, l):
    C = NA_HEADS * NA_HEAD_DIM
    rows = T // GRID_W
    tq = NA_Q_ROWS * GRID_W
    nq = T // tq
    Lc = cache_k.shape[2]
    qb0 = row0 // tq
    sb0 = row0 // T

    def variant(i):
        r0 = i * NA_Q_ROWS
        return (r0 - jnp.clip(r0 - NA_WIN_H // 2, 0, rows - NA_BAND)) // NA_Q_ROWS

    return pl.pallas_call(
        functools.partial(_na_kernel, rows=rows),
        grid=(n_seq, nq),
        in_specs=[pl.BlockSpec((tq, C), lambda b, i: (qb0 + b * nq + i, 7)),
                  pl.BlockSpec((T, C), lambda b, i: (sb0 + b, 8)),
                  pl.BlockSpec((T, C), lambda b, i: (sb0 + b, 9)),
                  pl.BlockSpec((None, None, Lc, C), lambda b, i: (b, l, 0, 0)),
                  pl.BlockSpec((None, None, Lc, C), lambda b, i: (b, l, 0, 0)),
                  pl.BlockSpec((None, NA_HEADS, tq, NA_BAND * GRID_W), lambda b, i: (variant(i), 0, 0, 0)),
                  pl.BlockSpec(memory_space=pl.ANY)],
        out_specs=pl.BlockSpec((tq, C), lambda b, i: (qb0 + b * nq + i, 0)),
        out_shape=jax.ShapeDtypeStruct(out.shape, F32),
        input_output_aliases={6: 0},
        compiler_params=_params(("arbitrary", "arbitrary")),
        name="neighbourhood_attention",
    )(p, p, p, cache_k, cache_v, bias, out)


def _merge_kernel(pool_ref, dn_ref, na_ref, x_ref, mod_ref, g1_ref, g2_ref, w_ref, rw_ref, rb_ref,
                  x1_ref, h2_ref, route_ref, cnt_ref, *, D):
    c0 = pool_ref.shape[1]
    c1 = c0 + dn_ref.shape[1]
    mix = (_dot(pool_ref[...].astype(BF16), w_ref[0:c0, :])
           + _dot(dn_ref[...].astype(BF16), w_ref[c0:c1, :])
           + _dot(na_ref[...].astype(BF16), w_ref[c1:, :]))
    x1 = x_ref[...] + mod_ref[:, 2 * D:3 * D] * (_rms(mix) * g1_ref[...])
    x1_ref[...] = x1
    h2 = (_rms(x1) * g2_ref[...]) * (1.0 + mod_ref[:, 4 * D:5 * D]) + mod_ref[:, 3 * D:4 * D]
    h2_ref[...] = h2

    logits = _dot3(h2, rw_ref[...]) + rb_ref[...]
    tm = logits.shape[0]
    lane = lax.broadcasted_iota(I32, (tm, LANES), 1)
    lane_f = lane.astype(F32)
    work = logits
    vals, hots = [], []
    for _ in range(TOP_K):
        mx = jnp.max(work, axis=-1, keepdims=True)
        idx = jnp.min(jnp.where(work == mx, lane_f, float(LANES)), axis=-1, keepdims=True)
        hot = lane_f == idx
        vals.append(mx)
        hots.append(hot)
        work = jnp.where(hot, -jnp.inf, work)
    es = [jnp.exp(v - vals[0]) for v in vals]
    inv = 1.0 / (es[0] + es[1] + es[2] + es[3])

    sel = jnp.zeros((tm, LANES), F32)
    for hot in hots:
        sel = jnp.where(hot, 1.0, sel)
    r = lax.broadcasted_iota(I32, (tm, tm), 0)
    c = lax.broadcasted_iota(I32, (tm, tm), 1)
    before = jnp.where(c < r, 1.0, 0.0).astype(BF16)
    rank = _dot(before, sel.astype(BF16))
    cnt = jnp.sum(sel, axis=0, keepdims=True)
    cnt_ref[...] = cnt
    run = jnp.floor((cnt + (RUN_ALIGN - 1)) * (1.0 / RUN_ALIGN)) * RUN_ALIGN
    er = lax.broadcasted_iota(I32, (LANES, LANES), 0)
    ec = lax.broadcasted_iota(I32, (LANES, LANES), 1)
    earlier = jnp.where(er < ec, 1.0, 0.0).astype(BF16)
    run_start = _dot(jnp.broadcast_to(run, (8, LANES)).astype(BF16), earlier)[0:1, :]
    pos = rank + run_start

    route = jnp.zeros((tm, LANES), F32)
    for k in range(TOP_K):
        e_k = jnp.sum(jnp.where(hots[k], lane_f, 0.0), axis=-1, keepdims=True)
        p_k = jnp.sum(jnp.where(hots[k], pos, 0.0), axis=-1, keepdims=True)
        route = jnp.where(lane == k, e_k, route)
        route = jnp.where(lane == TOP_K + k, p_k, route)
        route = jnp.where(lane == 2 * TOP_K + k, es[k] * inv, route)
    route_ref[...] = route


def _merge_route(pool_o, dn, na_o, x, mod4, g1, g2, w_out_bf, rw_pad, rb_pad, *, tm, ctx_blocks,
                 blocks_per_seq, l):
    N, D = x.shape
    nb = N // tm

    def grp(i):
        return jnp.where(i < ctx_blocks, 0, 1 + (i - ctx_blocks) // blocks_per_seq)

    row = lambda w: pl.BlockSpec((tm, w), lambda i: (i, 0))
    full = lambda a: pl.BlockSpec(a.shape, lambda i: (0,) * a.ndim)
    return pl.pallas_call(
        functools.partial(_merge_kernel, D=D),
        grid=(nb,),
        in_specs=[row(pool_o.shape[1]), row(dn.shape[1]), row(na_o.shape[1]), row(D),
                  pl.BlockSpec((None, None, 1, N_MOD * D), lambda i: (l, grp(i), 0, 0)),
                  full(g1), full(g2), full(w_out_bf), full(rw_pad), full(rb_pad)],
        out_specs=[row(D), row(D), row(LANES), pl.BlockSpec((None, 1, LANES), lambda i: (i, 0, 0))],
        out_shape=[jax.ShapeDtypeStruct((N, D), F32), jax.ShapeDtypeStruct((N, D), F32),
                   jax.ShapeDtypeStruct((N, LANES), F32), jax.ShapeDtypeStruct((nb, 1, LANES), F32)],
        compiler_params=_params(("arbitrary",)),
        name="merge_route",
    )(pool_o, dn, na_o, x, mod4, g1, g2, w_out_bf, rw_pad, rb_pad)


def _pack_pairs(x):
    C = x.shape[1] // 2
    bits = lax.bitcast_convert_type(x.astype(BF16).astype(F32), jnp.uint32)
    return bits[:, C:] | (bits[:, :C] >> 16)


def _unpack_pairs(w):
    lo = lax.bitcast_convert_type(w << 16, F32)
    hi = lax.bitcast_convert_type(w & jnp.uint32(0xFFFF0000), F32)
    return jnp.concatenate([lo, hi], axis=1).astype(BF16)


def _run_copies(n, src_at, dst_at, sem, wait):
    done = jnp.int32(0)
    size = TOK_CHUNK
    while size >= RUN_ALIGN:
        bit = n & size

        @pl.when(bit != 0)
        def _(size=size, done=done):
            cp = pltpu.make_async_copy(src_at(done, size), dst_at(done, size), sem)
            cp.wait() if wait else cp.start()

        done = done + bit
        size //= 2


def _block_runs(step, n_exp, run_ref, src_ref, dst_ref, hbm_ref, buf, sem, *, to_hbm, wait):
    def body(e, carry):
        j = step * n_exp + e
        so = src_ref[j]
        do = dst_ref[j]
        in_buf = lambda o, s: buf.at[pl.ds(pl.multiple_of(so + o, RUN_ALIGN), s)]
        in_hbm = lambda o, s: hbm_ref.at[pl.ds(pl.multiple_of(do + o, RUN_ALIGN), s)]
        if to_hbm:
            _run_copies(run_ref[j], in_buf, in_hbm, sem, wait)
        else:
            _run_copies(run_ref[j], in_hbm, in_buf, sem, wait)
        return carry
    lax.fori_loop(0, n_exp, body, 0)


def _dispatch_kernel(run_ref, src_ref, dst_ref, tail_ref, taildst_ref, nu_ref, route_ref, h_ref, xs_ref,
                     sorted_buf, zero_buf, sems, *, n_exp):
    b = pl.program_id(0)
    nb = pl.num_programs(0)
    slot = b % 2
    runs = functools.partial(_block_runs, n_exp=n_exp, run_ref=run_ref, src_ref=src_ref, dst_ref=dst_ref,
                             hbm_ref=xs_ref, to_hbm=True)

    @pl.when(b >= 2)
    def _():
        runs(b - 2, buf=sorted_buf.at[slot], sem=sems.at[slot], wait=True)

    route = route_ref[...]
    col = lax.broadcasted_iota(I32, (TOK_CHUNK, sorted_buf.shape[1]), 1).astype(F32)
    place = jnp.zeros(col.shape, F32)
    for k in range(TOP_K):
        place = jnp.where(col == route[:, TOP_K + k:TOP_K + k + 1], 1.0, place)
    srt = lax.dot_general(place.astype(BF16), h_ref[...].astype(BF16), (((0,), (0,)), ((), ())),
                          preferred_element_type=F32)
    sorted_buf[slot] = _pack_pairs(srt)
    runs(b, buf=sorted_buf.at[slot], sem=sems.at[slot], wait=False)

    @pl.when(b == nb - 1)
    def _():
        @pl.when(b >= 1)
        def _():
            runs(b - 1, buf=sorted_buf.at[1 - slot], sem=sems.at[1 - slot], wait=True)
        runs(b, buf=sorted_buf.at[slot], sem=sems.at[slot], wait=True)

        zero_buf[...] = jnp.zeros(zero_buf.shape, zero_buf.dtype)
        sem = sems.at[0]

        def each_tail(wait):
            def body(e, carry):
                do = taildst_ref[e]
                _run_copies(tail_ref[e],
                            lambda o, s: zero_buf.at[pl.ds(0, s)],
                            lambda o, s: xs_ref.at[pl.ds(pl.multiple_of(do + o, RUN_ALIGN), s)], sem, wait)
                return carry
            lax.fori_loop(0, n_exp, body, 0)

        def spare_block(wait):
            def body(i, carry):
                cp = pltpu.make_async_copy(zero_buf.at[pl.ds(0, MOE_ROWS)],
                                           xs_ref.at[pl.ds(pl.multiple_of(i * MOE_ROWS, MOE_ROWS), MOE_ROWS)], sem)
                cp.wait() if wait else cp.start()
                return carry
            lax.fori_loop(nu_ref[0], xs_ref.shape[0] // MOE_ROWS, body, 0)

        each_tail(False)
        spare_block(False)
        each_tail(True)
        spare_block(True)


def _dispatch(plan, n_used, route, h2, n_rows, n_exp):
    N, D = h2.shape
    C = D // 2
    grid_spec = pltpu.PrefetchScalarGridSpec(
        num_scalar_prefetch=6,
        grid=(N // TOK_CHUNK,),
        in_specs=[pl.BlockSpec((TOK_CHUNK, LANES), lambda i, *_: (i, 0)),
                  pl.BlockSpec((TOK_CHUNK, D), lambda i, *_: (i, 0))],
        out_specs=pl.BlockSpec(memory_space=pl.ANY),
        scratch_shapes=[pltpu.VMEM((2, _sorted_rows(n_exp), C), jnp.uint32),
                        pltpu.VMEM((max(TOK_CHUNK, MOE_ROWS), C), jnp.uint32),
                        pltpu.SemaphoreType.DMA((2,))],
    )
    return pl.pallas_call(
        functools.partial(_dispatch_kernel, n_exp=n_exp),
        grid_spec=grid_spec,
        out_shape=jax.ShapeDtypeStruct((n_rows, C), jnp.uint32),
        compiler_params=_params(("arbitrary",)),
        name="moe_dispatch",
    )(plan["run"], plan["src"], plan["dst"], plan["tail"], plan["tail_dst"], n_used, route, h2)


def _expert_kernel(be_ref, nu_ref, xs_ref, w1_ref, b1_ref, w2_ref, b2_ref, o_ref, w1b, w2b, *, F):
    i = pl.program_id(0)
    e = be_ref[i]
    prev = be_ref[jnp.maximum(i - 1, 0)]

    @pl.when((i == 0) | (e != prev))
    def _():
        w1b[...] = w1_ref[...].astype(BF16)
        w2b[...] = w2_ref[...].astype(BF16)

    @pl.when(i < nu_ref[0])
    def _():
        hh = _dot(_unpack_pairs(xs_ref[...]), w1b[...]) + b1_ref[...]
        g = jnp.minimum(hh[:, 0:F], SWIGLU_LIMIT)
        u = jnp.clip(hh[:, F:2 * F], -SWIGLU_LIMIT, SWIGLU_LIMIT)
        a = (g * (1.0 / (1.0 + jnp.exp(-SWIGLU_ALPHA * g)))) * (u + 1.0)
        o_ref[...] = _pack_pairs(_dot(a.astype(BF16), w2b[...]) + b2_ref[...])

    @pl.when(i >= nu_ref[0])
    def _():
        o_ref[...] = jnp.zeros(o_ref.shape, o_ref.dtype)


def _experts(block_e, n_used, xs, w1, b1, w2, b2, *, l):
    R, C = xs.shape
    L, E, D, F2 = w1.shape
    F = F2 // 2
    nblk = R // MOE_ROWS
    grid_spec = pltpu.PrefetchScalarGridSpec(
        num_scalar_prefetch=2,
        grid=(nblk,),
        in_specs=[pl.BlockSpec((MOE_ROWS, C), lambda i, be, nu: (jnp.minimum(i, nu[0] - 1), 0)),
                  pl.BlockSpec((None, None, D, F2), lambda i, be, nu: (l, be[i], 0, 0)),
                  pl.BlockSpec((None, None, 1, F2), lambda i, be, nu: (l, be[i], 0, 0)),
                  pl.BlockSpec((None, None, F, D), lambda i, be, nu: (l, be[i], 0, 0)),
                  pl.BlockSpec((None, None, 1, D), lambda i, be, nu: (l, be[i], 0, 0))],
        out_specs=pl.BlockSpec((MOE_ROWS, C), lambda i, be, nu: (i, 0)),
        scratch_shapes=[pltpu.VMEM((D, F2), BF16), pltpu.VMEM((F, D), BF16)],
    )
    return pl.pallas_call(
        functools.partial(_expert_kernel, F=F),
        grid_spec=grid_spec,
        out_shape=jax.ShapeDtypeStruct((R, C), jnp.uint32),
        compiler_params=_params(("arbitrary",)),
        name="moe_experts",
    )(block_e, n_used, xs, w1, b1.reshape(L, E, 1, F2), w2, b2.reshape(L, E, 1, D))


def _combine_kernel(run_ref, src_ref, dst_ref, yb_ref, route_ref, x1_ref, mod_ref, g_ref, o_ref,
                    sorted_buf, sems, *, D, n_exp):
    b = pl.program_id(0)
    nb = pl.num_programs(0)
    slot = b % 2
    runs = functools.partial(_block_runs, n_exp=n_exp, run_ref=run_ref, src_ref=src_ref, dst_ref=dst_ref,
                             hbm_ref=yb_ref, to_hbm=False)

    @pl.when(b == 0)
    def _():
        sorted_buf[...] = jnp.zeros(sorted_buf.shape, sorted_buf.dtype)
        runs(b, buf=sorted_buf.at[slot], sem=sems.at[slot], wait=False)

    @pl.when(b + 1 < nb)
    def _():
        runs(b + 1, buf=sorted_buf.at[1 - slot], sem=sems.at[1 - slot], wait=False)

    runs(b, buf=sorted_buf.at[slot], sem=sems.at[slot], wait=True)
    route = route_ref[...]
    col = lax.broadcasted_iota(I32, (TOK_CHUNK, sorted_buf.shape[1]), 1).astype(F32)
    gate = jnp.zeros(col.shape, F32)
    for k in range(TOP_K):
        gate = jnp.where(col == route[:, TOP_K + k:TOP_K + k + 1], route[:, 2 * TOP_K + k:2 * TOP_K + k + 1], gate)
    g_hi, g_lo = _split(gate)
    yb = _unpack_pairs(sorted_buf[slot])
    y = _dot(g_hi, yb) + _dot(g_lo, yb)
    o_ref[...] = x1_ref[...] + mod_ref[:, 5 * D:6 * D] * (_rms(y) * g_ref[...])


def _combine(plan, yb, route, x1, mod4, g3, *, n_exp, ctx_blocks, blocks_per_seq, l):
    N, D = x1.shape

    def grp(i):
        return jnp.where(i < ctx_blocks, 0, 1 + (i - ctx_blocks) // blocks_per_seq)

    grid_spec = pltpu.PrefetchScalarGridSpec(
        num_scalar_prefetch=3,
        grid=(N // TOK_CHUNK,),
        in_specs=[pl.BlockSpec(memory_space=pl.ANY),
                  pl.BlockSpec((TOK_CHUNK, LANES), lambda i, *_: (i, 0)),
                  pl.BlockSpec((TOK_CHUNK, D), lambda i, *_: (i, 0)),
                  pl.BlockSpec((None, None, 1, N_MOD * D), lambda i, *_: (l, grp(i), 0, 0)),
                  pl.BlockSpec((1, D), lambda i, *_: (0, 0))],
        out_specs=pl.BlockSpec((TOK_CHUNK, D), lambda i, *_: (i, 0)),
        scratch_shapes=[pltpu.VMEM((2, _sorted_rows(n_exp), yb.shape[1]), jnp.uint32),
                        pltpu.SemaphoreType.DMA((2,))],
    )
    return pl.pallas_call(
        functools.partial(_combine_kernel, D=D, n_exp=n_exp),
        grid_spec=grid_spec,
        out_shape=jax.ShapeDtypeStruct((N, D), F32),
        compiler_params=_params(("arbitrary",)),
        name="moe_combine",
    )(plan["run"], plan["src"], plan["dst"], yb, route, x1, mod4, g3)


def _rope_tables(Ts, tm):
    nf = DIFF_QK_DIM // 4
    inv = ROPE_BASE ** (-jnp.arange(nf, dtype=F32) / nf)
    t = jnp.arange(Ts)
    pos = jnp.stack([(t // GRID_W).astype(F32), (t % GRID_W).astype(F32)], axis=1)
    ang = pos[:, :, None] * inv[None, None, :]
    cos = jnp.repeat(jnp.cos(ang)[:, :, None, :], 2, axis=2).reshape(Ts, DIFF_QK_DIM)
    sin = jnp.sin(ang)
    sin = jnp.stack([-sin, sin], axis=2).reshape(Ts, DIFF_QK_DIM)
    reps = LANES // DIFF_QK_DIM
    cos = jnp.concatenate([jnp.ones((tm, LANES), F32), jnp.tile(cos, (1, reps))], axis=0)
    sin = jnp.concatenate([jnp.zeros((tm, LANES), F32), jnp.tile(sin, (1, reps))], axis=0)
    return cos, sin


def _block_diag(w):
    G, a, b = w.shape
    out = jnp.zeros((G * a, G * b), w.dtype)
    for g in range(G):
        out = out.at[g * a:(g + 1) * a, g * b:(g + 1) * b].set(w[g])
    return out


def _route_plan(counts, n_exp, nblk):
    cnt = counts[:, 0, :n_exp].astype(I32)
    run = (cnt + RUN_ALIGN - 1) // RUN_ALIGN * RUN_ALIGN
    src = jnp.cumsum(run, axis=1) - run
    tot = jnp.sum(run, axis=0)
    region = (tot + MOE_ROWS - 1) // MOE_ROWS * MOE_ROWS
    region_end = jnp.cumsum(region)
    region_start = region_end - region
    dst = region_start[None, :] + jnp.cumsum(run, axis=0) - run
    n_used = (region_end[-1] // MOE_ROWS).astype(I32)
    blk = jnp.arange(nblk, dtype=I32) * MOE_ROWS
    block_e = jnp.minimum(jnp.sum((blk[:, None] >= region_end[None, :]).astype(I32), axis=1), n_exp - 1)
    last = jnp.sum(jnp.where(jnp.arange(nblk) == n_used - 1, block_e, 0))
    block_e = jnp.where(jnp.arange(nblk) < n_used, block_e, last).astype(I32)
    plan = dict(run=run.reshape(-1).astype(I32), src=src.reshape(-1).astype(I32), dst=dst.reshape(-1).astype(I32),
                tail=(region - tot).astype(I32), tail_dst=(region_start + tot).astype(I32))
    return plan, block_e, n_used.reshape(1)


def kernel(x_prompt, x_sample, cache_diff_k, cache_diff_v, cache_na_k, cache_na_v, c, c_ctx, w_ada, b_ada,
           norm_gain, w_in, w_out, pool_w, pool_scale, diff_lambda, diff_subln, na_rpb, router_w, router_b,
           moe_w1, moe_b1, moe_w2, moe_b2):
    Bp, Tp, D = x_prompt.shape
    Bs, Ts, _ = x_sample.shape
    L = w_ada.shape[0]
    E = router_w.shape[-1]
    Np, Ns = Bp * Tp, Bs * Ts
    N = Np + Ns
    tm = TOK_CHUNK
    assert Np % Ts == 0 or Bs == 0, "context rows must be a whole number of latent-sequence blocks"
    assert Np % tm == 0 and Ts % tm == 0 and Ts % (NA_Q_ROWS * GRID_W) == 0
    assert Ts // GRID_W >= NA_BAND + NA_Q_ROWS
    ctx_blocks, blocks_per_seq = Np // tm, Ts // tm

    G = 16
    cvec = jnp.zeros((G, D), F32).at[0].set(c_ctx).at[1:1 + Bs].set(c)
    mod4 = _modulation(cvec, w_ada, b_ada).reshape(L, G, 1, N_MOD * D)
    tp = PROJ_ROWS
    assert Np % tp == 0 and Ts % tp == 0
    cos_t, sin_t = _rope_tables(Ts, tp)
    w_in_bf = w_in.astype(BF16)
    w_out_bf = w_out.astype(BF16)
    rw_pad = jnp.zeros((L, D, LANES), F32).at[:, :, :E].set(router_w)
    rb_pad = jnp.full((L, 1, LANES), MASKED, F32).at[:, 0, :E].set(router_b)
    ck = cache_diff_k.reshape(Bs, L, -1, DIFF_HEADS * 2 * DIFF_QK_DIM)
    cv = cache_diff_v.reshape(Bs, L, -1, DIFF_HEADS * DIFF_V_DIM)
    nk = cache_na_k.reshape(Bs, L, -1, NA_HEADS * NA_HEAD_DIM)
    nv = cache_na_v.reshape(Bs, L, -1, NA_HEADS * NA_HEAD_DIM)
    nb = N // TOK_CHUNK
    nblk = -(-(N * TOP_K + nb * E * (RUN_ALIGN - 1) + E * (MOE_ROWS - 1)) // MOE_ROWS)
    blocks = dict(ctx_blocks=ctx_blocks, blocks_per_seq=blocks_per_seq)

    x = jnp.concatenate([x_prompt.reshape(Np, D), x_sample.reshape(Ns, D)], axis=0)
    new_dk, new_dv, new_nk, new_nv = [], [], [], []
    for l in range(L):
        lam_init = 0.8 - 0.6 * math.exp(-0.3 * l)
        g = norm_gain[l]
        p = _in_projection(x, mod4, g[0:1], w_in_bf[l], cos_t, sin_t, tm=tp, l=l,
                           ctx_blocks=Np // tp, blocks_per_seq=Ts // tp)

        pw = _block_diag(pool_w[l]).astype(BF16)
        ps = pool_scale[l].reshape(1, -1)
        pool_o = jnp.zeros((N, pw.shape[0]), F32)
        pool_o = _pool(p, pw, ps, row0=0, n_seq=Bp, T=Tp, out=pool_o)
        pool_o = _pool(p, pw, ps, row0=Np, n_seq=Bs, T=Ts, out=pool_o)

        sub = diff_subln[l].reshape(1, -1)
        dn = jnp.zeros((N, DIFF_HEADS * DIFF_V_DIM), F32)
        dn = _diff_attention(p, diff_lambda[l], sub, dn, row0=0, n_seq=Bp, T=Tp, tq=Tp, lam_init=lam_init)
        dn = _diff_attention(p, diff_lambda[l], sub, dn, row0=Np, n_seq=Bs, T=Ts, tq=DIFF_Q_ROWS, lam_init=lam_init,
                             cache_k=ck, cache_v=cv, l=l)

        na_o = jnp.zeros((N, NA_HEADS * NA_HEAD_DIM), F32)
        na_o = _dense_attention(p, na_o, n_seq=Bp, T=Tp)
        bias = _na_bias_table(na_rpb[l], Ts // GRID_W)
        na_o = _neighbourhood_attention(p, nk, nv, bias, na_o, row0=Np, n_seq=Bs, T=Ts, l=l)

        x1, h2, route, counts = _merge_route(pool_o, dn, na_o, x, mod4, g[1:2], g[2:3], w_out_bf[l],
                                             rw_pad[l], rb_pad[l], tm=tm, l=l, **blocks)
        plan, block_e, n_used = _route_plan(counts, E, nblk)
        xs = _dispatch(plan, n_used, route, h2, nblk * MOE_ROWS, E)
        yb = _experts(block_e, n_used, xs, moe_w1, moe_b1, moe_w2, moe_b2, l=l)
        x = _combine(plan, yb, route, x1, mod4, g[3:4], n_exp=E, l=l, **blocks)

        pc = p[:Np]
        new_dk.append(pc[:, 768:1280].reshape(Bp, Tp, DIFF_HEADS, 2 * DIFF_QK_DIM))
        new_dv.append(pc[:, 1280:1792].reshape(Bp, Tp, DIFF_HEADS, DIFF_V_DIM))
        new_nk.append(pc[:, 2048:2304].reshape(Bp, Tp, NA_HEADS, NA_HEAD_DIM))
        new_nv.append(pc[:, 2304:2560].reshape(Bp, Tp, NA_HEADS, NA_HEAD_DIM))

    return (x[:Np].reshape(Bp, Tp, D), x[Np:].reshape(Bs, Ts, D),
            jnp.stack(new_dk, axis=1), jnp.stack(new_dv, axis=1),
            jnp.stack(new_nk, axis=1), jnp.stack(new_nv, axis=1))
```

```python
import functools
import math

import numpy as np
import jax
import jax.numpy as jnp
from jax import lax
from jax.experimental import pallas as pl
from jax.experimental.pallas import tpu as pltpu

F32 = jnp.float32
BF16 = jnp.bfloat16
I32 = jnp.int32

GRID_W = 64
POOL_GROUPS = 4
POOL_MAX_HALF = 8
DIFF_HEADS = 4
DIFF_V_DIM = 128
DIFF_QK_DIM = 64
NA_HEADS = 4
NA_HEAD_DIM = 64
NA_WIN_H = 8
NA_WIN_W = 16
NA_Q_ROWS = 4
NA_BAND = 12
TOP_K = 4
SWIGLU_ALPHA = 1.702
SWIGLU_LIMIT = 7.0
ROPE_BASE = 10000.0
NORM_EPS = 1e-6
N_MOD = 6

LOG2E = 1.4426950408889634
MASKED = -1e30
LANES = 128
MXU_TILE = 256
MOE_ROWS = 512
COL_DQ, COL_DK, COL_DV = 256, 768, 1280
COL_NQ, COL_NK, COL_NV = 1792, 2048, 2304
DIFF_KEYS = 256
PROJ_ROWS = 512
DIFF_Q_ROWS = 256
TOK_CHUNK = 256
RUN_ALIGN = 8
RUN_LARGE = 64
VMEM_LIMIT = 56 * 1024 * 1024


def _sorted_rows(n_exp):
    rows = TOK_CHUNK * TOP_K + n_exp * (RUN_ALIGN - 1)
    return -(-rows // MXU_TILE) * MXU_TILE


def _params(sem, vmem=VMEM_LIMIT):
    return pltpu.CompilerParams(dimension_semantics=sem, vmem_limit_bytes=vmem)


def _dot(a, b):
    return jnp.dot(a, b, preferred_element_type=F32)


def _dot_nt(a, b):
    return lax.dot_general(a, b, (((1,), (1,)), ((), ())), preferred_element_type=F32)


def _split(x):
    hi = x.astype(BF16)
    return hi, (x - hi.astype(F32)).astype(BF16)


def _dot3(a, b):
    ah, al = _split(a)
    bh, bl = _split(b)
    return _dot(ah, bh) + _dot(al, bh) + _dot(ah, bl)


def _rms(x):
    return x * lax.rsqrt(jnp.mean(x * x, axis=-1, keepdims=True) + NORM_EPS)


def _mod_kernel(c_ref, w_ref, b_ref, o_ref):
    c = c_ref[...]
    a = c * (1.0 / (1.0 + jnp.exp(-c)))
    o_ref[...] = _dot3(a, w_ref[...]) + b_ref[...]


def _modulation(cvec, w_ada, b_ada):
    L, D, W = w_ada.shape
    G = cvec.shape[0]
    return pl.pallas_call(
        _mod_kernel,
        grid=(L, W // D),
        in_specs=[pl.BlockSpec((G, D), lambda l, j: (0, 0)),
                  pl.BlockSpec((None, D, D), lambda l, j: (l, 0, j)),
                  pl.BlockSpec((None, 1, D), lambda l, j: (l, 0, j))],
        out_specs=pl.BlockSpec((None, G, D), lambda l, j: (l, 0, j)),
        out_shape=jax.ShapeDtypeStruct((L, G, W), F32),
        compiler_params=_params(("arbitrary", "arbitrary")),
        name="ada_modulation",
    )(cvec, w_ada, b_ada.reshape(L, 1, W))


def _inproj_kernel(x_ref, mod_ref, g_ref, w_ref, cos_ref, sin_ref, o_ref, *, D, rope_lo, rope_hi):
    h = _rms(x_ref[...]) * g_ref[...]
    h = h * (1.0 + mod_ref[:, D:2 * D]) + mod_ref[:, 0:D]
    p = _dot(h.astype(BF16), w_ref[...])
    W = p.shape[1]
    o_ref[:, 0:rope_lo] = p[:, 0:rope_lo]
    o_ref[:, rope_hi:W] = p[:, rope_hi:W]
    cos = cos_ref[...]
    sin = sin_ref[...]
    lane = lax.broadcasted_iota(I32, cos.shape, 1)
    first = (lane % 32) < 16
    for c0 in range(rope_lo, rope_hi, LANES):
        xc = p[:, c0:c0 + LANES]
        partner = jnp.where(first, pltpu.roll(xc, LANES - 16, 1), pltpu.roll(xc, 16, 1))
        o_ref[:, c0:c0 + LANES] = xc * cos + partner * sin


def _in_projection(x, mod4, gain, w_bf, cos_t, sin_t, *, tm, ctx_blocks, blocks_per_seq, l):
    N, D = x.shape
    W = w_bf.shape[1]

    def grp(i):
        return jnp.where(i < ctx_blocks, 0, 1 + (i - ctx_blocks) // blocks_per_seq)

    def rope_blk(i):
        return jnp.where(i < ctx_blocks, 0, 1 + (i - ctx_blocks) % blocks_per_seq)

    kern = functools.partial(_inproj_kernel, D=D, rope_lo=256, rope_hi=256 + 2 * DIFF_HEADS * 2 * DIFF_QK_DIM)
    return pl.pallas_call(
        kern,
        grid=(N // tm,),
        in_specs=[pl.BlockSpec((tm, D), lambda i: (i, 0)),
                  pl.BlockSpec((None, None, 1, N_MOD * D), lambda i: (l, grp(i), 0, 0)),
                  pl.BlockSpec((1, D), lambda i: (0, 0)),
                  pl.BlockSpec((D, W), lambda i: (0, 0)),
                  pl.BlockSpec((tm, LANES), lambda i: (rope_blk(i), 0)),
                  pl.BlockSpec((tm, LANES), lambda i: (rope_blk(i), 0))],
        out_specs=pl.BlockSpec((tm, W), lambda i: (i, 0)),
        out_shape=jax.ShapeDtypeStruct((N, W), F32),
        compiler_params=_params(("arbitrary",)),
        name="in_projection",
    )(x, mod4, gain, w_bf, cos_t, sin_t)


def _pool_kernel(u_ref, w_ref, sc_ref, o_ref, pad_ref, *, T, CH):
    H = 2 * POOL_MAX_HALF
    zeros = jnp.zeros((H, pad_ref.shape[1]), F32)
    pad_ref[0:H, :] = zeros
    pad_ref[H + T:2 * H + T, :] = zeros
    pad_ref[H:H + T, :] = u_ref[...]
    C = pad_ref.shape[1]
    lane = lax.broadcasted_iota(I32, (CH, C), 1)
    half = jnp.left_shift(1, lane // (C // POOL_GROUPS))
    row = lax.broadcasted_iota(I32, (CH, C), 0)

    def body(ci, carry):
        base = pl.multiple_of(ci * CH, CH)
        win = pad_ref[pl.ds(base + POOL_MAX_HALF, CH + H), :]
        acc = jnp.zeros((CH, C), F32)
        for j in range(-POOL_MAX_HALF, POOL_MAX_HALF):
            sl = win[POOL_MAX_HALF + j:POOL_MAX_HALF + j + CH, :]
            inside = (half > j) if j >= 0 else (half >= -j)
            acc = acc + jnp.where(inside, sl, 0.0)
        t = row + base
        cnt = jnp.minimum(t + half, T) - jnp.maximum(t - half, 0)
        d = acc / cnt.astype(F32) - win[POOL_MAX_HALF:POOL_MAX_HALF + CH, :]
        o_ref[pl.ds(base, CH), :] = _dot(d.astype(BF16), w_ref[...]) * sc_ref[...]
        return carry

    lax.fori_loop(0, T // CH, body, 0)


def _pool(p, w_bd, scale, *, row0, n_seq, T, out):
    C = w_bd.shape[0]
    CH = min(T, 256)
    blk0 = row0 // T

    def kern(u_ref, w_ref, sc_ref, prev_ref, o_ref, pad_ref):
        del prev_ref
        _pool_kernel(u_ref, w_ref, sc_ref, o_ref, pad_ref, T=T, CH=CH)

    return pl.pallas_call(
        kern,
        grid=(n_seq,),
        in_specs=[pl.BlockSpec((T, C), lambda s: (blk0 + s, 0)),
                  pl.BlockSpec((C, C), lambda s: (0, 0)),
                  pl.BlockSpec((1, C), lambda s: (0, 0)),
                  pl.BlockSpec(memory_space=pl.ANY)],
        out_specs=pl.BlockSpec((T, C), lambda s: (blk0 + s, 0)),
        out_shape=jax.ShapeDtypeStruct(out.shape, F32),
        scratch_shapes=[pltpu.VMEM((T + 4 * POOL_MAX_HALF, C), F32)],
        input_output_aliases={3: 0},
        compiler_params=_params(("arbitrary",)),
        name="pool_mixer",
    )(p, w_bd, scale, out)


def _softmax_pv(s, v):
    m = jnp.max(s, axis=-1, keepdims=True)
    e = jnp.exp2(s - m)
    l = jnp.sum(e, axis=-1, keepdims=True)
    return _dot(e.astype(BF16), v) * (1.0 / l)


def _diff_kernel(*refs, Ts, Lc, lam_init):
    if Lc:
        lam_ref, sub_ref, q_ref, ks_ref, vs_ref, kc_ref, vc_ref, prev_ref, o_ref, kb, vb, s_a, s_b, m_a, m_b = refs
    else:
        lam_ref, sub_ref, q_ref, ks_ref, vs_ref, prev_ref, o_ref, kb, vb, s_a, s_b, m_a, m_b = refs
    del prev_ref
    i = pl.program_id(2)
    S = Ts + Lc
    dv = DIFF_V_DIM

    @pl.when(i == 0)
    def _():
        kb[0:Ts, :] = ks_ref[...].astype(BF16)
        vb[0:Ts, 0:dv] = vs_ref[...].astype(BF16)
        if Lc:
            kb[Ts:S, :] = kc_ref[...].astype(BF16)
            vb[Ts:S, 0:dv] = vc_ref[...].astype(BF16)
        vb[:, dv:2 * dv] = jnp.ones((S, dv), BF16)
        s_b[...] = jnp.zeros(s_b.shape, F32)
        m_b[...] = jnp.zeros(m_b.shape, F32)

    lm = lam_ref[...]
    lam = (jnp.exp(jnp.sum(lm[0:1, :] * lm[1:2, :], axis=-1, keepdims=True))
           - jnp.exp(jnp.sum(lm[2:3, :] * lm[3:4, :], axis=-1, keepdims=True)) + lam_init)

    def stage(s_new, m_new, s_old, m_old):
        q = q_ref[...] * (DIFF_QK_DIM ** -0.5 * LOG2E)
        part = lax.broadcasted_iota(I32, q.shape, 1) // DIFF_QK_DIM
        qm = [jnp.where(part == u, q, 0.0).astype(BF16) for u in range(2)]
        top = [m_old[u][:, 0:1] for u in range(2)]
        acc = [jnp.zeros((q.shape[0], 2 * dv), F32) for _ in range(2)]
        run = [jnp.full((q.shape[0], LANES), -jnp.inf, F32) for _ in range(2)]
        for c0 in range(0, S, DIFF_KEYS):
            keys = slice(c0, c0 + DIFF_KEYS)
            for u in range(2):
                sc = _dot_nt(qm[u], kb[keys, :])
                s_new[u, :, keys] = sc
                for j in range(0, DIFF_KEYS, LANES):
                    run[u] = jnp.maximum(run[u], sc[:, j:j + LANES])
                e = jnp.exp2(s_old[u, :, keys] - top[u])
                acc[u] = acc[u] + _dot(e.astype(BF16), vb[keys, :])
        for u in range(2):
            m_new[u] = jnp.broadcast_to(jnp.max(run[u], axis=-1, keepdims=True), m_new.shape[1:])
        o = (acc[0][:, 0:dv] * (1.0 / acc[0][:, dv:dv + 1])
             - lam * (acc[1][:, 0:dv] * (1.0 / acc[1][:, dv:dv + 1])))
        o_ref[...] = _rms(o) * sub_ref[...] * (1.0 - lam_init)

    @pl.when(i % 2 == 0)
    def _():
        stage(s_a, m_a, s_b, m_b)

    @pl.when(i % 2 == 1)
    def _():
        stage(s_b, m_b, s_a, m_a)


def _diff_attention(p, lam_l, subln, out, *, row0, n_seq, T, tq, lam_init, cache_k=None, cache_v=None, l=0):
    W = DIFF_V_DIM
    assert 2 * DIFF_QK_DIM == W == LANES
    Lc = 0 if cache_k is None else cache_k.shape[2]
    S = T + Lc
    assert S % DIFF_KEYS == 0
    nq = T // tq
    qb0 = row0 // tq
    sb0 = row0 // T
    in_specs = [pl.BlockSpec((4, DIFF_QK_DIM), lambda b, h, i: (0, 0)),
                pl.BlockSpec((1, W), lambda b, h, i: (0, 0)),
                pl.BlockSpec((tq, W), lambda b, h, i: (qb0 + b * nq + jnp.minimum(i, nq - 1), COL_DQ // W + h)),
                pl.BlockSpec((T, W), lambda b, h, i: (sb0 + b, COL_DK // W + h)),
                pl.BlockSpec((T, W), lambda b, h, i: (sb0 + b, COL_DV // W + h))]
    args = [lam_l, subln, p, p, p]
    if Lc:
        in_specs += [pl.BlockSpec((None, None, Lc, W), lambda b, h, i: (b, l, 0, h)),
                     pl.BlockSpec((None, None, Lc, W), lambda b, h, i: (b, l, 0, h))]
        args += [cache_k, cache_v]
    in_specs.append(pl.BlockSpec(memory_space=pl.ANY))
    args.append(out)
    scores = pltpu.VMEM((2, tq, S), F32)
    row_max = pltpu.VMEM((2, tq, LANES), F32)
    return pl.pallas_call(
        functools.partial(_diff_kernel, Ts=T, Lc=Lc, lam_init=lam_init),
        grid=(n_seq, DIFF_HEADS, nq + 1),
        in_specs=in_specs,
        out_specs=pl.BlockSpec((tq, W), lambda b, h, i: (qb0 + b * nq + jnp.maximum(i - 1, 0), h)),
        out_shape=jax.ShapeDtypeStruct(out.shape, F32),
        scratch_shapes=[pltpu.VMEM((S, W), BF16), pltpu.VMEM((S, 2 * W), BF16), scores, scores, row_max, row_max],
        input_output_aliases={len(args) - 1: 0},
        compiler_params=_params(("arbitrary", "arbitrary", "arbitrary")),
        name="diff_attention",
    )(*args)


def _na_heads(q, score_fn, pv_fn):
    lane = lax.broadcasted_iota(I32, q.shape, 1) // NA_HEAD_DIM
    out = jnp.zeros(q.shape, F32)
    for h in range(NA_HEADS):
        qh = jnp.where(lane == h, q, 0.0).astype(BF16)
        out = jnp.where(lane == h, pv_fn(score_fn(qh, h)), out)
    return out


def _dense_kernel(q_ref, k_ref, v_ref, prev_ref, o_ref):
    del prev_ref
    q = q_ref[...] * (NA_HEAD_DIM ** -0.5 * LOG2E)
    k = k_ref[...].astype(BF16)
    v = v_ref[...].astype(BF16)
    o_ref[...] = _na_heads(q, lambda qh, h: _dot_nt(qh, k), lambda s: _softmax_pv(s, v))


def _dense_attention(p, out, *, n_seq, T):
    C = NA_HEADS * NA_HEAD_DIM
    return pl.pallas_call(
        _dense_kernel,
        grid=(n_seq,),
        in_specs=[pl.BlockSpec((T, C), lambda b: (b, 7)),
                  pl.BlockSpec((T, C), lambda b: (b, 8)),
                  pl.BlockSpec((T, C), lambda b: (b, 9)),
                  pl.BlockSpec(memory_space=pl.ANY)],
        out_specs=pl.BlockSpec((T, C), lambda b: (b, 0)),
        out_shape=jax.ShapeDtypeStruct(out.shape, F32),
        input_output_aliases={3: 0},
        compiler_params=_params(("arbitrary",)),
        name="dense_attention",
    )(p, p, p, out)


def _na_kernel(q_ref, ks_ref, vs_ref, kc_ref, vc_ref, bias_ref, prev_ref, o_ref, *, rows):
    del prev_ref
    r0 = pl.program_id(1) * NA_Q_ROWS
    bs = jnp.clip(r0 - NA_WIN_H // 2, 0, rows - NA_BAND)
    start = pl.multiple_of(bs * GRID_W, GRID_W)
    nb = NA_BAND * GRID_W
    kb = ks_ref[pl.ds(start, nb), :].astype(BF16)
    vb = vs_ref[pl.ds(start, nb), :].astype(BF16)
    kc = kc_ref[...].astype(BF16)
    vc = vc_ref[...].astype(BF16)
    q = q_ref[...] * (NA_HEAD_DIM ** -0.5 * LOG2E)

    def scores(qh, h):
        return _dot_nt(qh, kb) + bias_ref[h], _dot_nt(qh, kc)

    def pv(s):
        s_loc, s_ctx = s
        m = jnp.maximum(jnp.max(s_loc, axis=-1, keepdims=True), jnp.max(s_ctx, axis=-1, keepdims=True))
        e_loc = jnp.exp2(s_loc - m)
        e_ctx = jnp.exp2(s_ctx - m)
        l = jnp.sum(e_loc, axis=-1, keepdims=True) + jnp.sum(e_ctx, axis=-1, keepdims=True)
        return (_dot(e_loc.astype(BF16), vb) + _dot(e_ctx.astype(BF16), vc)) * (1.0 / l)

    o_ref[...] = _na_heads(q, scores, pv)


def _na_bias_table(rpb_l, rows):
    n_ro, n_co = 2 * NA_WIN_H - 1, 2 * NA_WIN_W - 1
    c = np.arange(GRID_W)[:, None]
    kc = np.arange(GRID_W)[None, :]
    cs = np.clip(c - NA_WIN_W // 2, 0, GRID_W - NA_WIN_W)
    col_ok = (kc >= cs) & (kc < cs + NA_WIN_W)
    co = kc - c + (NA_WIN_W - 1)
    pick = ((np.arange(n_co)[:, None, None] == co[None]) & col_ok[None]).astype(np.float32)
    toep = jnp.dot(rpb_l.reshape(NA_HEADS * n_ro, n_co).astype(F32), jnp.asarray(pick.reshape(n_co, -1)),
                   precision=lax.Precision.HIGHEST).reshape(NA_HEADS, n_ro, GRID_W, GRID_W) * LOG2E
    toep = jnp.where(jnp.asarray(col_ok)[None, None], toep, MASKED)
    toep = jnp.concatenate([toep, jnp.full((NA_HEADS, 1, GRID_W, GRID_W), MASKED, F32)], axis=1)
    blk = np.full((3, NA_Q_ROWS, NA_BAND), n_ro, np.int32)
    for v, r0 in enumerate((0, NA_Q_ROWS, rows - NA_Q_ROWS)):
        bs = int(np.clip(r0 - NA_WIN_H // 2, 0, rows - NA_BAND))
        for j in range(NA_Q_ROWS):
            rs = int(np.clip(r0 + j - NA_WIN_H // 2, 0, rows - NA_WIN_H))
            for i in range(NA_BAND):
                if rs <= bs + i < rs + NA_WIN_H:
                    blk[v, j, i] = bs + i - (r0 + j) + (NA_WIN_H - 1)
    tab = toep[:, blk]
    tab = jnp.transpose(tab, (1, 0, 2, 4, 3, 5))
    return tab.reshape(3, NA_HEADS, NA_Q_ROWS * GRID_W, NA_BAND * GRID_W)


def _neighbourhood_attention(p, cache_k, cache_v, bias, out, *, row0, n_seq, T, l):
    C = NA_HEADS * NA_HEAD_DIM
    rows = T // GRID_W
    tq = NA_Q_ROWS * GRID_W
    nq = T // tq
    Lc = cache_k.shape[2]
    qb0 = row0 // tq
    sb0 = row0 // T

    def variant(i):
        r0 = i * NA_Q_ROWS
        return (r0 - jnp.clip(r0 - NA_WIN_H // 2, 0, rows - NA_BAND)) // NA_Q_ROWS

    return pl.pallas_call(
        functools.partial(_na_kernel, rows=rows),
        grid=(n_seq, nq),
        in_specs=[pl.BlockSpec((tq, C), lambda b, i: (qb0 + b * nq + i, 7)),
                  pl.BlockSpec((T, C), lambda b, i: (sb0 + b, 8)),
                  pl.BlockSpec((T, C), lambda b, i: (sb0 + b, 9)),
                  pl.BlockSpec((None, None, Lc, C), lambda b, i: (b, l, 0, 0)),
                  pl.BlockSpec((None, None, Lc, C), lambda b, i: (b, l, 0, 0)),
                  pl.BlockSpec((None, NA_HEADS, tq, NA_BAND * GRID_W), lambda b, i: (variant(i), 0, 0, 0)),
                  pl.BlockSpec(memory_space=pl.ANY)],
        out_specs=pl.BlockSpec((tq, C), lambda b, i: (qb0 + b * nq + i, 0)),
        out_shape=jax.ShapeDtypeStruct(out.shape, F32),
        input_output_aliases={6: 0},
        compiler_params=_params(("arbitrary", "arbitrary")),
        name="neighbourhood_attention",
    )(p, p, p, cache_k, cache_v, bias, out)


def _merge_kernel(pool_ref, dn_ref, na_ref, x_ref, mod_ref, g1_ref, g2_ref, w_ref, rw_ref, rb_ref,
                  x1_ref, h2_ref, route_ref, cnt_ref, *, D):
    c0 = pool_ref.shape[1]
    c1 = c0 + dn_ref.shape[1]
    mix = (_dot(pool_ref[...].astype(BF16), w_ref[0:c0, :])
           + _dot(dn_ref[...].astype(BF16), w_ref[c0:c1, :])
           + _dot(na_ref[...].astype(BF16), w_ref[c1:, :]))
    x1 = x_ref[...] + mod_ref[:, 2 * D:3 * D] * (_rms(mix) * g1_ref[...])
    x1_ref[...] = x1
    h2 = (_rms(x1) * g2_ref[...]) * (1.0 + mod_ref[:, 4 * D:5 * D]) + mod_ref[:, 3 * D:4 * D]
    h2_ref[...] = h2

    logits = _dot3(h2, rw_ref[...]) + rb_ref[...]
    tm = logits.shape[0]
    lane = lax.broadcasted_iota(I32, (tm, LANES), 1)
    lane_f = lane.astype(F32)
    work = logits
    vals, hots = [], []
    for _ in range(TOP_K):
        mx = jnp.max(work, axis=-1, keepdims=True)
        idx = jnp.min(jnp.where(work == mx, lane_f, float(LANES)), axis=-1, keepdims=True)
        hot = lane_f == idx
        vals.append(mx)
        hots.append(hot)
        work = jnp.where(hot, -jnp.inf, work)
    es = [jnp.exp(v - vals[0]) for v in vals]
    inv = 1.0 / (es[0] + es[1] + es[2] + es[3])

    sel = jnp.zeros((tm, LANES), F32)
    for hot in hots:
        sel = jnp.where(hot, 1.0, sel)
    r = lax.broadcasted_iota(I32, (TOK_CHUNK, TOK_CHUNK), 0)
    c = lax.broadcasted_iota(I32, (TOK_CHUNK, TOK_CHUNK), 1)
    before = jnp.where(c < r, 1.0, 0.0).astype(BF16)
    er = lax.broadcasted_iota(I32, (LANES, LANES), 0)
    ec = lax.broadcasted_iota(I32, (LANES, LANES), 1)
    earlier = jnp.where(er < ec, 1.0, 0.0).astype(BF16)
    pos = []
    for j in range(tm // TOK_CHUNK):
        sel_j = sel[j * TOK_CHUNK:(j + 1) * TOK_CHUNK, :]
        rank = _dot(before, sel_j.astype(BF16))
        cnt = jnp.sum(sel_j, axis=0, keepdims=True)
        cnt_ref[j] = cnt
        run = jnp.floor((cnt + (RUN_ALIGN - 1)) * (1.0 / RUN_ALIGN)) * RUN_ALIGN
        run_start = _dot(jnp.broadcast_to(run, (8, LANES)).astype(BF16), earlier)[0:1, :]
        pos.append(rank + run_start)
    pos = jnp.concatenate(pos, axis=0)

    route = jnp.zeros((tm, LANES), F32)
    for k in range(TOP_K):
        e_k = jnp.sum(jnp.where(hots[k], lane_f, 0.0), axis=-1, keepdims=True)
        p_k = jnp.sum(jnp.where(hots[k], pos, 0.0), axis=-1, keepdims=True)
        route = jnp.where(lane == k, e_k, route)
        route = jnp.where(lane == TOP_K + k, p_k, route)
        route = jnp.where(lane == 2 * TOP_K + k, es[k] * inv, route)
    route_ref[...] = route


def _merge_route(pool_o, dn, na_o, x, mod4, g1, g2, w_out_bf, rw_pad, rb_pad, *, tm, ctx_blocks,
                 blocks_per_seq, l):
    N, D = x.shape
    per_step = tm // TOK_CHUNK

    def grp(i):
        return jnp.where(i < ctx_blocks, 0, 1 + (i - ctx_blocks) // blocks_per_seq)

    row = lambda w: pl.BlockSpec((tm, w), lambda i: (i, 0))
    full = lambda a: pl.BlockSpec(a.shape, lambda i: (0,) * a.ndim)
    return pl.pallas_call(
        functools.partial(_merge_kernel, D=D),
        grid=(N // tm,),
        in_specs=[row(pool_o.shape[1]), row(dn.shape[1]), row(na_o.shape[1]), row(D),
                  pl.BlockSpec((None, None, 1, N_MOD * D), lambda i: (l, grp(i), 0, 0)),
                  full(g1), full(g2), full(w_out_bf), full(rw_pad), full(rb_pad)],
        out_specs=[row(D), row(D), row(LANES), pl.BlockSpec((per_step, 1, LANES), lambda i: (i, 0, 0))],
        out_shape=[jax.ShapeDtypeStruct((N, D), F32), jax.ShapeDtypeStruct((N, D), F32),
                   jax.ShapeDtypeStruct((N, LANES), F32), jax.ShapeDtypeStruct((N // TOK_CHUNK, 1, LANES), F32)],
        compiler_params=_params(("arbitrary",)),
        name="merge_route",
    )(pool_o, dn, na_o, x, mod4, g1, g2, w_out_bf, rw_pad, rb_pad)


def _pack_pairs(x):
    C = x.shape[1] // 2
    bits = lax.bitcast_convert_type(x.astype(BF16).astype(F32), jnp.uint32)
    return bits[:, C:] | (bits[:, :C] >> 16)


def _unpack_pairs(w):
    lo = lax.bitcast_convert_type(w << 16, F32)
    hi = lax.bitcast_convert_type(w & jnp.uint32(0xFFFF0000), F32)
    return jnp.concatenate([lo, hi], axis=1).astype(BF16)


def _run_copies(n, src_at, dst_at, sem, wait):
    def pieces(sizes, done):
        for size in sizes:
            bit = n & size

            @pl.when(bit != 0)
            def _(size=size, done=done):
                cp = pltpu.make_async_copy(src_at(done, size), dst_at(done, size), sem)
                cp.wait() if wait else cp.start()

            done = done + bit

    sizes = [TOK_CHUNK >> s for s in range(TOK_CHUNK.bit_length()) if TOK_CHUNK >> s >= RUN_ALIGN]
    large = [s for s in sizes if s >= RUN_LARGE]

    @pl.when(n >= RUN_LARGE)
    def _():
        pieces(large, jnp.int32(0))

    pieces([s for s in sizes if s < RUN_LARGE], n & -RUN_LARGE)


def _block_runs(step, n_exp, run_ref, src_ref, dst_ref, hbm_ref, buf, sem, *, to_hbm, wait):
    def body(e, carry):
        j = step * n_exp + e
        so = src_ref[j]
        do = dst_ref[j]
        in_buf = lambda o, s: buf.at[pl.ds(pl.multiple_of(so + o, RUN_ALIGN), s)]
        in_hbm = lambda o, s: hbm_ref.at[pl.ds(pl.multiple_of(do + o, RUN_ALIGN), s)]
        if to_hbm:
            _run_copies(run_ref[j], in_buf, in_hbm, sem, wait)
        else:
            _run_copies(run_ref[j], in_hbm, in_buf, sem, wait)
        return carry
    lax.fori_loop(0, n_exp, body, 0)


def _dispatch_kernel(run_ref, src_ref, dst_ref, tail_ref, taildst_ref, nu_ref, route_ref, h_ref, xs_ref,
                     sorted_buf, zero_buf, sems, *, n_exp):
    b = pl.program_id(0)
    nb = pl.num_programs(0)
    slot = b % 2
    runs = functools.partial(_block_runs, n_exp=n_exp, run_ref=run_ref, src_ref=src_ref, dst_ref=dst_ref,
                             hbm_ref=xs_ref, to_hbm=True)

    @pl.when(b >= 2)
    def _():
        runs(b - 2, buf=sorted_buf.at[slot], sem=sems.at[slot], wait=True)

    route = route_ref[...]
    col = lax.broadcasted_iota(I32, (TOK_CHUNK, sorted_buf.shape[1]), 1).astype(F32)
    place = jnp.zeros(col.shape, F32)
    for k in range(TOP_K):
        place = jnp.where(col == route[:, TOP_K + k:TOP_K + k + 1], 1.0, place)
    srt = lax.dot_general(place.astype(BF16), h_ref[...].astype(BF16), (((0,), (0,)), ((), ())),
                          preferred_element_type=F32)
    sorted_buf[slot] = _pack_pairs(srt)
    runs(b, buf=sorted_buf.at[slot], sem=sems.at[slot], wait=False)

    @pl.when(b == nb - 1)
    def _():
        @pl.when(b >= 1)
        def _():
            runs(b - 1, buf=sorted_buf.at[1 - slot], sem=sems.at[1 - slot], wait=True)
        runs(b, buf=sorted_buf.at[slot], sem=sems.at[slot], wait=True)

        zero_buf[...] = jnp.zeros(zero_buf.shape, zero_buf.dtype)
        sem = sems.at[0]

        def each_tail(wait):
            def body(e, carry):
                do = taildst_ref[e]
                _run_copies(tail_ref[e],
                            lambda o, s: zero_buf.at[pl.ds(0, s)],
                            lambda o, s: xs_ref.at[pl.ds(pl.multiple_of(do + o, RUN_ALIGN), s)], sem, wait)
                return carry
            lax.fori_loop(0, n_exp, body, 0)

        def spare_block(wait):
            def body(i, carry):
                cp = pltpu.make_async_copy(zero_buf.at[pl.ds(0, MOE_ROWS)],
                                           xs_ref.at[pl.ds(pl.multiple_of(i * MOE_ROWS, MOE_ROWS), MOE_ROWS)], sem)
                cp.wait() if wait else cp.start()
                return carry
            lax.fori_loop(nu_ref[0], xs_ref.shape[0] // MOE_ROWS, body, 0)

        each_tail(False)
        spare_block(False)
        each_tail(True)
        spare_block(True)


def _dispatch(plan, n_used, route, h2, n_rows, n_exp):
    N, D = h2.shape
    C = D // 2
    grid_spec = pltpu.PrefetchScalarGridSpec(
        num_scalar_prefetch=6,
        grid=(N // TOK_CHUNK,),
        in_specs=[pl.BlockSpec((TOK_CHUNK, LANES), lambda i, *_: (i, 0)),
                  pl.BlockSpec((TOK_CHUNK, D), lambda i, *_: (i, 0))],
        out_specs=pl.BlockSpec(memory_space=pl.ANY),
        scratch_shapes=[pltpu.VMEM((2, _sorted_rows(n_exp), C), jnp.uint32),
                        pltpu.VMEM((max(TOK_CHUNK, MOE_ROWS), C), jnp.uint32),
                        pltpu.SemaphoreType.DMA((2,))],
    )
    return pl.pallas_call(
        functools.partial(_dispatch_kernel, n_exp=n_exp),
        grid_spec=grid_spec,
        out_shape=jax.ShapeDtypeStruct((n_rows, C), jnp.uint32),
        compiler_params=_params(("arbitrary",)),
        name="moe_dispatch",
    )(plan["run"], plan["src"], plan["dst"], plan["tail"], plan["tail_dst"], n_used, route, h2)


def _expert_kernel(be_ref, nu_ref, xs_ref, w1_ref, b1_ref, w2_ref, b2_ref, o_ref, w1b, w2b, *, F):
    i = pl.program_id(0)
    e = be_ref[i]
    prev = be_ref[jnp.maximum(i - 1, 0)]

    @pl.when((i == 0) | (e != prev))
    def _():
        w1b[...] = w1_ref[...].astype(BF16)
        w2b[...] = w2_ref[...].astype(BF16)

    @pl.when(i < nu_ref[0])
    def _():
        hh = _dot(_unpack_pairs(xs_ref[...]), w1b[...]) + b1_ref[...]
        g = jnp.minimum(hh[:, 0:F], SWIGLU_LIMIT)
        u = jnp.clip(hh[:, F:2 * F], -SWIGLU_LIMIT, SWIGLU_LIMIT)
        a = (g * (1.0 / (1.0 + jnp.exp(-SWIGLU_ALPHA * g)))) * (u + 1.0)
        o_ref[...] = _pack_pairs(_dot(a.astype(BF16), w2b[...]) + b2_ref[...])

    @pl.when(i >= nu_ref[0])
    def _():
        o_ref[...] = jnp.zeros(o_ref.shape, o_ref.dtype)


def _experts(block_e, n_used, xs, w1, b1, w2, b2, *, l):
    R, C = xs.shape
    L, E, D, F2 = w1.shape
    F = F2 // 2
    nblk = R // MOE_ROWS
    grid_spec = pltpu.PrefetchScalarGridSpec(
        num_scalar_prefetch=2,
        grid=(nblk,),
        in_specs=[pl.BlockSpec((MOE_ROWS, C), lambda i, be, nu: (jnp.minimum(i, nu[0] - 1), 0)),
                  pl.BlockSpec((None, None, D, F2), lambda i, be, nu: (l, be[i], 0, 0)),
                  pl.BlockSpec((None, None, 1, F2), lambda i, be, nu: (l, be[i], 0, 0)),
                  pl.BlockSpec((None, None, F, D), lambda i, be, nu: (l, be[i], 0, 0)),
                  pl.BlockSpec((None, None, 1, D), lambda i, be, nu: (l, be[i], 0, 0))],
        out_specs=pl.BlockSpec((MOE_ROWS, C), lambda i, be, nu: (i, 0)),
        scratch_shapes=[pltpu.VMEM((D, F2), BF16), pltpu.VMEM((F, D), BF16)],
    )
    return pl.pallas_call(
        functools.partial(_expert_kernel, F=F),
        grid_spec=grid_spec,
        out_shape=jax.ShapeDtypeStruct((R, C), jnp.uint32),
        compiler_params=_params(("arbitrary",)),
        name="moe_experts",
    )(block_e, n_used, xs, w1, b1.reshape(L, E, 1, F2), w2, b2.reshape(L, E, 1, D))


def _combine_kernel(run_ref, src_ref, dst_ref, yb_ref, route_ref, x1_ref, mod_ref, g_ref, o_ref,
                    sorted_buf, sems, *, D, n_exp):
    b = pl.program_id(0)
    nb = pl.num_programs(0)
    slot = b % 2
    runs = functools.partial(_block_runs, n_exp=n_exp, run_ref=run_ref, src_ref=src_ref, dst_ref=dst_ref,
                             hbm_ref=yb_ref, to_hbm=False)

    @pl.when(b == 0)
    def _():
        sorted_buf[...] = jnp.zeros(sorted_buf.shape, sorted_buf.dtype)
        runs(b, buf=sorted_buf.at[slot], sem=sems.at[slot], wait=False)

    @pl.when(b + 1 < nb)
    def _():
        runs(b + 1, buf=sorted_buf.at[1 - slot], sem=sems.at[1 - slot], wait=False)

    runs(b, buf=sorted_buf.at[slot], sem=sems.at[slot], wait=True)
    route = route_ref[...]
    col = lax.broadcasted_iota(I32, (TOK_CHUNK, sorted_buf.shape[1]), 1).astype(F32)
    gate = jnp.zeros(col.shape, F32)
    for k in range(TOP_K):
        gate = jnp.where(col == route[:, TOP_K + k:TOP_K + k + 1], route[:, 2 * TOP_K + k:2 * TOP_K + k + 1], gate)
    g_hi, g_lo = _split(gate)
    yb = _unpack_pairs(sorted_buf[slot])
    y = _dot(g_hi, yb) + _dot(g_lo, yb)
    o_ref[...] = x1_ref[...] + mod_ref[:, 5 * D:6 * D] * (_rms(y) * g_ref[...])


def _combine(plan, yb, route, x1, mod4, g3, *, n_exp, ctx_blocks, blocks_per_seq, l):
    N, D = x1.shape

    def grp(i):
        return jnp.where(i < ctx_blocks, 0, 1 + (i - ctx_blocks) // blocks_per_seq)

    grid_spec = pltpu.PrefetchScalarGridSpec(
        num_scalar_prefetch=3,
        grid=(N // TOK_CHUNK,),
        in_specs=[pl.BlockSpec(memory_space=pl.ANY),
                  pl.BlockSpec((TOK_CHUNK, LANES), lambda i, *_: (i, 0)),
                  pl.BlockSpec((TOK_CHUNK, D), lambda i, *_: (i, 0)),
                  pl.BlockSpec((None, None, 1, N_MOD * D), lambda i, *_: (l, grp(i), 0, 0)),
                  pl.BlockSpec((1, D), lambda i, *_: (0, 0))],
        out_specs=pl.BlockSpec((TOK_CHUNK, D), lambda i, *_: (i, 0)),
        scratch_shapes=[pltpu.VMEM((2, _sorted_rows(n_exp), yb.shape[1]), jnp.uint32),
                        pltpu.SemaphoreType.DMA((2,))],
    )
    return pl.pallas_call(
        functools.partial(_combine_kernel, D=D, n_exp=n_exp),
        grid_spec=grid_spec,
        out_shape=jax.ShapeDtypeStruct((N, D), F32),
        compiler_params=_params(("arbitrary",)),
        name="moe_combine",
    )(plan["run"], plan["src"], plan["dst"], yb, route, x1, mod4, g3)


def _rope_tables(Ts, tm):
    nf = DIFF_QK_DIM // 4
    inv = ROPE_BASE ** (-jnp.arange(nf, dtype=F32) / nf)
    t = jnp.arange(Ts)
    pos = jnp.stack([(t // GRID_W).astype(F32), (t % GRID_W).astype(F32)], axis=1)
    ang = pos[:, :, None] * inv[None, None, :]
    cos = jnp.repeat(jnp.cos(ang)[:, :, None, :], 2, axis=2).reshape(Ts, DIFF_QK_DIM)
    sin = jnp.sin(ang)
    sin = jnp.stack([-sin, sin], axis=2).reshape(Ts, DIFF_QK_DIM)
    reps = LANES // DIFF_QK_DIM
    cos = jnp.concatenate([jnp.ones((tm, LANES), F32), jnp.tile(cos, (1, reps))], axis=0)
    sin = jnp.concatenate([jnp.zeros((tm, LANES), F32), jnp.tile(sin, (1, reps))], axis=0)
    return cos, sin


def _block_diag(w):
    G, a, b = w.shape
    out = jnp.zeros((G * a, G * b), w.dtype)
    for g in range(G):
        out = out.at[g * a:(g + 1) * a, g * b:(g + 1) * b].set(w[g])
    return out


def _route_plan(counts, n_exp, nblk):
    cnt = counts[:, 0, :n_exp].astype(I32)
    run = (cnt + RUN_ALIGN - 1) // RUN_ALIGN * RUN_ALIGN
    src = jnp.cumsum(run, axis=1) - run
    tot = jnp.sum(run, axis=0)
    region = (tot + MOE_ROWS - 1) // MOE_ROWS * MOE_ROWS
    region_end = jnp.cumsum(region)
    region_start = region_end - region
    dst = region_start[None, :] + jnp.cumsum(run, axis=0) - run
    n_used = (region_end[-1] // MOE_ROWS).astype(I32)
    blk = jnp.arange(nblk, dtype=I32) * MOE_ROWS
    block_e = jnp.minimum(jnp.sum((blk[:, None] >= region_end[None, :]).astype(I32), axis=1), n_exp - 1)
    last = jnp.sum(jnp.where(jnp.arange(nblk) == n_used - 1, block_e, 0))
    block_e = jnp.where(jnp.arange(nblk) < n_used, block_e, last).astype(I32)
    plan = dict(run=run.reshape(-1).astype(I32), src=src.reshape(-1).astype(I32), dst=dst.reshape(-1).astype(I32),
                tail=(region - tot).astype(I32), tail_dst=(region_start + tot).astype(I32))
    return plan, block_e, n_used.reshape(1)


def kernel(x_prompt, x_sample, cache_diff_k, cache_diff_v, cache_na_k, cache_na_v, c, c_ctx, w_ada, b_ada,
           norm_gain, w_in, w_out, pool_w, pool_scale, diff_lambda, diff_subln, na_rpb, router_w, router_b,
           moe_w1, moe_b1, moe_w2, moe_b2):
    Bp, Tp, D = x_prompt.shape
    Bs, Ts, _ = x_sample.shape
    L = w_ada.shape[0]
    E = router_w.shape[-1]
    Np, Ns = Bp * Tp, Bs * Ts
    N = Np + Ns
    tm = TOK_CHUNK
    assert Np % Ts == 0 or Bs == 0, "context rows must be a whole number of latent-sequence blocks"
    assert Np % tm == 0 and Ts % tm == 0 and Ts % (NA_Q_ROWS * GRID_W) == 0
    assert Ts // GRID_W >= NA_BAND + NA_Q_ROWS
    ctx_blocks, blocks_per_seq = Np // tm, Ts // tm

    G = 16
    cvec = jnp.zeros((G, D), F32).at[0].set(c_ctx).at[1:1 + Bs].set(c)
    mod4 = _modulation(cvec, w_ada, b_ada).reshape(L, G, 1, N_MOD * D)
    tp = PROJ_ROWS
    assert Np % tp == 0 and Ts % tp == 0
    cos_t, sin_t = _rope_tables(Ts, tp)
    w_in_bf = w_in.astype(BF16)
    w_out_bf = w_out.astype(BF16)
    rw_pad = jnp.zeros((L, D, LANES), F32).at[:, :, :E].set(router_w)
    rb_pad = jnp.full((L, 1, LANES), MASKED, F32).at[:, 0, :E].set(router_b)
    ck = cache_diff_k.reshape(Bs, L, -1, DIFF_HEADS * 2 * DIFF_QK_DIM)
    cv = cache_diff_v.reshape(Bs, L, -1, DIFF_HEADS * DIFF_V_DIM)
    nk = cache_na_k.reshape(Bs, L, -1, NA_HEADS * NA_HEAD_DIM)
    nv = cache_na_v.reshape(Bs, L, -1, NA_HEADS * NA_HEAD_DIM)
    nb = N // TOK_CHUNK
    nblk = -(-(N * TOP_K + nb * E * (RUN_ALIGN - 1) + E * (MOE_ROWS - 1)) // MOE_ROWS)
    blocks = dict(ctx_blocks=ctx_blocks, blocks_per_seq=blocks_per_seq)

    x = jnp.concatenate([x_prompt.reshape(Np, D), x_sample.reshape(Ns, D)], axis=0)
    new_dk, new_dv, new_nk, new_nv = [], [], [], []
    for l in range(L):
        lam_init = 0.8 - 0.6 * math.exp(-0.3 * l)
        g = norm_gain[l]
        p = _in_projection(x, mod4, g[0:1], w_in_bf[l], cos_t, sin_t, tm=tp, l=l,
                           ctx_blocks=Np // tp, blocks_per_seq=Ts // tp)

        pw = _block_diag(pool_w[l]).astype(BF16)
        ps = pool_scale[l].reshape(1, -1)
        pool_o = jnp.zeros((N, pw.shape[0]), F32)
        pool_o = _pool(p, pw, ps, row0=0, n_seq=Bp, T=Tp, out=pool_o)
        pool_o = _pool(p, pw, ps, row0=Np, n_seq=Bs, T=Ts, out=pool_o)

        sub = diff_subln[l].reshape(1, -1)
        dn = jnp.zeros((N, DIFF_HEADS * DIFF_V_DIM), F32)
        dn = _diff_attention(p, diff_lambda[l], sub, dn, row0=0, n_seq=Bp, T=Tp, tq=Tp, lam_init=lam_init)
        dn = _diff_attention(p, diff_lambda[l], sub, dn, row0=Np, n_seq=Bs, T=Ts, tq=DIFF_Q_ROWS, lam_init=lam_init,
                             cache_k=ck, cache_v=cv, l=l)

        na_o = jnp.zeros((N, NA_HEADS * NA_HEAD_DIM), F32)
        na_o = _dense_attention(p, na_o, n_seq=Bp, T=Tp)
        bias = _na_bias_table(na_rpb[l], Ts // GRID_W)
        na_o = _neighbourhood_attention(p, nk, nv, bias, na_o, row0=Np, n_seq=Bs, T=Ts, l=l)

        x1, h2, route, counts = _merge_route(pool_o, dn, na_o, x, mod4, g[1:2], g[2:3], w_out_bf[l],
                                             rw_pad[l], rb_pad[l], tm=tp, l=l,
                                             ctx_blocks=Np // tp, blocks_per_seq=Ts // tp)
        plan, block_e, n_used = _route_plan(counts, E, nblk)
        xs = _dispatch(plan, n_used, route, h2, nblk * MOE_ROWS, E)
        yb = _experts(block_e, n_used, xs, moe_w1, moe_b1, moe_w2, moe_b2, l=l)
        x = _combine(plan, yb, route, x1, mod4, g[3:4], n_exp=E, l=l, **blocks)

        pc = p[:Np]
        new_dk.append(pc[:, 768:1280].reshape(Bp, Tp, DIFF_HEADS, 2 * DIFF_QK_DIM))
        new_dv.append(pc[:, 1280:1792].reshape(Bp, Tp, DIFF_HEADS, DIFF_V_DIM))
        new_nk.append(pc[:, 2048:2304].reshape(Bp, Tp, NA_HEADS, NA_HEAD_DIM))
        new_nv.append(pc[:, 2304:2560].reshape(Bp, Tp, NA_HEADS, NA_HEAD_DIM))

    return (x[:Np].reshape(Bp, Tp, D), x[Np:].reshape(Bs, Ts, D),
            jnp.stack(new_dk, axis=1), jnp.stack(new_dv, axis=1),
            jnp.stack(new_nk, axis=1), jnp.stack(new_nv, axis=1))
```

```python
import functools
import math

import numpy as np
import jax
import jax.numpy as jnp
from jax import lax
from jax.experimental import pallas as pl
from jax.experimental.pallas import tpu as pltpu

F32 = jnp.float32
BF16 = jnp.bfloat16
I32 = jnp.int32

GRID_W = 64
POOL_GROUPS = 4
POOL_MAX_HALF = 8
DIFF_HEADS = 4
DIFF_V_DIM = 128
DIFF_QK_DIM = 64
NA_HEADS = 4
NA_HEAD_DIM = 64
NA_WIN_H = 8
NA_WIN_W = 16
NA_Q_ROWS = 4
NA_BAND = 12
TOP_K = 4
SWIGLU_ALPHA = 1.702
SWIGLU_LIMIT = 7.0
ROPE_BASE = 10000.0
NORM_EPS = 1e-6
N_MOD = 6

LOG2E = 1.4426950408889634
MASKED = -1e30
LANES = 128
MXU_TILE = 256
MOE_ROWS = 512
COL_DQ, COL_DK, COL_DV = 256, 768, 1280
COL_NQ, COL_NK, COL_NV = 1792, 2048, 2304
DIFF_KEYS = 256
PROJ_ROWS = 512
DIFF_Q_ROWS = 256
TOK_CHUNK = 256
RUN_ALIGN = 8
RUN_UNROLL = 4
VMEM_LIMIT = 56 * 1024 * 1024


def _sorted_rows(n_exp):
    rows = TOK_CHUNK * TOP_K + n_exp * (RUN_ALIGN - 1)
    return -(-rows // MXU_TILE) * MXU_TILE


def _params(sem, vmem=VMEM_LIMIT):
    return pltpu.CompilerParams(dimension_semantics=sem, vmem_limit_bytes=vmem)


def _dot(a, b):
    return jnp.dot(a, b, preferred_element_type=F32)


def _dot_nt(a, b):
    return lax.dot_general(a, b, (((1,), (1,)), ((), ())), preferred_element_type=F32)


def _split(x):
    hi = x.astype(BF16)
    return hi, (x - hi.astype(F32)).astype(BF16)


def _dot3(a, b):
    ah, al = _split(a)
    bh, bl = _split(b)
    return _dot(ah, bh) + _dot(al, bh) + _dot(ah, bl)


def _rms(x):
    return x * lax.rsqrt(jnp.mean(x * x, axis=-1, keepdims=True) + NORM_EPS)


def _mod_kernel(c_ref, w_ref, b_ref, o_ref):
    c = c_ref[...]
    a = c * (1.0 / (1.0 + jnp.exp(-c)))
    o_ref[...] = _dot3(a, w_ref[...]) + b_ref[...]


def _modulation(cvec, w_ada, b_ada):
    L, D, W = w_ada.shape
    G = cvec.shape[0]
    return pl.pallas_call(
        _mod_kernel,
        grid=(L, W // D),
        in_specs=[pl.BlockSpec((G, D), lambda l, j: (0, 0)),
                  pl.BlockSpec((None, D, D), lambda l, j: (l, 0, j)),
                  pl.BlockSpec((None, 1, D), lambda l, j: (l, 0, j))],
        out_specs=pl.BlockSpec((None, G, D), lambda l, j: (l, 0, j)),
        out_shape=jax.ShapeDtypeStruct((L, G, W), F32),
        compiler_params=_params(("arbitrary", "arbitrary")),
        name="ada_modulation",
    )(cvec, w_ada, b_ada.reshape(L, 1, W))


def _inproj_kernel(x_ref, mod_ref, g_ref, w_ref, cos_ref, sin_ref, o_ref, dk_ref, dv_ref, nk_ref, nv_ref, *,
                   D, rope_lo, rope_hi, ctx_blocks):
    h = _rms(x_ref[...]) * g_ref[...]
    h = h * (1.0 + mod_ref[:, D:2 * D]) + mod_ref[:, 0:D]
    p = _dot(h.astype(BF16), w_ref[...])
    W = p.shape[1]
    o_ref[:, 0:rope_lo] = p[:, 0:rope_lo]
    o_ref[:, rope_hi:W] = p[:, rope_hi:W]
    cos = cos_ref[...]
    sin = sin_ref[...]
    lane = lax.broadcasted_iota(I32, cos.shape, 1)
    first = (lane % 32) < 16
    for c0 in range(rope_lo, rope_hi, LANES):
        xc = p[:, c0:c0 + LANES]
        partner = jnp.where(first, pltpu.roll(xc, LANES - 16, 1), pltpu.roll(xc, 16, 1))
        o_ref[:, c0:c0 + LANES] = xc * cos + partner * sin

    @pl.when(pl.program_id(0) < ctx_blocks)
    def _():
        dk_ref[...] = o_ref[:, COL_DK:COL_DV]
        dv_ref[...] = o_ref[:, COL_DV:COL_NQ]
        nk_ref[...] = o_ref[:, COL_NK:COL_NV]
        nv_ref[...] = o_ref[:, COL_NV:W]


def _in_projection(x, mod4, gain, w_bf, cos_t, sin_t, *, tm, ctx_blocks, blocks_per_seq, l):
    N, D = x.shape
    W = w_bf.shape[1]

    def grp(i):
        return jnp.where(i < ctx_blocks, 0, 1 + (i - ctx_blocks) // blocks_per_seq)

    def rope_blk(i):
        return jnp.where(i < ctx_blocks, 0, 1 + (i - ctx_blocks) % blocks_per_seq)

    kern = functools.partial(_inproj_kernel, D=D, rope_lo=COL_DQ, rope_hi=COL_DV, ctx_blocks=ctx_blocks)
    ctx_rows = ctx_blocks * tm
    ctx_out = lambda w: pl.BlockSpec((tm, w), lambda i: (jnp.minimum(i, ctx_blocks - 1), 0))
    widths = (COL_DV - COL_DK, COL_NQ - COL_DV, COL_NV - COL_NK, W - COL_NV)
    return pl.pallas_call(
        kern,
        grid=(N // tm,),
        in_specs=[pl.BlockSpec((tm, D), lambda i: (i, 0)),
                  pl.BlockSpec((None, None, 1, N_MOD * D), lambda i: (l, grp(i), 0, 0)),
                  pl.BlockSpec((1, D), lambda i: (0, 0)),
                  pl.BlockSpec((D, W), lambda i: (0, 0)),
                  pl.BlockSpec((tm, LANES), lambda i: (rope_blk(i), 0)),
                  pl.BlockSpec((tm, LANES), lambda i: (rope_blk(i), 0))],
        out_specs=[pl.BlockSpec((tm, W), lambda i: (i, 0))] + [ctx_out(w) for w in widths],
        out_shape=[jax.ShapeDtypeStruct((N, W), F32)] + [jax.ShapeDtypeStruct((ctx_rows, w), F32) for w in widths],
        compiler_params=_params(("arbitrary",)),
        name="in_projection",
    )(x, mod4, gain, w_bf, cos_t, sin_t)


def _pool_kernel(u_ref, w_ref, sc_ref, o_ref, pad_ref, *, T, CH):
    H = 2 * POOL_MAX_HALF
    zeros = jnp.zeros((H, pad_ref.shape[1]), F32)
    pad_ref[0:H, :] = zeros
    pad_ref[H + T:2 * H + T, :] = zeros
    pad_ref[H:H + T, :] = u_ref[...]
    C = pad_ref.shape[1]
    lane = lax.broadcasted_iota(I32, (CH, C), 1)
    half = jnp.left_shift(1, lane // (C // POOL_GROUPS))
    row = lax.broadcasted_iota(I32, (CH, C), 0)

    def body(ci, carry):
        base = pl.multiple_of(ci * CH, CH)
        win = pad_ref[pl.ds(base + POOL_MAX_HALF, CH + H), :]
        acc = jnp.zeros((CH, C), F32)
        for j in range(-POOL_MAX_HALF, POOL_MAX_HALF):
            sl = win[POOL_MAX_HALF + j:POOL_MAX_HALF + j + CH, :]
            inside = (half > j) if j >= 0 else (half >= -j)
            acc = acc + jnp.where(inside, sl, 0.0)
        t = row + base
        cnt = jnp.minimum(t + half, T) - jnp.maximum(t - half, 0)
        d = acc / cnt.astype(F32) - win[POOL_MAX_HALF:POOL_MAX_HALF + CH, :]
        o_ref[pl.ds(base, CH), :] = _dot(d.astype(BF16), w_ref[...]) * sc_ref[...]
        return carry

    lax.fori_loop(0, T // CH, body, 0)


def _pool(p, w_bd, scale, *, row0, n_seq, T, out):
    C = w_bd.shape[0]
    CH = min(T, 256)
    blk0 = row0 // T

    def kern(u_ref, w_ref, sc_ref, prev_ref, o_ref, pad_ref):
        del prev_ref
        _pool_kernel(u_ref, w_ref, sc_ref, o_ref, pad_ref, T=T, CH=CH)

    return pl.pallas_call(
        kern,
        grid=(n_seq,),
        in_specs=[pl.BlockSpec((T, C), lambda s: (blk0 + s, 0)),
                  pl.BlockSpec((C, C), lambda s: (0, 0)),
                  pl.BlockSpec((1, C), lambda s: (0, 0)),
                  pl.BlockSpec(memory_space=pl.ANY)],
        out_specs=pl.BlockSpec((T, C), lambda s: (blk0 + s, 0)),
        out_shape=jax.ShapeDtypeStruct(out.shape, F32),
        scratch_shapes=[pltpu.VMEM((T + 4 * POOL_MAX_HALF, C), F32)],
        input_output_aliases={3: 0},
        compiler_params=_params(("arbitrary",)),
        name="pool_mixer",
    )(p, w_bd, scale, out)


def _softmax_pv(s, v):
    m = jnp.max(s, axis=-1, keepdims=True)
    e = jnp.exp2(s - m)
    l = jnp.sum(e, axis=-1, keepdims=True)
    return _dot(e.astype(BF16), v) * (1.0 / l)


def _diff_kernel(*refs, Ts, Lc, lam_init):
    if Lc:
        lam_ref, sub_ref, q_ref, ks_ref, vs_ref, kc_ref, vc_ref, prev_ref, o_ref, kb, vb, s_a, s_b, m_a, m_b = refs
    else:
        lam_ref, sub_ref, q_ref, ks_ref, vs_ref, prev_ref, o_ref, kb, vb, s_a, s_b, m_a, m_b = refs
    del prev_ref
    i = pl.program_id(2)
    S = Ts + Lc
    dv = DIFF_V_DIM

    @pl.when(i == 0)
    def _():
        kb[0:Ts, :] = ks_ref[...].astype(BF16)
        vb[0:Ts, 0:dv] = vs_ref[...].astype(BF16)
        if Lc:
            kb[Ts:S, :] = kc_ref[...].astype(BF16)
            vb[Ts:S, 0:dv] = vc_ref[...].astype(BF16)
        vb[:, dv:2 * dv] = jnp.ones((S, dv), BF16)
        s_b[...] = jnp.zeros(s_b.shape, F32)
        m_b[...] = jnp.zeros(m_b.shape, F32)

    lm = lam_ref[...]
    lam = (jnp.exp(jnp.sum(lm[0:1, :] * lm[1:2, :], axis=-1, keepdims=True))
           - jnp.exp(jnp.sum(lm[2:3, :] * lm[3:4, :], axis=-1, keepdims=True)) + lam_init)

    def stage(s_new, m_new, s_old, m_old):
        q = q_ref[...] * (DIFF_QK_DIM ** -0.5 * LOG2E)
        part = lax.broadcasted_iota(I32, q.shape, 1) // DIFF_QK_DIM
        qm = [jnp.where(part == u, q, 0.0).astype(BF16) for u in range(2)]
        top = [m_old[u][:, 0:1] for u in range(2)]
        acc = [jnp.zeros((q.shape[0], 2 * dv), F32) for _ in range(2)]
        run = [jnp.full((q.shape[0], LANES), -jnp.inf, F32) for _ in range(2)]
        for c0 in range(0, S, DIFF_KEYS):
            keys = slice(c0, c0 + DIFF_KEYS)
            for u in range(2):
                sc = _dot_nt(qm[u], kb[keys, :])
                s_new[u, :, keys] = sc
                for j in range(0, DIFF_KEYS, LANES):
                    run[u] = jnp.maximum(run[u], sc[:, j:j + LANES])
                e = jnp.exp2(s_old[u, :, keys] - top[u])
                acc[u] = acc[u] + _dot(e.astype(BF16), vb[keys, :])
        for u in range(2):
            m_new[u] = jnp.broadcast_to(jnp.max(run[u], axis=-1, keepdims=True), m_new.shape[1:])
        o = (acc[0][:, 0:dv] * (1.0 / acc[0][:, dv:dv + 1])
             - lam * (acc[1][:, 0:dv] * (1.0 / acc[1][:, dv:dv + 1])))
        o_ref[...] = _rms(o) * sub_ref[...] * (1.0 - lam_init)

    @pl.when(i % 2 == 0)
    def _():
        stage(s_a, m_a, s_b, m_b)

    @pl.when(i % 2 == 1)
    def _():
        stage(s_b, m_b, s_a, m_a)


def _diff_ctx_kernel(lam_ref, sub_ref, q_ref, k_ref, v_ref, prev_ref, o_ref, *, lam_init, heads):
    del prev_ref
    lm = lam_ref[...]
    lam = (jnp.exp(jnp.sum(lm[0:1, :] * lm[1:2, :], axis=-1, keepdims=True))
           - jnp.exp(jnp.sum(lm[2:3, :] * lm[3:4, :], axis=-1, keepdims=True)) + lam_init)
    q = q_ref[...] * (DIFF_QK_DIM ** -0.5 * LOG2E)
    part = lax.broadcasted_iota(I32, q.shape, 1) // DIFF_QK_DIM
    k = k_ref[...].astype(BF16)
    v = v_ref[...].astype(BF16)
    for hh in range(heads):
        o1 = _softmax_pv(_dot_nt(jnp.where(part == 2 * hh, q, 0.0).astype(BF16), k), v)
        o2 = _softmax_pv(_dot_nt(jnp.where(part == 2 * hh + 1, q, 0.0).astype(BF16), k), v)
        cols = slice(hh * DIFF_V_DIM, (hh + 1) * DIFF_V_DIM)
        o = o1[:, cols] - lam * o2[:, cols]
        o_ref[:, cols] = _rms(o) * sub_ref[...] * (1.0 - lam_init)


def _diff_attention_ctx(p, lam_l, subln, out, *, n_seq, T, lam_init):
    heads = 2
    W = heads * DIFF_V_DIM
    spec = lambda col0: pl.BlockSpec((T, W), lambda b, h: (b, col0 // W + h))
    return pl.pallas_call(
        functools.partial(_diff_ctx_kernel, lam_init=lam_init, heads=heads),
        grid=(n_seq, DIFF_HEADS // heads),
        in_specs=[pl.BlockSpec((4, DIFF_QK_DIM), lambda b, h: (0, 0)),
                  pl.BlockSpec((1, DIFF_V_DIM), lambda b, h: (0, 0)),
                  spec(COL_DQ), spec(COL_DK), spec(COL_DV),
                  pl.BlockSpec(memory_space=pl.ANY)],
        out_specs=pl.BlockSpec((T, W), lambda b, h: (b, h)),
        out_shape=jax.ShapeDtypeStruct(out.shape, F32),
        input_output_aliases={5: 0},
        compiler_params=_params(("arbitrary", "arbitrary")),
        name="diff_attention_ctx",
    )(lam_l, subln, p, p, p, out)


def _diff_attention(p, lam_l, subln, out, *, row0, n_seq, T, tq, lam_init, cache_k=None, cache_v=None, l=0):
    W = DIFF_V_DIM
    assert 2 * DIFF_QK_DIM == W == LANES
    Lc = 0 if cache_k is None else cache_k.shape[2]
    S = T + Lc
    assert S % DIFF_KEYS == 0
    nq = T // tq
    qb0 = row0 // tq
    sb0 = row0 // T
    in_specs = [pl.BlockSpec((4, DIFF_QK_DIM), lambda b, h, i: (0, 0)),
                pl.BlockSpec((1, W), lambda b, h, i: (0, 0)),
                pl.BlockSpec((tq, W), lambda b, h, i: (qb0 + b * nq + jnp.minimum(i, nq - 1), COL_DQ // W + h)),
                pl.BlockSpec((T, W), lambda b, h, i: (sb0 + b, COL_DK // W + h)),
                pl.BlockSpec((T, W), lambda b, h, i: (sb0 + b, COL_DV // W + h))]
    args = [lam_l, subln, p, p, p]
    if Lc:
        in_specs += [pl.BlockSpec((None, None, Lc, W), lambda b, h, i: (b, l, 0, h)),
                     pl.BlockSpec((None, None, Lc, W), lambda b, h, i: (b, l, 0, h))]
        args += [cache_k, cache_v]
    in_specs.append(pl.BlockSpec(memory_space=pl.ANY))
    args.append(out)
    scores = pltpu.VMEM((2, tq, S), F32)
    row_max = pltpu.VMEM((2, tq, LANES), F32)
    return pl.pallas_call(
        functools.partial(_diff_kernel, Ts=T, Lc=Lc, lam_init=lam_init),
        grid=(n_seq, DIFF_HEADS, nq + 1),
        in_specs=in_specs,
        out_specs=pl.BlockSpec((tq, W), lambda b, h, i: (qb0 + b * nq + jnp.maximum(i - 1, 0), h)),
        out_shape=jax.ShapeDtypeStruct(out.shape, F32),
        scratch_shapes=[pltpu.VMEM((S, W), BF16), pltpu.VMEM((S, 2 * W), BF16), scores, scores, row_max, row_max],
        input_output_aliases={len(args) - 1: 0},
        compiler_params=_params(("arbitrary", "arbitrary", "arbitrary")),
        name="diff_attention",
    )(*args)


def _na_heads(q, score_fn, pv_fn):
    lane = lax.broadcasted_iota(I32, q.shape, 1) // NA_HEAD_DIM
    out = jnp.zeros(q.shape, F32)
    for h in range(NA_HEADS):
        qh = jnp.where(lane == h, q, 0.0).astype(BF16)
        out = jnp.where(lane == h, pv_fn(score_fn(qh, h)), out)
    return out


def _dense_kernel(q_ref, k_ref, v_ref, prev_ref, o_ref):
    del prev_ref
    q = q_ref[...] * (NA_HEAD_DIM ** -0.5 * LOG2E)
    k = k_ref[...].astype(BF16)
    v = v_ref[...].astype(BF16)
    o_ref[...] = _na_heads(q, lambda qh, h: _dot_nt(qh, k), lambda s: _softmax_pv(s, v))


def _dense_attention(p, out, *, n_seq, T):
    C = NA_HEADS * NA_HEAD_DIM
    return pl.pallas_call(
        _dense_kernel,
        grid=(n_seq,),
        in_specs=[pl.BlockSpec((T, C), lambda b: (b, 7)),
                  pl.BlockSpec((T, C), lambda b: (b, 8)),
                  pl.BlockSpec((T, C), lambda b: (b, 9)),
                  pl.BlockSpec(memory_space=pl.ANY)],
        out_specs=pl.BlockSpec((T, C), lambda b: (b, 0)),
        out_shape=jax.ShapeDtypeStruct(out.shape, F32),
        input_output_aliases={3: 0},
        compiler_params=_params(("arbitrary",)),
        name="dense_attention",
    )(p, p, p, out)


def _na_kernel(q_ref, ks_ref, vs_ref, kc_ref, vc_ref, bias_ref, prev_ref, o_ref, *, rows):
    del prev_ref
    r0 = pl.program_id(1) * NA_Q_ROWS
    bs = jnp.clip(r0 - NA_WIN_H // 2, 0, rows - NA_BAND)
    start = pl.multiple_of(bs * GRID_W, GRID_W)
    nb = NA_BAND * GRID_W
    kb = ks_ref[pl.ds(start, nb), :].astype(BF16)
    vb = vs_ref[pl.ds(start, nb), :].astype(BF16)
    kc = kc_ref[...].astype(BF16)
    vc = vc_ref[...].astype(BF16)
    q = q_ref[...] * (NA_HEAD_DIM ** -0.5 * LOG2E)

    def scores(qh, h):
        return _dot_nt(qh, kb) + bias_ref[h], _dot_nt(qh, kc)

    def pv(s):
        s_loc, s_ctx = s
        m = jnp.maximum(jnp.max(s_loc, axis=-1, keepdims=True), jnp.max(s_ctx, axis=-1, keepdims=True))
        e_loc = jnp.exp2(s_loc - m)
        e_ctx = jnp.exp2(s_ctx - m)
        l = jnp.sum(e_loc, axis=-1, keepdims=True) + jnp.sum(e_ctx, axis=-1, keepdims=True)
        return (_dot(e_loc.astype(BF16), vb) + _dot(e_ctx.astype(BF16), vc)) * (1.0 / l)

    o_ref[...] = _na_heads(q, scores, pv)


def _na_bias_table(rpb_l, rows):
    n_ro, n_co = 2 * NA_WIN_H - 1, 2 * NA_WIN_W - 1
    c = np.arange(GRID_W)[:, None]
    kc = np.arange(GRID_W)[None, :]
    cs = np.clip(c - NA_WIN_W // 2, 0, GRID_W - NA_WIN_W)
    col_ok = (kc >= cs) & (kc < cs + NA_WIN_W)
    co = kc - c + (NA_WIN_W - 1)
    pick = ((np.arange(n_co)[:, None, None] == co[None]) & col_ok[None]).astype(np.float32)
    toep = jnp.dot(rpb_l.reshape(NA_HEADS * n_ro, n_co).astype(F32), jnp.asarray(pick.reshape(n_co, -1)),
                   precision=lax.Precision.HIGHEST).reshape(NA_HEADS, n_ro, GRID_W, GRID_W) * LOG2E
    toep = jnp.where(jnp.asarray(col_ok)[None, None], toep, MASKED)
    toep = jnp.concatenate([toep, jnp.full((NA_HEADS, 1, GRID_W, GRID_W), MASKED, F32)], axis=1)
    blk = np.full((3, NA_Q_ROWS, NA_BAND), n_ro, np.int32)
    for v, r0 in enumerate((0, NA_Q_ROWS, rows - NA_Q_ROWS)):
        bs = int(np.clip(r0 - NA_WIN_H // 2, 0, rows - NA_BAND))
        for j in range(NA_Q_ROWS):
            rs = int(np.clip(r0 + j - NA_WIN_H // 2, 0, rows - NA_WIN_H))
            for i in range(NA_BAND):
                if rs <= bs + i < rs + NA_WIN_H:
                    blk[v, j, i] = bs + i - (r0 + j) + (NA_WIN_H - 1)
    tab = toep[:, blk]
    tab = jnp.transpose(tab, (1, 0, 2, 4, 3, 5))
    return tab.reshape(3, NA_HEADS, NA_Q_ROWS * GRID_W, NA_BAND * GRID_W)


def _neighbourhood_attention(p, cache_k, cache_v, bias, out, *, row0, n_seq, T, l):
    C = NA_HEADS * NA_HEAD_DIM
    rows = T // GRID_W
    tq = NA_Q_ROWS * GRID_W
    nq = T // tq
    Lc = cache_k.shape[2]
    qb0 = row0 // tq
    sb0 = row0 // T

    def variant(i):
        r0 = i * NA_Q_ROWS
        return (r0 - jnp.clip(r0 - NA_WIN_H // 2, 0, rows - NA_BAND)) // NA_Q_ROWS

    return pl.pallas_call(
        functools.partial(_na_kernel, rows=rows),
        grid=(n_seq, nq),
        in_specs=[pl.BlockSpec((tq, C), lambda b, i: (qb0 + b * nq + i, 7)),
                  pl.BlockSpec((T, C), lambda b, i: (sb0 + b, 8)),
                  pl.BlockSpec((T, C), lambda b, i: (sb0 + b, 9)),
                  pl.BlockSpec((None, None, Lc, C), lambda b, i: (b, l, 0, 0)),
                  pl.BlockSpec((None, None, Lc, C), lambda b, i: (b, l, 0, 0)),
                  pl.BlockSpec((None, NA_HEADS, tq, NA_BAND * GRID_W), lambda b, i: (variant(i), 0, 0, 0)),
                  pl.BlockSpec(memory_space=pl.ANY)],
        out_specs=pl.BlockSpec((tq, C), lambda b, i: (qb0 + b * nq + i, 0)),
        out_shape=jax.ShapeDtypeStruct(out.shape, F32),
        input_output_aliases={6: 0},
        compiler_params=_params(("arbitrary", "arbitrary")),
        name="neighbourhood_attention",
    )(p, p, p, cache_k, cache_v, bias, out)


def _merge_kernel(pool_ref, dn_ref, na_ref, x_ref, mod_ref, g1_ref, g2_ref, w_ref, rw_ref, rb_ref,
                  x1_ref, h2_ref, route_ref, cnt_ref, *, D):
    c0 = pool_ref.shape[1]
    c1 = c0 + dn_ref.shape[1]
    mix = (_dot(pool_ref[...].astype(BF16), w_ref[0:c0, :])
           + _dot(dn_ref[...].astype(BF16), w_ref[c0:c1, :])
           + _dot(na_ref[...].astype(BF16), w_ref[c1:, :]))
    x1 = x_ref[...] + mod_ref[:, 2 * D:3 * D] * (_rms(mix) * g1_ref[...])
    x1_ref[...] = x1
    h2 = (_rms(x1) * g2_ref[...]) * (1.0 + mod_ref[:, 4 * D:5 * D]) + mod_ref[:, 3 * D:4 * D]
    h2_ref[...] = h2

    logits = _dot3(h2, rw_ref[...]) + rb_ref[...]
    tm = logits.shape[0]
    lane = lax.broadcasted_iota(I32, (tm, LANES), 1)
    lane_f = lane.astype(F32)
    work = logits
    vals, hots = [], []
    for _ in range(TOP_K):
        mx = jnp.max(work, axis=-1, keepdims=True)
        idx = jnp.min(jnp.where(work == mx, lane_f, float(LANES)), axis=-1, keepdims=True)
        hot = lane_f == idx
        vals.append(mx)
        hots.append(hot)
        work = jnp.where(hot, -jnp.inf, work)
    es = [jnp.exp(v - vals[0]) for v in vals]
    inv = 1.0 / (es[0] + es[1] + es[2] + es[3])

    sel = jnp.zeros((tm, LANES), F32)
    for hot in hots:
        sel = jnp.where(hot, 1.0, sel)
    r = lax.broadcasted_iota(I32, (TOK_CHUNK, TOK_CHUNK), 0)
    c = lax.broadcasted_iota(I32, (TOK_CHUNK, TOK_CHUNK), 1)
    before = jnp.where(c < r, 1.0, 0.0).astype(BF16)
    er = lax.broadcasted_iota(I32, (LANES, LANES), 0)
    ec = lax.broadcasted_iota(I32, (LANES, LANES), 1)
    earlier = jnp.where(er < ec, 1.0, 0.0).astype(BF16)
    pos = []
    for j in range(tm // TOK_CHUNK):
        sel_j = sel[j * TOK_CHUNK:(j + 1) * TOK_CHUNK, :]
        rank = _dot(before, sel_j.astype(BF16))
        cnt = jnp.sum(sel_j, axis=0, keepdims=True)
        cnt_ref[j] = cnt
        run = jnp.floor((cnt + (RUN_ALIGN - 1)) * (1.0 / RUN_ALIGN)) * RUN_ALIGN
        run_start = _dot(jnp.broadcast_to(run, (8, LANES)).astype(BF16), earlier)[0:1, :]
        pos.append(rank + run_start)
    pos = jnp.concatenate(pos, axis=0)

    route = jnp.zeros((tm, LANES), F32)
    for k in range(TOP_K):
        e_k = jnp.sum(jnp.where(hots[k], lane_f, 0.0), axis=-1, keepdims=True)
        p_k = jnp.sum(jnp.where(hots[k], pos, 0.0), axis=-1, keepdims=True)
        route = jnp.where(lane == k, e_k, route)
        route = jnp.where(lane == TOP_K + k, p_k, route)
        route = jnp.where(lane == 2 * TOP_K + k, es[k] * inv, route)
    route_ref[...] = route


def _merge_route(pool_o, dn, na_o, x, mod4, g1, g2, w_out_bf, rw_pad, rb_pad, *, tm, ctx_blocks,
                 blocks_per_seq, l):
    N, D = x.shape
    per_step = tm // TOK_CHUNK

    def grp(i):
        return jnp.where(i < ctx_blocks, 0, 1 + (i - ctx_blocks) // blocks_per_seq)

    row = lambda w: pl.BlockSpec((tm, w), lambda i: (i, 0))
    full = lambda a: pl.BlockSpec(a.shape, lambda i: (0,) * a.ndim)
    return pl.pallas_call(
        functools.partial(_merge_kernel, D=D),
        grid=(N // tm,),
        in_specs=[row(pool_o.shape[1]), row(dn.shape[1]), row(na_o.shape[1]), row(D),
                  pl.BlockSpec((None, None, 1, N_MOD * D), lambda i: (l, grp(i), 0, 0)),
                  full(g1), full(g2), full(w_out_bf), full(rw_pad), full(rb_pad)],
        out_specs=[row(D), row(D), row(LANES), pl.BlockSpec((per_step, 1, LANES), lambda i: (i, 0, 0))],
        out_shape=[jax.ShapeDtypeStruct((N, D), F32), jax.ShapeDtypeStruct((N, D), F32),
                   jax.ShapeDtypeStruct((N, LANES), F32), jax.ShapeDtypeStruct((N // TOK_CHUNK, 1, LANES), F32)],
        compiler_params=_params(("arbitrary",)),
        name="merge_route",
    )(pool_o, dn, na_o, x, mod4, g1, g2, w_out_bf, rw_pad, rb_pad)


def _pack_pairs(x):
    C = x.shape[1] // 2
    bits = lax.bitcast_convert_type(x.astype(BF16).astype(F32), jnp.uint32)
    return bits[:, C:] | (bits[:, :C] >> 16)


def _unpack_pairs(w):
    lo = lax.bitcast_convert_type(w << 16, F32)
    hi = lax.bitcast_convert_type(w & jnp.uint32(0xFFFF0000), F32)
    return jnp.concatenate([lo, hi], axis=1).astype(BF16)


def _run_copies(n, src_at, dst_at, sem, wait):
    done = jnp.int32(0)
    size = TOK_CHUNK
    while size >= RUN_ALIGN:
        bit = n & size

        @pl.when(bit != 0)
        def _(size=size, done=done):
            cp = pltpu.make_async_copy(src_at(done, size), dst_at(done, size), sem)
            cp.wait() if wait else cp.start()

        done = done + bit
        size //= 2


def _block_runs(step, n_exp, run_ref, src_ref, dst_ref, hbm_ref, buf, sem, *, to_hbm, wait):
    def body(e, carry):
        j = step * n_exp + e
        so = src_ref[j]
        do = dst_ref[j]
        in_buf = lambda o, s: buf.at[pl.ds(pl.multiple_of(so + o, RUN_ALIGN), s)]
        in_hbm = lambda o, s: hbm_ref.at[pl.ds(pl.multiple_of(do + o, RUN_ALIGN), s)]
        if to_hbm:
            _run_copies(run_ref[j], in_buf, in_hbm, sem, wait)
        else:
            _run_copies(run_ref[j], in_hbm, in_buf, sem, wait)
        return carry
    lax.fori_loop(0, n_exp, body, 0, unroll=RUN_UNROLL)


def _dispatch_kernel(run_ref, src_ref, dst_ref, tail_ref, taildst_ref, nu_ref, route_ref, h_ref, xs_ref,
                     sorted_buf, zero_buf, sems, *, n_exp):
    b = pl.program_id(0)
    nb = pl.num_programs(0)
    slot = b % 2
    runs = functools.partial(_block_runs, n_exp=n_exp, run_ref=run_ref, src_ref=src_ref, dst_ref=dst_ref,
                             hbm_ref=xs_ref, to_hbm=True)

    @pl.when(b >= 2)
    def _():
        runs(b - 2, buf=sorted_buf.at[slot], sem=sems.at[slot], wait=True)

    route = route_ref[...]
    col = lax.broadcasted_iota(I32, (TOK_CHUNK, sorted_buf.shape[1]), 1).astype(F32)
    place = jnp.zeros(col.shape, F32)
    for k in range(TOP_K):
        place = jnp.where(col == route[:, TOP_K + k:TOP_K + k + 1], 1.0, place)
    srt = lax.dot_general(place.astype(BF16), h_ref[...].astype(BF16), (((0,), (0,)), ((), ())),
                          preferred_element_type=F32)
    sorted_buf[slot] = _pack_pairs(srt)
    runs(b, buf=sorted_buf.at[slot], sem=sems.at[slot], wait=False)

    @pl.when(b == nb - 1)
    def _():
        @pl.when(b >= 1)
        def _():
            runs(b - 1, buf=sorted_buf.at[1 - slot], sem=sems.at[1 - slot], wait=True)
        runs(b, buf=sorted_buf.at[slot], sem=sems.at[slot], wait=True)

        zero_buf[...] = jnp.zeros(zero_buf.shape, zero_buf.dtype)
        sem = sems.at[0]

        def each_tail(wait):
            def body(e, carry):
                do = taildst_ref[e]
                _run_copies(tail_ref[e],
                            lambda o, s: zero_buf.at[pl.ds(0, s)],
                            lambda o, s: xs_ref.at[pl.ds(pl.multiple_of(do + o, RUN_ALIGN), s)], sem, wait)
                return carry
            lax.fori_loop(0, n_exp, body, 0)

        def spare_block(wait):
            def body(i, carry):
                cp = pltpu.make_async_copy(zero_buf.at[pl.ds(0, MOE_ROWS)],
                                           xs_ref.at[pl.ds(pl.multiple_of(i * MOE_ROWS, MOE_ROWS), MOE_ROWS)], sem)
                cp.wait() if wait else cp.start()
                return carry
            lax.fori_loop(nu_ref[0], xs_ref.shape[0] // MOE_ROWS, body, 0)

        each_tail(False)
        spare_block(False)
        each_tail(True)
        spare_block(True)


def _dispatch(plan, n_used, route, h2, n_rows, n_exp):
    N, D = h2.shape
    C = D // 2
    grid_spec = pltpu.PrefetchScalarGridSpec(
        num_scalar_prefetch=6,
        grid=(N // TOK_CHUNK,),
        in_specs=[pl.BlockSpec((TOK_CHUNK, LANES), lambda i, *_: (i, 0)),
                  pl.BlockSpec((TOK_CHUNK, D), lambda i, *_: (i, 0))],
        out_specs=pl.BlockSpec(memory_space=pl.ANY),
        scratch_shapes=[pltpu.VMEM((2, _sorted_rows(n_exp), C), jnp.uint32),
                        pltpu.VMEM((max(TOK_CHUNK, MOE_ROWS), C), jnp.uint32),
                        pltpu.SemaphoreType.DMA((2,))],
    )
    return pl.pallas_call(
        functools.partial(_dispatch_kernel, n_exp=n_exp),
        grid_spec=grid_spec,
        out_shape=jax.ShapeDtypeStruct((n_rows, C), jnp.uint32),
        compiler_params=_params(("arbitrary",)),
        name="moe_dispatch",
    )(plan["run"], plan["src"], plan["dst"], plan["tail"], plan["tail_dst"], n_used, route, h2)


def _expert_kernel(be_ref, nu_ref, rows_ref, xs_ref, w1_ref, b1_ref, w2_ref, b2_ref, o_ref, w1b, w2b, *, F):
    del nu_ref
    i = pl.program_id(0)
    e = be_ref[i]
    prev = be_ref[jnp.maximum(i - 1, 0)]
    rows = rows_ref[i]
    half = MOE_ROWS // 2

    @pl.when((i == 0) | (e != prev))
    def _():
        w1b[...] = w1_ref[...].astype(BF16)
        w2b[...] = w2_ref[...].astype(BF16)

    def ffn(xw):
        hh = _dot(_unpack_pairs(xw), w1b[...]) + b1_ref[...]
        g = jnp.minimum(hh[:, 0:F], SWIGLU_LIMIT)
        u = jnp.clip(hh[:, F:2 * F], -SWIGLU_LIMIT, SWIGLU_LIMIT)
        a = (g * (1.0 / (1.0 + jnp.exp(-SWIGLU_ALPHA * g)))) * (u + 1.0)
        return _pack_pairs(_dot(a.astype(BF16), w2b[...]) + b2_ref[...])

    @pl.when(rows > half)
    def _():
        o_ref[...] = ffn(xs_ref[...])

    @pl.when((rows > 0) & (rows <= half))
    def _():
        o_ref[0:half, :] = ffn(xs_ref[0:half, :])
        o_ref[half:MOE_ROWS, :] = jnp.zeros((MOE_ROWS - half, o_ref.shape[1]), o_ref.dtype)

    @pl.when(rows == 0)
    def _():
        o_ref[...] = jnp.zeros(o_ref.shape, o_ref.dtype)


def _experts(block_e, n_used, block_rows, xs, w1, b1, w2, b2, *, l):
    R, C = xs.shape
    L, E, D, F2 = w1.shape
    F = F2 // 2
    nblk = R // MOE_ROWS
    grid_spec = pltpu.PrefetchScalarGridSpec(
        num_scalar_prefetch=3,
        grid=(nblk,),
        in_specs=[pl.BlockSpec((MOE_ROWS, C), lambda i, be, nu, br: (jnp.minimum(i, nu[0] - 1), 0)),
                  pl.BlockSpec((None, None, D, F2), lambda i, be, nu, br: (l, be[i], 0, 0)),
                  pl.BlockSpec((None, None, 1, F2), lambda i, be, nu, br: (l, be[i], 0, 0)),
                  pl.BlockSpec((None, None, F, D), lambda i, be, nu, br: (l, be[i], 0, 0)),
                  pl.BlockSpec((None, None, 1, D), lambda i, be, nu, br: (l, be[i], 0, 0))],
        out_specs=pl.BlockSpec((MOE_ROWS, C), lambda i, be, nu, br: (i, 0)),
        scratch_shapes=[pltpu.VMEM((D, F2), BF16), pltpu.VMEM((F, D), BF16)],
    )
    return pl.pallas_call(
        functools.partial(_expert_kernel, F=F),
        grid_spec=grid_spec,
        out_shape=jax.ShapeDtypeStruct((R, C), jnp.uint32),
        compiler_params=_params(("arbitrary",)),
        name="moe_experts",
    )(block_e, n_used, block_rows, xs, w1, b1.reshape(L, E, 1, F2), w2, b2.reshape(L, E, 1, D))


def _combine_kernel(run_ref, src_ref, dst_ref, yb_ref, route_ref, x1_ref, mod_ref, g_ref, *rest, D, n_exp,
                    ctx_blocks, split):
    if split:
        ctx_ref, lat_ref, sorted_buf, sems = rest
    else:
        o_ref, sorted_buf, sems = rest
    b = pl.program_id(0)
    nb = pl.num_programs(0)
    slot = b % 2
    runs = functools.partial(_block_runs, n_exp=n_exp, run_ref=run_ref, src_ref=src_ref, dst_ref=dst_ref,
                             hbm_ref=yb_ref, to_hbm=False)

    @pl.when(b == 0)
    def _():
        sorted_buf[...] = jnp.zeros(sorted_buf.shape, sorted_buf.dtype)
        runs(b, buf=sorted_buf.at[slot], sem=sems.at[slot], wait=False)

    @pl.when(b + 1 < nb)
    def _():
        runs(b + 1, buf=sorted_buf.at[1 - slot], sem=sems.at[1 - slot], wait=False)

    runs(b, buf=sorted_buf.at[slot], sem=sems.at[slot], wait=True)
    route = route_ref[...]
    col = lax.broadcasted_iota(I32, (TOK_CHUNK, sorted_buf.shape[1]), 1).astype(F32)
    gate = jnp.zeros(col.shape, F32)
    for k in range(TOP_K):
        gate = jnp.where(col == route[:, TOP_K + k:TOP_K + k + 1], route[:, 2 * TOP_K + k:2 * TOP_K + k + 1], gate)
    g_hi, g_lo = _split(gate)
    yb = _unpack_pairs(sorted_buf[slot])
    y = _dot(g_hi, yb) + _dot(g_lo, yb)
    x2 = x1_ref[...] + mod_ref[:, 5 * D:6 * D] * (_rms(y) * g_ref[...])
    if split:
        @pl.when(b < ctx_blocks)
        def _():
            ctx_ref[...] = x2

        @pl.when(b >= ctx_blocks)
        def _():
            lat_ref[...] = x2
    else:
        o_ref[...] = x2


def _combine(plan, yb, route, x1, mod4, g3, *, n_exp, ctx_blocks, blocks_per_seq, l, split):
    N, D = x1.shape
    row = lambda f: pl.BlockSpec((TOK_CHUNK, D), lambda i, *_: (f(i), 0))
    if split:
        out_specs = [row(lambda i: jnp.minimum(i, ctx_blocks - 1)), row(lambda i: jnp.maximum(i - ctx_blocks, 0))]
        out_shape = [jax.ShapeDtypeStruct((ctx_blocks * TOK_CHUNK, D), F32),
                     jax.ShapeDtypeStruct((N - ctx_blocks * TOK_CHUNK, D), F32)]
    else:
        out_specs = row(lambda i: i)
        out_shape = jax.ShapeDtypeStruct((N, D), F32)

    def grp(i):
        return jnp.where(i < ctx_blocks, 0, 1 + (i - ctx_blocks) // blocks_per_seq)

    grid_spec = pltpu.PrefetchScalarGridSpec(
        num_scalar_prefetch=3,
        grid=(N // TOK_CHUNK,),
        in_specs=[pl.BlockSpec(memory_space=pl.ANY),
                  pl.BlockSpec((TOK_CHUNK, LANES), lambda i, *_: (i, 0)),
                  pl.BlockSpec((TOK_CHUNK, D), lambda i, *_: (i, 0)),
                  pl.BlockSpec((None, None, 1, N_MOD * D), lambda i, *_: (l, grp(i), 0, 0)),
                  pl.BlockSpec((1, D), lambda i, *_: (0, 0))],
        out_specs=out_specs,
        scratch_shapes=[pltpu.VMEM((2, _sorted_rows(n_exp), yb.shape[1]), jnp.uint32),
                        pltpu.SemaphoreType.DMA((2,))],
    )
    return pl.pallas_call(
        functools.partial(_combine_kernel, D=D, n_exp=n_exp, ctx_blocks=ctx_blocks, split=split),
        grid_spec=grid_spec,
        out_shape=out_shape,
        compiler_params=_params(("arbitrary",)),
        name="moe_combine",
    )(plan["run"], plan["src"], plan["dst"], yb, route, x1, mod4, g3)


def _rope_tables(Ts, tm):
    nf = DIFF_QK_DIM // 4
    inv = ROPE_BASE ** (-jnp.arange(nf, dtype=F32) / nf)
    t = jnp.arange(Ts)
    pos = jnp.stack([(t // GRID_W).astype(F32), (t % GRID_W).astype(F32)], axis=1)
    ang = pos[:, :, None] * inv[None, None, :]
    cos = jnp.repeat(jnp.cos(ang)[:, :, None, :], 2, axis=2).reshape(Ts, DIFF_QK_DIM)
    sin = jnp.sin(ang)
    sin = jnp.stack([-sin, sin], axis=2).reshape(Ts, DIFF_QK_DIM)
    reps = LANES // DIFF_QK_DIM
    cos = jnp.concatenate([jnp.ones((tm, LANES), F32), jnp.tile(cos, (1, reps))], axis=0)
    sin = jnp.concatenate([jnp.zeros((tm, LANES), F32), jnp.tile(sin, (1, reps))], axis=0)
    return cos, sin


def _block_diag(w):
    G, a, b = w.shape
    out = jnp.zeros((G * a, G * b), w.dtype)
    for g in range(G):
        out = out.at[g * a:(g + 1) * a, g * b:(g + 1) * b].set(w[g])
    return out


def _route_plan(counts, n_exp, nblk):
    cnt = counts[:, 0, :n_exp].astype(I32)
    run = (cnt + RUN_ALIGN - 1) // RUN_ALIGN * RUN_ALIGN
    src = jnp.cumsum(run, axis=1) - run
    tot = jnp.sum(run, axis=0)
    region = (tot + MOE_ROWS - 1) // MOE_ROWS * MOE_ROWS
    region_end = jnp.cumsum(region)
    region_start = region_end - region
    dst = region_start[None, :] + jnp.cumsum(run, axis=0) - run
    n_used = (region_end[-1] // MOE_ROWS).astype(I32)
    blk = jnp.arange(nblk, dtype=I32) * MOE_ROWS
    block_e = jnp.minimum(jnp.sum((blk[:, None] >= region_end[None, :]).astype(I32), axis=1), n_exp - 1)
    last = jnp.sum(jnp.where(jnp.arange(nblk) == n_used - 1, block_e, 0))
    used = jnp.arange(nblk) < n_used
    block_rows = jnp.where(used, jnp.clip((region_start + tot)[block_e] - blk, 0, MOE_ROWS), 0).astype(I32)
    block_e = jnp.where(used, block_e, last).astype(I32)
    plan = dict(run=run.reshape(-1).astype(I32), src=src.reshape(-1).astype(I32), dst=dst.reshape(-1).astype(I32),
                tail=(region - tot).astype(I32), tail_dst=(region_start + tot).astype(I32))
    return plan, block_e, n_used.reshape(1), block_rows


def kernel(x_prompt, x_sample, cache_diff_k, cache_diff_v, cache_na_k, cache_na_v, c, c_ctx, w_ada, b_ada,
           norm_gain, w_in, w_out, pool_w, pool_scale, diff_lambda, diff_subln, na_rpb, router_w, router_b,
           moe_w1, moe_b1, moe_w2, moe_b2):
    Bp, Tp, D = x_prompt.shape
    Bs, Ts, _ = x_sample.shape
    L = w_ada.shape[0]
    E = router_w.shape[-1]
    Np, Ns = Bp * Tp, Bs * Ts
    N = Np + Ns
    tm = TOK_CHUNK
    assert Np % Ts == 0 or Bs == 0, "context rows must be a whole number of latent-sequence blocks"
    assert Np % tm == 0 and Ts % tm == 0 and Ts % (NA_Q_ROWS * GRID_W) == 0
    assert Ts // GRID_W >= NA_BAND + NA_Q_ROWS
    ctx_blocks, blocks_per_seq = Np // tm, Ts // tm

    G = 16
    cvec = jnp.zeros((G, D), F32).at[0].set(c_ctx).at[1:1 + Bs].set(c)
    mod4 = _modulation(cvec, w_ada, b_ada).reshape(L, G, 1, N_MOD * D)
    tp = PROJ_ROWS
    assert Np % tp == 0 and Ts % tp == 0
    cos_t, sin_t = _rope_tables(Ts, tp)
    w_in_bf = w_in.astype(BF16)
    w_out_bf = w_out.astype(BF16)
    rw_pad = jnp.zeros((L, D, LANES), F32).at[:, :, :E].set(router_w)
    rb_pad = jnp.full((L, 1, LANES), MASKED, F32).at[:, 0, :E].set(router_b)
    ck = cache_diff_k.reshape(Bs, L, -1, DIFF_HEADS * 2 * DIFF_QK_DIM)
    cv = cache_diff_v.reshape(Bs, L, -1, DIFF_HEADS * DIFF_V_DIM)
    nk = cache_na_k.reshape(Bs, L, -1, NA_HEADS * NA_HEAD_DIM)
    nv = cache_na_v.reshape(Bs, L, -1, NA_HEADS * NA_HEAD_DIM)
    nb = N // TOK_CHUNK
    nblk = -(-(N * TOP_K + nb * E * (RUN_ALIGN - 1) + E * (MOE_ROWS - 1)) // MOE_ROWS)
    blocks = dict(ctx_blocks=ctx_blocks, blocks_per_seq=blocks_per_seq)

    x = jnp.concatenate([x_prompt.reshape(Np, D), x_sample.reshape(Ns, D)], axis=0)
    new_dk, new_dv, new_nk, new_nv = [], [], [], []
    for l in range(L):
        lam_init = 0.8 - 0.6 * math.exp(-0.3 * l)
        g = norm_gain[l]
        p, dk_l, dv_l, nk_l, nv_l = _in_projection(x, mod4, g[0:1], w_in_bf[l], cos_t, sin_t, tm=tp, l=l,
                                                   ctx_blocks=Np // tp, blocks_per_seq=Ts // tp)

        pw = _block_diag(pool_w[l]).astype(BF16)
        ps = pool_scale[l].reshape(1, -1)
        pool_o = jnp.zeros((N, pw.shape[0]), F32)
        pool_o = _pool(p, pw, ps, row0=0, n_seq=Bp, T=Tp, out=pool_o)
        pool_o = _pool(p, pw, ps, row0=Np, n_seq=Bs, T=Ts, out=pool_o)

        sub = diff_subln[l].reshape(1, -1)
        dn = jnp.zeros((N, DIFF_HEADS * DIFF_V_DIM), F32)
        dn = _diff_attention_ctx(p, diff_lambda[l], sub, dn, n_seq=Bp, T=Tp, lam_init=lam_init)
        dn = _diff_attention(p, diff_lambda[l], sub, dn, row0=Np, n_seq=Bs, T=Ts, tq=DIFF_Q_ROWS, lam_init=lam_init,
                             cache_k=ck, cache_v=cv, l=l)

        na_o = jnp.zeros((N, NA_HEADS * NA_HEAD_DIM), F32)
        na_o = _dense_attention(p, na_o, n_seq=Bp, T=Tp)
        bias = _na_bias_table(na_rpb[l], Ts // GRID_W)
        na_o = _neighbourhood_attention(p, nk, nv, bias, na_o, row0=Np, n_seq=Bs, T=Ts, l=l)

        x1, h2, route, counts = _merge_route(pool_o, dn, na_o, x, mod4, g[1:2], g[2:3], w_out_bf[l],
                                             rw_pad[l], rb_pad[l], tm=tp, l=l,
                                             ctx_blocks=Np // tp, blocks_per_seq=Ts // tp)
        plan, block_e, n_used, block_rows = _route_plan(counts, E, nblk)
        xs = _dispatch(plan, n_used, route, h2, nblk * MOE_ROWS, E)
        yb = _experts(block_e, n_used, block_rows, xs, moe_w1, moe_b1, moe_w2, moe_b2, l=l)
        x = _combine(plan, yb, route, x1, mod4, g[3:4], n_exp=E, l=l, split=l == L - 1, **blocks)

        new_dk.append(dk_l.reshape(Bp, Tp, DIFF_HEADS, 2 * DIFF_QK_DIM))
        new_dv.append(dv_l.reshape(Bp, Tp, DIFF_HEADS, DIFF_V_DIM))
        new_nk.append(nk_l.reshape(Bp, Tp, NA_HEADS, NA_HEAD_DIM))
        new_nv.append(nv_l.reshape(Bp, Tp, NA_HEADS, NA_HEAD_DIM))

    return (x[0].reshape(Bp, Tp, D), x[1].reshape(Bs, Ts, D),
            jnp.stack(new_dk, axis=1), jnp.stack(new_dv, axis=1),
            jnp.stack(new_nk, axis=1), jnp.stack(new_nv, axis=1))
```

```python
import functools
import math

import numpy as np
import jax
import jax.numpy as jnp
from jax import lax
from jax.experimental import pallas as pl
from jax.experimental.pallas import tpu as pltpu

F32 = jnp.float32
BF16 = jnp.bfloat16
I32 = jnp.int32

GRID_W = 64
POOL_GROUPS = 4
POOL_MAX_HALF = 8
DIFF_HEADS = 4
DIFF_V_DIM = 128
DIFF_QK_DIM = 64
NA_HEADS = 4
NA_HEAD_DIM = 64
NA_WIN_H = 8
NA_WIN_W = 16
NA_Q_ROWS = 4
NA_BAND = 12
TOP_K = 4
SWIGLU_ALPHA = 1.702
SWIGLU_LIMIT = 7.0
ROPE_BASE = 10000.0
NORM_EPS = 1e-6
N_MOD = 6

LOG2E = 1.4426950408889634
MASKED = -1e30
LANES = 128
MXU_TILE = 256
MOE_ROWS = 512
COL_DQ, COL_DK, COL_DV = 256, 768, 1280
COL_NQ, COL_NK, COL_NV = 1792, 2048, 2304
DIFF_KEYS = 256
PROJ_ROWS = 512
DIFF_Q_ROWS = 256
TOK_CHUNK = 256
RUN_ALIGN = 8
RUN_LARGE = 64
RUN_UNROLL = 4
VMEM_LIMIT = 56 * 1024 * 1024


def _sorted_rows(n_exp):
    rows = TOK_CHUNK * TOP_K + n_exp * (RUN_ALIGN - 1)
    return -(-rows // MXU_TILE) * MXU_TILE


def _params(sem, vmem=VMEM_LIMIT):
    return pltpu.CompilerParams(dimension_semantics=sem, vmem_limit_bytes=vmem)


def _dot(a, b):
    return jnp.dot(a, b, preferred_element_type=F32)


def _dot_nt(a, b):
    return lax.dot_general(a, b, (((1,), (1,)), ((), ())), preferred_element_type=F32)


def _split(x):
    hi = x.astype(BF16)
    return hi, (x - hi.astype(F32)).astype(BF16)


def _dot3(a, b):
    ah, al = _split(a)
    bh, bl = _split(b)
    return _dot(ah, bh) + _dot(al, bh) + _dot(ah, bl)


def _rms(x):
    return x * lax.rsqrt(jnp.mean(x * x, axis=-1, keepdims=True) + NORM_EPS)


def _mod_kernel(c_ref, w_ref, b_ref, o_ref):
    c = c_ref[...]
    a = c * (1.0 / (1.0 + jnp.exp(-c)))
    o_ref[...] = _dot3(a, w_ref[...]) + b_ref[...]


def _modulation(cvec, w_ada, b_ada):
    L, D, W = w_ada.shape
    G = cvec.shape[0]
    return pl.pallas_call(
        _mod_kernel,
        grid=(L, W // D),
        in_specs=[pl.BlockSpec((G, D), lambda l, j: (0, 0)),
                  pl.BlockSpec((None, D, D), lambda l, j: (l, 0, j)),
                  pl.BlockSpec((None, 1, D), lambda l, j: (l, 0, j))],
        out_specs=pl.BlockSpec((None, G, D), lambda l, j: (l, 0, j)),
        out_shape=jax.ShapeDtypeStruct((L, G, W), F32),
        compiler_params=_params(("arbitrary", "arbitrary")),
        name="ada_modulation",
    )(cvec, w_ada, b_ada.reshape(L, 1, W))


def _inproj_kernel(x_ref, mod_ref, g_ref, w_ref, cos_ref, sin_ref, o_ref, dk_ref, dv_ref, nk_ref, nv_ref, *,
                   D, rope_lo, rope_hi, ctx_blocks):
    h = _rms(x_ref[...]) * g_ref[...]
    h = h * (1.0 + mod_ref[:, D:2 * D]) + mod_ref[:, 0:D]
    p = _dot(h.astype(BF16), w_ref[...])
    W = p.shape[1]
    o_ref[:, 0:rope_lo] = p[:, 0:rope_lo]
    o_ref[:, rope_hi:W] = p[:, rope_hi:W]
    cos = cos_ref[...]
    sin = sin_ref[...]
    lane = lax.broadcasted_iota(I32, cos.shape, 1)
    first = (lane % 32) < 16
    for c0 in range(rope_lo, rope_hi, LANES):
        xc = p[:, c0:c0 + LANES]
        partner = jnp.where(first, pltpu.roll(xc, LANES - 16, 1), pltpu.roll(xc, 16, 1))
        o_ref[:, c0:c0 + LANES] = xc * cos + partner * sin

    @pl.when(pl.program_id(0) < ctx_blocks)
    def _():
        dk_ref[...] = o_ref[:, COL_DK:COL_DV]
        dv_ref[...] = o_ref[:, COL_DV:COL_NQ]
        nk_ref[...] = o_ref[:, COL_NK:COL_NV]
        nv_ref[...] = o_ref[:, COL_NV:W]


def _in_projection(x, mod4, gain, w_bf, cos_t, sin_t, *, tm, ctx_blocks, blocks_per_seq, l):
    N, D = x.shape
    W = w_bf.shape[1]

    def grp(i):
        return jnp.where(i < ctx_blocks, 0, 1 + (i - ctx_blocks) // blocks_per_seq)

    def rope_blk(i):
        return jnp.where(i < ctx_blocks, 0, 1 + (i - ctx_blocks) % blocks_per_seq)

    kern = functools.partial(_inproj_kernel, D=D, rope_lo=COL_DQ, rope_hi=COL_DV, ctx_blocks=ctx_blocks)
    ctx_rows = ctx_blocks * tm
    ctx_out = lambda w: pl.BlockSpec((tm, w), lambda i: (jnp.minimum(i, ctx_blocks - 1), 0))
    widths = (COL_DV - COL_DK, COL_NQ - COL_DV, COL_NV - COL_NK, W - COL_NV)
    return pl.pallas_call(
        kern,
        grid=(N // tm,),
        in_specs=[pl.BlockSpec((tm, D), lambda i: (i, 0)),
                  pl.BlockSpec((None, None, 1, N_MOD * D), lambda i: (l, grp(i), 0, 0)),
                  pl.BlockSpec((1, D), lambda i: (0, 0)),
                  pl.BlockSpec((D, W), lambda i: (0, 0)),
                  pl.BlockSpec((tm, LANES), lambda i: (rope_blk(i), 0)),
                  pl.BlockSpec((tm, LANES), lambda i: (rope_blk(i), 0))],
        out_specs=[pl.BlockSpec((tm, W), lambda i: (i, 0))] + [ctx_out(w) for w in widths],
        out_shape=[jax.ShapeDtypeStruct((N, W), F32)] + [jax.ShapeDtypeStruct((ctx_rows, w), F32) for w in widths],
        compiler_params=_params(("arbitrary",)),
        name="in_projection",
    )(x, mod4, gain, w_bf, cos_t, sin_t)


def _pool_kernel(u_ref, w_ref, sc_ref, o_ref, pad_ref, *, T, CH):
    H = 2 * POOL_MAX_HALF
    zeros = jnp.zeros((H, pad_ref.shape[1]), F32)
    pad_ref[0:H, :] = zeros
    pad_ref[H + T:2 * H + T, :] = zeros
    pad_ref[H:H + T, :] = u_ref[...]
    C = pad_ref.shape[1]
    lane = lax.broadcasted_iota(I32, (CH, C), 1)
    half = jnp.left_shift(1, lane // (C // POOL_GROUPS))
    row = lax.broadcasted_iota(I32, (CH, C), 0)

    def body(ci, carry):
        base = pl.multiple_of(ci * CH, CH)
        win = pad_ref[pl.ds(base + POOL_MAX_HALF, CH + H), :]
        acc = jnp.zeros((CH, C), F32)
        for j in range(-POOL_MAX_HALF, POOL_MAX_HALF):
            sl = win[POOL_MAX_HALF + j:POOL_MAX_HALF + j + CH, :]
            inside = (half > j) if j >= 0 else (half >= -j)
            acc = acc + jnp.where(inside, sl, 0.0)
        t = row + base
        cnt = jnp.minimum(t + half, T) - jnp.maximum(t - half, 0)
        d = acc / cnt.astype(F32) - win[POOL_MAX_HALF:POOL_MAX_HALF + CH, :]
        o_ref[pl.ds(base, CH), :] = _dot(d.astype(BF16), w_ref[...]) * sc_ref[...]
        return carry

    lax.fori_loop(0, T // CH, body, 0)


def _pool(p, w_bd, scale, *, row0, n_seq, T, out):
    C = w_bd.shape[0]
    CH = min(T, 256)
    blk0 = row0 // T

    def kern(u_ref, w_ref, sc_ref, prev_ref, o_ref, pad_ref):
        del prev_ref
        _pool_kernel(u_ref, w_ref, sc_ref, o_ref, pad_ref, T=T, CH=CH)

    return pl.pallas_call(
        kern,
        grid=(n_seq,),
        in_specs=[pl.BlockSpec((T, C), lambda s: (blk0 + s, 0)),
                  pl.BlockSpec((C, C), lambda s: (0, 0)),
                  pl.BlockSpec((1, C), lambda s: (0, 0)),
                  pl.BlockSpec(memory_space=pl.ANY)],
        out_specs=pl.BlockSpec((T, C), lambda s: (blk0 + s, 0)),
        out_shape=jax.ShapeDtypeStruct(out.shape, F32),
        scratch_shapes=[pltpu.VMEM((T + 4 * POOL_MAX_HALF, C), F32)],
        input_output_aliases={3: 0},
        compiler_params=_params(("arbitrary",)),
        name="pool_mixer",
    )(p, w_bd, scale, out)


def _softmax_pv(s, v):
    m = jnp.max(s, axis=-1, keepdims=True)
    e = jnp.exp2(s - m)
    l = jnp.sum(e, axis=-1, keepdims=True)
    return _dot(e.astype(BF16), v) * (1.0 / l)


def _diff_kernel(*refs, Ts, Lc, lam_init):
    if Lc:
        lam_ref, sub_ref, q_ref, ks_ref, vs_ref, kc_ref, vc_ref, prev_ref, o_ref, kb, vt, s_a, s_b, m_a, m_b = refs
    else:
        lam_ref, sub_ref, q_ref, ks_ref, vs_ref, prev_ref, o_ref, kb, vt, s_a, s_b, m_a, m_b = refs
    del prev_ref
    i = pl.program_id(2)
    S = Ts + Lc
    dv = DIFF_V_DIM

    @pl.when(i == 0)
    def _():
        kb[0:Ts, :] = ks_ref[...].astype(BF16)
        vt[0:dv, 0:Ts] = vs_ref[...].T.astype(BF16)
        if Lc:
            kb[Ts:S, :] = kc_ref[...].astype(BF16)
            vt[0:dv, Ts:S] = vc_ref[...].T.astype(BF16)
        vt[dv:, :] = jnp.ones((vt.shape[0] - dv, S), BF16)
        s_b[...] = jnp.zeros(s_b.shape, F32)
        m_b[...] = jnp.zeros(m_b.shape, F32)

    lm = lam_ref[...]
    lam = (jnp.exp(jnp.sum(lm[0:1, :] * lm[1:2, :], axis=-1, keepdims=True))
           - jnp.exp(jnp.sum(lm[2:3, :] * lm[3:4, :], axis=-1, keepdims=True)) + lam_init)

    def stage(s_new, m_new, s_old, m_old):
        q = q_ref[...] * (DIFF_QK_DIM ** -0.5 * LOG2E)
        tq = q.shape[0]
        part = lax.broadcasted_iota(I32, q.shape, 1) // DIFF_QK_DIM
        qm = [jnp.where(part == u, q, 0.0).astype(BF16) for u in range(2)]
        top = [m_old[u][0:1, :] for u in range(2)]
        acc = [jnp.zeros((vt.shape[0], tq), F32) for _ in range(2)]
        run = [jnp.full((8, tq), -jnp.inf, F32) for _ in range(2)]
        for c0 in range(0, S, DIFF_KEYS):
            keys = slice(c0, c0 + DIFF_KEYS)
            for u in range(2):
                sc = _dot_nt(kb[keys, :], qm[u])
                s_new[u, keys, :] = sc
                for j in range(0, DIFF_KEYS, 8):
                    run[u] = jnp.maximum(run[u], sc[j:j + 8, :])
                e = jnp.exp2(s_old[u, keys, :] - top[u])
                acc[u] = acc[u] + _dot(vt[:, keys], e.astype(BF16))
        for u in range(2):
            m_new[u] = jnp.broadcast_to(jnp.max(run[u], axis=0, keepdims=True), m_new.shape[1:])
        o = (acc[0][0:dv, :] * (1.0 / acc[0][dv:dv + 1, :])
             - lam * (acc[1][0:dv, :] * (1.0 / acc[1][dv:dv + 1, :])))
        o_ref[...] = _rms(o.T) * sub_ref[...] * (1.0 - lam_init)

    @pl.when(i % 2 == 0)
    def _():
        stage(s_a, m_a, s_b, m_b)

    @pl.when(i % 2 == 1)
    def _():
        stage(s_b, m_b, s_a, m_a)


def _diff_ctx_kernel(lam_ref, sub_ref, q_ref, k_ref, v_ref, prev_ref, o_ref, *, lam_init, heads):
    del prev_ref
    lm = lam_ref[...]
    lam = (jnp.exp(jnp.sum(lm[0:1, :] * lm[1:2, :], axis=-1, keepdims=True))
           - jnp.exp(jnp.sum(lm[2:3, :] * lm[3:4, :], axis=-1, keepdims=True)) + lam_init)
    q = q_ref[...] * (DIFF_QK_DIM ** -0.5 * LOG2E)
    part = lax.broadcasted_iota(I32, q.shape, 1) // DIFF_QK_DIM
    k = k_ref[...].astype(BF16)
    v = v_ref[...].astype(BF16)
    for hh in range(heads):
        o1 = _softmax_pv(_dot_nt(jnp.where(part == 2 * hh, q, 0.0).astype(BF16), k), v)
        o2 = _softmax_pv(_dot_nt(jnp.where(part == 2 * hh + 1, q, 0.0).astype(BF16), k), v)
        cols = slice(hh * DIFF_V_DIM, (hh + 1) * DIFF_V_DIM)
        o = o1[:, cols] - lam * o2[:, cols]
        o_ref[:, cols] = _rms(o) * sub_ref[...] * (1.0 - lam_init)


def _diff_attention_ctx(p, lam_l, subln, out, *, n_seq, T, lam_init):
    heads = 2
    W = heads * DIFF_V_DIM
    spec = lambda col0: pl.BlockSpec((T, W), lambda b, h: (b, col0 // W + h))
    return pl.pallas_call(
        functools.partial(_diff_ctx_kernel, lam_init=lam_init, heads=heads),
        grid=(n_seq, DIFF_HEADS // heads),
        in_specs=[pl.BlockSpec((4, DIFF_QK_DIM), lambda b, h: (0, 0)),
                  pl.BlockSpec((1, DIFF_V_DIM), lambda b, h: (0, 0)),
                  spec(COL_DQ), spec(COL_DK), spec(COL_DV),
                  pl.BlockSpec(memory_space=pl.ANY)],
        out_specs=pl.BlockSpec((T, W), lambda b, h: (b, h)),
        out_shape=jax.ShapeDtypeStruct(out.shape, F32),
        input_output_aliases={5: 0},
        compiler_params=_params(("arbitrary", "arbitrary")),
        name="diff_attention_ctx",
    )(lam_l, subln, p, p, p, out)


def _diff_attention(p, lam_l, subln, out, *, row0, n_seq, T, tq, lam_init, cache_k=None, cache_v=None, l=0):
    W = DIFF_V_DIM
    assert 2 * DIFF_QK_DIM == W == LANES
    Lc = 0 if cache_k is None else cache_k.shape[2]
    S = T + Lc
    assert S % DIFF_KEYS == 0
    nq = T // tq
    qb0 = row0 // tq
    sb0 = row0 // T
    in_specs = [pl.BlockSpec((4, DIFF_QK_DIM), lambda b, h, i: (0, 0)),
                pl.BlockSpec((1, W), lambda b, h, i: (0, 0)),
                pl.BlockSpec((tq, W), lambda b, h, i: (qb0 + b * nq + jnp.minimum(i, nq - 1), COL_DQ // W + h)),
                pl.BlockSpec((T, W), lambda b, h, i: (sb0 + b, COL_DK // W + h)),
                pl.BlockSpec((T, W), lambda b, h, i: (sb0 + b, COL_DV // W + h))]
    args = [lam_l, subln, p, p, p]
    if Lc:
        in_specs += [pl.BlockSpec((None, None, Lc, W), lambda b, h, i: (b, l, 0, h)),
                     pl.BlockSpec((None, None, Lc, W), lambda b, h, i: (b, l, 0, h))]
        args += [cache_k, cache_v]
    in_specs.append(pl.BlockSpec(memory_space=pl.ANY))
    args.append(out)
    scores = pltpu.VMEM((2, S, tq), F32)
    row_max = pltpu.VMEM((2, 8, tq), F32)
    vt_rows = W + 16
    return pl.pallas_call(
        functools.partial(_diff_kernel, Ts=T, Lc=Lc, lam_init=lam_init),
        grid=(n_seq, DIFF_HEADS, nq + 1),
        in_specs=in_specs,
        out_specs=pl.BlockSpec((tq, W), lambda b, h, i: (qb0 + b * nq + jnp.maximum(i - 1, 0), h)),
        out_shape=jax.ShapeDtypeStruct(out.shape, F32),
        scratch_shapes=[pltpu.VMEM((S, W), BF16), pltpu.VMEM((vt_rows, S), BF16), scores, scores, row_max, row_max],
        input_output_aliases={len(args) - 1: 0},
        compiler_params=_params(("arbitrary", "arbitrary", "arbitrary")),
        name="diff_attention",
    )(*args)


def _na_heads(q, score_fn, pv_fn):
    lane = lax.broadcasted_iota(I32, q.shape, 1) // NA_HEAD_DIM
    out = jnp.zeros(q.shape, F32)
    for h in range(NA_HEADS):
        qh = jnp.where(lane == h, q, 0.0).astype(BF16)
        out = jnp.where(lane == h, pv_fn(score_fn(qh, h)), out)
    return out


def _dense_kernel(q_ref, k_ref, v_ref, prev_ref, o_ref):
    del prev_ref
    q = q_ref[...] * (NA_HEAD_DIM ** -0.5 * LOG2E)
    k = k_ref[...].astype(BF16)
    v = v_ref[...].astype(BF16)
    o_ref[...] = _na_heads(q, lambda qh, h: _dot_nt(qh, k), lambda s: _softmax_pv(s, v))


def _dense_attention(p, out, *, n_seq, T):
    C = NA_HEADS * NA_HEAD_DIM
    return pl.pallas_call(
        _dense_kernel,
        grid=(n_seq,),
        in_specs=[pl.BlockSpec((T, C), lambda b: (b, 7)),
                  pl.BlockSpec((T, C), lambda b: (b, 8)),
                  pl.BlockSpec((T, C), lambda b: (b, 9)),
                  pl.BlockSpec(memory_space=pl.ANY)],
        out_specs=pl.BlockSpec((T, C), lambda b: (b, 0)),
        out_shape=jax.ShapeDtypeStruct(out.shape, F32),
        input_output_aliases={3: 0},
        compiler_params=_params(("arbitrary",)),
        name="dense_attention",
    )(p, p, p, out)


def _na_kernel(q_ref, ks_ref, vs_ref, kc_ref, vc_ref, bias_ref, prev_ref, o_ref, *, rows):
    del prev_ref
    r0 = pl.program_id(1) * NA_Q_ROWS
    bs = jnp.clip(r0 - NA_WIN_H // 2, 0, rows - NA_BAND)
    start = pl.multiple_of(bs * GRID_W, GRID_W)
    nb = NA_BAND * GRID_W
    kb = ks_ref[pl.ds(start, nb), :].astype(BF16)
    vb = vs_ref[pl.ds(start, nb), :].astype(BF16)
    kc = kc_ref[...].astype(BF16)
    vc = vc_ref[...].astype(BF16)
    q = q_ref[...] * (NA_HEAD_DIM ** -0.5 * LOG2E)

    def scores(qh, h):
        return _dot_nt(qh, kb) + bias_ref[h], _dot_nt(qh, kc)

    def pv(s):
        s_loc, s_ctx = s
        m = jnp.maximum(jnp.max(s_loc, axis=-1, keepdims=True), jnp.max(s_ctx, axis=-1, keepdims=True))
        e_loc = jnp.exp2(s_loc - m)
        e_ctx = jnp.exp2(s_ctx - m)
        l = jnp.sum(e_loc, axis=-1, keepdims=True) + jnp.sum(e_ctx, axis=-1, keepdims=True)
        return (_dot(e_loc.astype(BF16), vb) + _dot(e_ctx.astype(BF16), vc)) * (1.0 / l)

    o_ref[...] = _na_heads(q, scores, pv)


def _na_bias_table(rpb_l, rows):
    n_ro, n_co = 2 * NA_WIN_H - 1, 2 * NA_WIN_W - 1
    c = np.arange(GRID_W)[:, None]
    kc = np.arange(GRID_W)[None, :]
    cs = np.clip(c - NA_WIN_W // 2, 0, GRID_W - NA_WIN_W)
    col_ok = (kc >= cs) & (kc < cs + NA_WIN_W)
    co = kc - c + (NA_WIN_W - 1)
    pick = ((np.arange(n_co)[:, None, None] == co[None]) & col_ok[None]).astype(np.float32)
    toep = jnp.dot(rpb_l.reshape(NA_HEADS * n_ro, n_co).astype(F32), jnp.asarray(pick.reshape(n_co, -1)),
                   precision=lax.Precision.HIGHEST).reshape(NA_HEADS, n_ro, GRID_W, GRID_W) * LOG2E
    toep = jnp.where(jnp.asarray(col_ok)[None, None], toep, MASKED)
    toep = jnp.concatenate([toep, jnp.full((NA_HEADS, 1, GRID_W, GRID_W), MASKED, F32)], axis=1)
    blk = np.full((3, NA_Q_ROWS, NA_BAND), n_ro, np.int32)
    for v, r0 in enumerate((0, NA_Q_ROWS, rows - NA_Q_ROWS)):
        bs = int(np.clip(r0 - NA_WIN_H // 2, 0, rows - NA_BAND))
        for j in range(NA_Q_ROWS):
            rs = int(np.clip(r0 + j - NA_WIN_H // 2, 0, rows - NA_WIN_H))
            for i in range(NA_BAND):
                if rs <= bs + i < rs + NA_WIN_H:
                    blk[v, j, i] = bs + i - (r0 + j) + (NA_WIN_H - 1)
    tab = toep[:, blk]
    tab = jnp.transpose(tab, (1, 0, 2, 4, 3, 5))
    return tab.reshape(3, NA_HEADS, NA_Q_ROWS * GRID_W, NA_BAND * GRID_W)


def _neighbourhood_attention(p, cache_k, cache_v, bias, out, *, row0, n_seq, T, l):
    C = NA_HEADS * NA_HEAD_DIM
    rows = T // GRID_W
    tq = NA_Q_ROWS * GRID_W
    nq = T // tq
    Lc = cache_k.shape[2]
    qb0 = row0 // tq
    sb0 = row0 // T

    def variant(i):
        r0 = i * NA_Q_ROWS
        return (r0 - jnp.clip(r0 - NA_WIN_H // 2, 0, rows - NA_BAND)) // NA_Q_ROWS

    return pl.pallas_call(
        functools.partial(_na_kernel, rows=rows),
        grid=(n_seq, nq),
        in_specs=[pl.BlockSpec((tq, C), lambda b, i: (qb0 + b * nq + i, 7)),
                  pl.BlockSpec((T, C), lambda b, i: (sb0 + b, 8)),
                  pl.BlockSpec((T, C), lambda b, i: (sb0 + b, 9)),
                  pl.BlockSpec((None, None, Lc, C), lambda b, i: (b, l, 0, 0)),
                  pl.BlockSpec((None, None, Lc, C), lambda b, i: (b, l, 0, 0)),
                  pl.BlockSpec((None, NA_HEADS, tq, NA_BAND * GRID_W), lambda b, i: (variant(i), 0, 0, 0)),
                  pl.BlockSpec(memory_space=pl.ANY)],
        out_specs=pl.BlockSpec((tq, C), lambda b, i: (qb0 + b * nq + i, 0)),
        out_shape=jax.ShapeDtypeStruct(out.shape, F32),
        input_output_aliases={6: 0},
        compiler_params=_params(("arbitrary", "arbitrary")),
        name="neighbourhood_attention",
    )(p, p, p, cache_k, cache_v, bias, out)


def _merge_kernel(pool_ref, dn_ref, na_ref, x_ref, mod_ref, g1_ref, g2_ref, w_ref, rw_ref, rb_ref,
                  x1_ref, h2_ref, route_ref, cnt_ref, *, D):
    c0 = pool_ref.shape[1]
    c1 = c0 + dn_ref.shape[1]
    mix = (_dot(pool_ref[...].astype(BF16), w_ref[0:c0, :])
           + _dot(dn_ref[...].astype(BF16), w_ref[c0:c1, :])
           + _dot(na_ref[...].astype(BF16), w_ref[c1:, :]))
    x1 = x_ref[...] + mod_ref[:, 2 * D:3 * D] * (_rms(mix) * g1_ref[...])
    x1_ref[...] = x1
    h2 = (_rms(x1) * g2_ref[...]) * (1.0 + mod_ref[:, 4 * D:5 * D]) + mod_ref[:, 3 * D:4 * D]
    h2_ref[...] = h2

    logits = _dot3(h2, rw_ref[...]) + rb_ref[...]
    tm = logits.shape[0]
    lane = lax.broadcasted_iota(I32, (tm, LANES), 1)
    lane_f = lane.astype(F32)
    work = logits
    vals, hots = [], []
    for _ in range(TOP_K):
        mx = jnp.max(work, axis=-1, keepdims=True)
        idx = jnp.min(jnp.where(work == mx, lane_f, float(LANES)), axis=-1, keepdims=True)
        hot = lane_f == idx
        vals.append(mx)
        hots.append(hot)
        work = jnp.where(hot, -jnp.inf, work)
    es = [jnp.exp(v - vals[0]) for v in vals]
    inv = 1.0 / (es[0] + es[1] + es[2] + es[3])

    sel = jnp.zeros((tm, LANES), F32)
    for hot in hots:
        sel = jnp.where(hot, 1.0, sel)
    r = lax.broadcasted_iota(I32, (TOK_CHUNK, TOK_CHUNK), 0)
    c = lax.broadcasted_iota(I32, (TOK_CHUNK, TOK_CHUNK), 1)
    before = jnp.where(c < r, 1.0, 0.0).astype(BF16)
    er = lax.broadcasted_iota(I32, (LANES, LANES), 0)
    ec = lax.broadcasted_iota(I32, (LANES, LANES), 1)
    earlier = jnp.where(er < ec, 1.0, 0.0).astype(BF16)
    pos = []
    for j in range(tm // TOK_CHUNK):
        sel_j = sel[j * TOK_CHUNK:(j + 1) * TOK_CHUNK, :]
        rank = _dot(before, sel_j.astype(BF16))
        cnt = jnp.sum(sel_j, axis=0, keepdims=True)
        cnt_ref[j] = cnt
        run = jnp.floor((cnt + (RUN_ALIGN - 1)) * (1.0 / RUN_ALIGN)) * RUN_ALIGN
        run_start = _dot(jnp.broadcast_to(run, (8, LANES)).astype(BF16), earlier)[0:1, :]
        pos.append(rank + run_start)
    pos = jnp.concatenate(pos, axis=0)

    route = jnp.zeros((tm, LANES), F32)
    for k in range(TOP_K):
        e_k = jnp.sum(jnp.where(hots[k], lane_f, 0.0), axis=-1, keepdims=True)
        p_k = jnp.sum(jnp.where(hots[k], pos, 0.0), axis=-1, keepdims=True)
        route = jnp.where(lane == k, e_k, route)
        route = jnp.where(lane == TOP_K + k, p_k, route)
        route = jnp.where(lane == 2 * TOP_K + k, es[k] * inv, route)
    route_ref[...] = route


def _merge_route(pool_o, dn, na_o, x, mod4, g1, g2, w_out_bf, rw_pad, rb_pad, *, tm, ctx_blocks,
                 blocks_per_seq, l):
    N, D = x.shape
    per_step = tm // TOK_CHUNK

    def grp(i):
        return jnp.where(i < ctx_blocks, 0, 1 + (i - ctx_blocks) // blocks_per_seq)

    row = lambda w: pl.BlockSpec((tm, w), lambda i: (i, 0))
    full = lambda a: pl.BlockSpec(a.shape, lambda i: (0,) * a.ndim)
    return pl.pallas_call(
        functools.partial(_merge_kernel, D=D),
        grid=(N // tm,),
        in_specs=[row(pool_o.shape[1]), row(dn.shape[1]), row(na_o.shape[1]), row(D),
                  pl.BlockSpec((None, None, 1, N_MOD * D), lambda i: (l, grp(i), 0, 0)),
                  full(g1), full(g2), full(w_out_bf), full(rw_pad), full(rb_pad)],
        out_specs=[row(D), row(D), row(LANES), pl.BlockSpec((per_step, 1, LANES), lambda i: (i, 0, 0))],
        out_shape=[jax.ShapeDtypeStruct((N, D), F32), jax.ShapeDtypeStruct((N, D), F32),
                   jax.ShapeDtypeStruct((N, LANES), F32), jax.ShapeDtypeStruct((N // TOK_CHUNK, 1, LANES), F32)],
        compiler_params=_params(("arbitrary",)),
        name="merge_route",
    )(pool_o, dn, na_o, x, mod4, g1, g2, w_out_bf, rw_pad, rb_pad)


def _pack_pairs(x):
    C = x.shape[1] // 2
    bits = lax.bitcast_convert_type(x.astype(BF16).astype(F32), jnp.uint32)
    return bits[:, C:] | (bits[:, :C] >> 16)


def _unpack_pairs(w):
    lo = lax.bitcast_convert_type(w << 16, F32)
    hi = lax.bitcast_convert_type(w & jnp.uint32(0xFFFF0000), F32)
    return jnp.concatenate([lo, hi], axis=1).astype(BF16)


def _run_sizes(lo, hi):
    return [1 << k for k in range(hi.bit_length() - 1, lo.bit_length() - 2, -1)]


def _run_copies(n, src_at, dst_at, sem, wait, sizes):
    for size in sizes:
        @pl.when((n & size) != 0)
        def _(size=size):
            off = n & -(2 * size)
            cp = pltpu.make_async_copy(src_at(off, size), dst_at(off, size), sem)
            cp.wait() if wait else cp.start()


def _block_runs(step, n_exp, run_ref, src_ref, dst_ref, large_ref, hbm_ref, buf, sem, *, to_hbm, wait):
    def each_run(sizes):
        def body(e, carry):
            j = step * n_exp + e
            so = src_ref[j]
            do = dst_ref[j]
            in_buf = lambda o, s: buf.at[pl.ds(pl.multiple_of(so + o, RUN_ALIGN), s)]
            in_hbm = lambda o, s: hbm_ref.at[pl.ds(pl.multiple_of(do + o, RUN_ALIGN), s)]
            if to_hbm:
                _run_copies(run_ref[j], in_buf, in_hbm, sem, wait, sizes)
            else:
                _run_copies(run_ref[j], in_hbm, in_buf, sem, wait, sizes)
            return carry
        lax.fori_loop(0, n_exp, body, 0, unroll=RUN_UNROLL)

    each_run(_run_sizes(RUN_ALIGN, RUN_LARGE // 2))

    @pl.when(large_ref[step] != 0)
    def _():
        each_run(_run_sizes(RUN_LARGE, TOK_CHUNK))


def _dispatch_kernel(run_ref, src_ref, dst_ref, large_ref, tail_ref, taildst_ref, nu_ref, route_ref, h_ref, xs_ref,
                     sorted_buf, zero_buf, sems, *, n_exp):
    b = pl.program_id(0)
    nb = pl.num_programs(0)
    slot = b % 2
    runs = functools.partial(_block_runs, n_exp=n_exp, run_ref=run_ref, src_ref=src_ref, dst_ref=dst_ref,
                             large_ref=large_ref, hbm_ref=xs_ref, to_hbm=True)

    @pl.when(b >= 2)
    def _():
        runs(b - 2, buf=sorted_buf.at[slot], sem=sems.at[slot], wait=True)

    route = route_ref[...]
    col = lax.broadcasted_iota(I32, (TOK_CHUNK, sorted_buf.shape[1]), 1).astype(F32)
    place = jnp.zeros(col.shape, F32)
    for k in range(TOP_K):
        place = jnp.where(col == route[:, TOP_K + k:TOP_K + k + 1], 1.0, place)
    srt = lax.dot_general(place.astype(BF16), h_ref[...].astype(BF16), (((0,), (0,)), ((), ())),
                          preferred_element_type=F32)
    sorted_buf[slot] = _pack_pairs(srt)
    runs(b, buf=sorted_buf.at[slot], sem=sems.at[slot], wait=False)

    @pl.when(b == nb - 1)
    def _():
        @pl.when(b >= 1)
        def _():
            runs(b - 1, buf=sorted_buf.at[1 - slot], sem=sems.at[1 - slot], wait=True)
        runs(b, buf=sorted_buf.at[slot], sem=sems.at[slot], wait=True)

        zero_buf[...] = jnp.zeros(zero_buf.shape, zero_buf.dtype)
        sem = sems.at[0]

        def each_tail(wait):
            def body(e, carry):
                do = taildst_ref[e]
                _run_copies(tail_ref[e],
                            lambda o, s: zero_buf.at[pl.ds(0, s)],
                            lambda o, s: xs_ref.at[pl.ds(pl.multiple_of(do + o, RUN_ALIGN), s)], sem, wait,
                            _run_sizes(RUN_ALIGN, MOE_ROWS // 2))
                return carry
            lax.fori_loop(0, n_exp, body, 0)

        def spare_block(wait):
            def body(i, carry):
                cp = pltpu.make_async_copy(zero_buf.at[pl.ds(0, MOE_ROWS)],
                                           xs_ref.at[pl.ds(pl.multiple_of(i * MOE_ROWS, MOE_ROWS), MOE_ROWS)], sem)
                cp.wait() if wait else cp.start()
                return carry
            lax.fori_loop(nu_ref[0], xs_ref.shape[0] // MOE_ROWS, body, 0)

        each_tail(False)
        spare_block(False)
        each_tail(True)
        spare_block(True)


def _dispatch(plan, n_used, route, h2, n_rows, n_exp):
    N, D = h2.shape
    C = D // 2
    grid_spec = pltpu.PrefetchScalarGridSpec(
        num_scalar_prefetch=7,
        grid=(N // TOK_CHUNK,),
        in_specs=[pl.BlockSpec((TOK_CHUNK, LANES), lambda i, *_: (i, 0)),
                  pl.BlockSpec((TOK_CHUNK, D), lambda i, *_: (i, 0))],
        out_specs=pl.BlockSpec(memory_space=pl.ANY),
        scratch_shapes=[pltpu.VMEM((2, _sorted_rows(n_exp), C), jnp.uint32),
                        pltpu.VMEM((max(TOK_CHUNK, MOE_ROWS), C), jnp.uint32),
                        pltpu.SemaphoreType.DMA((2,))],
    )
    return pl.pallas_call(
        functools.partial(_dispatch_kernel, n_exp=n_exp),
        grid_spec=grid_spec,
        out_shape=jax.ShapeDtypeStruct((n_rows, C), jnp.uint32),
        compiler_params=_params(("arbitrary",)),
        name="moe_dispatch",
    )(plan["run"], plan["src"], plan["dst"], plan["large"], plan["tail"], plan["tail_dst"], n_used, route, h2)


def _expert_kernel(be_ref, nu_ref, rows_ref, xs_ref, w1_ref, b1_ref, w2_ref, b2_ref, o_ref, w1b, w2b, *, F):
    del nu_ref
    i = pl.program_id(0)
    e = be_ref[i]
    prev = be_ref[jnp.maximum(i - 1, 0)]
    rows = rows_ref[i]
    half = MOE_ROWS // 2

    @pl.when((i == 0) | (e != prev))
    def _():
        w1b[...] = w1_ref[...].astype(BF16)
        w2b[...] = w2_ref[...].astype(BF16)

    def ffn(xw):
        hh = _dot(_unpack_pairs(xw), w1b[...]) + b1_ref[...]
        g = jnp.minimum(hh[:, 0:F], SWIGLU_LIMIT)
        u = jnp.clip(hh[:, F:2 * F], -SWIGLU_LIMIT, SWIGLU_LIMIT)
        a = (g * (1.0 / (1.0 + jnp.exp(-SWIGLU_ALPHA * g)))) * (u + 1.0)
        return _pack_pairs(_dot(a.astype(BF16), w2b[...]) + b2_ref[...])

    @pl.when(rows > half)
    def _():
        o_ref[...] = ffn(xs_ref[...])

    @pl.when((rows > 0) & (rows <= half))
    def _():
        o_ref[0:half, :] = ffn(xs_ref[0:half, :])
        o_ref[half:MOE_ROWS, :] = jnp.zeros((MOE_ROWS - half, o_ref.shape[1]), o_ref.dtype)

    @pl.when(rows == 0)
    def _():
        o_ref[...] = jnp.zeros(o_ref.shape, o_ref.dtype)


def _experts(block_e, n_used, block_rows, xs, w1, b1, w2, b2, *, l):
    R, C = xs.shape
    L, E, D, F2 = w1.shape
    F = F2 // 2
    nblk = R // MOE_ROWS
    grid_spec = pltpu.PrefetchScalarGridSpec(
        num_scalar_prefetch=3,
        grid=(nblk,),
        in_specs=[pl.BlockSpec((MOE_ROWS, C), lambda i, be, nu, br: (jnp.minimum(i, nu[0] - 1), 0)),
                  pl.BlockSpec((None, None, D, F2), lambda i, be, nu, br: (l, be[i], 0, 0)),
                  pl.BlockSpec((None, None, 1, F2), lambda i, be, nu, br: (l, be[i], 0, 0)),
                  pl.BlockSpec((None, None, F, D), lambda i, be, nu, br: (l, be[i], 0, 0)),
                  pl.BlockSpec((None, None, 1, D), lambda i, be, nu, br: (l, be[i], 0, 0))],
        out_specs=pl.BlockSpec((MOE_ROWS, C), lambda i, be, nu, br: (i, 0)),
        scratch_shapes=[pltpu.VMEM((D, F2), BF16), pltpu.VMEM((F, D), BF16)],
    )
    return pl.pallas_call(
        functools.partial(_expert_kernel, F=F),
        grid_spec=grid_spec,
        out_shape=jax.ShapeDtypeStruct((R, C), jnp.uint32),
        compiler_params=_params(("arbitrary",)),
        name="moe_experts",
    )(block_e, n_used, block_rows, xs, w1, b1.reshape(L, E, 1, F2), w2, b2.reshape(L, E, 1, D))


def _combine_kernel(run_ref, src_ref, dst_ref, large_ref, yb_ref, route_ref, x1_ref, mod_ref, g_ref, *rest, D,
                    n_exp, ctx_blocks, split):
    if split:
        ctx_ref, lat_ref, sorted_buf, sems = rest
    else:
        o_ref, sorted_buf, sems = rest
    b = pl.program_id(0)
    nb = pl.num_programs(0)
    slot = b % 2
    runs = functools.partial(_block_runs, n_exp=n_exp, run_ref=run_ref, src_ref=src_ref, dst_ref=dst_ref,
                             large_ref=large_ref, hbm_ref=yb_ref, to_hbm=False)

    @pl.when(b == 0)
    def _():
        sorted_buf[...] = jnp.zeros(sorted_buf.shape, sorted_buf.dtype)
        runs(b, buf=sorted_buf.at[slot], sem=sems.at[slot], wait=False)

    @pl.when(b + 1 < nb)
    def _():
        runs(b + 1, buf=sorted_buf.at[1 - slot], sem=sems.at[1 - slot], wait=False)

    runs(b, buf=sorted_buf.at[slot], sem=sems.at[slot], wait=True)
    route = route_ref[...]
    col = lax.broadcasted_iota(I32, (TOK_CHUNK, sorted_buf.shape[1]), 1).astype(F32)
    gate = jnp.zeros(col.shape, F32)
    for k in range(TOP_K):
        gate = jnp.where(col == route[:, TOP_K + k:TOP_K + k + 1], route[:, 2 * TOP_K + k:2 * TOP_K + k + 1], gate)
    g_hi, g_lo = _split(gate)
    yb = _unpack_pairs(sorted_buf[slot])
    y = _dot(g_hi, yb) + _dot(g_lo, yb)
    x2 = x1_ref[...] + mod_ref[:, 5 * D:6 * D] * (_rms(y) * g_ref[...])
    if split:
        @pl.when(b < ctx_blocks)
        def _():
            ctx_ref[...] = x2

        @pl.when(b >= ctx_blocks)
        def _():
            lat_ref[...] = x2
    else:
        o_ref[...] = x2


def _combine(plan, yb, route, x1, mod4, g3, *, n_exp, ctx_blocks, blocks_per_seq, l, split):
    N, D = x1.shape
    row = lambda f: pl.BlockSpec((TOK_CHUNK, D), lambda i, *_: (f(i), 0))
    if split:
        out_specs = [row(lambda i: jnp.minimum(i, ctx_blocks - 1)), row(lambda i: jnp.maximum(i - ctx_blocks, 0))]
        out_shape = [jax.ShapeDtypeStruct((ctx_blocks * TOK_CHUNK, D), F32),
                     jax.ShapeDtypeStruct((N - ctx_blocks * TOK_CHUNK, D), F32)]
    else:
        out_specs = row(lambda i: i)
        out_shape = jax.ShapeDtypeStruct((N, D), F32)

    def grp(i):
        return jnp.where(i < ctx_blocks, 0, 1 + (i - ctx_blocks) // blocks_per_seq)

    grid_spec = pltpu.PrefetchScalarGridSpec(
        num_scalar_prefetch=4,
        grid=(N // TOK_CHUNK,),
        in_specs=[pl.BlockSpec(memory_space=pl.ANY),
                  pl.BlockSpec((TOK_CHUNK, LANES), lambda i, *_: (i, 0)),
                  pl.BlockSpec((TOK_CHUNK, D), lambda i, *_: (i, 0)),
                  pl.BlockSpec((None, None, 1, N_MOD * D), lambda i, *_: (l, grp(i), 0, 0)),
                  pl.BlockSpec((1, D), lambda i, *_: (0, 0))],
        out_specs=out_specs,
        scratch_shapes=[pltpu.VMEM((2, _sorted_rows(n_exp), yb.shape[1]), jnp.uint32),
                        pltpu.SemaphoreType.DMA((2,))],
    )
    return pl.pallas_call(
        functools.partial(_combine_kernel, D=D, n_exp=n_exp, ctx_blocks=ctx_blocks, split=split),
        grid_spec=grid_spec,
        out_shape=out_shape,
        compiler_params=_params(("arbitrary",)),
        name="moe_combine",
    )(plan["run"], plan["src"], plan["dst"], plan["large"], yb, route, x1, mod4, g3)


def _rope_tables(Ts, tm):
    nf = DIFF_QK_DIM // 4
    inv = ROPE_BASE ** (-jnp.arange(nf, dtype=F32) / nf)
    t = jnp.arange(Ts)
    pos = jnp.stack([(t // GRID_W).astype(F32), (t % GRID_W).astype(F32)], axis=1)
    ang = pos[:, :, None] * inv[None, None, :]
    cos = jnp.repeat(jnp.cos(ang)[:, :, None, :], 2, axis=2).reshape(Ts, DIFF_QK_DIM)
    sin = jnp.sin(ang)
    sin = jnp.stack([-sin, sin], axis=2).reshape(Ts, DIFF_QK_DIM)
    reps = LANES // DIFF_QK_DIM
    cos = jnp.concatenate([jnp.ones((tm, LANES), F32), jnp.tile(cos, (1, reps))], axis=0)
    sin = jnp.concatenate([jnp.zeros((tm, LANES), F32), jnp.tile(sin, (1, reps))], axis=0)
    return cos, sin


def _block_diag(w):
    G, a, b = w.shape
    out = jnp.zeros((G * a, G * b), w.dtype)
    for g in range(G):
        out = out.at[g * a:(g + 1) * a, g * b:(g + 1) * b].set(w[g])
    return out


def _route_plan(counts, n_exp, nblk):
    cnt = counts[:, 0, :n_exp].astype(I32)
    run = (cnt + RUN_ALIGN - 1) // RUN_ALIGN * RUN_ALIGN
    src = jnp.cumsum(run, axis=1) - run
    tot = jnp.sum(run, axis=0)
    region = (tot + MOE_ROWS - 1) // MOE_ROWS * MOE_ROWS
    region_end = jnp.cumsum(region)
    region_start = region_end - region
    dst = region_start[None, :] + jnp.cumsum(run, axis=0) - run
    n_used = (region_end[-1] // MOE_ROWS).astype(I32)
    blk = jnp.arange(nblk, dtype=I32) * MOE_ROWS
    block_e = jnp.minimum(jnp.sum((blk[:, None] >= region_end[None, :]).astype(I32), axis=1), n_exp - 1)
    last = jnp.sum(jnp.where(jnp.arange(nblk) == n_used - 1, block_e, 0))
    used = jnp.arange(nblk) < n_used
    block_rows = jnp.where(used, jnp.clip((region_start + tot)[block_e] - blk, 0, MOE_ROWS), 0).astype(I32)
    block_e = jnp.where(used, block_e, last).astype(I32)
    plan = dict(run=run.reshape(-1).astype(I32), src=src.reshape(-1).astype(I32), dst=dst.reshape(-1).astype(I32),
                large=jnp.any(run >= RUN_LARGE, axis=1).astype(I32),
                tail=(region - tot).astype(I32), tail_dst=(region_start + tot).astype(I32))
    return plan, block_e, n_used.reshape(1), block_rows


def kernel(x_prompt, x_sample, cache_diff_k, cache_diff_v, cache_na_k, cache_na_v, c, c_ctx, w_ada, b_ada,
           norm_gain, w_in, w_out, pool_w, pool_scale, diff_lambda, diff_subln, na_rpb, router_w, router_b,
           moe_w1, moe_b1, moe_w2, moe_b2):
    Bp, Tp, D = x_prompt.shape
    Bs, Ts, _ = x_sample.shape
    L = w_ada.shape[0]
    E = router_w.shape[-1]
    Np, Ns = Bp * Tp, Bs * Ts
    N = Np + Ns
    tm = TOK_CHUNK
    assert Np % Ts == 0 or Bs == 0, "context rows must be a whole number of latent-sequence blocks"
    assert Np % tm == 0 and Ts % tm == 0 and Ts % (NA_Q_ROWS * GRID_W) == 0
    assert Ts // GRID_W >= NA_BAND + NA_Q_ROWS
    ctx_blocks, blocks_per_seq = Np // tm, Ts // tm

    G = 16
    cvec = jnp.zeros((G, D), F32).at[0].set(c_ctx).at[1:1 + Bs].set(c)
    mod4 = _modulation(cvec, w_ada, b_ada).reshape(L, G, 1, N_MOD * D)
    tp = PROJ_ROWS
    assert Np % tp == 0 and Ts % tp == 0
    cos_t, sin_t = _rope_tables(Ts, tp)
    w_in_bf = w_in.astype(BF16)
    w_out_bf = w_out.astype(BF16)
    rw_pad = jnp.zeros((L, D, LANES), F32).at[:, :, :E].set(router_w)
    rb_pad = jnp.full((L, 1, LANES), MASKED, F32).at[:, 0, :E].set(router_b)
    ck = cache_diff_k.reshape(Bs, L, -1, DIFF_HEADS * 2 * DIFF_QK_DIM)
    cv = cache_diff_v.reshape(Bs, L, -1, DIFF_HEADS * DIFF_V_DIM)
    nk = cache_na_k.reshape(Bs, L, -1, NA_HEADS * NA_HEAD_DIM)
    nv = cache_na_v.reshape(Bs, L, -1, NA_HEADS * NA_HEAD_DIM)
    nb = N // TOK_CHUNK
    nblk = -(-(N * TOP_K + nb * E * (RUN_ALIGN - 1) + E * (MOE_ROWS - 1)) // MOE_ROWS)
    blocks = dict(ctx_blocks=ctx_blocks, blocks_per_seq=blocks_per_seq)

    x = jnp.concatenate([x_prompt.reshape(Np, D), x_sample.reshape(Ns, D)], axis=0)
    new_dk, new_dv, new_nk, new_nv = [], [], [], []
    for l in range(L):
        lam_init = 0.8 - 0.6 * math.exp(-0.3 * l)
        g = norm_gain[l]
        p, dk_l, dv_l, nk_l, nv_l = _in_projection(x, mod4, g[0:1], w_in_bf[l], cos_t, sin_t, tm=tp, l=l,
                                                   ctx_blocks=Np // tp, blocks_per_seq=Ts // tp)

        pw = _block_diag(pool_w[l]).astype(BF16)
        ps = pool_scale[l].reshape(1, -1)
        pool_o = jnp.zeros((N, pw.shape[0]), F32)
        pool_o = _pool(p, pw, ps, row0=0, n_seq=Bp, T=Tp, out=pool_o)
        pool_o = _pool(p, pw, ps, row0=Np, n_seq=Bs, T=Ts, out=pool_o)

        sub = diff_subln[l].reshape(1, -1)
        dn = jnp.zeros((N, DIFF_HEADS * DIFF_V_DIM), F32)
        dn = _diff_attention_ctx(p, diff_lambda[l], sub, dn, n_seq=Bp, T=Tp, lam_init=lam_init)
        dn = _diff_attention(p, diff_lambda[l], sub, dn, row0=Np, n_seq=Bs, T=Ts, tq=DIFF_Q_ROWS, lam_init=lam_init,
                             cache_k=ck, cache_v=cv, l=l)

        na_o = jnp.zeros((N, NA_HEADS * NA_HEAD_DIM), F32)
        na_o = _dense_attention(p, na_o, n_seq=Bp, T=Tp)
        bias = _na_bias_table(na_rpb[l], Ts // GRID_W)
        na_o = _neighbourhood_attention(p, nk, nv, bias, na_o, row0=Np, n_seq=Bs, T=Ts, l=l)

        x1, h2, route, counts = _merge_route(pool_o, dn, na_o, x, mod4, g[1:2], g[2:3], w_out_bf[l],
                                             rw_pad[l], rb_pad[l], tm=tp, l=l,
                                             ctx_blocks=Np // tp, blocks_per_seq=Ts // tp)
        plan, block_e, n_used, block_rows = _route_plan(counts, E, nblk)
        xs = _dispatch(plan, n_used, route, h2, nblk * MOE_ROWS, E)
        yb = _experts(block_e, n_used, block_rows, xs, moe_w1, moe_b1, moe_w2, moe_b2, l=l)
        x = _combine(plan, yb, route, x1, mod4, g[3:4], n_exp=E, l=l, split=l == L - 1, **blocks)

        new_dk.append(dk_l.reshape(Bp, Tp, DIFF_HEADS, 2 * DIFF_QK_DIM))
        new_dv.append(dv_l.reshape(Bp, Tp, DIFF_HEADS, DIFF_V_DIM))
        new_nk.append(nk_l.reshape(Bp, Tp, NA_HEADS, NA_HEAD_DIM))
        new_nv.append(nv_l.reshape(Bp, Tp, NA_HEADS, NA_HEAD_DIM))

    return (x[0].reshape(Bp, Tp, D), x[1].reshape(Bs, Ts, D),
            jnp.stack(new_dk, axis=1), jnp.stack(new_dv, axis=1),
            jnp.stack(new_nk, axis=1), jnp.stack(new_nv, axis=1))
```

```python
import functools
import math

import numpy as np
import jax
import jax.numpy as jnp
from jax import lax
from jax.experimental import pallas as pl
from jax.experimental.pallas import tpu as pltpu

F32 = jnp.float32
BF16 = jnp.bfloat16
I32 = jnp.int32

GRID_W = 64
POOL_GROUPS = 4
POOL_MAX_HALF = 8
DIFF_HEADS = 4
DIFF_V_DIM = 128
DIFF_QK_DIM = 64
NA_HEADS = 4
NA_HEAD_DIM = 64
NA_WIN_H = 8
NA_WIN_W = 16
NA_Q_ROWS = 4
NA_BAND = 12
TOP_K = 4
SWIGLU_ALPHA = 1.702
SWIGLU_LIMIT = 7.0
ROPE_BASE = 10000.0
NORM_EPS = 1e-6
N_MOD = 6

LOG2E = 1.4426950408889634
MASKED = -1e30
LANES = 128
MXU_TILE = 256
MOE_ROWS = 512
COL_DQ, COL_DK, COL_DV = 256, 768, 1280
COL_NQ, COL_NK, COL_NV = 1792, 2048, 2304
DIFF_KEYS = 256
PROJ_ROWS = 512
MERGE_ROWS = 1024
DIFF_Q_ROWS = 256
TOK_CHUNK = 256
RUN_ALIGN = 8
RUN_LARGE = 128
RUN_UNROLL = 4
VMEM_LIMIT = 56 * 1024 * 1024


def _sorted_rows(n_exp):
    rows = TOK_CHUNK * TOP_K + n_exp * (RUN_ALIGN - 1)
    return -(-rows // MXU_TILE) * MXU_TILE


def _params(sem, vmem=VMEM_LIMIT):
    return pltpu.CompilerParams(dimension_semantics=sem, vmem_limit_bytes=vmem)


def _dot(a, b):
    return jnp.dot(a, b, preferred_element_type=F32)


def _dot_nt(a, b):
    return lax.dot_general(a, b, (((1,), (1,)), ((), ())), preferred_element_type=F32)


def _split(x):
    hi = x.astype(BF16)
    return hi, (x - hi.astype(F32)).astype(BF16)


def _dot3(a, b):
    ah, al = _split(a)
    bh, bl = _split(b)
    return _dot(ah, bh) + _dot(al, bh) + _dot(ah, bl)


def _rms(x):
    return x * lax.rsqrt(jnp.mean(x * x, axis=-1, keepdims=True) + NORM_EPS)


def _mod_kernel(c_ref, w_ref, b_ref, o_ref):
    c = c_ref[...]
    a = c * (1.0 / (1.0 + jnp.exp(-c)))
    o_ref[...] = _dot3(a, w_ref[...]) + b_ref[...]


def _modulation(cvec, w_ada, b_ada):
    L, D, W = w_ada.shape
    G = cvec.shape[0]
    return pl.pallas_call(
        _mod_kernel,
        grid=(L, W // D),
        in_specs=[pl.BlockSpec((G, D), lambda l, j: (0, 0)),
                  pl.BlockSpec((None, D, D), lambda l, j: (l, 0, j)),
                  pl.BlockSpec((None, 1, D), lambda l, j: (l, 0, j))],
        out_specs=pl.BlockSpec((None, G, D), lambda l, j: (l, 0, j)),
        out_shape=jax.ShapeDtypeStruct((L, G, W), F32),
        compiler_params=_params(("arbitrary", "arbitrary")),
        name="ada_modulation",
    )(cvec, w_ada, b_ada.reshape(L, 1, W))


def _inproj_kernel(x_ref, mod_ref, g_ref, w_ref, cos_ref, sin_ref, o_ref, dk_ref, dv_ref, nk_ref, nv_ref, *,
                   D, rope_lo, rope_hi, ctx_blocks):
    h = _rms(x_ref[...]) * g_ref[...]
    h = h * (1.0 + mod_ref[:, D:2 * D]) + mod_ref[:, 0:D]
    p = _dot(h.astype(BF16), w_ref[...])
    W = p.shape[1]
    o_ref[:, 0:rope_lo] = p[:, 0:rope_lo]
    o_ref[:, rope_hi:W] = p[:, rope_hi:W]
    cos = cos_ref[...]
    sin = sin_ref[...]
    lane = lax.broadcasted_iota(I32, cos.shape, 1)
    first = (lane % 32) < 16
    for c0 in range(rope_lo, rope_hi, LANES):
        xc = p[:, c0:c0 + LANES]
        partner = jnp.where(first, pltpu.roll(xc, LANES - 16, 1), pltpu.roll(xc, 16, 1))
        o_ref[:, c0:c0 + LANES] = xc * cos + partner * sin

    @pl.when(pl.program_id(0) < ctx_blocks)
    def _():
        dk_ref[...] = o_ref[:, COL_DK:COL_DV]
        dv_ref[...] = o_ref[:, COL_DV:COL_NQ]
        nk_ref[...] = o_ref[:, COL_NK:COL_NV]
        nv_ref[...] = o_ref[:, COL_NV:W]


def _in_projection(x, mod4, gain, w_bf, cos_t, sin_t, *, tm, ctx_blocks, blocks_per_seq, l):
    N, D = x.shape
    W = w_bf.shape[1]

    def grp(i):
        return jnp.where(i < ctx_blocks, 0, 1 + (i - ctx_blocks) // blocks_per_seq)

    def rope_blk(i):
        return jnp.where(i < ctx_blocks, 0, 1 + (i - ctx_blocks) % blocks_per_seq)

    kern = functools.partial(_inproj_kernel, D=D, rope_lo=COL_DQ, rope_hi=COL_DV, ctx_blocks=ctx_blocks)
    ctx_rows = ctx_blocks * tm
    ctx_out = lambda w: pl.BlockSpec((tm, w), lambda i: (jnp.minimum(i, ctx_blocks - 1), 0))
    widths = (COL_DV - COL_DK, COL_NQ - COL_DV, COL_NV - COL_NK, W - COL_NV)
    return pl.pallas_call(
        kern,
        grid=(N // tm,),
        in_specs=[pl.BlockSpec((tm, D), lambda i: (i, 0)),
                  pl.BlockSpec((None, None, 1, N_MOD * D), lambda i: (l, grp(i), 0, 0)),
                  pl.BlockSpec((1, D), lambda i: (0, 0)),
                  pl.BlockSpec((D, W), lambda i: (0, 0)),
                  pl.BlockSpec((tm, LANES), lambda i: (rope_blk(i), 0)),
                  pl.BlockSpec((tm, LANES), lambda i: (rope_blk(i), 0))],
        out_specs=[pl.BlockSpec((tm, W), lambda i: (i, 0))] + [ctx_out(w) for w in widths],
        out_shape=[jax.ShapeDtypeStruct((N, W), F32)] + [jax.ShapeDtypeStruct((ctx_rows, w), F32) for w in widths],
        compiler_params=_params(("arbitrary",)),
        name="in_projection",
    )(x, mod4, gain, w_bf, cos_t, sin_t)


def _pool_kernel(u_ref, w_ref, sc_ref, o_ref, pad_ref, *, T, CH):
    H = 2 * POOL_MAX_HALF
    zeros = jnp.zeros((H, pad_ref.shape[1]), F32)
    pad_ref[0:H, :] = zeros
    pad_ref[H + T:2 * H + T, :] = zeros
    pad_ref[H:H + T, :] = u_ref[...]
    C = pad_ref.shape[1]
    lane = lax.broadcasted_iota(I32, (CH, C), 1)
    half = jnp.left_shift(1, lane // (C // POOL_GROUPS))
    row = lax.broadcasted_iota(I32, (CH, C), 0)

    def body(ci, carry):
        base = pl.multiple_of(ci * CH, CH)
        win = pad_ref[pl.ds(base + POOL_MAX_HALF, CH + H), :]
        acc = jnp.zeros((CH, C), F32)
        for j in range(-POOL_MAX_HALF, POOL_MAX_HALF):
            sl = win[POOL_MAX_HALF + j:POOL_MAX_HALF + j + CH, :]
            inside = (half > j) if j >= 0 else (half >= -j)
            acc = acc + jnp.where(inside, sl, 0.0)
        t = row + base
        cnt = jnp.minimum(t + half, T) - jnp.maximum(t - half, 0)
        d = acc / cnt.astype(F32) - win[POOL_MAX_HALF:POOL_MAX_HALF + CH, :]
        o_ref[pl.ds(base, CH), :] = _dot(d.astype(BF16), w_ref[...]) * sc_ref[...]
        return carry

    lax.fori_loop(0, T // CH, body, 0)


def _pool(p, w_bd, scale, *, row0, n_seq, T, out):
    C = w_bd.shape[0]
    CH = min(T, 256)
    blk0 = row0 // T

    def kern(u_ref, w_ref, sc_ref, prev_ref, o_ref, pad_ref):
        del prev_ref
        _pool_kernel(u_ref, w_ref, sc_ref, o_ref, pad_ref, T=T, CH=CH)

    return pl.pallas_call(
        kern,
        grid=(n_seq,),
        in_specs=[pl.BlockSpec((T, C), lambda s: (blk0 + s, 0)),
                  pl.BlockSpec((C, C), lambda s: (0, 0)),
                  pl.BlockSpec((1, C), lambda s: (0, 0)),
                  pl.BlockSpec(memory_space=pl.ANY)],
        out_specs=pl.BlockSpec((T, C), lambda s: (blk0 + s, 0)),
        out_shape=jax.ShapeDtypeStruct(out.shape, F32),
        scratch_shapes=[pltpu.VMEM((T + 4 * POOL_MAX_HALF, C), F32)],
        input_output_aliases={3: 0},
        compiler_params=_params(("arbitrary",)),
        name="pool_mixer",
    )(p, w_bd, scale, out)


def _softmax_pv(s, v):
    m = jnp.max(s, axis=-1, keepdims=True)
    e = jnp.exp2(s - m)
    l = jnp.sum(e, axis=-1, keepdims=True)
    return _dot(e.astype(BF16), v) * (1.0 / l)


def _diff_kernel(*refs, Ts, Lc, lam_init):
    if Lc:
        lam_ref, sub_ref, q_ref, ks_ref, vs_ref, kc_ref, vc_ref, prev_ref, o_ref, kb, vt, s_a, s_b, m_a, m_b = refs
    else:
        lam_ref, sub_ref, q_ref, ks_ref, vs_ref, prev_ref, o_ref, kb, vt, s_a, s_b, m_a, m_b = refs
    del prev_ref
    i = pl.program_id(2)
    S = Ts + Lc
    dv = DIFF_V_DIM

    @pl.when(i == 0)
    def _():
        kb[0:Ts, :] = ks_ref[...].astype(BF16)
        vt[0:dv, 0:Ts] = vs_ref[...].T.astype(BF16)
        if Lc:
            kb[Ts:S, :] = kc_ref[...].astype(BF16)
            vt[0:dv, Ts:S] = vc_ref[...].T.astype(BF16)
        vt[dv:, :] = jnp.ones((vt.shape[0] - dv, S), BF16)
        s_b[...] = jnp.zeros(s_b.shape, F32)
        m_b[...] = jnp.zeros(m_b.shape, F32)

    lm = lam_ref[...]
    lam = (jnp.exp(jnp.sum(lm[0:1, :] * lm[1:2, :], axis=-1, keepdims=True))
           - jnp.exp(jnp.sum(lm[2:3, :] * lm[3:4, :], axis=-1, keepdims=True)) + lam_init)

    def stage(s_new, m_new, s_old, m_old):
        q = q_ref[...] * (DIFF_QK_DIM ** -0.5 * LOG2E)
        tq = q.shape[0]
        part = lax.broadcasted_iota(I32, q.shape, 1) // DIFF_QK_DIM
        qm = [jnp.where(part == u, q, 0.0).astype(BF16) for u in range(2)]
        top = [m_old[u][0:1, :] for u in range(2)]
        acc = [jnp.zeros((vt.shape[0], tq), F32) for _ in range(2)]
        run = [jnp.full((8, tq), -jnp.inf, F32) for _ in range(2)]
        for c0 in range(0, S, DIFF_KEYS):
            keys = slice(c0, c0 + DIFF_KEYS)
            for u in range(2):
                sc = _dot_nt(kb[keys, :], qm[u])
                s_new[u, keys, :] = sc
                for j in range(0, DIFF_KEYS, 8):
                    run[u] = jnp.maximum(run[u], sc[j:j + 8, :])
                e = jnp.exp2(s_old[u, keys, :] - top[u])
                acc[u] = acc[u] + _dot(vt[:, keys], e.astype(BF16))
        for u in range(2):
            m_new[u] = jnp.broadcast_to(jnp.max(run[u], axis=0, keepdims=True), m_new.shape[1:])
        o = (acc[0][0:dv, :] * (1.0 / acc[0][dv:dv + 1, :])
             - lam * (acc[1][0:dv, :] * (1.0 / acc[1][dv:dv + 1, :])))
        o_ref[...] = _rms(o.T) * sub_ref[...] * (1.0 - lam_init)

    @pl.when(i % 2 == 0)
    def _():
        stage(s_a, m_a, s_b, m_b)

    @pl.when(i % 2 == 1)
    def _():
        stage(s_b, m_b, s_a, m_a)


def _diff_ctx_kernel(lam_ref, sub_ref, q_ref, k_ref, v_ref, prev_ref, o_ref, *, lam_init, heads):
    del prev_ref
    lm = lam_ref[...]
    lam = (jnp.exp(jnp.sum(lm[0:1, :] * lm[1:2, :], axis=-1, keepdims=True))
           - jnp.exp(jnp.sum(lm[2:3, :] * lm[3:4, :], axis=-1, keepdims=True)) + lam_init)
    q = q_ref[...] * (DIFF_QK_DIM ** -0.5 * LOG2E)
    part = lax.broadcasted_iota(I32, q.shape, 1) // DIFF_QK_DIM
    k = k_ref[...].astype(BF16)
    v = v_ref[...].astype(BF16)
    for hh in range(heads):
        o1 = _softmax_pv(_dot_nt(jnp.where(part == 2 * hh, q, 0.0).astype(BF16), k), v)
        o2 = _softmax_pv(_dot_nt(jnp.where(part == 2 * hh + 1, q, 0.0).astype(BF16), k), v)
        cols = slice(hh * DIFF_V_DIM, (hh + 1) * DIFF_V_DIM)
        o = o1[:, cols] - lam * o2[:, cols]
        o_ref[:, cols] = _rms(o) * sub_ref[...] * (1.0 - lam_init)


def _diff_attention_ctx(p, lam_l, subln, out, *, n_seq, T, lam_init):
    heads = 2
    W = heads * DIFF_V_DIM
    spec = lambda col0: pl.BlockSpec((T, W), lambda b, h: (b, col0 // W + h))
    return pl.pallas_call(
        functools.partial(_diff_ctx_kernel, lam_init=lam_init, heads=heads),
        grid=(n_seq, DIFF_HEADS // heads),
        in_specs=[pl.BlockSpec((4, DIFF_QK_DIM), lambda b, h: (0, 0)),
                  pl.BlockSpec((1, DIFF_V_DIM), lambda b, h: (0, 0)),
                  spec(COL_DQ), spec(COL_DK), spec(COL_DV),
                  pl.BlockSpec(memory_space=pl.ANY)],
        out_specs=pl.BlockSpec((T, W), lambda b, h: (b, h)),
        out_shape=jax.ShapeDtypeStruct(out.shape, F32),
        input_output_aliases={5: 0},
        compiler_params=_params(("arbitrary", "arbitrary")),
        name="diff_attention_ctx",
    )(lam_l, subln, p, p, p, out)


def _diff_attention(p, lam_l, subln, out, *, row0, n_seq, T, tq, lam_init, cache_k=None, cache_v=None, l=0):
    W = DIFF_V_DIM
    assert 2 * DIFF_QK_DIM == W == LANES
    Lc = 0 if cache_k is None else cache_k.shape[2]
    S = T + Lc
    assert S % DIFF_KEYS == 0
    nq = T // tq
    qb0 = row0 // tq
    sb0 = row0 // T
    in_specs = [pl.BlockSpec((4, DIFF_QK_DIM), lambda b, h, i: (0, 0)),
                pl.BlockSpec((1, W), lambda b, h, i: (0, 0)),
                pl.BlockSpec((tq, W), lambda b, h, i: (qb0 + b * nq + jnp.minimum(i, nq - 1), COL_DQ // W + h)),
                pl.BlockSpec((T, W), lambda b, h, i: (sb0 + b, COL_DK // W + h)),
                pl.BlockSpec((T, W), lambda b, h, i: (sb0 + b, COL_DV // W + h))]
    args = [lam_l, subln, p, p, p]
    if Lc:
        in_specs += [pl.BlockSpec((None, None, Lc, W), lambda b, h, i: (b, l, 0, h)),
                     pl.BlockSpec((None, None, Lc, W), lambda b, h, i: (b, l, 0, h))]
        args += [cache_k, cache_v]
    in_specs.append(pl.BlockSpec(memory_space=pl.ANY))
    args.append(out)
    scores = pltpu.VMEM((2, S, tq), F32)
    row_max = pltpu.VMEM((2, 8, tq), F32)
    vt_rows = W + 16
    return pl.pallas_call(
        functools.partial(_diff_kernel, Ts=T, Lc=Lc, lam_init=lam_init),
        grid=(n_seq, DIFF_HEADS, nq + 1),
        in_specs=in_specs,
        out_specs=pl.BlockSpec((tq, W), lambda b, h, i: (qb0 + b * nq + jnp.maximum(i - 1, 0), h)),
        out_shape=jax.ShapeDtypeStruct(out.shape, F32),
        scratch_shapes=[pltpu.VMEM((S, W), BF16), pltpu.VMEM((vt_rows, S), BF16), scores, scores, row_max, row_max],
        input_output_aliases={len(args) - 1: 0},
        compiler_params=_params(("arbitrary", "arbitrary", "arbitrary")),
        name="diff_attention",
    )(*args)


def _na_heads(q, score_fn, pv_fn):
    lane = lax.broadcasted_iota(I32, q.shape, 1) // NA_HEAD_DIM
    out = jnp.zeros(q.shape, F32)
    for h in range(NA_HEADS):
        qh = jnp.where(lane == h, q, 0.0).astype(BF16)
        out = jnp.where(lane == h, pv_fn(score_fn(qh, h)), out)
    return out


def _dense_kernel(q_ref, k_ref, v_ref, prev_ref, o_ref):
    del prev_ref
    q = q_ref[...] * (NA_HEAD_DIM ** -0.5 * LOG2E)
    k = k_ref[...].astype(BF16)
    v = v_ref[...].astype(BF16)
    o_ref[...] = _na_heads(q, lambda qh, h: _dot_nt(qh, k), lambda s: _softmax_pv(s, v))


def _dense_attention(p, out, *, n_seq, T):
    C = NA_HEADS * NA_HEAD_DIM
    return pl.pallas_call(
        _dense_kernel,
        grid=(n_seq,),
        in_specs=[pl.BlockSpec((T, C), lambda b: (b, 7)),
                  pl.BlockSpec((T, C), lambda b: (b, 8)),
                  pl.BlockSpec((T, C), lambda b: (b, 9)),
                  pl.BlockSpec(memory_space=pl.ANY)],
        out_specs=pl.BlockSpec((T, C), lambda b: (b, 0)),
        out_shape=jax.ShapeDtypeStruct(out.shape, F32),
        input_output_aliases={3: 0},
        compiler_params=_params(("arbitrary",)),
        name="dense_attention",
    )(p, p, p, out)


def _na_kernel(q_ref, ks_ref, vs_ref, kc_ref, vc_ref, bias_ref, prev_ref, o_ref, *, rows):
    del prev_ref
    r0 = pl.program_id(1) * NA_Q_ROWS
    bs = jnp.clip(r0 - NA_WIN_H // 2, 0, rows - NA_BAND)
    start = pl.multiple_of(bs * GRID_W, GRID_W)
    nb = NA_BAND * GRID_W
    kb = ks_ref[pl.ds(start, nb), :].astype(BF16)
    vb = vs_ref[pl.ds(start, nb), :].astype(BF16)
    kc = kc_ref[...].astype(BF16)
    vc = vc_ref[...].astype(BF16)
    q = q_ref[...] * (NA_HEAD_DIM ** -0.5 * LOG2E)

    def scores(qh, h):
        return _dot_nt(qh, kb) + bias_ref[h], _dot_nt(qh, kc)

    def pv(s):
        s_loc, s_ctx = s
        m = jnp.maximum(jnp.max(s_loc, axis=-1, keepdims=True), jnp.max(s_ctx, axis=-1, keepdims=True))
        e_loc = jnp.exp2(s_loc - m)
        e_ctx = jnp.exp2(s_ctx - m)
        l = jnp.sum(e_loc, axis=-1, keepdims=True) + jnp.sum(e_ctx, axis=-1, keepdims=True)
        return (_dot(e_loc.astype(BF16), vb) + _dot(e_ctx.astype(BF16), vc)) * (1.0 / l)

    o_ref[...] = _na_heads(q, scores, pv)


def _na_bias_table(rpb_l, rows):
    n_ro, n_co = 2 * NA_WIN_H - 1, 2 * NA_WIN_W - 1
    c = np.arange(GRID_W)[:, None]
    kc = np.arange(GRID_W)[None, :]
    cs = np.clip(c - NA_WIN_W // 2, 0, GRID_W - NA_WIN_W)
    col_ok = (kc >= cs) & (kc < cs + NA_WIN_W)
    co = kc - c + (NA_WIN_W - 1)
    pick = ((np.arange(n_co)[:, None, None] == co[None]) & col_ok[None]).astype(np.float32)
    toep = jnp.dot(rpb_l.reshape(NA_HEADS * n_ro, n_co).astype(F32), jnp.asarray(pick.reshape(n_co, -1)),
                   precision=lax.Precision.HIGHEST).reshape(NA_HEADS, n_ro, GRID_W, GRID_W) * LOG2E
    toep = jnp.where(jnp.asarray(col_ok)[None, None], toep, MASKED)
    toep = jnp.concatenate([toep, jnp.full((NA_HEADS, 1, GRID_W, GRID_W), MASKED, F32)], axis=1)
    blk = np.full((3, NA_Q_ROWS, NA_BAND), n_ro, np.int32)
    for v, r0 in enumerate((0, NA_Q_ROWS, rows - NA_Q_ROWS)):
        bs = int(np.clip(r0 - NA_WIN_H // 2, 0, rows - NA_BAND))
        for j in range(NA_Q_ROWS):
            rs = int(np.clip(r0 + j - NA_WIN_H // 2, 0, rows - NA_WIN_H))
            for i in range(NA_BAND):
                if rs <= bs + i < rs + NA_WIN_H:
                    blk[v, j, i] = bs + i - (r0 + j) + (NA_WIN_H - 1)
    tab = toep[:, blk]
    tab = jnp.transpose(tab, (1, 0, 2, 4, 3, 5))
    return tab.reshape(3, NA_HEADS, NA_Q_ROWS * GRID_W, NA_BAND * GRID_W)


def _neighbourhood_attention(p, cache_k, cache_v, bias, out, *, row0, n_seq, T, l):
    C = NA_HEADS * NA_HEAD_DIM
    rows = T // GRID_W
    tq = NA_Q_ROWS * GRID_W
    nq = T // tq
    Lc = cache_k.shape[2]
    qb0 = row0 // tq
    sb0 = row0 // T

    def variant(i):
        r0 = i * NA_Q_ROWS
        return (r0 - jnp.clip(r0 - NA_WIN_H // 2, 0, rows - NA_BAND)) // NA_Q_ROWS

    return pl.pallas_call(
        functools.partial(_na_kernel, rows=rows),
        grid=(n_seq, nq),
        in_specs=[pl.BlockSpec((tq, C), lambda b, i: (qb0 + b * nq + i, 7)),
                  pl.BlockSpec((T, C), lambda b, i: (sb0 + b, 8)),
                  pl.BlockSpec((T, C), lambda b, i: (sb0 + b, 9)),
                  pl.BlockSpec((None, None, Lc, C), lambda b, i: (b, l, 0, 0)),
                  pl.BlockSpec((None, None, Lc, C), lambda b, i: (b, l, 0, 0)),
                  pl.BlockSpec((None, NA_HEADS, tq, NA_BAND * GRID_W), lambda b, i: (variant(i), 0, 0, 0)),
                  pl.BlockSpec(memory_space=pl.ANY)],
        out_specs=pl.BlockSpec((tq, C), lambda b, i: (qb0 + b * nq + i, 0)),
        out_shape=jax.ShapeDtypeStruct(out.shape, F32),
        input_output_aliases={6: 0},
        compiler_params=_params(("arbitrary", "arbitrary")),
        name="neighbourhood_attention",
    )(p, p, p, cache_k, cache_v, bias, out)


def _merge_kernel(pool_ref, dn_ref, na_ref, x_ref, mod_ref, g1_ref, g2_ref, w_ref, rw_ref, rb_ref,
                  x1_ref, h2_ref, route_ref, cnt_ref, *, D):
    parts = [slice(j * TOK_CHUNK, (j + 1) * TOK_CHUNK) for j in range(x_ref.shape[0] // TOK_CHUNK)]
    c0 = pool_ref.shape[1]
    c1 = c0 + dn_ref.shape[1]
    mix = [_dot(pool_ref[p, :].astype(BF16), w_ref[0:c0, :])
           + _dot(dn_ref[p, :].astype(BF16), w_ref[c0:c1, :])
           + _dot(na_ref[p, :].astype(BF16), w_ref[c1:, :]) for p in parts]
    x1 = [x_ref[p, :] + mod_ref[:, 2 * D:3 * D] * (_rms(m) * g1_ref[...]) for p, m in zip(parts, mix)]
    for p, v in zip(parts, x1):
        x1_ref[p, :] = v
    h2 = [(_rms(v) * g2_ref[...]) * (1.0 + mod_ref[:, 4 * D:5 * D]) + mod_ref[:, 3 * D:4 * D] for v in x1]
    for p, v in zip(parts, h2):
        h2_ref[p, :] = v

    rw_hi, rw_lo = _split(rw_ref[...])
    h2_split = [_split(v) for v in h2]
    logits = [_dot(hi, rw_hi) + _dot(lo, rw_hi) + _dot(hi, rw_lo) + rb_ref[...] for hi, lo in h2_split]
    lane = lax.broadcasted_iota(I32, (TOK_CHUNK, LANES), 1)
    lane_f = lane.astype(F32)
    work = logits
    vals, hots = [], []
    for _ in range(TOP_K):
        mx = [jnp.max(w, axis=-1, keepdims=True) for w in work]
        idx = [jnp.min(jnp.where(w == m, lane_f, float(LANES)), axis=-1, keepdims=True)
               for w, m in zip(work, mx)]
        hot = [lane_f == i for i in idx]
        vals.append(mx)
        hots.append(hot)
        work = [jnp.where(h, -jnp.inf, w) for h, w in zip(hot, work)]

    r = lax.broadcasted_iota(I32, (TOK_CHUNK, TOK_CHUNK), 0)
    c = lax.broadcasted_iota(I32, (TOK_CHUNK, TOK_CHUNK), 1)
    before = jnp.where(c < r, 1.0, 0.0).astype(BF16)
    er = lax.broadcasted_iota(I32, (LANES, LANES), 0)
    ec = lax.broadcasted_iota(I32, (LANES, LANES), 1)
    earlier = jnp.where(er < ec, 1.0, 0.0).astype(BF16)
    for j, p in enumerate(parts):
        hot_j = [hots[k][j] for k in range(TOP_K)]
        es = [jnp.exp(vals[k][j] - vals[0][j]) for k in range(TOP_K)]
        inv = 1.0 / (es[0] + es[1] + es[2] + es[3])
        sel = jnp.zeros((TOK_CHUNK, LANES), F32)
        for hot in hot_j:
            sel = jnp.where(hot, 1.0, sel)
        rank = _dot(before, sel.astype(BF16))
        cnt = jnp.sum(sel, axis=0, keepdims=True)
        cnt_ref[j] = cnt
        run = jnp.floor((cnt + (RUN_ALIGN - 1)) * (1.0 / RUN_ALIGN)) * RUN_ALIGN
        run_start = _dot(jnp.broadcast_to(run, (8, LANES)).astype(BF16), earlier)[0:1, :]
        pos = rank + run_start
        route = jnp.zeros((TOK_CHUNK, LANES), F32)
        for k in range(TOP_K):
            e_k = jnp.sum(jnp.where(hot_j[k], lane_f, 0.0), axis=-1, keepdims=True)
            p_k = jnp.sum(jnp.where(hot_j[k], pos, 0.0), axis=-1, keepdims=True)
            route = jnp.where(lane == k, e_k, route)
            route = jnp.where(lane == TOP_K + k, p_k, route)
            route = jnp.where(lane == 2 * TOP_K + k, es[k] * inv, route)
        route_ref[p, :] = route


def _merge_route(pool_o, dn, na_o, x, mod4, g1, g2, w_out_bf, rw_pad, rb_pad, *, tm, ctx_blocks,
                 blocks_per_seq, l):
    N, D = x.shape
    per_step = tm // TOK_CHUNK

    def grp(i):
        return jnp.where(i < ctx_blocks, 0, 1 + (i - ctx_blocks) // blocks_per_seq)

    row = lambda w: pl.BlockSpec((tm, w), lambda i: (i, 0))
    full = lambda a: pl.BlockSpec(a.shape, lambda i: (0,) * a.ndim)
    return pl.pallas_call(
        functools.partial(_merge_kernel, D=D),
        grid=(N // tm,),
        in_specs=[row(pool_o.shape[1]), row(dn.shape[1]), row(na_o.shape[1]), row(D),
                  pl.BlockSpec((None, None, 1, N_MOD * D), lambda i: (l, grp(i), 0, 0)),
                  full(g1), full(g2), full(w_out_bf), full(rw_pad), full(rb_pad)],
        out_specs=[row(D), row(D), row(LANES), pl.BlockSpec((per_step, 1, LANES), lambda i: (i, 0, 0))],
        out_shape=[jax.ShapeDtypeStruct((N, D), F32), jax.ShapeDtypeStruct((N, D), F32),
                   jax.ShapeDtypeStruct((N, LANES), F32), jax.ShapeDtypeStruct((N // TOK_CHUNK, 1, LANES), F32)],
        compiler_params=_params(("arbitrary",)),
        name="merge_route",
    )(pool_o, dn, na_o, x, mod4, g1, g2, w_out_bf, rw_pad, rb_pad)


def _pack_pairs(x):
    C = x.shape[1] // 2
    bits = lax.bitcast_convert_type(x.astype(BF16).astype(F32), jnp.uint32)
    return bits[:, C:] | (bits[:, :C] >> 16)


def _unpack_pairs(w):
    lo = lax.bitcast_convert_type(w << 16, F32)
    hi = lax.bitcast_convert_type(w & jnp.uint32(0xFFFF0000), F32)
    return jnp.concatenate([lo, hi], axis=1).astype(BF16)


def _run_sizes(lo, hi):
    return [1 << k for k in range(hi.bit_length() - 1, lo.bit_length() - 2, -1)]


def _run_copies(n, src_at, dst_at, sem, wait, sizes):
    for size in sizes:
        @pl.when((n & size) != 0)
        def _(size=size):
            off = n & -(2 * size)
            cp = pltpu.make_async_copy(src_at(off, size), dst_at(off, size), sem)
            cp.wait() if wait else cp.start()


def _block_runs(step, n_exp, run_ref, src_ref, dst_ref, large_ref, hbm_ref, buf, sem, *, to_hbm, wait):
    def each_run(sizes):
        def body(e, carry):
            j = step * n_exp + e
            so = src_ref[j]
            do = dst_ref[j]
            in_buf = lambda o, s: buf.at[pl.ds(pl.multiple_of(so + o, RUN_ALIGN), s)]
            in_hbm = lambda o, s: hbm_ref.at[pl.ds(pl.multiple_of(do + o, RUN_ALIGN), s)]
            if to_hbm:
                _run_copies(run_ref[j], in_buf, in_hbm, sem, wait, sizes)
            else:
                _run_copies(run_ref[j], in_hbm, in_buf, sem, wait, sizes)
            return carry
        lax.fori_loop(0, n_exp, body, 0, unroll=RUN_UNROLL)

    each_run(_run_sizes(RUN_ALIGN, RUN_LARGE // 2))

    @pl.when(large_ref[step] != 0)
    def _():
        each_run(_run_sizes(RUN_LARGE, TOK_CHUNK))


def _dispatch_kernel(run_ref, src_ref, dst_ref, large_ref, tail_ref, taildst_ref, nu_ref, route_ref, h_ref, xs_ref,
                     sorted_buf, zero_buf, sems, *, n_exp):
    b = pl.program_id(0)
    nb = pl.num_programs(0)
    slot = b % 2
    runs = functools.partial(_block_runs, n_exp=n_exp, run_ref=run_ref, src_ref=src_ref, dst_ref=dst_ref,
                             large_ref=large_ref, hbm_ref=xs_ref, to_hbm=True)

    @pl.when(b >= 2)
    def _():
        runs(b - 2, buf=sorted_buf.at[slot], sem=sems.at[slot], wait=True)

    route = route_ref[...]
    col = lax.broadcasted_iota(I32, (TOK_CHUNK, sorted_buf.shape[1]), 1).astype(F32)
    place = jnp.zeros(col.shape, F32)
    for k in range(TOP_K):
        place = jnp.where(col == route[:, TOP_K + k:TOP_K + k + 1], 1.0, place)
    srt = lax.dot_general(place.astype(BF16), h_ref[...].astype(BF16), (((0,), (0,)), ((), ())),
                          preferred_element_type=F32)
    sorted_buf[slot] = _pack_pairs(srt)
    runs(b, buf=sorted_buf.at[slot], sem=sems.at[slot], wait=False)

    @pl.when(b == nb - 1)
    def _():
        @pl.when(b >= 1)
        def _():
            runs(b - 1, buf=sorted_buf.at[1 - slot], sem=sems.at[1 - slot], wait=True)
        runs(b, buf=sorted_buf.at[slot], sem=sems.at[slot], wait=True)

        zero_buf[...] = jnp.zeros(zero_buf.shape, zero_buf.dtype)
        sem = sems.at[0]

        def each_tail(wait):
            def body(e, carry):
                do = taildst_ref[e]
                _run_copies(tail_ref[e],
                            lambda o, s: zero_buf.at[pl.ds(0, s)],
                            lambda o, s: xs_ref.at[pl.ds(pl.multiple_of(do + o, RUN_ALIGN), s)], sem, wait,
                            _run_sizes(RUN_ALIGN, MOE_ROWS // 2))
                return carry
            lax.fori_loop(0, n_exp, body, 0)

        def spare_block(wait):
            def body(i, carry):
                cp = pltpu.make_async_copy(zero_buf.at[pl.ds(0, MOE_ROWS)],
                                           xs_ref.at[pl.ds(pl.multiple_of(i * MOE_ROWS, MOE_ROWS), MOE_ROWS)], sem)
                cp.wait() if wait else cp.start()
                return carry
            lax.fori_loop(nu_ref[0], xs_ref.shape[0] // MOE_ROWS, body, 0)

        each_tail(False)
        spare_block(False)
        each_tail(True)
        spare_block(True)


def _dispatch(plan, n_used, route, h2, n_rows, n_exp):
    N, D = h2.shape
    C = D // 2
    grid_spec = pltpu.PrefetchScalarGridSpec(
        num_scalar_prefetch=7,
        grid=(N // TOK_CHUNK,),
        in_specs=[pl.BlockSpec((TOK_CHUNK, LANES), lambda i, *_: (i, 0)),
                  pl.BlockSpec((TOK_CHUNK, D), lambda i, *_: (i, 0))],
        out_specs=pl.BlockSpec(memory_space=pl.ANY),
        scratch_shapes=[pltpu.VMEM((2, _sorted_rows(n_exp), C), jnp.uint32),
                        pltpu.VMEM((max(TOK_CHUNK, MOE_ROWS), C), jnp.uint32),
                        pltpu.SemaphoreType.DMA((2,))],
    )
    return pl.pallas_call(
        functools.partial(_dispatch_kernel, n_exp=n_exp),
        grid_spec=grid_spec,
        out_shape=jax.ShapeDtypeStruct((n_rows, C), jnp.uint32),
        compiler_params=_params(("arbitrary",)),
        name="moe_dispatch",
    )(plan["run"], plan["src"], plan["dst"], plan["large"], plan["tail"], plan["tail_dst"], n_used, route, h2)


def _expert_kernel(be_ref, nu_ref, rows_ref, xs_ref, w1_ref, b1_ref, w2_ref, b2_ref, o_ref, w1b, w2b, *, F):
    del nu_ref
    i = pl.program_id(0)
    e = be_ref[i]
    prev = be_ref[jnp.maximum(i - 1, 0)]
    rows = rows_ref[i]
    half = MOE_ROWS // 2

    @pl.when((i == 0) | (e != prev))
    def _():
        w1b[...] = w1_ref[...].astype(BF16)
        w2b[...] = w2_ref[...].astype(BF16)

    def ffn(xw):
        hh = _dot(_unpack_pairs(xw), w1b[...]) + b1_ref[...]
        g = jnp.minimum(hh[:, 0:F], SWIGLU_LIMIT)
        u = jnp.clip(hh[:, F:2 * F], -SWIGLU_LIMIT, SWIGLU_LIMIT)
        a = (g * (1.0 / (1.0 + jnp.exp(-SWIGLU_ALPHA * g)))) * (u + 1.0)
        return _pack_pairs(_dot(a.astype(BF16), w2b[...]) + b2_ref[...])

    @pl.when(rows > half)
    def _():
        o_ref[...] = ffn(xs_ref[...])

    @pl.when((rows > 0) & (rows <= half))
    def _():
        o_ref[0:half, :] = ffn(xs_ref[0:half, :])
        o_ref[half:MOE_ROWS, :] = jnp.zeros((MOE_ROWS - half, o_ref.shape[1]), o_ref.dtype)

    @pl.when(rows == 0)
    def _():
        o_ref[...] = jnp.zeros(o_ref.shape, o_ref.dtype)


def _experts(block_e, n_used, block_rows, xs, w1, b1, w2, b2, *, l):
    R, C = xs.shape
    L, E, D, F2 = w1.shape
    F = F2 // 2
    nblk = R // MOE_ROWS
    grid_spec = pltpu.PrefetchScalarGridSpec(
        num_scalar_prefetch=3,
        grid=(nblk,),
        in_specs=[pl.BlockSpec((MOE_ROWS, C), lambda i, be, nu, br: (jnp.minimum(i, nu[0] - 1), 0)),
                  pl.BlockSpec((None, None, D, F2), lambda i, be, nu, br: (l, be[i], 0, 0)),
                  pl.BlockSpec((None, None, 1, F2), lambda i, be, nu, br: (l, be[i], 0, 0)),
                  pl.BlockSpec((None, None, F, D), lambda i, be, nu, br: (l, be[i], 0, 0)),
                  pl.BlockSpec((None, None, 1, D), lambda i, be, nu, br: (l, be[i], 0, 0))],
        out_specs=pl.BlockSpec((MOE_ROWS, C), lambda i, be, nu, br: (i, 0)),
        scratch_shapes=[pltpu.VMEM((D, F2), BF16), pltpu.VMEM((F, D), BF16)],
    )
    return pl.pallas_call(
        functools.partial(_expert_kernel, F=F),
        grid_spec=grid_spec,
        out_shape=jax.ShapeDtypeStruct((R, C), jnp.uint32),
        compiler_params=_params(("arbitrary",)),
        name="moe_experts",
    )(block_e, n_used, block_rows, xs, w1, b1.reshape(L, E, 1, F2), w2, b2.reshape(L, E, 1, D))


def _combine_kernel(run_ref, src_ref, dst_ref, large_ref, yb_ref, route_ref, x1_ref, mod_ref, g_ref, *rest, D,
                    n_exp, ctx_blocks, split):
    if split:
        ctx_ref, lat_ref, sorted_buf, sems = rest
    else:
        o_ref, sorted_buf, sems = rest
    b = pl.program_id(0)
    nb = pl.num_programs(0)
    slot = b % 2
    runs = functools.partial(_block_runs, n_exp=n_exp, run_ref=run_ref, src_ref=src_ref, dst_ref=dst_ref,
                             large_ref=large_ref, hbm_ref=yb_ref, to_hbm=False)

    @pl.when(b == 0)
    def _():
        sorted_buf[...] = jnp.zeros(sorted_buf.shape, sorted_buf.dtype)
        runs(b, buf=sorted_buf.at[slot], sem=sems.at[slot], wait=False)

    @pl.when(b + 1 < nb)
    def _():
        runs(b + 1, buf=sorted_buf.at[1 - slot], sem=sems.at[1 - slot], wait=False)

    runs(b, buf=sorted_buf.at[slot], sem=sems.at[slot], wait=True)
    route = route_ref[...]
    col = lax.broadcasted_iota(I32, (TOK_CHUNK, sorted_buf.shape[1]), 1).astype(F32)
    gate = jnp.zeros(col.shape, F32)
    for k in range(TOP_K):
        gate = jnp.where(col == route[:, TOP_K + k:TOP_K + k + 1], route[:, 2 * TOP_K + k:2 * TOP_K + k + 1], gate)
    g_hi, g_lo = _split(gate)
    yb = _unpack_pairs(sorted_buf[slot])
    y = _dot(g_hi, yb) + _dot(g_lo, yb)
    x2 = x1_ref[...] + mod_ref[:, 5 * D:6 * D] * (_rms(y) * g_ref[...])
    if split:
        @pl.when(b < ctx_blocks)
        def _():
            ctx_ref[...] = x2

        @pl.when(b >= ctx_blocks)
        def _():
            lat_ref[...] = x2
    else:
        o_ref[...] = x2


def _combine(plan, yb, route, x1, mod4, g3, *, n_exp, ctx_blocks, blocks_per_seq, l, split):
    N, D = x1.shape
    row = lambda f: pl.BlockSpec((TOK_CHUNK, D), lambda i, *_: (f(i), 0))
    if split:
        out_specs = [row(lambda i: jnp.minimum(i, ctx_blocks - 1)), row(lambda i: jnp.maximum(i - ctx_blocks, 0))]
        out_shape = [jax.ShapeDtypeStruct((ctx_blocks * TOK_CHUNK, D), F32),
                     jax.ShapeDtypeStruct((N - ctx_blocks * TOK_CHUNK, D), F32)]
    else:
        out_specs = row(lambda i: i)
        out_shape = jax.ShapeDtypeStruct((N, D), F32)

    def grp(i):
        return jnp.where(i < ctx_blocks, 0, 1 + (i - ctx_blocks) // blocks_per_seq)

    grid_spec = pltpu.PrefetchScalarGridSpec(
        num_scalar_prefetch=4,
        grid=(N // TOK_CHUNK,),
        in_specs=[pl.BlockSpec(memory_space=pl.ANY),
                  pl.BlockSpec((TOK_CHUNK, LANES), lambda i, *_: (i, 0)),
                  pl.BlockSpec((TOK_CHUNK, D), lambda i, *_: (i, 0)),
                  pl.BlockSpec((None, None, 1, N_MOD * D), lambda i, *_: (l, grp(i), 0, 0)),
                  pl.BlockSpec((1, D), lambda i, *_: (0, 0))],
        out_specs=out_specs,
        scratch_shapes=[pltpu.VMEM((2, _sorted_rows(n_exp), yb.shape[1]), jnp.uint32),
                        pltpu.SemaphoreType.DMA((2,))],
    )
    return pl.pallas_call(
        functools.partial(_combine_kernel, D=D, n_exp=n_exp, ctx_blocks=ctx_blocks, split=split),
        grid_spec=grid_spec,
        out_shape=out_shape,
        compiler_params=_params(("arbitrary",)),
        name="moe_combine",
    )(plan["run"], plan["src"], plan["dst"], plan["large"], yb, route, x1, mod4, g3)


def _rope_tables(Ts, tm):
    nf = DIFF_QK_DIM // 4
    inv = ROPE_BASE ** (-jnp.arange(nf, dtype=F32) / nf)
    t = jnp.arange(Ts)
    pos = jnp.stack([(t // GRID_W).astype(F32), (t % GRID_W).astype(F32)], axis=1)
    ang = pos[:, :, None] * inv[None, None, :]
    cos = jnp.repeat(jnp.cos(ang)[:, :, None, :], 2, axis=2).reshape(Ts, DIFF_QK_DIM)
    sin = jnp.sin(ang)
    sin = jnp.stack([-sin, sin], axis=2).reshape(Ts, DIFF_QK_DIM)
    reps = LANES // DIFF_QK_DIM
    cos = jnp.concatenate([jnp.ones((tm, LANES), F32), jnp.tile(cos, (1, reps))], axis=0)
    sin = jnp.concatenate([jnp.zeros((tm, LANES), F32), jnp.tile(sin, (1, reps))], axis=0)
    return cos, sin


def _block_diag(w):
    G, a, b = w.shape
    out = jnp.zeros((G * a, G * b), w.dtype)
    for g in range(G):
        out = out.at[g * a:(g + 1) * a, g * b:(g + 1) * b].set(w[g])
    return out


def _route_plan(counts, n_exp, nblk):
    cnt = counts[:, 0, :n_exp].astype(I32)
    run = (cnt + RUN_ALIGN - 1) // RUN_ALIGN * RUN_ALIGN
    src = jnp.cumsum(run, axis=1) - run
    tot = jnp.sum(run, axis=0)
    region = (tot + MOE_ROWS - 1) // MOE_ROWS * MOE_ROWS
    region_end = jnp.cumsum(region)
    region_start = region_end - region
    dst = region_start[None, :] + jnp.cumsum(run, axis=0) - run
    n_used = (region_end[-1] // MOE_ROWS).astype(I32)
    blk = jnp.arange(nblk, dtype=I32) * MOE_ROWS
    block_e = jnp.minimum(jnp.sum((blk[:, None] >= region_end[None, :]).astype(I32), axis=1), n_exp - 1)
    last = jnp.sum(jnp.where(jnp.arange(nblk) == n_used - 1, block_e, 0))
    used = jnp.arange(nblk) < n_used
    block_rows = jnp.where(used, jnp.clip((region_start + tot)[block_e] - blk, 0, MOE_ROWS), 0).astype(I32)
    block_e = jnp.where(used, block_e, last).astype(I32)
    plan = dict(run=run.reshape(-1).astype(I32), src=src.reshape(-1).astype(I32), dst=dst.reshape(-1).astype(I32),
                large=jnp.any(run >= RUN_LARGE, axis=1).astype(I32),
                tail=(region - tot).astype(I32), tail_dst=(region_start + tot).astype(I32))
    return plan, block_e, n_used.reshape(1), block_rows


def kernel(x_prompt, x_sample, cache_diff_k, cache_diff_v, cache_na_k, cache_na_v, c, c_ctx, w_ada, b_ada,
           norm_gain, w_in, w_out, pool_w, pool_scale, diff_lambda, diff_subln, na_rpb, router_w, router_b,
           moe_w1, moe_b1, moe_w2, moe_b2):
    Bp, Tp, D = x_prompt.shape
    Bs, Ts, _ = x_sample.shape
    L = w_ada.shape[0]
    E = router_w.shape[-1]
    Np, Ns = Bp * Tp, Bs * Ts
    N = Np + Ns
    tm = TOK_CHUNK
    assert Np % Ts == 0 or Bs == 0, "context rows must be a whole number of latent-sequence blocks"
    assert Np % tm == 0 and Ts % tm == 0 and Ts % (NA_Q_ROWS * GRID_W) == 0
    assert Ts // GRID_W >= NA_BAND + NA_Q_ROWS
    ctx_blocks, blocks_per_seq = Np // tm, Ts // tm

    G = 16
    cvec = jnp.zeros((G, D), F32).at[0].set(c_ctx).at[1:1 + Bs].set(c)
    mod4 = _modulation(cvec, w_ada, b_ada).reshape(L, G, 1, N_MOD * D)
    tp = PROJ_ROWS
    assert Np % tp == 0 and Ts % tp == 0 and Np % MERGE_ROWS == 0 and Ts % MERGE_ROWS == 0
    cos_t, sin_t = _rope_tables(Ts, tp)
    w_in_bf = w_in.astype(BF16)
    w_out_bf = w_out.astype(BF16)
    rw_pad = jnp.zeros((L, D, LANES), F32).at[:, :, :E].set(router_w)
    rb_pad = jnp.full((L, 1, LANES), MASKED, F32).at[:, 0, :E].set(router_b)
    ck = cache_diff_k.reshape(Bs, L, -1, DIFF_HEADS * 2 * DIFF_QK_DIM)
    cv = cache_diff_v.reshape(Bs, L, -1, DIFF_HEADS * DIFF_V_DIM)
    nk = cache_na_k.reshape(Bs, L, -1, NA_HEADS * NA_HEAD_DIM)
    nv = cache_na_v.reshape(Bs, L, -1, NA_HEADS * NA_HEAD_DIM)
    nb = N // TOK_CHUNK
    nblk = -(-(N * TOP_K + nb * E * (RUN_ALIGN - 1) + E * (MOE_ROWS - 1)) // MOE_ROWS)
    blocks = dict(ctx_blocks=ctx_blocks, blocks_per_seq=blocks_per_seq)

    x = jnp.concatenate([x_prompt.reshape(Np, D), x_sample.reshape(Ns, D)], axis=0)
    new_dk, new_dv, new_nk, new_nv = [], [], [], []
    for l in range(L):
        lam_init = 0.8 - 0.6 * math.exp(-0.3 * l)
        g = norm_gain[l]
        p, dk_l, dv_l, nk_l, nv_l = _in_projection(x, mod4, g[0:1], w_in_bf[l], cos_t, sin_t, tm=tp, l=l,
                                                   ctx_blocks=Np // tp, blocks_per_seq=Ts // tp)

        pw = _block_diag(pool_w[l]).astype(BF16)
        ps = pool_scale[l].reshape(1, -1)
        pool_o = jnp.zeros((N, pw.shape[0]), F32)
        pool_o = _pool(p, pw, ps, row0=0, n_seq=Bp, T=Tp, out=pool_o)
        pool_o = _pool(p, pw, ps, row0=Np, n_seq=Bs, T=Ts, out=pool_o)

        sub = diff_subln[l].reshape(1, -1)
        dn = jnp.zeros((N, DIFF_HEADS * DIFF_V_DIM), F32)
        dn = _diff_attention_ctx(p, diff_lambda[l], sub, dn, n_seq=Bp, T=Tp, lam_init=lam_init)
        dn = _diff_attention(p, diff_lambda[l], sub, dn, row0=Np, n_seq=Bs, T=Ts, tq=DIFF_Q_ROWS, lam_init=lam_init,
                             cache_k=ck, cache_v=cv, l=l)

        na_o = jnp.zeros((N, NA_HEADS * NA_HEAD_DIM), F32)
        na_o = _dense_attention(p, na_o, n_seq=Bp, T=Tp)
        bias = _na_bias_table(na_rpb[l], Ts // GRID_W)
        na_o = _neighbourhood_attention(p, nk, nv, bias, na_o, row0=Np, n_seq=Bs, T=Ts, l=l)

        x1, h2, route, counts = _merge_route(pool_o, dn, na_o, x, mod4, g[1:2], g[2:3], w_out_bf[l],
                                             rw_pad[l], rb_pad[l], tm=MERGE_ROWS, l=l,
                                             ctx_blocks=Np // MERGE_ROWS, blocks_per_seq=Ts // MERGE_ROWS)
        plan, block_e, n_used, block_rows = _route_plan(counts, E, nblk)
        xs = _dispatch(plan, n_used, route, h2, nblk * MOE_ROWS, E)
        yb = _experts(block_e, n_used, block_rows, xs, moe_w1, moe_b1, moe_w2, moe_b2, l=l)
        x = _combine(plan, yb, route, x1, mod4, g[3:4], n_exp=E, l=l, split=l == L - 1, **blocks)

        new_dk.append(dk_l.reshape(Bp, Tp, DIFF_HEADS, 2 * DIFF_QK_DIM))
        new_dv.append(dv_l.reshape(Bp, Tp, DIFF_HEADS, DIFF_V_DIM))
        new_nk.append(nk_l.reshape(Bp, Tp, NA_HEADS, NA_HEAD_DIM))
        new_nv.append(nv_l.reshape(Bp, Tp, NA_HEADS, NA_HEAD_DIM))

    return (x[0].reshape(Bp, Tp, D), x[1].reshape(Bs, Ts, D),
            jnp.stack(new_dk, axis=1), jnp.stack(new_dv, axis=1),
            jnp.stack(new_nk, axis=1), jnp.stack(new_nv, axis=1))
```

```python
import functools
import math

import numpy as np
import jax
import jax.numpy as jnp
from jax import lax
from jax.experimental import pallas as pl
from jax.experimental.pallas import tpu as pltpu

F32 = jnp.float32
BF16 = jnp.bfloat16
I32 = jnp.int32

GRID_W = 64
POOL_GROUPS = 4
POOL_MAX_HALF = 8
DIFF_HEADS = 4
DIFF_V_DIM = 128
DIFF_QK_DIM = 64
NA_HEADS = 4
NA_HEAD_DIM = 64
NA_WIN_H = 8
NA_WIN_W = 16
NA_Q_ROWS = 4
NA_BAND = 12
TOP_K = 4
SWIGLU_ALPHA = 1.702
SWIGLU_LIMIT = 7.0
ROPE_BASE = 10000.0
NORM_EPS = 1e-6
N_MOD = 6

LOG2E = 1.4426950408889634
MASKED = -1e30
LANES = 128
MXU_TILE = 256
MOE_ROWS = 512
COL_DQ, COL_DK, COL_DV = 256, 768, 1280
COL_NQ, COL_NK, COL_NV = 1792, 2048, 2304
DIFF_KEYS = 256
PROJ_ROWS = 512
MERGE_ROWS = 1024
DIFF_Q_ROWS = 256
TOK_CHUNK = 256
RUN_ALIGN = 8
RUN_LARGE = 128
RUN_UNROLL = 4
VMEM_LIMIT = 56 * 1024 * 1024


def _sorted_rows(n_exp):
    rows = TOK_CHUNK * TOP_K + n_exp * (RUN_ALIGN - 1)
    return -(-rows // MXU_TILE) * MXU_TILE


def _params(sem, vmem=VMEM_LIMIT):
    return pltpu.CompilerParams(dimension_semantics=sem, vmem_limit_bytes=vmem)


def _dot(a, b):
    return jnp.dot(a, b, preferred_element_type=F32)


def _dot_nt(a, b):
    return lax.dot_general(a, b, (((1,), (1,)), ((), ())), preferred_element_type=F32)


def _split(x):
    hi = x.astype(BF16)
    return hi, (x - hi.astype(F32)).astype(BF16)


def _dot3(a, b):
    ah, al = _split(a)
    bh, bl = _split(b)
    return _dot(ah, bh) + _dot(al, bh) + _dot(ah, bl)


def _rms(x):
    return x * lax.rsqrt(jnp.mean(x * x, axis=-1, keepdims=True) + NORM_EPS)


def _mod_kernel(c_ref, w_ref, b_ref, o_ref):
    c = c_ref[...]
    a = c * (1.0 / (1.0 + jnp.exp(-c)))
    o_ref[...] = _dot3(a, w_ref[...]) + b_ref[...]


def _modulation(cvec, w_ada, b_ada):
    L, D, W = w_ada.shape
    G = cvec.shape[0]
    return pl.pallas_call(
        _mod_kernel,
        grid=(L, W // D),
        in_specs=[pl.BlockSpec((G, D), lambda l, j: (0, 0)),
                  pl.BlockSpec((None, D, D), lambda l, j: (l, 0, j)),
                  pl.BlockSpec((None, 1, D), lambda l, j: (l, 0, j))],
        out_specs=pl.BlockSpec((None, G, D), lambda l, j: (l, 0, j)),
        out_shape=jax.ShapeDtypeStruct((L, G, W), F32),
        compiler_params=_params(("arbitrary", "arbitrary")),
        name="ada_modulation",
    )(cvec, w_ada, b_ada.reshape(L, 1, W))


def _token_rows(x_refs, ctx_blocks):
    if len(x_refs) == 1:
        return lambda rows=slice(None): x_refs[0][rows, :]
    is_ctx = pl.program_id(0) < ctx_blocks
    return lambda rows=slice(None): jnp.where(is_ctx, x_refs[0][rows, :], x_refs[1][rows, :])


def _token_specs(x, tm, ctx_blocks):
    if not isinstance(x, tuple):
        return [x], [pl.BlockSpec((tm, x.shape[1]), lambda i: (i, 0))]
    D = x[0].shape[1]
    return list(x), [pl.BlockSpec((tm, D), lambda i: (jnp.minimum(i, ctx_blocks - 1), 0)),
                     pl.BlockSpec((tm, D), lambda i: (jnp.maximum(i - ctx_blocks, 0), 0))]


def _inproj_kernel(*refs, D, rope_lo, rope_hi, ctx_blocks):
    mod_ref, g_ref, w_ref, cos_ref, sin_ref, o_ref, dk_ref, dv_ref, nk_ref, nv_ref = refs[-10:]
    h = _rms(_token_rows(refs[:-10], ctx_blocks)()) * g_ref[...]
    h = h * (1.0 + mod_ref[:, D:2 * D]) + mod_ref[:, 0:D]
    p = _dot(h.astype(BF16), w_ref[...])
    W = p.shape[1]
    o_ref[:, 0:rope_lo] = p[:, 0:rope_lo].astype(o_ref.dtype)
    o_ref[:, rope_hi:W] = p[:, rope_hi:W].astype(o_ref.dtype)
    cos = cos_ref[...]
    sin = sin_ref[...]
    lane = lax.broadcasted_iota(I32, cos.shape, 1)
    first = (lane % 32) < 16
    for c0 in range(rope_lo, rope_hi, LANES):
        xc = p[:, c0:c0 + LANES]
        partner = jnp.where(first, pltpu.roll(xc, LANES - 16, 1), pltpu.roll(xc, 16, 1))
        o_ref[:, c0:c0 + LANES] = (xc * cos + partner * sin).astype(o_ref.dtype)

    @pl.when(pl.program_id(0) < ctx_blocks)
    def _():
        dk_ref[...] = p[:, COL_DK:COL_DV]
        dv_ref[...] = p[:, COL_DV:COL_NQ]
        nk_ref[...] = p[:, COL_NK:COL_NV]
        nv_ref[...] = p[:, COL_NV:W]


def _in_projection(x, mod4, gain, w_bf, cos_t, sin_t, *, tm, ctx_blocks, blocks_per_seq, l):
    xs, x_specs = _token_specs(x, tm, ctx_blocks)
    N, D = sum(a.shape[0] for a in xs), xs[0].shape[1]
    W = w_bf.shape[1]

    def grp(i):
        return jnp.where(i < ctx_blocks, 0, 1 + (i - ctx_blocks) // blocks_per_seq)

    def rope_blk(i):
        return jnp.where(i < ctx_blocks, 0, 1 + (i - ctx_blocks) % blocks_per_seq)

    kern = functools.partial(_inproj_kernel, D=D, rope_lo=COL_DQ, rope_hi=COL_DV, ctx_blocks=ctx_blocks)
    ctx_rows = ctx_blocks * tm
    ctx_out = lambda w: pl.BlockSpec((tm, w), lambda i: (jnp.minimum(i, ctx_blocks - 1), 0))
    widths = (COL_DV - COL_DK, COL_NQ - COL_DV, COL_NV - COL_NK, W - COL_NV)
    return pl.pallas_call(
        kern,
        grid=(N // tm,),
        in_specs=x_specs + [
                  pl.BlockSpec((None, None, 1, N_MOD * D), lambda i: (l, grp(i), 0, 0)),
                  pl.BlockSpec((1, D), lambda i: (0, 0)),
                  pl.BlockSpec((D, W), lambda i: (0, 0)),
                  pl.BlockSpec((tm, LANES), lambda i: (rope_blk(i), 0)),
                  pl.BlockSpec((tm, LANES), lambda i: (rope_blk(i), 0))],
        out_specs=[pl.BlockSpec((tm, W), lambda i: (i, 0))] + [ctx_out(w) for w in widths],
        out_shape=[jax.ShapeDtypeStruct((N, W), BF16)] + [jax.ShapeDtypeStruct((ctx_rows, w), F32) for w in widths],
        compiler_params=_params(("arbitrary",)),
        name="in_projection",
    )(*xs, mod4, gain, w_bf, cos_t, sin_t)


def _pool_kernel(u_ref, w_ref, sc_ref, o_ref, pad_ref, *, T, CH):
    H = 2 * POOL_MAX_HALF
    zeros = jnp.zeros((H, pad_ref.shape[1]), F32)
    pad_ref[0:H, :] = zeros
    pad_ref[H + T:2 * H + T, :] = zeros
    pad_ref[H:H + T, :] = u_ref[...].astype(F32)
    C = pad_ref.shape[1]
    lane = lax.broadcasted_iota(I32, (CH, C), 1)
    half = jnp.left_shift(1, lane // (C // POOL_GROUPS))
    row = lax.broadcasted_iota(I32, (CH, C), 0)

    def body(ci, carry):
        base = pl.multiple_of(ci * CH, CH)
        win = pad_ref[pl.ds(base + POOL_MAX_HALF, CH + H), :]
        acc = jnp.zeros((CH, C), F32)
        for j in range(-POOL_MAX_HALF, POOL_MAX_HALF):
            sl = win[POOL_MAX_HALF + j:POOL_MAX_HALF + j + CH, :]
            inside = (half > j) if j >= 0 else (half >= -j)
            acc = acc + jnp.where(inside, sl, 0.0)
        t = row + base
        cnt = jnp.minimum(t + half, T) - jnp.maximum(t - half, 0)
        d = acc / cnt.astype(F32) - win[POOL_MAX_HALF:POOL_MAX_HALF + CH, :]
        o_ref[pl.ds(base, CH), :] = _dot(d.astype(BF16), w_ref[...]) * sc_ref[...]
        return carry

    lax.fori_loop(0, T // CH, body, 0)


def _pool(p, w_bd, scale, *, row0, n_seq, T, out):
    C = w_bd.shape[0]
    CH = min(T, 256)
    blk0 = row0 // T

    def kern(u_ref, w_ref, sc_ref, prev_ref, o_ref, pad_ref):
        del prev_ref
        _pool_kernel(u_ref, w_ref, sc_ref, o_ref, pad_ref, T=T, CH=CH)

    return pl.pallas_call(
        kern,
        grid=(n_seq,),
        in_specs=[pl.BlockSpec((T, C), lambda s: (blk0 + s, 0)),
                  pl.BlockSpec((C, C), lambda s: (0, 0)),
                  pl.BlockSpec((1, C), lambda s: (0, 0)),
                  pl.BlockSpec(memory_space=pl.ANY)],
        out_specs=pl.BlockSpec((T, C), lambda s: (blk0 + s, 0)),
        out_shape=jax.ShapeDtypeStruct(out.shape, F32),
        scratch_shapes=[pltpu.VMEM((T + 4 * POOL_MAX_HALF, C), F32)],
        input_output_aliases={3: 0},
        compiler_params=_params(("arbitrary",)),
        name="pool_mixer",
    )(p, w_bd, scale, out)


def _softmax_pv(s, v):
    m = jnp.max(s, axis=-1, keepdims=True)
    e = jnp.exp2(s - m)
    l = jnp.sum(e, axis=-1, keepdims=True)
    return _dot(e.astype(BF16), v) * (1.0 / l)


def _diff_kernel(*refs, Ts, Lc, lam_init):
    if Lc:
        lam_ref, sub_ref, q_ref, ks_ref, vs_ref, kc_ref, vc_ref, prev_ref, o_ref, kb, vt, s_a, s_b, m_a, m_b = refs
    else:
        lam_ref, sub_ref, q_ref, ks_ref, vs_ref, prev_ref, o_ref, kb, vt, s_a, s_b, m_a, m_b = refs
    del prev_ref
    i = pl.program_id(2)
    S = Ts + Lc
    dv = DIFF_V_DIM

    @pl.when(i == 0)
    def _():
        kb[0:Ts, :] = ks_ref[...].astype(BF16)
        vt[0:dv, 0:Ts] = vs_ref[...].astype(F32).T.astype(BF16)
        if Lc:
            kb[Ts:S, :] = kc_ref[...].astype(BF16)
            vt[0:dv, Ts:S] = vc_ref[...].T.astype(BF16)
        vt[dv:, :] = jnp.ones((vt.shape[0] - dv, S), BF16)
        s_b[...] = jnp.zeros(s_b.shape, F32)
        m_b[...] = jnp.zeros(m_b.shape, F32)

    lm = lam_ref[...]
    lam = (jnp.exp(jnp.sum(lm[0:1, :] * lm[1:2, :], axis=-1, keepdims=True))
           - jnp.exp(jnp.sum(lm[2:3, :] * lm[3:4, :], axis=-1, keepdims=True)) + lam_init)

    def stage(s_new, m_new, s_old, m_old):
        q = q_ref[...].astype(F32) * (DIFF_QK_DIM ** -0.5 * LOG2E)
        tq = q.shape[0]
        part = lax.broadcasted_iota(I32, q.shape, 1) // DIFF_QK_DIM
        qm = [jnp.where(part == u, q, 0.0).astype(BF16) for u in range(2)]
        top = [m_old[u][0:1, :] for u in range(2)]
        acc = [jnp.zeros((vt.shape[0], tq), F32) for _ in range(2)]
        run = [jnp.full((8, tq), -jnp.inf, F32) for _ in range(2)]
        for c0 in range(0, S, DIFF_KEYS):
            keys = slice(c0, c0 + DIFF_KEYS)
            for u in range(2):
                sc = _dot_nt(kb[keys, :], qm[u])
                s_new[u, keys, :] = sc
                for j in range(0, DIFF_KEYS, 8):
                    run[u] = jnp.maximum(run[u], sc[j:j + 8, :])
                e = jnp.exp2(s_old[u, keys, :] - top[u])
                acc[u] = acc[u] + _dot(vt[:, keys], e.astype(BF16))
        for u in range(2):
            m_new[u] = jnp.broadcast_to(jnp.max(run[u], axis=0, keepdims=True), m_new.shape[1:])
        o = (acc[0][0:dv, :] * (1.0 / acc[0][dv:dv + 1, :])
             - lam * (acc[1][0:dv, :] * (1.0 / acc[1][dv:dv + 1, :])))
        o_ref[...] = _rms(o.T) * sub_ref[...] * (1.0 - lam_init)

    @pl.when(i % 2 == 0)
    def _():
        stage(s_a, m_a, s_b, m_b)

    @pl.when(i % 2 == 1)
    def _():
        stage(s_b, m_b, s_a, m_a)


def _diff_ctx_kernel(lam_ref, sub_ref, q_ref, k_ref, v_ref, prev_ref, o_ref, *, lam_init, heads):
    del prev_ref
    lm = lam_ref[...]
    lam = (jnp.exp(jnp.sum(lm[0:1, :] * lm[1:2, :], axis=-1, keepdims=True))
           - jnp.exp(jnp.sum(lm[2:3, :] * lm[3:4, :], axis=-1, keepdims=True)) + lam_init)
    q = q_ref[...].astype(F32) * (DIFF_QK_DIM ** -0.5 * LOG2E)
    part = lax.broadcasted_iota(I32, q.shape, 1) // DIFF_QK_DIM
    k = k_ref[...].astype(BF16)
    v = v_ref[...].astype(BF16)
    for hh in range(heads):
        o1 = _softmax_pv(_dot_nt(jnp.where(part == 2 * hh, q, 0.0).astype(BF16), k), v)
        o2 = _softmax_pv(_dot_nt(jnp.where(part == 2 * hh + 1, q, 0.0).astype(BF16), k), v)
        cols = slice(hh * DIFF_V_DIM, (hh + 1) * DIFF_V_DIM)
        o = o1[:, cols] - lam * o2[:, cols]
        o_ref[:, cols] = _rms(o) * sub_ref[...] * (1.0 - lam_init)


def _diff_attention_ctx(p, lam_l, subln, out, *, n_seq, T, lam_init):
    heads = 2
    W = heads * DIFF_V_DIM
    spec = lambda col0: pl.BlockSpec((T, W), lambda b, h: (b, col0 // W + h))
    return pl.pallas_call(
        functools.partial(_diff_ctx_kernel, lam_init=lam_init, heads=heads),
        grid=(n_seq, DIFF_HEADS // heads),
        in_specs=[pl.BlockSpec((4, DIFF_QK_DIM), lambda b, h: (0, 0)),
                  pl.BlockSpec((1, DIFF_V_DIM), lambda b, h: (0, 0)),
                  spec(COL_DQ), spec(COL_DK), spec(COL_DV),
                  pl.BlockSpec(memory_space=pl.ANY)],
        out_specs=pl.BlockSpec((T, W), lambda b, h: (b, h)),
        out_shape=jax.ShapeDtypeStruct(out.shape, F32),
        input_output_aliases={5: 0},
        compiler_params=_params(("arbitrary", "arbitrary")),
        name="diff_attention_ctx",
    )(lam_l, subln, p, p, p, out)


def _diff_attention(p, lam_l, subln, out, *, row0, n_seq, T, tq, lam_init, cache_k=None, cache_v=None, l=0):
    W = DIFF_V_DIM
    assert 2 * DIFF_QK_DIM == W == LANES
    Lc = 0 if cache_k is None else cache_k.shape[2]
    S = T + Lc
    assert S % DIFF_KEYS == 0
    nq = T // tq
    qb0 = row0 // tq
    sb0 = row0 // T
    in_specs = [pl.BlockSpec((4, DIFF_QK_DIM), lambda b, h, i: (0, 0)),
                pl.BlockSpec((1, W), lambda b, h, i: (0, 0)),
                pl.BlockSpec((tq, W), lambda b, h, i: (qb0 + b * nq + jnp.minimum(i, nq - 1), COL_DQ // W + h)),
                pl.BlockSpec((T, W), lambda b, h, i: (sb0 + b, COL_DK // W + h)),
                pl.BlockSpec((T, W), lambda b, h, i: (sb0 + b, COL_DV // W + h))]
    args = [lam_l, subln, p, p, p]
    if Lc:
        in_specs += [pl.BlockSpec((None, None, Lc, W), lambda b, h, i: (b, l, 0, h)),
                     pl.BlockSpec((None, None, Lc, W), lambda b, h, i: (b, l, 0, h))]
        args += [cache_k, cache_v]
    in_specs.append(pl.BlockSpec(memory_space=pl.ANY))
    args.append(out)
    scores = pltpu.VMEM((2, S, tq), F32)
    row_max = pltpu.VMEM((2, 8, tq), F32)
    vt_rows = W + 16
    return pl.pallas_call(
        functools.partial(_diff_kernel, Ts=T, Lc=Lc, lam_init=lam_init),
        grid=(n_seq, DIFF_HEADS, nq + 1),
        in_specs=in_specs,
        out_specs=pl.BlockSpec((tq, W), lambda b, h, i: (qb0 + b * nq + jnp.maximum(i - 1, 0), h)),
        out_shape=jax.ShapeDtypeStruct(out.shape, F32),
        scratch_shapes=[pltpu.VMEM((S, W), BF16), pltpu.VMEM((vt_rows, S), BF16), scores, scores, row_max, row_max],
        input_output_aliases={len(args) - 1: 0},
        compiler_params=_params(("arbitrary", "arbitrary", "arbitrary")),
        name="diff_attention",
    )(*args)


def _na_heads(q, score_fn, pv_fn):
    lane = lax.broadcasted_iota(I32, q.shape, 1) // NA_HEAD_DIM
    out = jnp.zeros(q.shape, F32)
    for h in range(NA_HEADS):
        qh = jnp.where(lane == h, q, 0.0).astype(BF16)
        out = jnp.where(lane == h, pv_fn(score_fn(qh, h)), out)
    return out


def _dense_kernel(q_ref, k_ref, v_ref, prev_ref, o_ref):
    del prev_ref
    q = q_ref[...].astype(F32) * (NA_HEAD_DIM ** -0.5 * LOG2E)
    k = k_ref[...].astype(BF16)
    v = v_ref[...].astype(BF16)
    o_ref[...] = _na_heads(q, lambda qh, h: _dot_nt(qh, k), lambda s: _softmax_pv(s, v))


def _dense_attention(p, out, *, n_seq, T):
    C = NA_HEADS * NA_HEAD_DIM
    return pl.pallas_call(
        _dense_kernel,
        grid=(n_seq,),
        in_specs=[pl.BlockSpec((T, C), lambda b: (b, 7)),
                  pl.BlockSpec((T, C), lambda b: (b, 8)),
                  pl.BlockSpec((T, C), lambda b: (b, 9)),
                  pl.BlockSpec(memory_space=pl.ANY)],
        out_specs=pl.BlockSpec((T, C), lambda b: (b, 0)),
        out_shape=jax.ShapeDtypeStruct(out.shape, F32),
        input_output_aliases={3: 0},
        compiler_params=_params(("arbitrary",)),
        name="dense_attention",
    )(p, p, p, out)


def _na_kernel(q_ref, ks_ref, vs_ref, kc_ref, vc_ref, bias_ref, prev_ref, o_ref, *, rows):
    del prev_ref
    r0 = pl.program_id(1) * NA_Q_ROWS
    bs = jnp.clip(r0 - NA_WIN_H // 2, 0, rows - NA_BAND)
    start = pl.multiple_of(bs * GRID_W, GRID_W)
    nb = NA_BAND * GRID_W
    kb = ks_ref[pl.ds(start, nb), :].astype(BF16)
    vb = vs_ref[pl.ds(start, nb), :].astype(BF16)
    kc = kc_ref[...].astype(BF16)
    vc = vc_ref[...].astype(BF16)
    q = q_ref[...].astype(F32) * (NA_HEAD_DIM ** -0.5 * LOG2E)

    def scores(qh, h):
        return _dot_nt(qh, kb) + bias_ref[h], _dot_nt(qh, kc)

    def pv(s):
        s_loc, s_ctx = s
        m = jnp.maximum(jnp.max(s_loc, axis=-1, keepdims=True), jnp.max(s_ctx, axis=-1, keepdims=True))
        e_loc = jnp.exp2(s_loc - m)
        e_ctx = jnp.exp2(s_ctx - m)
        l = jnp.sum(e_loc, axis=-1, keepdims=True) + jnp.sum(e_ctx, axis=-1, keepdims=True)
        return (_dot(e_loc.astype(BF16), vb) + _dot(e_ctx.astype(BF16), vc)) * (1.0 / l)

    o_ref[...] = _na_heads(q, scores, pv)


def _na_bias_table(rpb_l, rows):
    n_ro, n_co = 2 * NA_WIN_H - 1, 2 * NA_WIN_W - 1
    c = np.arange(GRID_W)[:, None]
    kc = np.arange(GRID_W)[None, :]
    cs = np.clip(c - NA_WIN_W // 2, 0, GRID_W - NA_WIN_W)
    col_ok = (kc >= cs) & (kc < cs + NA_WIN_W)
    co = kc - c + (NA_WIN_W - 1)
    pick = ((np.arange(n_co)[:, None, None] == co[None]) & col_ok[None]).astype(np.float32)
    toep = jnp.dot(rpb_l.reshape(NA_HEADS * n_ro, n_co).astype(F32), jnp.asarray(pick.reshape(n_co, -1)),
                   precision=lax.Precision.HIGHEST).reshape(NA_HEADS, n_ro, GRID_W, GRID_W) * LOG2E
    toep = jnp.where(jnp.asarray(col_ok)[None, None], toep, MASKED)
    toep = jnp.concatenate([toep, jnp.full((NA_HEADS, 1, GRID_W, GRID_W), MASKED, F32)], axis=1)
    blk = np.full((3, NA_Q_ROWS, NA_BAND), n_ro, np.int32)
    for v, r0 in enumerate((0, NA_Q_ROWS, rows - NA_Q_ROWS)):
        bs = int(np.clip(r0 - NA_WIN_H // 2, 0, rows - NA_BAND))
        for j in range(NA_Q_ROWS):
            rs = int(np.clip(r0 + j - NA_WIN_H // 2, 0, rows - NA_WIN_H))
            for i in range(NA_BAND):
                if rs <= bs + i < rs + NA_WIN_H:
                    blk[v, j, i] = bs + i - (r0 + j) + (NA_WIN_H - 1)
    tab = toep[:, blk]
    tab = jnp.transpose(tab, (1, 0, 2, 4, 3, 5))
    return tab.reshape(3, NA_HEADS, NA_Q_ROWS * GRID_W, NA_BAND * GRID_W)


def _neighbourhood_attention(p, cache_k, cache_v, bias, out, *, row0, n_seq, T, l):
    C = NA_HEADS * NA_HEAD_DIM
    rows = T // GRID_W
    tq = NA_Q_ROWS * GRID_W
    nq = T // tq
    Lc = cache_k.shape[2]
    qb0 = row0 // tq
    sb0 = row0 // T

    def variant(i):
        r0 = i * NA_Q_ROWS
        return (r0 - jnp.clip(r0 - NA_WIN_H // 2, 0, rows - NA_BAND)) // NA_Q_ROWS

    return pl.pallas_call(
        functools.partial(_na_kernel, rows=rows),
        grid=(n_seq, nq),
        in_specs=[pl.BlockSpec((tq, C), lambda b, i: (qb0 + b * nq + i, 7)),
                  pl.BlockSpec((T, C), lambda b, i: (sb0 + b, 8)),
                  pl.BlockSpec((T, C), lambda b, i: (sb0 + b, 9)),
                  pl.BlockSpec((None, None, Lc, C), lambda b, i: (b, l, 0, 0)),
                  pl.BlockSpec((None, None, Lc, C), lambda b, i: (b, l, 0, 0)),
                  pl.BlockSpec((None, NA_HEADS, tq, NA_BAND * GRID_W), lambda b, i: (variant(i), 0, 0, 0)),
                  pl.BlockSpec(memory_space=pl.ANY)],
        out_specs=pl.BlockSpec((tq, C), lambda b, i: (qb0 + b * nq + i, 0)),
        out_shape=jax.ShapeDtypeStruct(out.shape, F32),
        input_output_aliases={6: 0},
        compiler_params=_params(("arbitrary", "arbitrary")),
        name="neighbourhood_attention",
    )(p, p, p, cache_k, cache_v, bias, out)


def _merge_kernel(pool_ref, dn_ref, na_ref, *refs, D, ctx_blocks):
    mod_ref, g1_ref, g2_ref, w_ref, rw_ref, rb_ref, x1_ref, h2_ref, route_ref, cnt_ref = refs[-10:]
    x_rows = _token_rows(refs[:-10], ctx_blocks)
    parts = [slice(j * TOK_CHUNK, (j + 1) * TOK_CHUNK) for j in range(x1_ref.shape[0] // TOK_CHUNK)]
    c0 = pool_ref.shape[1]
    c1 = c0 + dn_ref.shape[1]
    mix = [_dot(pool_ref[p, :].astype(BF16), w_ref[0:c0, :])
           + _dot(dn_ref[p, :].astype(BF16), w_ref[c0:c1, :])
           + _dot(na_ref[p, :].astype(BF16), w_ref[c1:, :]) for p in parts]
    x1 = [x_rows(p) + mod_ref[:, 2 * D:3 * D] * (_rms(m) * g1_ref[...]) for p, m in zip(parts, mix)]
    for p, v in zip(parts, x1):
        x1_ref[p, :] = v
    h2 = [(_rms(v) * g2_ref[...]) * (1.0 + mod_ref[:, 4 * D:5 * D]) + mod_ref[:, 3 * D:4 * D] for v in x1]
    for p, v in zip(parts, h2):
        h2_ref[p, :] = v

    rw_hi, rw_lo = _split(rw_ref[...])
    h2_split = [_split(v) for v in h2]
    logits = [_dot(hi, rw_hi) + _dot(lo, rw_hi) + _dot(hi, rw_lo) + rb_ref[...] for hi, lo in h2_split]
    lane = lax.broadcasted_iota(I32, (TOK_CHUNK, LANES), 1)
    lane_f = lane.astype(F32)
    work = logits
    vals, hots = [], []
    for _ in range(TOP_K):
        mx = [jnp.max(w, axis=-1, keepdims=True) for w in work]
        idx = [jnp.min(jnp.where(w == m, lane_f, float(LANES)), axis=-1, keepdims=True)
               for w, m in zip(work, mx)]
        hot = [lane_f == i for i in idx]
        vals.append(mx)
        hots.append(hot)
        work = [jnp.where(h, -jnp.inf, w) for h, w in zip(hot, work)]

    r = lax.broadcasted_iota(I32, (TOK_CHUNK, TOK_CHUNK), 0)
    c = lax.broadcasted_iota(I32, (TOK_CHUNK, TOK_CHUNK), 1)
    before = jnp.where(c < r, 1.0, 0.0).astype(BF16)
    er = lax.broadcasted_iota(I32, (LANES, LANES), 0)
    ec = lax.broadcasted_iota(I32, (LANES, LANES), 1)
    earlier = jnp.where(er < ec, 1.0, 0.0).astype(BF16)
    for j, p in enumerate(parts):
        hot_j = [hots[k][j] for k in range(TOP_K)]
        es = [jnp.exp(vals[k][j] - vals[0][j]) for k in range(TOP_K)]
        inv = 1.0 / (es[0] + es[1] + es[2] + es[3])
        sel = jnp.zeros((TOK_CHUNK, LANES), F32)
        for hot in hot_j:
            sel = jnp.where(hot, 1.0, sel)
        rank = _dot(before, sel.astype(BF16))
        cnt = jnp.sum(sel, axis=0, keepdims=True)
        cnt_ref[j] = cnt
        run = jnp.floor((cnt + (RUN_ALIGN - 1)) * (1.0 / RUN_ALIGN)) * RUN_ALIGN
        run_start = _dot(jnp.broadcast_to(run, (8, LANES)).astype(BF16), earlier)[0:1, :]
        pos = rank + run_start
        route = jnp.zeros((TOK_CHUNK, LANES), F32)
        for k in range(TOP_K):
            e_k = jnp.sum(jnp.where(hot_j[k], lane_f, 0.0), axis=-1, keepdims=True)
            p_k = jnp.sum(jnp.where(hot_j[k], pos, 0.0), axis=-1, keepdims=True)
            route = jnp.where(lane == k, e_k, route)
            route = jnp.where(lane == TOP_K + k, p_k, route)
            route = jnp.where(lane == 2 * TOP_K + k, es[k] * inv, route)
        route_ref[p, :] = route


def _merge_route(pool_o, dn, na_o, x, mod4, g1, g2, w_out_bf, rw_pad, rb_pad, *, tm, ctx_blocks,
                 blocks_per_seq, l):
    xs, x_specs = _token_specs(x, tm, ctx_blocks)
    N, D = sum(a.shape[0] for a in xs), xs[0].shape[1]
    per_step = tm // TOK_CHUNK

    def grp(i):
        return jnp.where(i < ctx_blocks, 0, 1 + (i - ctx_blocks) // blocks_per_seq)

    row = lambda w: pl.BlockSpec((tm, w), lambda i: (i, 0))
    full = lambda a: pl.BlockSpec(a.shape, lambda i: (0,) * a.ndim)
    return pl.pallas_call(
        functools.partial(_merge_kernel, D=D, ctx_blocks=ctx_blocks),
        grid=(N // tm,),
        in_specs=[row(pool_o.shape[1]), row(dn.shape[1]), row(na_o.shape[1])] + x_specs + [
                  pl.BlockSpec((None, None, 1, N_MOD * D), lambda i: (l, grp(i), 0, 0)),
                  full(g1), full(g2), full(w_out_bf), full(rw_pad), full(rb_pad)],
        out_specs=[row(D), row(D), row(LANES), pl.BlockSpec((per_step, 1, LANES), lambda i: (i, 0, 0))],
        out_shape=[jax.ShapeDtypeStruct((N, D), F32), jax.ShapeDtypeStruct((N, D), F32),
                   jax.ShapeDtypeStruct((N, LANES), F32), jax.ShapeDtypeStruct((N // TOK_CHUNK, 1, LANES), F32)],
        compiler_params=_params(("arbitrary",)),
        name="merge_route",
    )(pool_o, dn, na_o, *xs, mod4, g1, g2, w_out_bf, rw_pad, rb_pad)


def _pack_pairs(x):
    C = x.shape[1] // 2
    bits = lax.bitcast_convert_type(x.astype(BF16).astype(F32), jnp.uint32)
    return bits[:, C:] | (bits[:, :C] >> 16)


def _unpack_pairs(w):
    lo = lax.bitcast_convert_type(w << 16, F32)
    hi = lax.bitcast_convert_type(w & jnp.uint32(0xFFFF0000), F32)
    return jnp.concatenate([lo, hi], axis=1).astype(BF16)


def _run_sizes(lo, hi):
    return [1 << k for k in range(hi.bit_length() - 1, lo.bit_length() - 2, -1)]


def _run_copies(n, src_at, dst_at, sem, wait, sizes):
    for size in sizes:
        @pl.when((n & size) != 0)
        def _(size=size):
            off = n & -(2 * size)
            cp = pltpu.make_async_copy(src_at(off, size), dst_at(off, size), sem)
            cp.wait() if wait else cp.start()


def _block_runs(step, n_exp, run_ref, src_ref, dst_ref, large_ref, hbm_ref, buf, sem, *, to_hbm, wait):
    def each_run(sizes):
        def body(e, carry):
            j = step * n_exp + e
            so = src_ref[j]
            do = dst_ref[j]
            in_buf = lambda o, s: buf.at[pl.ds(pl.multiple_of(so + o, RUN_ALIGN), s)]
            in_hbm = lambda o, s: hbm_ref.at[pl.ds(pl.multiple_of(do + o, RUN_ALIGN), s)]
            if to_hbm:
                _run_copies(run_ref[j], in_buf, in_hbm, sem, wait, sizes)
            else:
                _run_copies(run_ref[j], in_hbm, in_buf, sem, wait, sizes)
            return carry
        lax.fori_loop(0, n_exp, body, 0, unroll=RUN_UNROLL)

    each_run(_run_sizes(RUN_ALIGN, RUN_LARGE // 2))

    @pl.when(large_ref[step] != 0)
    def _():
        each_run(_run_sizes(RUN_LARGE, TOK_CHUNK))


def _dispatch_kernel(run_ref, src_ref, dst_ref, large_ref, tail_ref, taildst_ref, nu_ref, route_ref, h_ref, xs_ref,
                     sorted_buf, zero_buf, sems, *, n_exp):
    b = pl.program_id(0)
    nb = pl.num_programs(0)
    slot = b % 2
    runs = functools.partial(_block_runs, n_exp=n_exp, run_ref=run_ref, src_ref=src_ref, dst_ref=dst_ref,
                             large_ref=large_ref, hbm_ref=xs_ref, to_hbm=True)

    @pl.when(b >= 2)
    def _():
        runs(b - 2, buf=sorted_buf.at[slot], sem=sems.at[slot], wait=True)

    route = route_ref[...]
    col = lax.broadcasted_iota(I32, (TOK_CHUNK, sorted_buf.shape[1]), 1).astype(F32)
    place = jnp.zeros(col.shape, F32)
    for k in range(TOP_K):
        place = jnp.where(col == route[:, TOP_K + k:TOP_K + k + 1], 1.0, place)
    srt = lax.dot_general(place.astype(BF16), h_ref[...].astype(BF16), (((0,), (0,)), ((), ())),
                          preferred_element_type=F32)
    sorted_buf[slot] = _pack_pairs(srt)
    runs(b, buf=sorted_buf.at[slot], sem=sems.at[slot], wait=False)

    @pl.when(b == nb - 1)
    def _():
        @pl.when(b >= 1)
        def _():
            runs(b - 1, buf=sorted_buf.at[1 - slot], sem=sems.at[1 - slot], wait=True)
        runs(b, buf=sorted_buf.at[slot], sem=sems.at[slot], wait=True)

        zero_buf[...] = jnp.zeros(zero_buf.shape, zero_buf.dtype)
        sem = sems.at[0]

        def each_tail(wait):
            def body(e, carry):
                do = taildst_ref[e]
                _run_copies(tail_ref[e],
                            lambda o, s: zero_buf.at[pl.ds(0, s)],
                            lambda o, s: xs_ref.at[pl.ds(pl.multiple_of(do + o, RUN_ALIGN), s)], sem, wait,
                            _run_sizes(RUN_ALIGN, MOE_ROWS // 2))
                return carry
            lax.fori_loop(0, n_exp, body, 0)

        def spare_block(wait):
            def body(i, carry):
                cp = pltpu.make_async_copy(zero_buf.at[pl.ds(0, MOE_ROWS)],
                                           xs_ref.at[pl.ds(pl.multiple_of(i * MOE_ROWS, MOE_ROWS), MOE_ROWS)], sem)
                cp.wait() if wait else cp.start()
                return carry
            lax.fori_loop(nu_ref[0], xs_ref.shape[0] // MOE_ROWS, body, 0)

        each_tail(False)
        spare_block(False)
        each_tail(True)
        spare_block(True)


def _dispatch(plan, n_used, route, h2, n_rows, n_exp):
    N, D = h2.shape
    C = D // 2
    grid_spec = pltpu.PrefetchScalarGridSpec(
        num_scalar_prefetch=7,
        grid=(N // TOK_CHUNK,),
        in_specs=[pl.BlockSpec((TOK_CHUNK, LANES), lambda i, *_: (i, 0)),
                  pl.BlockSpec((TOK_CHUNK, D), lambda i, *_: (i, 0))],
        out_specs=pl.BlockSpec(memory_space=pl.ANY),
        scratch_shapes=[pltpu.VMEM((2, _sorted_rows(n_exp), C), jnp.uint32),
                        pltpu.VMEM((max(TOK_CHUNK, MOE_ROWS), C), jnp.uint32),
                        pltpu.SemaphoreType.DMA((2,))],
    )
    return pl.pallas_call(
        functools.partial(_dispatch_kernel, n_exp=n_exp),
        grid_spec=grid_spec,
        out_shape=jax.ShapeDtypeStruct((n_rows, C), jnp.uint32),
        compiler_params=_params(("arbitrary",)),
        name="moe_dispatch",
    )(plan["run"], plan["src"], plan["dst"], plan["large"], plan["tail"], plan["tail_dst"], n_used, route, h2)


def _expert_kernel(be_ref, nu_ref, rows_ref, xs_ref, w1_ref, b1_ref, w2_ref, b2_ref, o_ref, w1b, w2b, *, F):
    del nu_ref
    i = pl.program_id(0)
    e = be_ref[i]
    prev = be_ref[jnp.maximum(i - 1, 0)]
    rows = rows_ref[i]
    half = MOE_ROWS // 2

    @pl.when((i == 0) | (e != prev))
    def _():
        w1b[...] = w1_ref[...].astype(BF16)
        w2b[...] = w2_ref[...].astype(BF16)

    def ffn(xw):
        hh = _dot(_unpack_pairs(xw), w1b[...]) + b1_ref[...]
        g = jnp.minimum(hh[:, 0:F], SWIGLU_LIMIT)
        u = jnp.clip(hh[:, F:2 * F], -SWIGLU_LIMIT, SWIGLU_LIMIT)
        a = (g * (1.0 / (1.0 + jnp.exp(-SWIGLU_ALPHA * g)))) * (u + 1.0)
        return _pack_pairs(_dot(a.astype(BF16), w2b[...]) + b2_ref[...])

    @pl.when(rows > half)
    def _():
        o_ref[...] = ffn(xs_ref[...])

    @pl.when((rows > 0) & (rows <= half))
    def _():
        o_ref[0:half, :] = ffn(xs_ref[0:half, :])
        o_ref[half:MOE_ROWS, :] = jnp.zeros((MOE_ROWS - half, o_ref.shape[1]), o_ref.dtype)

    @pl.when(rows == 0)
    def _():
        o_ref[...] = jnp.zeros(o_ref.shape, o_ref.dtype)


def _experts(block_e, n_used, block_rows, xs, w1, b1, w2, b2, *, l):
    R, C = xs.shape
    L, E, D, F2 = w1.shape
    F = F2 // 2
    nblk = R // MOE_ROWS
    grid_spec = pltpu.PrefetchScalarGridSpec(
        num_scalar_prefetch=3,
        grid=(nblk,),
        in_specs=[pl.BlockSpec((MOE_ROWS, C), lambda i, be, nu, br: (jnp.minimum(i, nu[0] - 1), 0)),
                  pl.BlockSpec((None, None, D, F2), lambda i, be, nu, br: (l, be[i], 0, 0)),
                  pl.BlockSpec((None, None, 1, F2), lambda i, be, nu, br: (l, be[i], 0, 0)),
                  pl.BlockSpec((None, None, F, D), lambda i, be, nu, br: (l, be[i], 0, 0)),
                  pl.BlockSpec((None, None, 1, D), lambda i, be, nu, br: (l, be[i], 0, 0))],
        out_specs=pl.BlockSpec((MOE_ROWS, C), lambda i, be, nu, br: (i, 0)),
        scratch_shapes=[pltpu.VMEM((D, F2), BF16), pltpu.VMEM((F, D), BF16)],
    )
    return pl.pallas_call(
        functools.partial(_expert_kernel, F=F),
        grid_spec=grid_spec,
        out_shape=jax.ShapeDtypeStruct((R, C), jnp.uint32),
        compiler_params=_params(("arbitrary",)),
        name="moe_experts",
    )(block_e, n_used, block_rows, xs, w1, b1.reshape(L, E, 1, F2), w2, b2.reshape(L, E, 1, D))


def _combine_kernel(run_ref, src_ref, dst_ref, large_ref, yb_ref, route_ref, x1_ref, mod_ref, g_ref, *rest, D,
                    n_exp, ctx_blocks, split):
    if split:
        ctx_ref, lat_ref, sorted_buf, sems = rest
    else:
        o_ref, sorted_buf, sems = rest
    b = pl.program_id(0)
    nb = pl.num_programs(0)
    slot = b % 2
    runs = functools.partial(_block_runs, n_exp=n_exp, run_ref=run_ref, src_ref=src_ref, dst_ref=dst_ref,
                             large_ref=large_ref, hbm_ref=yb_ref, to_hbm=False)

    @pl.when(b == 0)
    def _():
        sorted_buf[...] = jnp.zeros(sorted_buf.shape, sorted_buf.dtype)
        runs(b, buf=sorted_buf.at[slot], sem=sems.at[slot], wait=False)

    @pl.when(b + 1 < nb)
    def _():
        runs(b + 1, buf=sorted_buf.at[1 - slot], sem=sems.at[1 - slot], wait=False)

    runs(b, buf=sorted_buf.at[slot], sem=sems.at[slot], wait=True)
    route = route_ref[...]
    col = lax.broadcasted_iota(I32, (TOK_CHUNK, sorted_buf.shape[1]), 1).astype(F32)
    gate = jnp.zeros(col.shape, F32)
    for k in range(TOP_K):
        gate = jnp.where(col == route[:, TOP_K + k:TOP_K + k + 1], route[:, 2 * TOP_K + k:2 * TOP_K + k + 1], gate)
    g_hi, g_lo = _split(gate)
    yb = _unpack_pairs(sorted_buf[slot])
    y = _dot(g_hi, yb) + _dot(g_lo, yb)
    x2 = x1_ref[...] + mod_ref[:, 5 * D:6 * D] * (_rms(y) * g_ref[...])
    if split:
        @pl.when(b < ctx_blocks)
        def _():
            ctx_ref[...] = x2

        @pl.when(b >= ctx_blocks)
        def _():
            lat_ref[...] = x2
    else:
        o_ref[...] = x2


def _combine(plan, yb, route, x1, mod4, g3, *, n_exp, ctx_blocks, blocks_per_seq, l, split):
    N, D = x1.shape
    row = lambda f: pl.BlockSpec((TOK_CHUNK, D), lambda i, *_: (f(i), 0))
    if split:
        out_specs = [row(lambda i: jnp.minimum(i, ctx_blocks - 1)), row(lambda i: jnp.maximum(i - ctx_blocks, 0))]
        out_shape = [jax.ShapeDtypeStruct((ctx_blocks * TOK_CHUNK, D), F32),
                     jax.ShapeDtypeStruct((N - ctx_blocks * TOK_CHUNK, D), F32)]
    else:
        out_specs = row(lambda i: i)
        out_shape = jax.ShapeDtypeStruct((N, D), F32)

    def grp(i):
        return jnp.where(i < ctx_blocks, 0, 1 + (i - ctx_blocks) // blocks_per_seq)

    grid_spec = pltpu.PrefetchScalarGridSpec(
        num_scalar_prefetch=4,
        grid=(N // TOK_CHUNK,),
        in_specs=[pl.BlockSpec(memory_space=pl.ANY),
                  pl.BlockSpec((TOK_CHUNK, LANES), lambda i, *_: (i, 0)),
                  pl.BlockSpec((TOK_CHUNK, D), lambda i, *_: (i, 0)),
                  pl.BlockSpec((None, None, 1, N_MOD * D), lambda i, *_: (l, grp(i), 0, 0)),
                  pl.BlockSpec((1, D), lambda i, *_: (0, 0))],
        out_specs=out_specs,
        scratch_shapes=[pltpu.VMEM((2, _sorted_rows(n_exp), yb.shape[1]), jnp.uint32),
                        pltpu.SemaphoreType.DMA((2,))],
    )
    return pl.pallas_call(
        functools.partial(_combine_kernel, D=D, n_exp=n_exp, ctx_blocks=ctx_blocks, split=split),
        grid_spec=grid_spec,
        out_shape=out_shape,
        compiler_params=_params(("arbitrary",)),
        name="moe_combine",
    )(plan["run"], plan["src"], plan["dst"], plan["large"], yb, route, x1, mod4, g3)


def _rope_tables(Ts, tm):
    nf = DIFF_QK_DIM // 4
    inv = ROPE_BASE ** (-jnp.arange(nf, dtype=F32) / nf)
    t = jnp.arange(Ts)
    pos = jnp.stack([(t // GRID_W).astype(F32), (t % GRID_W).astype(F32)], axis=1)
    ang = pos[:, :, None] * inv[None, None, :]
    cos = jnp.repeat(jnp.cos(ang)[:, :, None, :], 2, axis=2).reshape(Ts, DIFF_QK_DIM)
    sin = jnp.sin(ang)
    sin = jnp.stack([-sin, sin], axis=2).reshape(Ts, DIFF_QK_DIM)
    reps = LANES // DIFF_QK_DIM
    cos = jnp.concatenate([jnp.ones((tm, LANES), F32), jnp.tile(cos, (1, reps))], axis=0)
    sin = jnp.concatenate([jnp.zeros((tm, LANES), F32), jnp.tile(sin, (1, reps))], axis=0)
    return cos, sin


def _block_diag(w):
    G, a, b = w.shape
    out = jnp.zeros((G * a, G * b), w.dtype)
    for g in range(G):
        out = out.at[g * a:(g + 1) * a, g * b:(g + 1) * b].set(w[g])
    return out


def _route_plan(counts, n_exp, nblk):
    cnt = counts[:, 0, :n_exp].astype(I32)
    run = (cnt + RUN_ALIGN - 1) // RUN_ALIGN * RUN_ALIGN
    src = jnp.cumsum(run, axis=1) - run
    tot = jnp.sum(run, axis=0)
    region = (tot + MOE_ROWS - 1) // MOE_ROWS * MOE_ROWS
    region_end = jnp.cumsum(region)
    region_start = region_end - region
    dst = region_start[None, :] + jnp.cumsum(run, axis=0) - run
    n_used = (region_end[-1] // MOE_ROWS).astype(I32)
    blk = jnp.arange(nblk, dtype=I32) * MOE_ROWS
    block_e = jnp.minimum(jnp.sum((blk[:, None] >= region_end[None, :]).astype(I32), axis=1), n_exp - 1)
    last = jnp.sum(jnp.where(jnp.arange(nblk) == n_used - 1, block_e, 0))
    used = jnp.arange(nblk) < n_used
    block_rows = jnp.where(used, jnp.clip((region_start + tot)[block_e] - blk, 0, MOE_ROWS), 0).astype(I32)
    block_e = jnp.where(used, block_e, last).astype(I32)
    plan = dict(run=run.reshape(-1).astype(I32), src=src.reshape(-1).astype(I32), dst=dst.reshape(-1).astype(I32),
                large=jnp.any(run >= RUN_LARGE, axis=1).astype(I32),
                tail=(region - tot).astype(I32), tail_dst=(region_start + tot).astype(I32))
    return plan, block_e, n_used.reshape(1), block_rows


def kernel(x_prompt, x_sample, cache_diff_k, cache_diff_v, cache_na_k, cache_na_v, c, c_ctx, w_ada, b_ada,
           norm_gain, w_in, w_out, pool_w, pool_scale, diff_lambda, diff_subln, na_rpb, router_w, router_b,
           moe_w1, moe_b1, moe_w2, moe_b2):
    Bp, Tp, D = x_prompt.shape
    Bs, Ts, _ = x_sample.shape
    L = w_ada.shape[0]
    E = router_w.shape[-1]
    Np, Ns = Bp * Tp, Bs * Ts
    N = Np + Ns
    tm = TOK_CHUNK
    assert Np % Ts == 0 or Bs == 0, "context rows must be a whole number of latent-sequence blocks"
    assert Np % tm == 0 and Ts % tm == 0 and Ts % (NA_Q_ROWS * GRID_W) == 0
    assert Ts // GRID_W >= NA_BAND + NA_Q_ROWS
    ctx_blocks, blocks_per_seq = Np // tm, Ts // tm

    G = 16
    cvec = jnp.zeros((G, D), F32).at[0].set(c_ctx).at[1:1 + Bs].set(c)
    mod4 = _modulation(cvec, w_ada, b_ada).reshape(L, G, 1, N_MOD * D)
    tp = PROJ_ROWS
    assert Np % tp == 0 and Ts % tp == 0 and Np % MERGE_ROWS == 0 and Ts % MERGE_ROWS == 0
    cos_t, sin_t = _rope_tables(Ts, tp)
    w_in_bf = w_in.astype(BF16)
    w_out_bf = w_out.astype(BF16)
    rw_pad = jnp.zeros((L, D, LANES), F32).at[:, :, :E].set(router_w)
    rb_pad = jnp.full((L, 1, LANES), MASKED, F32).at[:, 0, :E].set(router_b)
    ck = cache_diff_k.reshape(Bs, L, -1, DIFF_HEADS * 2 * DIFF_QK_DIM)
    cv = cache_diff_v.reshape(Bs, L, -1, DIFF_HEADS * DIFF_V_DIM)
    nk = cache_na_k.reshape(Bs, L, -1, NA_HEADS * NA_HEAD_DIM)
    nv = cache_na_v.reshape(Bs, L, -1, NA_HEADS * NA_HEAD_DIM)
    nb = N // TOK_CHUNK
    nblk = -(-(N * TOP_K + nb * E * (RUN_ALIGN - 1) + E * (MOE_ROWS - 1)) // MOE_ROWS)
    blocks = dict(ctx_blocks=ctx_blocks, blocks_per_seq=blocks_per_seq)

    x = (x_prompt.reshape(Np, D), x_sample.reshape(Ns, D))
    new_dk, new_dv, new_nk, new_nv = [], [], [], []
    for l in range(L):
        lam_init = 0.8 - 0.6 * math.exp(-0.3 * l)
        g = norm_gain[l]
        p, dk_l, dv_l, nk_l, nv_l = _in_projection(x, mod4, g[0:1], w_in_bf[l], cos_t, sin_t, tm=tp, l=l,
                                                   ctx_blocks=Np // tp, blocks_per_seq=Ts // tp)

        pw = _block_diag(pool_w[l]).astype(BF16)
        ps = pool_scale[l].reshape(1, -1)
        pool_o = jnp.zeros((N, pw.shape[0]), F32)
        pool_o = _pool(p, pw, ps, row0=0, n_seq=Bp, T=Tp, out=pool_o)
        pool_o = _pool(p, pw, ps, row0=Np, n_seq=Bs, T=Ts, out=pool_o)

        sub = diff_subln[l].reshape(1, -1)
        dn = jnp.zeros((N, DIFF_HEADS * DIFF_V_DIM), F32)
        dn = _diff_attention_ctx(p, diff_lambda[l], sub, dn, n_seq=Bp, T=Tp, lam_init=lam_init)
        dn = _diff_attention(p, diff_lambda[l], sub, dn, row0=Np, n_seq=Bs, T=Ts, tq=DIFF_Q_ROWS, lam_init=lam_init,
                             cache_k=ck, cache_v=cv, l=l)

        na_o = jnp.zeros((N, NA_HEADS * NA_HEAD_DIM), F32)
        na_o = _dense_attention(p, na_o, n_seq=Bp, T=Tp)
        bias = _na_bias_table(na_rpb[l], Ts // GRID_W)
        na_o = _neighbourhood_attention(p, nk, nv, bias, na_o, row0=Np, n_seq=Bs, T=Ts, l=l)

        x1, h2, route, counts = _merge_route(pool_o, dn, na_o, x, mod4, g[1:2], g[2:3], w_out_bf[l],
                                             rw_pad[l], rb_pad[l], tm=MERGE_ROWS, l=l,
                                             ctx_blocks=Np // MERGE_ROWS, blocks_per_seq=Ts // MERGE_ROWS)
        plan, block_e, n_used, block_rows = _route_plan(counts, E, nblk)
        xs = _dispatch(plan, n_used, route, h2, nblk * MOE_ROWS, E)
        yb = _experts(block_e, n_used, block_rows, xs, moe_w1, moe_b1, moe_w2, moe_b2, l=l)
        x = _combine(plan, yb, route, x1, mod4, g[3:4], n_exp=E, l=l, split=l == L - 1, **blocks)

        new_dk.append(dk_l.reshape(Bp, Tp, DIFF_HEADS, 2 * DIFF_QK_DIM))
        new_dv.append(dv_l.reshape(Bp, Tp, DIFF_HEADS, DIFF_V_DIM))
        new_nk.append(nk_l.reshape(Bp, Tp, NA_HEADS, NA_HEAD_DIM))
        new_nv.append(nv_l.reshape(Bp, Tp, NA_HEADS, NA_HEAD_DIM))

    return (x[0].reshape(Bp, Tp, D), x[1].reshape(Bs, Ts, D),
            jnp.stack(new_dk, axis=1), jnp.stack(new_dv, axis=1),
            jnp.stack(new_nk, axis=1), jnp.stack(new_nv, axis=1))
```

```python
import functools
import math

import numpy as np
import jax
import jax.numpy as jnp
from jax import lax
from jax.experimental import pallas as pl
from jax.experimental.pallas import tpu as pltpu

F32 = jnp.float32
BF16 = jnp.bfloat16
I32 = jnp.int32

GRID_W = 64
POOL_GROUPS = 4
POOL_MAX_HALF = 8
DIFF_HEADS = 4
DIFF_V_DIM = 128
DIFF_QK_DIM = 64
NA_HEADS = 4
NA_HEAD_DIM = 64
NA_WIN_H = 8
NA_WIN_W = 16
NA_Q_ROWS = 4
NA_BAND = 12
TOP_K = 4
SWIGLU_ALPHA = 1.702
SWIGLU_LIMIT = 7.0
ROPE_BASE = 10000.0
NORM_EPS = 1e-6
N_MOD = 6

LOG2E = 1.4426950408889634
MASKED = -1e30
LANES = 128
MXU_TILE = 256
MOE_ROWS = 512
COL_DQ, COL_DK, COL_DV = 256, 768, 1280
COL_NQ, COL_NK, COL_NV = 1792, 2048, 2304
DIFF_KEYS = 256
PROJ_ROWS = 512
MERGE_ROWS = 1024
DIFF_Q_ROWS = 256
TOK_CHUNK = 256
RUN_ALIGN = 8
RUN_LARGE = 128
RUN_UNROLL = 4
VMEM_LIMIT = 56 * 1024 * 1024


def _sorted_rows(n_exp):
    rows = TOK_CHUNK * TOP_K + n_exp * (RUN_ALIGN - 1)
    return -(-rows // MXU_TILE) * MXU_TILE


def _params(sem, vmem=VMEM_LIMIT):
    return pltpu.CompilerParams(dimension_semantics=sem, vmem_limit_bytes=vmem)


def _dot(a, b):
    return jnp.dot(a, b, preferred_element_type=F32)


def _dot_nt(a, b):
    return lax.dot_general(a, b, (((1,), (1,)), ((), ())), preferred_element_type=F32)


def _split(x):
    hi = x.astype(BF16)
    return hi, (x - hi.astype(F32)).astype(BF16)


def _dot3(a, b):
    ah, al = _split(a)
    bh, bl = _split(b)
    return _dot(ah, bh) + _dot(al, bh) + _dot(ah, bl)


def _rms(x):
    return x * lax.rsqrt(jnp.mean(x * x, axis=-1, keepdims=True) + NORM_EPS)


def _mod_kernel(c_ref, w_ref, b_ref, o_ref):
    c = c_ref[...]
    a = c * (1.0 / (1.0 + jnp.exp(-c)))
    o_ref[...] = _dot3(a, w_ref[...]) + b_ref[...]


def _modulation(cvec, w_ada, b_ada):
    L, D, W = w_ada.shape
    G = cvec.shape[0]
    return pl.pallas_call(
        _mod_kernel,
        grid=(L, W // D),
        in_specs=[pl.BlockSpec((G, D), lambda l, j: (0, 0)),
                  pl.BlockSpec((None, D, D), lambda l, j: (l, 0, j)),
                  pl.BlockSpec((None, 1, D), lambda l, j: (l, 0, j))],
        out_specs=pl.BlockSpec((None, G, D), lambda l, j: (l, 0, j)),
        out_shape=jax.ShapeDtypeStruct((L, G, W), F32),
        compiler_params=_params(("arbitrary", "arbitrary")),
        name="ada_modulation",
    )(cvec, w_ada, b_ada.reshape(L, 1, W))


def _token_rows(x_refs, ctx_blocks):
    if len(x_refs) == 1:
        return lambda rows=slice(None): x_refs[0][rows, :]
    is_ctx = pl.program_id(0) < ctx_blocks
    return lambda rows=slice(None): jnp.where(is_ctx, x_refs[0][rows, :], x_refs[1][rows, :])


def _token_specs(x, tm, ctx_blocks):
    if not isinstance(x, tuple):
        return [x], [pl.BlockSpec((tm, x.shape[1]), lambda i: (i, 0))]
    D = x[0].shape[1]
    return list(x), [pl.BlockSpec((tm, D), lambda i: (jnp.minimum(i, ctx_blocks - 1), 0)),
                     pl.BlockSpec((tm, D), lambda i: (jnp.maximum(i - ctx_blocks, 0), 0))]


def _inproj_kernel(*refs, D, rope_lo, rope_hi, ctx_blocks):
    mod_ref, g_ref, w_ref, cos_ref, sin_ref, o_ref, dk_ref, dv_ref, nk_ref, nv_ref = refs[-10:]
    h = _rms(_token_rows(refs[:-10], ctx_blocks)()) * g_ref[...]
    h = h * (1.0 + mod_ref[:, D:2 * D]) + mod_ref[:, 0:D]
    p = _dot(h.astype(BF16), w_ref[...])
    W = p.shape[1]
    o_ref[:, 0:rope_lo] = p[:, 0:rope_lo].astype(o_ref.dtype)
    o_ref[:, rope_hi:W] = p[:, rope_hi:W].astype(o_ref.dtype)
    cos = cos_ref[...]
    sin = sin_ref[...]
    lane = lax.broadcasted_iota(I32, cos.shape, 1)
    first = (lane % 32) < 16
    for c0 in range(rope_lo, rope_hi, LANES):
        xc = p[:, c0:c0 + LANES]
        partner = jnp.where(first, pltpu.roll(xc, LANES - 16, 1), pltpu.roll(xc, 16, 1))
        o_ref[:, c0:c0 + LANES] = (xc * cos + partner * sin).astype(o_ref.dtype)

    @pl.when(pl.program_id(0) < ctx_blocks)
    def _():
        dk_ref[...] = p[:, COL_DK:COL_DV]
        dv_ref[...] = p[:, COL_DV:COL_NQ]
        nk_ref[...] = p[:, COL_NK:COL_NV]
        nv_ref[...] = p[:, COL_NV:W]


def _in_projection(x, mod4, gain, w_bf, cos_t, sin_t, *, tm, ctx_blocks, blocks_per_seq, l):
    xs, x_specs = _token_specs(x, tm, ctx_blocks)
    N, D = sum(a.shape[0] for a in xs), xs[0].shape[1]
    W = w_bf.shape[1]

    def grp(i):
        return jnp.where(i < ctx_blocks, 0, 1 + (i - ctx_blocks) // blocks_per_seq)

    def rope_blk(i):
        return jnp.where(i < ctx_blocks, 0, 1 + (i - ctx_blocks) % blocks_per_seq)

    kern = functools.partial(_inproj_kernel, D=D, rope_lo=COL_DQ, rope_hi=COL_DV, ctx_blocks=ctx_blocks)
    ctx_rows = ctx_blocks * tm
    ctx_out = lambda w: pl.BlockSpec((tm, w), lambda i: (jnp.minimum(i, ctx_blocks - 1), 0))
    widths = (COL_DV - COL_DK, COL_NQ - COL_DV, COL_NV - COL_NK, W - COL_NV)
    return pl.pallas_call(
        kern,
        grid=(N // tm,),
        in_specs=x_specs + [
                  pl.BlockSpec((None, None, 1, N_MOD * D), lambda i: (l, grp(i), 0, 0)),
                  pl.BlockSpec((1, D), lambda i: (0, 0)),
                  pl.BlockSpec((D, W), lambda i: (0, 0)),
                  pl.BlockSpec((tm, LANES), lambda i: (rope_blk(i), 0)),
                  pl.BlockSpec((tm, LANES), lambda i: (rope_blk(i), 0))],
        out_specs=[pl.BlockSpec((tm, W), lambda i: (i, 0))] + [ctx_out(w) for w in widths],
        out_shape=[jax.ShapeDtypeStruct((N, W), BF16)] + [jax.ShapeDtypeStruct((ctx_rows, w), F32) for w in widths],
        compiler_params=_params(("arbitrary",)),
        name="in_projection",
    )(*xs, mod4, gain, w_bf, cos_t, sin_t)


def _pool_kernel(u_ref, w_ref, sc_ref, o_ref, pad_ref, *, T, CH):
    H = 2 * POOL_MAX_HALF
    zeros = jnp.zeros((H, pad_ref.shape[1]), F32)
    pad_ref[0:H, :] = zeros
    pad_ref[H + T:2 * H + T, :] = zeros
    pad_ref[H:H + T, :] = u_ref[...].astype(F32)
    C = pad_ref.shape[1]
    lane = lax.broadcasted_iota(I32, (CH, C), 1)
    group = lane // (C // POOL_GROUPS)
    half = jnp.left_shift(1, group)
    row = lax.broadcasted_iota(I32, (CH, C), 0)
    R = CH + 2 * H

    def body(ci, carry):
        base = pl.multiple_of(ci * CH, CH)
        win = pad_ref[pl.ds(base, R), :]
        run = win
        acc = None
        for g in range(POOL_GROUPS):
            run = run + pltpu.roll(run, R - (1 << g), 0)
            start = H - (1 << g)
            part = run[start:start + CH, :]
            acc = part if acc is None else jnp.where(group == g, part, acc)
        t = row + base
        cnt = jnp.minimum(t + half, T) - jnp.maximum(t - half, 0)
        d = acc / cnt.astype(F32) - win[H:H + CH, :]
        o_ref[pl.ds(base, CH), :] = _dot(d.astype(BF16), w_ref[...]) * sc_ref[...]
        return carry

    lax.fori_loop(0, T // CH, body, 0)


def _pool(p, w_bd, scale, *, row0, n_seq, T, out):
    C = w_bd.shape[0]
    CH = min(T, 256)
    blk0 = row0 // T

    def kern(u_ref, w_ref, sc_ref, prev_ref, o_ref, pad_ref):
        del prev_ref
        _pool_kernel(u_ref, w_ref, sc_ref, o_ref, pad_ref, T=T, CH=CH)

    return pl.pallas_call(
        kern,
        grid=(n_seq,),
        in_specs=[pl.BlockSpec((T, C), lambda s: (blk0 + s, 0)),
                  pl.BlockSpec((C, C), lambda s: (0, 0)),
                  pl.BlockSpec((1, C), lambda s: (0, 0)),
                  pl.BlockSpec(memory_space=pl.ANY)],
        out_specs=pl.BlockSpec((T, C), lambda s: (blk0 + s, 0)),
        out_shape=jax.ShapeDtypeStruct(out.shape, F32),
        scratch_shapes=[pltpu.VMEM((T + 4 * POOL_MAX_HALF, C), F32)],
        input_output_aliases={3: 0},
        compiler_params=_params(("arbitrary",)),
        name="pool_mixer",
    )(p, w_bd, scale, out)


def _softmax_pv(s, v):
    m = jnp.max(s, axis=-1, keepdims=True)
    e = jnp.exp2(s - m)
    l = jnp.sum(e, axis=-1, keepdims=True)
    return _dot(e.astype(BF16), v) * (1.0 / l)


def _diff_kernel(*refs, Ts, Lc, lam_init, nq, n_units):
    if Lc:
        lam_ref, sub_ref, q_ref, ks_ref, vs_ref, kc_ref, vc_ref, prev_ref, o_ref, kb, vt, s_a, s_b, m_a, m_b = refs
    else:
        lam_ref, sub_ref, q_ref, ks_ref, vs_ref, prev_ref, o_ref, kb, vt, s_a, s_b, m_a, m_b = refs
    del prev_ref
    t = pl.program_id(0)
    S = Ts + Lc
    dv = DIFF_V_DIM
    head_new = jnp.minimum(t, n_units - 1) // nq
    slot_new = head_new % 2
    slot_old = (jnp.maximum(t - 1, 0) // nq) % 2

    @pl.when((t % nq == 0) & (t < n_units))
    def _():
        kb[0:Ts, :] = ks_ref[...].astype(BF16)
        vt[slot_new, 0:dv, 0:Ts] = vs_ref[...].astype(F32).T.astype(BF16)
        if Lc:
            kb[Ts:S, :] = kc_ref[...].astype(BF16)
            vt[slot_new, 0:dv, Ts:S] = vc_ref[...].T.astype(BF16)
        vt[slot_new, dv:, :] = jnp.ones((vt.shape[1] - dv, S), BF16)

    @pl.when(t == 0)
    def _():
        s_b[...] = jnp.zeros(s_b.shape, F32)
        m_b[...] = jnp.zeros(m_b.shape, F32)

    lm = lam_ref[...]
    lam = (jnp.exp(jnp.sum(lm[0:1, :] * lm[1:2, :], axis=-1, keepdims=True))
           - jnp.exp(jnp.sum(lm[2:3, :] * lm[3:4, :], axis=-1, keepdims=True)) + lam_init)

    def stage(s_new, m_new, s_old, m_old):
        q = q_ref[...].astype(F32) * (DIFF_QK_DIM ** -0.5 * LOG2E)
        tq = q.shape[0]
        part = lax.broadcasted_iota(I32, q.shape, 1) // DIFF_QK_DIM
        qm = [jnp.where(part == u, q, 0.0).astype(BF16) for u in range(2)]
        top = [m_old[u][0:1, :] for u in range(2)]
        acc = [jnp.zeros((vt.shape[1], tq), F32) for _ in range(2)]
        run = [jnp.full((8, tq), -jnp.inf, F32) for _ in range(2)]
        for c0 in range(0, S, DIFF_KEYS):
            keys = slice(c0, c0 + DIFF_KEYS)
            for u in range(2):
                sc = _dot_nt(kb[keys, :], qm[u])
                s_new[u, keys, :] = sc
                for j in range(0, DIFF_KEYS, 8):
                    run[u] = jnp.maximum(run[u], sc[j:j + 8, :])
                e = jnp.exp2(s_old[u, keys, :] - top[u])
                acc[u] = acc[u] + _dot(vt[slot_old, :, keys], e.astype(BF16))
        for u in range(2):
            m_new[u] = jnp.broadcast_to(jnp.max(run[u], axis=0, keepdims=True), m_new.shape[1:])
        o = (acc[0][0:dv, :] * (1.0 / acc[0][dv:dv + 1, :])
             - lam * (acc[1][0:dv, :] * (1.0 / acc[1][dv:dv + 1, :])))
        o_ref[...] = _rms(o.T) * sub_ref[...] * (1.0 - lam_init)

    @pl.when(t % 2 == 0)
    def _():
        stage(s_a, m_a, s_b, m_b)

    @pl.when(t % 2 == 1)
    def _():
        stage(s_b, m_b, s_a, m_a)


def _diff_ctx_kernel(lam_ref, sub_ref, q_ref, k_ref, v_ref, prev_ref, o_ref, *, lam_init, heads):
    del prev_ref
    lm = lam_ref[...]
    lam = (jnp.exp(jnp.sum(lm[0:1, :] * lm[1:2, :], axis=-1, keepdims=True))
           - jnp.exp(jnp.sum(lm[2:3, :] * lm[3:4, :], axis=-1, keepdims=True)) + lam_init)
    q = q_ref[...].astype(F32) * (DIFF_QK_DIM ** -0.5 * LOG2E)
    part = lax.broadcasted_iota(I32, q.shape, 1) // DIFF_QK_DIM
    k = k_ref[...].astype(BF16)
    v = v_ref[...].astype(BF16)
    for hh in range(heads):
        o1 = _softmax_pv(_dot_nt(jnp.where(part == 2 * hh, q, 0.0).astype(BF16), k), v)
        o2 = _softmax_pv(_dot_nt(jnp.where(part == 2 * hh + 1, q, 0.0).astype(BF16), k), v)
        cols = slice(hh * DIFF_V_DIM, (hh + 1) * DIFF_V_DIM)
        o = o1[:, cols] - lam * o2[:, cols]
        o_ref[:, cols] = _rms(o) * sub_ref[...] * (1.0 - lam_init)


def _diff_attention_ctx(p, lam_l, subln, out, *, n_seq, T, lam_init):
    heads = 2
    W = heads * DIFF_V_DIM
    spec = lambda col0: pl.BlockSpec((T, W), lambda b, h: (b, col0 // W + h))
    return pl.pallas_call(
        functools.partial(_diff_ctx_kernel, lam_init=lam_init, heads=heads),
        grid=(n_seq, DIFF_HEADS // heads),
        in_specs=[pl.BlockSpec((4, DIFF_QK_DIM), lambda b, h: (0, 0)),
                  pl.BlockSpec((1, DIFF_V_DIM), lambda b, h: (0, 0)),
                  spec(COL_DQ), spec(COL_DK), spec(COL_DV),
                  pl.BlockSpec(memory_space=pl.ANY)],
        out_specs=pl.BlockSpec((T, W), lambda b, h: (b, h)),
        out_shape=jax.ShapeDtypeStruct(out.shape, F32),
        input_output_aliases={5: 0},
        compiler_params=_params(("arbitrary", "arbitrary")),
        name="diff_attention_ctx",
    )(lam_l, subln, p, p, p, out)


def _diff_attention(p, lam_l, subln, out, *, row0, n_seq, T, tq, lam_init, cache_k=None, cache_v=None, l=0):
    W = DIFF_V_DIM
    assert 2 * DIFF_QK_DIM == W == LANES
    Lc = 0 if cache_k is None else cache_k.shape[2]
    S = T + Lc
    assert S % DIFF_KEYS == 0
    nq = T // tq
    qb0 = row0 // tq
    sb0 = row0 // T
    n_units = n_seq * DIFF_HEADS * nq

    def unit(t):
        return t // (DIFF_HEADS * nq), (t // nq) % DIFF_HEADS, t % nq

    def scored(t, col0):
        b, h, i = unit(jnp.minimum(t, n_units - 1))
        return b, h, i, col0 // W + h

    in_specs = [pl.BlockSpec((4, DIFF_QK_DIM), lambda t: (0, 0)),
                pl.BlockSpec((1, W), lambda t: (0, 0)),
                pl.BlockSpec((tq, W), lambda t: (qb0 + scored(t, 0)[0] * nq + scored(t, 0)[2], scored(t, COL_DQ)[3])),
                pl.BlockSpec((T, W), lambda t: (sb0 + scored(t, 0)[0], scored(t, COL_DK)[3])),
                pl.BlockSpec((T, W), lambda t: (sb0 + scored(t, 0)[0], scored(t, COL_DV)[3]))]
    args = [lam_l, subln, p, p, p]
    if Lc:
        in_specs += [pl.BlockSpec((None, None, Lc, W), lambda t: (scored(t, 0)[0], l, 0, scored(t, 0)[1])),
                     pl.BlockSpec((None, None, Lc, W), lambda t: (scored(t, 0)[0], l, 0, scored(t, 0)[1]))]
        args += [cache_k, cache_v]
    in_specs.append(pl.BlockSpec(memory_space=pl.ANY))
    args.append(out)
    scores = pltpu.VMEM((2, S, tq), F32)
    row_max = pltpu.VMEM((2, 8, tq), F32)
    vt_rows = W + 16

    def finished(t):
        b, h, i = unit(jnp.maximum(t - 1, 0))
        return qb0 + b * nq + i, h

    return pl.pallas_call(
        functools.partial(_diff_kernel, Ts=T, Lc=Lc, lam_init=lam_init, nq=nq, n_units=n_units),
        grid=(n_units + 1,),
        in_specs=in_specs,
        out_specs=pl.BlockSpec((tq, W), finished),
        out_shape=jax.ShapeDtypeStruct(out.shape, F32),
        scratch_shapes=[pltpu.VMEM((S, W), BF16), pltpu.VMEM((2, vt_rows, S), BF16),
                        scores, scores, row_max, row_max],
        input_output_aliases={len(args) - 1: 0},
        compiler_params=_params(("arbitrary",)),
        name="diff_attention",
    )(*args)


def _na_heads(q, score_fn, pv_fn):
    lane = lax.broadcasted_iota(I32, q.shape, 1) // NA_HEAD_DIM
    out = jnp.zeros(q.shape, F32)
    for h in range(NA_HEADS):
        qh = jnp.where(lane == h, q, 0.0).astype(BF16)
        out = jnp.where(lane == h, pv_fn(score_fn(qh, h)), out)
    return out


def _dense_kernel(q_ref, k_ref, v_ref, prev_ref, o_ref):
    del prev_ref
    q = q_ref[...].astype(F32) * (NA_HEAD_DIM ** -0.5 * LOG2E)
    k = k_ref[...].astype(BF16)
    v = v_ref[...].astype(BF16)
    o_ref[...] = _na_heads(q, lambda qh, h: _dot_nt(qh, k), lambda s: _softmax_pv(s, v))


def _dense_attention(p, out, *, n_seq, T):
    C = NA_HEADS * NA_HEAD_DIM
    return pl.pallas_call(
        _dense_kernel,
        grid=(n_seq,),
        in_specs=[pl.BlockSpec((T, C), lambda b: (b, 7)),
                  pl.BlockSpec((T, C), lambda b: (b, 8)),
                  pl.BlockSpec((T, C), lambda b: (b, 9)),
                  pl.BlockSpec(memory_space=pl.ANY)],
        out_specs=pl.BlockSpec((T, C), lambda b: (b, 0)),
        out_shape=jax.ShapeDtypeStruct(out.shape, F32),
        input_output_aliases={3: 0},
        compiler_params=_params(("arbitrary",)),
        name="dense_attention",
    )(p, p, p, out)


def _na_kernel(q_ref, ks_ref, vs_ref, kc_ref, vc_ref, bias_ref, prev_ref, o_ref, *, rows):
    del prev_ref
    r0 = pl.program_id(1) * NA_Q_ROWS
    bs = jnp.clip(r0 - NA_WIN_H // 2, 0, rows - NA_BAND)
    start = pl.multiple_of(bs * GRID_W, GRID_W)
    nb = NA_BAND * GRID_W
    kb = ks_ref[pl.ds(start, nb), :].astype(BF16)
    vb = vs_ref[pl.ds(start, nb), :].astype(BF16)
    kc = kc_ref[...].astype(BF16)
    vc = vc_ref[...].astype(BF16)
    q = q_ref[...].astype(F32) * (NA_HEAD_DIM ** -0.5 * LOG2E)

    def scores(qh, h):
        return _dot_nt(qh, kb) + bias_ref[h], _dot_nt(qh, kc)

    def pv(s):
        s_loc, s_ctx = s
        m = jnp.maximum(jnp.max(s_loc, axis=-1, keepdims=True), jnp.max(s_ctx, axis=-1, keepdims=True))
        e_loc = jnp.exp2(s_loc - m)
        e_ctx = jnp.exp2(s_ctx - m)
        l = jnp.sum(e_loc, axis=-1, keepdims=True) + jnp.sum(e_ctx, axis=-1, keepdims=True)
        return (_dot(e_loc.astype(BF16), vb) + _dot(e_ctx.astype(BF16), vc)) * (1.0 / l)

    o_ref[...] = _na_heads(q, scores, pv)


def _na_bias_table(rpb_l, rows):
    n_ro, n_co = 2 * NA_WIN_H - 1, 2 * NA_WIN_W - 1
    c = np.arange(GRID_W)[:, None]
    kc = np.arange(GRID_W)[None, :]
    cs = np.clip(c - NA_WIN_W // 2, 0, GRID_W - NA_WIN_W)
    col_ok = (kc >= cs) & (kc < cs + NA_WIN_W)
    co = kc - c + (NA_WIN_W - 1)
    pick = ((np.arange(n_co)[:, None, None] == co[None]) & col_ok[None]).astype(np.float32)
    toep = jnp.dot(rpb_l.reshape(NA_HEADS * n_ro, n_co).astype(F32), jnp.asarray(pick.reshape(n_co, -1)),
                   precision=lax.Precision.HIGHEST).reshape(NA_HEADS, n_ro, GRID_W, GRID_W) * LOG2E
    toep = jnp.where(jnp.asarray(col_ok)[None, None], toep, MASKED)
    toep = jnp.concatenate([toep, jnp.full((NA_HEADS, 1, GRID_W, GRID_W), MASKED, F32)], axis=1)
    blk = np.full((3, NA_Q_ROWS, NA_BAND), n_ro, np.int32)
    for v, r0 in enumerate((0, NA_Q_ROWS, rows - NA_Q_ROWS)):
        bs = int(np.clip(r0 - NA_WIN_H // 2, 0, rows - NA_BAND))
        for j in range(NA_Q_ROWS):
            rs = int(np.clip(r0 + j - NA_WIN_H // 2, 0, rows - NA_WIN_H))
            for i in range(NA_BAND):
                if rs <= bs + i < rs + NA_WIN_H:
                    blk[v, j, i] = bs + i - (r0 + j) + (NA_WIN_H - 1)
    tab = toep[:, blk]
    tab = jnp.transpose(tab, (1, 0, 2, 4, 3, 5))
    return tab.reshape(3, NA_HEADS, NA_Q_ROWS * GRID_W, NA_BAND * GRID_W)


def _neighbourhood_attention(p, cache_k, cache_v, bias, out, *, row0, n_seq, T, l):
    C = NA_HEADS * NA_HEAD_DIM
    rows = T // GRID_W
    tq = NA_Q_ROWS * GRID_W
    nq = T // tq
    Lc = cache_k.shape[2]
    qb0 = row0 // tq
    sb0 = row0 // T

    def variant(i):
        r0 = i * NA_Q_ROWS
        return (r0 - jnp.clip(r0 - NA_WIN_H // 2, 0, rows - NA_BAND)) // NA_Q_ROWS

    return pl.pallas_call(
        functools.partial(_na_kernel, rows=rows),
        grid=(n_seq, nq),
        in_specs=[pl.BlockSpec((tq, C), lambda b, i: (qb0 + b * nq + i, 7)),
                  pl.BlockSpec((T, C), lambda b, i: (sb0 + b, 8)),
                  pl.BlockSpec((T, C), lambda b, i: (sb0 + b, 9)),
                  pl.BlockSpec((None, None, Lc, C), lambda b, i: (b, l, 0, 0)),
                  pl.BlockSpec((None, None, Lc, C), lambda b, i: (b, l, 0, 0)),
                  pl.BlockSpec((None, NA_HEADS, tq, NA_BAND * GRID_W), lambda b, i: (variant(i), 0, 0, 0)),
                  pl.BlockSpec(memory_space=pl.ANY)],
        out_specs=pl.BlockSpec((tq, C), lambda b, i: (qb0 + b * nq + i, 0)),
        out_shape=jax.ShapeDtypeStruct(out.shape, F32),
        input_output_aliases={6: 0},
        compiler_params=_params(("arbitrary", "arbitrary")),
        name="neighbourhood_attention",
    )(p, p, p, cache_k, cache_v, bias, out)


def _merge_kernel(pool_ref, dn_ref, na_ref, *refs, D, ctx_blocks):
    mod_ref, g1_ref, g2_ref, w_ref, rw_ref, rb_ref, x1_ref, h2_ref, route_ref, cnt_ref = refs[-10:]
    x_rows = _token_rows(refs[:-10], ctx_blocks)
    parts = [slice(j * TOK_CHUNK, (j + 1) * TOK_CHUNK) for j in range(x1_ref.shape[0] // TOK_CHUNK)]
    c0 = pool_ref.shape[1]
    c1 = c0 + dn_ref.shape[1]
    mix = [_dot(pool_ref[p, :].astype(BF16), w_ref[0:c0, :])
           + _dot(dn_ref[p, :].astype(BF16), w_ref[c0:c1, :])
           + _dot(na_ref[p, :].astype(BF16), w_ref[c1:, :]) for p in parts]
    x1 = [x_rows(p) + mod_ref[:, 2 * D:3 * D] * (_rms(m) * g1_ref[...]) for p, m in zip(parts, mix)]
    for p, v in zip(parts, x1):
        x1_ref[p, :] = v
    h2 = [(_rms(v) * g2_ref[...]) * (1.0 + mod_ref[:, 4 * D:5 * D]) + mod_ref[:, 3 * D:4 * D] for v in x1]
    for p, v in zip(parts, h2):
        h2_ref[p, :] = v

    rw_hi, rw_lo = _split(rw_ref[...])
    h2_split = [_split(v) for v in h2]
    logits = [_dot(hi, rw_hi) + _dot(lo, rw_hi) + _dot(hi, rw_lo) + rb_ref[...] for hi, lo in h2_split]
    lane = lax.broadcasted_iota(I32, (TOK_CHUNK, LANES), 1)
    lane_f = lane.astype(F32)
    work = logits
    vals, hots = [], []
    for _ in range(TOP_K):
        mx = [jnp.max(w, axis=-1, keepdims=True) for w in work]
        idx = [jnp.min(jnp.where(w == m, lane_f, float(LANES)), axis=-1, keepdims=True)
               for w, m in zip(work, mx)]
        hot = [lane_f == i for i in idx]
        vals.append(mx)
        hots.append(hot)
        work = [jnp.where(h, -jnp.inf, w) for h, w in zip(hot, work)]

    r = lax.broadcasted_iota(I32, (TOK_CHUNK, TOK_CHUNK), 0)
    c = lax.broadcasted_iota(I32, (TOK_CHUNK, TOK_CHUNK), 1)
    before = jnp.where(c < r, 1.0, 0.0).astype(BF16)
    er = lax.broadcasted_iota(I32, (LANES, LANES), 0)
    ec = lax.broadcasted_iota(I32, (LANES, LANES), 1)
    earlier = jnp.where(er < ec, 1.0, 0.0).astype(BF16)
    for j, p in enumerate(parts):
        hot_j = [hots[k][j] for k in range(TOP_K)]
        es = [jnp.exp(vals[k][j] - vals[0][j]) for k in range(TOP_K)]
        inv = 1.0 / (es[0] + es[1] + es[2] + es[3])
        sel = jnp.zeros((TOK_CHUNK, LANES), F32)
        for hot in hot_j:
            sel = jnp.where(hot, 1.0, sel)
        rank = _dot(before, sel.astype(BF16))
        cnt = jnp.sum(sel, axis=0, keepdims=True)
        cnt_ref[j] = cnt
        run = jnp.floor((cnt + (RUN_ALIGN - 1)) * (1.0 / RUN_ALIGN)) * RUN_ALIGN
        run_start = _dot(jnp.broadcast_to(run, (8, LANES)).astype(BF16), earlier)[0:1, :]
        pos = rank + run_start
        route = jnp.zeros((TOK_CHUNK, LANES), F32)
        for k in range(TOP_K):
            e_k = jnp.sum(jnp.where(hot_j[k], lane_f, 0.0), axis=-1, keepdims=True)
            p_k = jnp.sum(jnp.where(hot_j[k], pos, 0.0), axis=-1, keepdims=True)
            route = jnp.where(lane == k, e_k, route)
            route = jnp.where(lane == TOP_K + k, p_k, route)
            route = jnp.where(lane == 2 * TOP_K + k, es[k] * inv, route)
        route_ref[p, :] = route


def _merge_route(pool_o, dn, na_o, x, mod4, g1, g2, w_out_bf, rw_pad, rb_pad, *, tm, ctx_blocks,
                 blocks_per_seq, l):
    xs, x_specs = _token_specs(x, tm, ctx_blocks)
    N, D = sum(a.shape[0] for a in xs), xs[0].shape[1]
    per_step = tm // TOK_CHUNK

    def grp(i):
        return jnp.where(i < ctx_blocks, 0, 1 + (i - ctx_blocks) // blocks_per_seq)

    row = lambda w: pl.BlockSpec((tm, w), lambda i: (i, 0))
    full = lambda a: pl.BlockSpec(a.shape, lambda i: (0,) * a.ndim)
    return pl.pallas_call(
        functools.partial(_merge_kernel, D=D, ctx_blocks=ctx_blocks),
        grid=(N // tm,),
        in_specs=[row(pool_o.shape[1]), row(dn.shape[1]), row(na_o.shape[1])] + x_specs + [
                  pl.BlockSpec((None, None, 1, N_MOD * D), lambda i: (l, grp(i), 0, 0)),
                  full(g1), full(g2), full(w_out_bf), full(rw_pad), full(rb_pad)],
        out_specs=[row(D), row(D), row(LANES), pl.BlockSpec((per_step, 1, LANES), lambda i: (i, 0, 0))],
        out_shape=[jax.ShapeDtypeStruct((N, D), F32), jax.ShapeDtypeStruct((N, D), F32),
                   jax.ShapeDtypeStruct((N, LANES), F32), jax.ShapeDtypeStruct((N // TOK_CHUNK, 1, LANES), F32)],
        compiler_params=_params(("arbitrary",)),
        name="merge_route",
    )(pool_o, dn, na_o, *xs, mod4, g1, g2, w_out_bf, rw_pad, rb_pad)


def _pack_pairs(x):
    C = x.shape[1] // 2
    bits = lax.bitcast_convert_type(x.astype(BF16).astype(F32), jnp.uint32)
    return bits[:, C:] | (bits[:, :C] >> 16)


def _unpack_pairs(w):
    lo = lax.bitcast_convert_type(w << 16, F32)
    hi = lax.bitcast_convert_type(w & jnp.uint32(0xFFFF0000), F32)
    return jnp.concatenate([lo, hi], axis=1).astype(BF16)


def _run_sizes(lo, hi):
    return [1 << k for k in range(hi.bit_length() - 1, lo.bit_length() - 2, -1)]


def _run_copies(n, src_at, dst_at, sem, wait, sizes):
    for size in sizes:
        @pl.when((n & size) != 0)
        def _(size=size):
            off = n & -(2 * size)
            cp = pltpu.make_async_copy(src_at(off, size), dst_at(off, size), sem)
            cp.wait() if wait else cp.start()


def _block_runs(step, n_exp, run_ref, src_ref, dst_ref, large_ref, hbm_ref, buf, sem, *, to_hbm, wait):
    def each_run(sizes):
        def body(e, carry):
            j = step * n_exp + e
            so = src_ref[j]
            do = dst_ref[j]
            in_buf = lambda o, s: buf.at[pl.ds(pl.multiple_of(so + o, RUN_ALIGN), s)]
            in_hbm = lambda o, s: hbm_ref.at[pl.ds(pl.multiple_of(do + o, RUN_ALIGN), s)]
            if to_hbm:
                _run_copies(run_ref[j], in_buf, in_hbm, sem, wait, sizes)
            else:
                _run_copies(run_ref[j], in_hbm, in_buf, sem, wait, sizes)
            return carry
        lax.fori_loop(0, n_exp, body, 0, unroll=RUN_UNROLL)

    each_run(_run_sizes(RUN_ALIGN, RUN_LARGE // 2))

    @pl.when(large_ref[step] != 0)
    def _():
        each_run(_run_sizes(RUN_LARGE, TOK_CHUNK))


def _dispatch_kernel(run_ref, src_ref, dst_ref, large_ref, tail_ref, taildst_ref, nu_ref, route_ref, h_ref, xs_ref,
                     sorted_buf, zero_buf, sems, *, n_exp):
    b = pl.program_id(0)
    nb = pl.num_programs(0)
    slot = b % 2
    runs = functools.partial(_block_runs, n_exp=n_exp, run_ref=run_ref, src_ref=src_ref, dst_ref=dst_ref,
                             large_ref=large_ref, hbm_ref=xs_ref, to_hbm=True)

    @pl.when(b >= 2)
    def _():
        runs(b - 2, buf=sorted_buf.at[slot], sem=sems.at[slot], wait=True)

    route = route_ref[...]
    col = lax.broadcasted_iota(I32, (TOK_CHUNK, sorted_buf.shape[1]), 1).astype(F32)
    place = jnp.zeros(col.shape, F32)
    for k in range(TOP_K):
        place = jnp.where(col == route[:, TOP_K + k:TOP_K + k + 1], 1.0, place)
    srt = lax.dot_general(place.astype(BF16), h_ref[...].astype(BF16), (((0,), (0,)), ((), ())),
                          preferred_element_type=F32)
    sorted_buf[slot] = _pack_pairs(srt)
    runs(b, buf=sorted_buf.at[slot], sem=sems.at[slot], wait=False)

    @pl.when(b == nb - 1)
    def _():
        @pl.when(b >= 1)
        def _():
            runs(b - 1, buf=sorted_buf.at[1 - slot], sem=sems.at[1 - slot], wait=True)
        runs(b, buf=sorted_buf.at[slot], sem=sems.at[slot], wait=True)

        zero_buf[...] = jnp.zeros(zero_buf.shape, zero_buf.dtype)
        sem = sems.at[0]

        def each_tail(wait):
            def body(e, carry):
                do = taildst_ref[e]
                _run_copies(tail_ref[e],
                            lambda o, s: zero_buf.at[pl.ds(0, s)],
                            lambda o, s: xs_ref.at[pl.ds(pl.multiple_of(do + o, RUN_ALIGN), s)], sem, wait,
                            _run_sizes(RUN_ALIGN, MOE_ROWS // 2))
                return carry
            lax.fori_loop(0, n_exp, body, 0)

        def spare_block(wait):
            def body(i, carry):
                cp = pltpu.make_async_copy(zero_buf.at[pl.ds(0, MOE_ROWS)],
                                           xs_ref.at[pl.ds(pl.multiple_of(i * MOE_ROWS, MOE_ROWS), MOE_ROWS)], sem)
                cp.wait() if wait else cp.start()
                return carry
            lax.fori_loop(nu_ref[0], xs_ref.shape[0] // MOE_ROWS, body, 0)

        each_tail(False)
        spare_block(False)
        each_tail(True)
        spare_block(True)


def _dispatch(plan, n_used, route, h2, n_rows, n_exp):
    N, D = h2.shape
    C = D // 2
    grid_spec = pltpu.PrefetchScalarGridSpec(
        num_scalar_prefetch=7,
        grid=(N // TOK_CHUNK,),
        in_specs=[pl.BlockSpec((TOK_CHUNK, LANES), lambda i, *_: (i, 0)),
                  pl.BlockSpec((TOK_CHUNK, D), lambda i, *_: (i, 0))],
        out_specs=pl.BlockSpec(memory_space=pl.ANY),
        scratch_shapes=[pltpu.VMEM((2, _sorted_rows(n_exp), C), jnp.uint32),
                        pltpu.VMEM((max(TOK_CHUNK, MOE_ROWS), C), jnp.uint32),
                        pltpu.SemaphoreType.DMA((2,))],
    )
    return pl.pallas_call(
        functools.partial(_dispatch_kernel, n_exp=n_exp),
        grid_spec=grid_spec,
        out_shape=jax.ShapeDtypeStruct((n_rows, C), jnp.uint32),
        compiler_params=_params(("arbitrary",)),
        name="moe_dispatch",
    )(plan["run"], plan["src"], plan["dst"], plan["large"], plan["tail"], plan["tail_dst"], n_used, route, h2)


def _expert_kernel(be_ref, nu_ref, rows_ref, xs_ref, w1_ref, b1_ref, w2_ref, b2_ref, o_ref, w1b, w2b, *, F):
    del nu_ref
    i = pl.program_id(0)
    e = be_ref[i]
    prev = be_ref[jnp.maximum(i - 1, 0)]
    rows = rows_ref[i]
    half = MOE_ROWS // 2

    @pl.when((i == 0) | (e != prev))
    def _():
        w1b[...] = w1_ref[...].astype(BF16)
        w2b[...] = w2_ref[...].astype(BF16)

    def ffn(xw):
        hh = _dot(_unpack_pairs(xw), w1b[...]) + b1_ref[...]
        g = jnp.minimum(hh[:, 0:F], SWIGLU_LIMIT)
        u = jnp.clip(hh[:, F:2 * F], -SWIGLU_LIMIT, SWIGLU_LIMIT)
        a = (g * (1.0 / (1.0 + jnp.exp(-SWIGLU_ALPHA * g)))) * (u + 1.0)
        return _pack_pairs(_dot(a.astype(BF16), w2b[...]) + b2_ref[...])

    @pl.when(rows > half)
    def _():
        o_ref[...] = ffn(xs_ref[...])

    @pl.when((rows > 0) & (rows <= half))
    def _():
        o_ref[0:half, :] = ffn(xs_ref[0:half, :])
        o_ref[half:MOE_ROWS, :] = jnp.zeros((MOE_ROWS - half, o_ref.shape[1]), o_ref.dtype)

    @pl.when(rows == 0)
    def _():
        o_ref[...] = jnp.zeros(o_ref.shape, o_ref.dtype)


def _experts(block_e, n_used, block_rows, xs, w1, b1, w2, b2, *, l):
    R, C = xs.shape
    L, E, D, F2 = w1.shape
    F = F2 // 2
    nblk = R // MOE_ROWS
    grid_spec = pltpu.PrefetchScalarGridSpec(
        num_scalar_prefetch=3,
        grid=(nblk,),
        in_specs=[pl.BlockSpec((MOE_ROWS, C), lambda i, be, nu, br: (jnp.minimum(i, nu[0] - 1), 0)),
                  pl.BlockSpec((None, None, D, F2), lambda i, be, nu, br: (l, be[i], 0, 0)),
                  pl.BlockSpec((None, None, 1, F2), lambda i, be, nu, br: (l, be[i], 0, 0)),
                  pl.BlockSpec((None, None, F, D), lambda i, be, nu, br: (l, be[i], 0, 0)),
                  pl.BlockSpec((None, None, 1, D), lambda i, be, nu, br: (l, be[i], 0, 0))],
        out_specs=pl.BlockSpec((MOE_ROWS, C), lambda i, be, nu, br: (i, 0)),
        scratch_shapes=[pltpu.VMEM((D, F2), BF16), pltpu.VMEM((F, D), BF16)],
    )
    return pl.pallas_call(
        functools.partial(_expert_kernel, F=F),
        grid_spec=grid_spec,
        out_shape=jax.ShapeDtypeStruct((R, C), jnp.uint32),
        compiler_params=_params(("arbitrary",)),
        name="moe_experts",
    )(block_e, n_used, block_rows, xs, w1, b1.reshape(L, E, 1, F2), w2, b2.reshape(L, E, 1, D))


def _combine_kernel(run_ref, src_ref, dst_ref, large_ref, yb_ref, route_ref, x1_ref, mod_ref, g_ref, *rest, D,
                    n_exp, ctx_blocks, split):
    if split:
        ctx_ref, lat_ref, sorted_buf, sems = rest
    else:
        o_ref, sorted_buf, sems = rest
    b = pl.program_id(0)
    nb = pl.num_programs(0)
    slot = b % 2
    runs = functools.partial(_block_runs, n_exp=n_exp, run_ref=run_ref, src_ref=src_ref, dst_ref=dst_ref,
                             large_ref=large_ref, hbm_ref=yb_ref, to_hbm=False)

    @pl.when(b == 0)
    def _():
        sorted_buf[...] = jnp.zeros(sorted_buf.shape, sorted_buf.dtype)
        runs(b, buf=sorted_buf.at[slot], sem=sems.at[slot], wait=False)

    @pl.when(b + 1 < nb)
    def _():
        runs(b + 1, buf=sorted_buf.at[1 - slot], sem=sems.at[1 - slot], wait=False)

    runs(b, buf=sorted_buf.at[slot], sem=sems.at[slot], wait=True)
    route = route_ref[...]
    col = lax.broadcasted_iota(I32, (TOK_CHUNK, sorted_buf.shape[1]), 1).astype(F32)
    gate = jnp.zeros(col.shape, F32)
    for k in range(TOP_K):
        gate = jnp.where(col == route[:, TOP_K + k:TOP_K + k + 1], route[:, 2 * TOP_K + k:2 * TOP_K + k + 1], gate)
    g_hi, g_lo = _split(gate)
    yb = _unpack_pairs(sorted_buf[slot])
    y = _dot(g_hi, yb) + _dot(g_lo, yb)
    x2 = x1_ref[...] + mod_ref[:, 5 * D:6 * D] * (_rms(y) * g_ref[...])
    if split:
        @pl.when(b < ctx_blocks)
        def _():
            ctx_ref[...] = x2

        @pl.when(b >= ctx_blocks)
        def _():
            lat_ref[...] = x2
    else:
        o_ref[...] = x2


def _combine(plan, yb, route, x1, mod4, g3, *, n_exp, ctx_blocks, blocks_per_seq, l, split):
    N, D = x1.shape
    row = lambda f: pl.BlockSpec((TOK_CHUNK, D), lambda i, *_: (f(i), 0))
    if split:
        out_specs = [row(lambda i: jnp.minimum(i, ctx_blocks - 1)), row(lambda i: jnp.maximum(i - ctx_blocks, 0))]
        out_shape = [jax.ShapeDtypeStruct((ctx_blocks * TOK_CHUNK, D), F32),
                     jax.ShapeDtypeStruct((N - ctx_blocks * TOK_CHUNK, D), F32)]
    else:
        out_specs = row(lambda i: i)
        out_shape = jax.ShapeDtypeStruct((N, D), F32)

    def grp(i):
        return jnp.where(i < ctx_blocks, 0, 1 + (i - ctx_blocks) // blocks_per_seq)

    grid_spec = pltpu.PrefetchScalarGridSpec(
        num_scalar_prefetch=4,
        grid=(N // TOK_CHUNK,),
        in_specs=[pl.BlockSpec(memory_space=pl.ANY),
                  pl.BlockSpec((TOK_CHUNK, LANES), lambda i, *_: (i, 0)),
                  pl.BlockSpec((TOK_CHUNK, D), lambda i, *_: (i, 0)),
                  pl.BlockSpec((None, None, 1, N_MOD * D), lambda i, *_: (l, grp(i), 0, 0)),
                  pl.BlockSpec((1, D), lambda i, *_: (0, 0))],
        out_specs=out_specs,
        scratch_shapes=[pltpu.VMEM((2, _sorted_rows(n_exp), yb.shape[1]), jnp.uint32),
                        pltpu.SemaphoreType.DMA((2,))],
    )
    return pl.pallas_call(
        functools.partial(_combine_kernel, D=D, n_exp=n_exp, ctx_blocks=ctx_blocks, split=split),
        grid_spec=grid_spec,
        out_shape=out_shape,
        compiler_params=_params(("arbitrary",)),
        name="moe_combine",
    )(plan["run"], plan["src"], plan["dst"], plan["large"], yb, route, x1, mod4, g3)


def _rope_tables(Ts, tm):
    nf = DIFF_QK_DIM // 4
    inv = ROPE_BASE ** (-jnp.arange(nf, dtype=F32) / nf)
    t = jnp.arange(Ts)
    pos = jnp.stack([(t // GRID_W).astype(F32), (t % GRID_W).astype(F32)], axis=1)
    ang = pos[:, :, None] * inv[None, None, :]
    cos = jnp.repeat(jnp.cos(ang)[:, :, None, :], 2, axis=2).reshape(Ts, DIFF_QK_DIM)
    sin = jnp.sin(ang)
    sin = jnp.stack([-sin, sin], axis=2).reshape(Ts, DIFF_QK_DIM)
    reps = LANES // DIFF_QK_DIM
    cos = jnp.concatenate([jnp.ones((tm, LANES), F32), jnp.tile(cos, (1, reps))], axis=0)
    sin = jnp.concatenate([jnp.zeros((tm, LANES), F32), jnp.tile(sin, (1, reps))], axis=0)
    return cos, sin


def _block_diag(w):
    G, a, b = w.shape
    out = jnp.zeros((G * a, G * b), w.dtype)
    for g in range(G):
        out = out.at[g * a:(g + 1) * a, g * b:(g + 1) * b].set(w[g])
    return out


def _route_plan(counts, n_exp, nblk):
    cnt = counts[:, 0, :n_exp].astype(I32)
    run = (cnt + RUN_ALIGN - 1) // RUN_ALIGN * RUN_ALIGN
    src = jnp.cumsum(run, axis=1) - run
    tot = jnp.sum(run, axis=0)
    region = (tot + MOE_ROWS - 1) // MOE_ROWS * MOE_ROWS
    region_end = jnp.cumsum(region)
    region_start = region_end - region
    dst = region_start[None, :] + jnp.cumsum(run, axis=0) - run
    n_used = (region_end[-1] // MOE_ROWS).astype(I32)
    blk = jnp.arange(nblk, dtype=I32) * MOE_ROWS
    block_e = jnp.minimum(jnp.sum((blk[:, None] >= region_end[None, :]).astype(I32), axis=1), n_exp - 1)
    last = jnp.sum(jnp.where(jnp.arange(nblk) == n_used - 1, block_e, 0))
    used = jnp.arange(nblk) < n_used
    block_rows = jnp.where(used, jnp.clip((region_start + tot)[block_e] - blk, 0, MOE_ROWS), 0).astype(I32)
    block_e = jnp.where(used, block_e, last).astype(I32)
    plan = dict(run=run.reshape(-1).astype(I32), src=src.reshape(-1).astype(I32), dst=dst.reshape(-1).astype(I32),
                large=jnp.any(run >= RUN_LARGE, axis=1).astype(I32),
                tail=(region - tot).astype(I32), tail_dst=(region_start + tot).astype(I32))
    return plan, block_e, n_used.reshape(1), block_rows


def kernel(x_prompt, x_sample, cache_diff_k, cache_diff_v, cache_na_k, cache_na_v, c, c_ctx, w_ada, b_ada,
           norm_gain, w_in, w_out, pool_w, pool_scale, diff_lambda, diff_subln, na_rpb, router_w, router_b,
           moe_w1, moe_b1, moe_w2, moe_b2):
    Bp, Tp, D = x_prompt.shape
    Bs, Ts, _ = x_sample.shape
    L = w_ada.shape[0]
    E = router_w.shape[-1]
    Np, Ns = Bp * Tp, Bs * Ts
    N = Np + Ns
    tm = TOK_CHUNK
    assert Np % Ts == 0 or Bs == 0, "context rows must be a whole number of latent-sequence blocks"
    assert Np % tm == 0 and Ts % tm == 0 and Ts % (NA_Q_ROWS * GRID_W) == 0
    assert Ts // GRID_W >= NA_BAND + NA_Q_ROWS
    ctx_blocks, blocks_per_seq = Np // tm, Ts // tm

    G = 16
    cvec = jnp.zeros((G, D), F32).at[0].set(c_ctx).at[1:1 + Bs].set(c)
    mod4 = _modulation(cvec, w_ada, b_ada).reshape(L, G, 1, N_MOD * D)
    tp = PROJ_ROWS
    assert Np % tp == 0 and Ts % tp == 0 and Np % MERGE_ROWS == 0 and Ts % MERGE_ROWS == 0
    cos_t, sin_t = _rope_tables(Ts, tp)
    w_in_bf = w_in.astype(BF16)
    w_out_bf = w_out.astype(BF16)
    rw_pad = jnp.zeros((L, D, LANES), F32).at[:, :, :E].set(router_w)
    rb_pad = jnp.full((L, 1, LANES), MASKED, F32).at[:, 0, :E].set(router_b)
    ck = cache_diff_k.reshape(Bs, L, -1, DIFF_HEADS * 2 * DIFF_QK_DIM)
    cv = cache_diff_v.reshape(Bs, L, -1, DIFF_HEADS * DIFF_V_DIM)
    nk = cache_na_k.reshape(Bs, L, -1, NA_HEADS * NA_HEAD_DIM)
    nv = cache_na_v.reshape(Bs, L, -1, NA_HEADS * NA_HEAD_DIM)
    nb = N // TOK_CHUNK
    nblk = -(-(N * TOP_K + nb * E * (RUN_ALIGN - 1) + E * (MOE_ROWS - 1)) // MOE_ROWS)
    blocks = dict(ctx_blocks=ctx_blocks, blocks_per_seq=blocks_per_seq)

    x = (x_prompt.reshape(Np, D), x_sample.reshape(Ns, D))
    new_dk, new_dv, new_nk, new_nv = [], [], [], []
    for l in range(L):
        lam_init = 0.8 - 0.6 * math.exp(-0.3 * l)
        g = norm_gain[l]
        p, dk_l, dv_l, nk_l, nv_l = _in_projection(x, mod4, g[0:1], w_in_bf[l], cos_t, sin_t, tm=tp, l=l,
                                                   ctx_blocks=Np // tp, blocks_per_seq=Ts // tp)

        pw = _block_diag(pool_w[l]).astype(BF16)
        ps = pool_scale[l].reshape(1, -1)
        pool_o = jnp.zeros((N, pw.shape[0]), F32)
        pool_o = _pool(p, pw, ps, row0=0, n_seq=Bp, T=Tp, out=pool_o)
        pool_o = _pool(p, pw, ps, row0=Np, n_seq=Bs, T=Ts, out=pool_o)

        sub = diff_subln[l].reshape(1, -1)
        dn = jnp.zeros((N, DIFF_HEADS * DIFF_V_DIM), F32)
        dn = _diff_attention_ctx(p, diff_lambda[l], sub, dn, n_seq=Bp, T=Tp, lam_init=lam_init)
        dn = _diff_attention(p, diff_lambda[l], sub, dn, row0=Np, n_seq=Bs, T=Ts, tq=DIFF_Q_ROWS, lam_init=lam_init,
                             cache_k=ck, cache_v=cv, l=l)

        na_o = jnp.zeros((N, NA_HEADS * NA_HEAD_DIM), F32)
        na_o = _dense_attention(p, na_o, n_seq=Bp, T=Tp)
        bias = _na_bias_table(na_rpb[l], Ts // GRID_W)
        na_o = _neighbourhood_attention(p, nk, nv, bias, na_o, row0=Np, n_seq=Bs, T=Ts, l=l)

        x1, h2, route, counts = _merge_route(pool_o, dn, na_o, x, mod4, g[1:2], g[2:3], w_out_bf[l],
                                             rw_pad[l], rb_pad[l], tm=MERGE_ROWS, l=l,
                                             ctx_blocks=Np // MERGE_ROWS, blocks_per_seq=Ts // MERGE_ROWS)
        plan, block_e, n_used, block_rows = _route_plan(counts, E, nblk)
        xs = _dispatch(plan, n_used, route, h2, nblk * MOE_ROWS, E)
        yb = _experts(block_e, n_used, block_rows, xs, moe_w1, moe_b1, moe_w2, moe_b2, l=l)
        x = _combine(plan, yb, route, x1, mod4, g[3:4], n_exp=E, l=l, split=l == L - 1, **blocks)

        new_dk.append(dk_l.reshape(Bp, Tp, DIFF_HEADS, 2 * DIFF_QK_DIM))
        new_dv.append(dv_l.reshape(Bp, Tp, DIFF_HEADS, DIFF_V_DIM))
        new_nk.append(nk_l.reshape(Bp, Tp, NA_HEADS, NA_HEAD_DIM))
        new_nv.append(nv_l.reshape(Bp, Tp, NA_HEADS, NA_HEAD_DIM))

    return (x[0].reshape(Bp, Tp, D), x[1].reshape(Bs, Ts, D),
            jnp.stack(new_dk, axis=1), jnp.stack(new_dv, axis=1),
            jnp.stack(new_nk, axis=1), jnp.stack(new_nv, axis=1))
```

```python
import functools
import math

import numpy as np
import jax
import jax.numpy as jnp
from jax import lax
from jax.experimental import pallas as pl
from jax.experimental.pallas import tpu as pltpu

F32 = jnp.float32
BF16 = jnp.bfloat16
I32 = jnp.int32

GRID_W = 64
POOL_GROUPS = 4
POOL_MAX_HALF = 8
DIFF_HEADS = 4
DIFF_V_DIM = 128
DIFF_QK_DIM = 64
NA_HEADS = 4
NA_HEAD_DIM = 64
NA_WIN_H = 8
NA_WIN_W = 16
NA_Q_ROWS = 4
NA_BAND = 12
TOP_K = 4
SWIGLU_ALPHA = 1.702
SWIGLU_LIMIT = 7.0
ROPE_BASE = 10000.0
NORM_EPS = 1e-6
N_MOD = 6

LOG2E = 1.4426950408889634
MASKED = -1e30
LANES = 128
MXU_TILE = 256
MOE_ROWS = 512
COL_DQ, COL_DK, COL_DV = 256, 768, 1280
COL_NQ, COL_NK, COL_NV = 1792, 2048, 2304
DIFF_KEYS = 256
PROJ_ROWS = 512
MERGE_ROWS = 1024
DIFF_Q_ROWS = 256
TOK_CHUNK = 256
RUN_ALIGN = 8
RUN_LARGE = 128
RUN_UNROLL = 4
VMEM_LIMIT = 56 * 1024 * 1024


def _sorted_rows(n_exp):
    rows = TOK_CHUNK * TOP_K + n_exp * (RUN_ALIGN - 1)
    return -(-rows // MXU_TILE) * MXU_TILE


def _params(sem, vmem=VMEM_LIMIT):
    return pltpu.CompilerParams(dimension_semantics=sem, vmem_limit_bytes=vmem)


def _dot(a, b):
    return jnp.dot(a, b, preferred_element_type=F32)


def _dot_nt(a, b):
    return lax.dot_general(a, b, (((1,), (1,)), ((), ())), preferred_element_type=F32)


def _split(x):
    hi = x.astype(BF16)
    return hi, (x - hi.astype(F32)).astype(BF16)


def _dot3(a, b):
    ah, al = _split(a)
    bh, bl = _split(b)
    return _dot(ah, bh) + _dot(al, bh) + _dot(ah, bl)


def _rms(x):
    return x * lax.rsqrt(jnp.mean(x * x, axis=-1, keepdims=True) + NORM_EPS)


def _mod_kernel(c_ref, w_ref, b_ref, o_ref):
    c = c_ref[...]
    a = c * (1.0 / (1.0 + jnp.exp(-c)))
    o_ref[...] = _dot3(a, w_ref[...]) + b_ref[...]


def _modulation(cvec, w_ada, b_ada):
    L, D, W = w_ada.shape
    G = cvec.shape[0]
    return pl.pallas_call(
        _mod_kernel,
        grid=(L, W // D),
        in_specs=[pl.BlockSpec((G, D), lambda l, j: (0, 0)),
                  pl.BlockSpec((None, D, D), lambda l, j: (l, 0, j)),
                  pl.BlockSpec((None, 1, D), lambda l, j: (l, 0, j))],
        out_specs=pl.BlockSpec((None, G, D), lambda l, j: (l, 0, j)),
        out_shape=jax.ShapeDtypeStruct((L, G, W), F32),
        compiler_params=_params(("arbitrary", "arbitrary")),
        name="ada_modulation",
    )(cvec, w_ada, b_ada.reshape(L, 1, W))


def _token_rows(x_refs, ctx_blocks):
    if len(x_refs) == 1:
        return lambda rows=slice(None): x_refs[0][rows, :]
    is_ctx = pl.program_id(0) < ctx_blocks
    return lambda rows=slice(None): jnp.where(is_ctx, x_refs[0][rows, :], x_refs[1][rows, :])


def _token_specs(x, tm, ctx_blocks):
    if not isinstance(x, tuple):
        return [x], [pl.BlockSpec((tm, x.shape[1]), lambda i: (i, 0))]
    D = x[0].shape[1]
    return list(x), [pl.BlockSpec((tm, D), lambda i: (jnp.minimum(i, ctx_blocks - 1), 0)),
                     pl.BlockSpec((tm, D), lambda i: (jnp.maximum(i - ctx_blocks, 0), 0))]


def _inproj_kernel(*refs, D, rope_lo, rope_hi, ctx_blocks):
    mod_ref, g_ref, w_ref, cos_ref, sin_ref, o_ref, dk_ref, dv_ref, nk_ref, nv_ref = refs[-10:]
    h = _rms(_token_rows(refs[:-10], ctx_blocks)()) * g_ref[...]
    h = h * (1.0 + mod_ref[:, D:2 * D]) + mod_ref[:, 0:D]
    p = _dot(h.astype(BF16), w_ref[...])
    W = p.shape[1]
    o_ref[:, 0:rope_lo] = p[:, 0:rope_lo].astype(o_ref.dtype)
    o_ref[:, rope_hi:W] = p[:, rope_hi:W].astype(o_ref.dtype)
    cos = cos_ref[...]
    sin = sin_ref[...]
    lane = lax.broadcasted_iota(I32, cos.shape, 1)
    first = (lane % 32) < 16
    for c0 in range(rope_lo, rope_hi, LANES):
        xc = p[:, c0:c0 + LANES]
        partner = jnp.where(first, pltpu.roll(xc, LANES - 16, 1), pltpu.roll(xc, 16, 1))
        o_ref[:, c0:c0 + LANES] = (xc * cos + partner * sin).astype(o_ref.dtype)

    @pl.when(pl.program_id(0) < ctx_blocks)
    def _():
        dk_ref[...] = p[:, COL_DK:COL_DV]
        dv_ref[...] = p[:, COL_DV:COL_NQ]
        nk_ref[...] = p[:, COL_NK:COL_NV]
        nv_ref[...] = p[:, COL_NV:W]


def _in_projection(x, mod4, gain, w_bf, cos_t, sin_t, *, tm, ctx_blocks, blocks_per_seq, l):
    xs, x_specs = _token_specs(x, tm, ctx_blocks)
    N, D = sum(a.shape[0] for a in xs), xs[0].shape[1]
    W = w_bf.shape[1]

    def grp(i):
        return jnp.where(i < ctx_blocks, 0, 1 + (i - ctx_blocks) // blocks_per_seq)

    def rope_blk(i):
        return jnp.where(i < ctx_blocks, 0, 1 + (i - ctx_blocks) % blocks_per_seq)

    kern = functools.partial(_inproj_kernel, D=D, rope_lo=COL_DQ, rope_hi=COL_DV, ctx_blocks=ctx_blocks)
    ctx_rows = ctx_blocks * tm
    ctx_out = lambda w: pl.BlockSpec((tm, w), lambda i: (jnp.minimum(i, ctx_blocks - 1), 0))
    widths = (COL_DV - COL_DK, COL_NQ - COL_DV, COL_NV - COL_NK, W - COL_NV)
    return pl.pallas_call(
        kern,
        grid=(N // tm,),
        in_specs=x_specs + [
                  pl.BlockSpec((None, None, 1, N_MOD * D), lambda i: (l, grp(i), 0, 0)),
                  pl.BlockSpec((1, D), lambda i: (0, 0)),
                  pl.BlockSpec((D, W), lambda i: (0, 0)),
                  pl.BlockSpec((tm, LANES), lambda i: (rope_blk(i), 0)),
                  pl.BlockSpec((tm, LANES), lambda i: (rope_blk(i), 0))],
        out_specs=[pl.BlockSpec((tm, W), lambda i: (i, 0))] + [ctx_out(w) for w in widths],
        out_shape=[jax.ShapeDtypeStruct((N, W), BF16)] + [jax.ShapeDtypeStruct((ctx_rows, w), F32) for w in widths],
        compiler_params=_params(("arbitrary",)),
        name="in_projection",
    )(*xs, mod4, gain, w_bf, cos_t, sin_t)


def _pool_kernel(u_ref, w_ref, sc_ref, o_ref, pad_ref, *, T, CH):
    H = 2 * POOL_MAX_HALF
    zeros = jnp.zeros((H, pad_ref.shape[1]), F32)
    pad_ref[0:H, :] = zeros
    pad_ref[H + T:2 * H + T, :] = zeros
    pad_ref[H:H + T, :] = u_ref[...].astype(F32)
    C = pad_ref.shape[1]
    lane = lax.broadcasted_iota(I32, (CH, C), 1)
    group = lane // (C // POOL_GROUPS)
    half = jnp.left_shift(1, group)
    row = lax.broadcasted_iota(I32, (CH, C), 0)
    R = CH + 2 * H

    def body(ci, carry):
        base = pl.multiple_of(ci * CH, CH)
        win = pad_ref[pl.ds(base, R), :]
        run = win
        acc = None
        for g in range(POOL_GROUPS):
            run = run + pltpu.roll(run, R - (1 << g), 0)
            start = H - (1 << g)
            part = run[start:start + CH, :]
            acc = part if acc is None else jnp.where(group == g, part, acc)
        t = row + base
        cnt = jnp.minimum(t + half, T) - jnp.maximum(t - half, 0)
        d = acc / cnt.astype(F32) - win[H:H + CH, :]
        o_ref[pl.ds(base, CH), :] = (_dot(d.astype(BF16), w_ref[...]) * sc_ref[...]).astype(o_ref.dtype)
        return carry

    lax.fori_loop(0, T // CH, body, 0)


def _pool(p, w_bd, scale, *, row0, n_seq, T, out):
    C = w_bd.shape[0]
    CH = min(T, 256)
    blk0 = row0 // T

    def kern(u_ref, w_ref, sc_ref, prev_ref, o_ref, pad_ref):
        del prev_ref
        _pool_kernel(u_ref, w_ref, sc_ref, o_ref, pad_ref, T=T, CH=CH)

    return pl.pallas_call(
        kern,
        grid=(n_seq,),
        in_specs=[pl.BlockSpec((T, C), lambda s: (blk0 + s, 0)),
                  pl.BlockSpec((C, C), lambda s: (0, 0)),
                  pl.BlockSpec((1, C), lambda s: (0, 0)),
                  pl.BlockSpec(memory_space=pl.ANY)],
        out_specs=pl.BlockSpec((T, C), lambda s: (blk0 + s, 0)),
        out_shape=jax.ShapeDtypeStruct(out.shape, out.dtype),
        scratch_shapes=[pltpu.VMEM((T + 4 * POOL_MAX_HALF, C), F32)],
        input_output_aliases={3: 0},
        compiler_params=_params(("arbitrary",)),
        name="pool_mixer",
    )(p, w_bd, scale, out)


def _softmax_pv(s, v):
    m = jnp.max(s, axis=-1, keepdims=True)
    e = jnp.exp2(s - m)
    l = jnp.sum(e, axis=-1, keepdims=True)
    return _dot(e.astype(BF16), v) * (1.0 / l)


def _diff_kernel(*refs, Ts, Lc, lam_init, nq, n_units):
    if Lc:
        lam_ref, sub_ref, q_ref, ks_ref, vs_ref, kc_ref, vc_ref, prev_ref, o_ref, kb, vt, s_a, s_b, m_a, m_b = refs
    else:
        lam_ref, sub_ref, q_ref, ks_ref, vs_ref, prev_ref, o_ref, kb, vt, s_a, s_b, m_a, m_b = refs
    del prev_ref
    t = pl.program_id(0)
    S = Ts + Lc
    dv = DIFF_V_DIM
    head_new = jnp.minimum(t, n_units - 1) // nq
    slot_new = head_new % 2
    slot_old = (jnp.maximum(t - 1, 0) // nq) % 2

    @pl.when((t % nq == 0) & (t < n_units))
    def _():
        kb[0:Ts, :] = ks_ref[...].astype(BF16)
        vt[slot_new, 0:dv, 0:Ts] = vs_ref[...].astype(F32).T.astype(BF16)
        if Lc:
            kb[Ts:S, :] = kc_ref[...].astype(BF16)
            vt[slot_new, 0:dv, Ts:S] = vc_ref[...].T.astype(BF16)
        vt[slot_new, dv:, :] = jnp.ones((vt.shape[1] - dv, S), BF16)

    @pl.when(t == 0)
    def _():
        s_b[...] = jnp.zeros(s_b.shape, F32)
        m_b[...] = jnp.zeros(m_b.shape, F32)

    lm = lam_ref[...]
    lam = (jnp.exp(jnp.sum(lm[0:1, :] * lm[1:2, :], axis=-1, keepdims=True))
           - jnp.exp(jnp.sum(lm[2:3, :] * lm[3:4, :], axis=-1, keepdims=True)) + lam_init)

    def stage(s_new, m_new, s_old, m_old):
        q = q_ref[...].astype(F32) * (DIFF_QK_DIM ** -0.5 * LOG2E)
        tq = q.shape[0]
        part = lax.broadcasted_iota(I32, q.shape, 1) // DIFF_QK_DIM
        qm = [jnp.where(part == u, q, 0.0).astype(BF16) for u in range(2)]
        top = [m_old[u][0:1, :] for u in range(2)]
        acc = [jnp.zeros((vt.shape[1], tq), F32) for _ in range(2)]
        run = [jnp.full((8, tq), -jnp.inf, F32) for _ in range(2)]
        for c0 in range(0, S, DIFF_KEYS):
            keys = slice(c0, c0 + DIFF_KEYS)
            for u in range(2):
                sc = _dot_nt(kb[keys, :], qm[u])
                s_new[u, keys, :] = sc
                for j in range(0, DIFF_KEYS, 8):
                    run[u] = jnp.maximum(run[u], sc[j:j + 8, :])
                e = jnp.exp2(s_old[u, keys, :] - top[u])
                acc[u] = acc[u] + _dot(vt[slot_old, :, keys], e.astype(BF16))
        for u in range(2):
            m_new[u] = jnp.broadcast_to(jnp.max(run[u], axis=0, keepdims=True), m_new.shape[1:])
        o = (acc[0][0:dv, :] * (1.0 / acc[0][dv:dv + 1, :])
             - lam * (acc[1][0:dv, :] * (1.0 / acc[1][dv:dv + 1, :])))
        o_ref[...] = (_rms(o.T) * sub_ref[...] * (1.0 - lam_init)).astype(o_ref.dtype)

    @pl.when(t % 2 == 0)
    def _():
        stage(s_a, m_a, s_b, m_b)

    @pl.when(t % 2 == 1)
    def _():
        stage(s_b, m_b, s_a, m_a)


def _diff_ctx_kernel(lam_ref, sub_ref, q_ref, k_ref, v_ref, prev_ref, o_ref, *, lam_init, heads):
    del prev_ref
    lm = lam_ref[...]
    lam = (jnp.exp(jnp.sum(lm[0:1, :] * lm[1:2, :], axis=-1, keepdims=True))
           - jnp.exp(jnp.sum(lm[2:3, :] * lm[3:4, :], axis=-1, keepdims=True)) + lam_init)
    q = q_ref[...].astype(F32) * (DIFF_QK_DIM ** -0.5 * LOG2E)
    part = lax.broadcasted_iota(I32, q.shape, 1) // DIFF_QK_DIM
    k = k_ref[...].astype(BF16)
    v = v_ref[...].astype(BF16)
    for hh in range(heads):
        o1 = _softmax_pv(_dot_nt(jnp.where(part == 2 * hh, q, 0.0).astype(BF16), k), v)
        o2 = _softmax_pv(_dot_nt(jnp.where(part == 2 * hh + 1, q, 0.0).astype(BF16), k), v)
        cols = slice(hh * DIFF_V_DIM, (hh + 1) * DIFF_V_DIM)
        o = o1[:, cols] - lam * o2[:, cols]
        o_ref[:, cols] = (_rms(o) * sub_ref[...] * (1.0 - lam_init)).astype(o_ref.dtype)


def _diff_attention_ctx(p, lam_l, subln, out, *, n_seq, T, lam_init):
    heads = 2
    W = heads * DIFF_V_DIM
    spec = lambda col0: pl.BlockSpec((T, W), lambda b, h: (b, col0 // W + h))
    return pl.pallas_call(
        functools.partial(_diff_ctx_kernel, lam_init=lam_init, heads=heads),
        grid=(n_seq, DIFF_HEADS // heads),
        in_specs=[pl.BlockSpec((4, DIFF_QK_DIM), lambda b, h: (0, 0)),
                  pl.BlockSpec((1, DIFF_V_DIM), lambda b, h: (0, 0)),
                  spec(COL_DQ), spec(COL_DK), spec(COL_DV),
                  pl.BlockSpec(memory_space=pl.ANY)],
        out_specs=pl.BlockSpec((T, W), lambda b, h: (b, h)),
        out_shape=jax.ShapeDtypeStruct(out.shape, out.dtype),
        input_output_aliases={5: 0},
        compiler_params=_params(("arbitrary", "arbitrary")),
        name="diff_attention_ctx",
    )(lam_l, subln, p, p, p, out)


def _diff_attention(p, lam_l, subln, out, *, row0, n_seq, T, tq, lam_init, cache_k=None, cache_v=None, l=0):
    W = DIFF_V_DIM
    assert 2 * DIFF_QK_DIM == W == LANES
    Lc = 0 if cache_k is None else cache_k.shape[2]
    S = T + Lc
    assert S % DIFF_KEYS == 0
    nq = T // tq
    qb0 = row0 // tq
    sb0 = row0 // T
    n_units = n_seq * DIFF_HEADS * nq

    def unit(t):
        return t // (DIFF_HEADS * nq), (t // nq) % DIFF_HEADS, t % nq

    def scored(t, col0):
        b, h, i = unit(jnp.minimum(t, n_units - 1))
        return b, h, i, col0 // W + h

    in_specs = [pl.BlockSpec((4, DIFF_QK_DIM), lambda t: (0, 0)),
                pl.BlockSpec((1, W), lambda t: (0, 0)),
                pl.BlockSpec((tq, W), lambda t: (qb0 + scored(t, 0)[0] * nq + scored(t, 0)[2], scored(t, COL_DQ)[3])),
                pl.BlockSpec((T, W), lambda t: (sb0 + scored(t, 0)[0], scored(t, COL_DK)[3])),
                pl.BlockSpec((T, W), lambda t: (sb0 + scored(t, 0)[0], scored(t, COL_DV)[3]))]
    args = [lam_l, subln, p, p, p]
    if Lc:
        in_specs += [pl.BlockSpec((None, None, Lc, W), lambda t: (scored(t, 0)[0], l, 0, scored(t, 0)[1])),
                     pl.BlockSpec((None, None, Lc, W), lambda t: (scored(t, 0)[0], l, 0, scored(t, 0)[1]))]
        args += [cache_k, cache_v]
    in_specs.append(pl.BlockSpec(memory_space=pl.ANY))
    args.append(out)
    scores = pltpu.VMEM((2, S, tq), F32)
    row_max = pltpu.VMEM((2, 8, tq), F32)
    vt_rows = W + 16

    def finished(t):
        b, h, i = unit(jnp.maximum(t - 1, 0))
        return qb0 + b * nq + i, h

    return pl.pallas_call(
        functools.partial(_diff_kernel, Ts=T, Lc=Lc, lam_init=lam_init, nq=nq, n_units=n_units),
        grid=(n_units + 1,),
        in_specs=in_specs,
        out_specs=pl.BlockSpec((tq, W), finished),
        out_shape=jax.ShapeDtypeStruct(out.shape, out.dtype),
        scratch_shapes=[pltpu.VMEM((S, W), BF16), pltpu.VMEM((2, vt_rows, S), BF16),
                        scores, scores, row_max, row_max],
        input_output_aliases={len(args) - 1: 0},
        compiler_params=_params(("arbitrary",)),
        name="diff_attention",
    )(*args)


def _na_heads(q, score_fn, pv_fn):
    lane = lax.broadcasted_iota(I32, q.shape, 1) // NA_HEAD_DIM
    out = jnp.zeros(q.shape, F32)
    for h in range(NA_HEADS):
        qh = jnp.where(lane == h, q, 0.0).astype(BF16)
        out = jnp.where(lane == h, pv_fn(score_fn(qh, h)), out)
    return out


def _dense_kernel(q_ref, k_ref, v_ref, prev_ref, o_ref):
    del prev_ref
    q = q_ref[...].astype(F32) * (NA_HEAD_DIM ** -0.5 * LOG2E)
    k = k_ref[...].astype(BF16)
    v = v_ref[...].astype(BF16)
    o_ref[...] = _na_heads(q, lambda qh, h: _dot_nt(qh, k), lambda s: _softmax_pv(s, v)).astype(o_ref.dtype)


def _dense_attention(p, out, *, n_seq, T):
    C = NA_HEADS * NA_HEAD_DIM
    return pl.pallas_call(
        _dense_kernel,
        grid=(n_seq,),
        in_specs=[pl.BlockSpec((T, C), lambda b: (b, 7)),
                  pl.BlockSpec((T, C), lambda b: (b, 8)),
                  pl.BlockSpec((T, C), lambda b: (b, 9)),
                  pl.BlockSpec(memory_space=pl.ANY)],
        out_specs=pl.BlockSpec((T, C), lambda b: (b, 0)),
        out_shape=jax.ShapeDtypeStruct(out.shape, out.dtype),
        input_output_aliases={3: 0},
        compiler_params=_params(("arbitrary",)),
        name="dense_attention",
    )(p, p, p, out)


def _na_kernel(q_ref, ks_ref, vs_ref, kc_ref, vc_ref, bias_ref, prev_ref, o_ref, *, rows):
    del prev_ref
    r0 = pl.program_id(1) * NA_Q_ROWS
    bs = jnp.clip(r0 - NA_WIN_H // 2, 0, rows - NA_BAND)
    start = pl.multiple_of(bs * GRID_W, GRID_W)
    nb = NA_BAND * GRID_W
    kb = ks_ref[pl.ds(start, nb), :].astype(BF16)
    vb = vs_ref[pl.ds(start, nb), :].astype(BF16)
    kc = kc_ref[...].astype(BF16)
    vc = vc_ref[...].astype(BF16)
    q = q_ref[...].astype(F32) * (NA_HEAD_DIM ** -0.5 * LOG2E)

    def scores(qh, h):
        return _dot_nt(qh, kb) + bias_ref[h], _dot_nt(qh, kc)

    def pv(s):
        s_loc, s_ctx = s
        m = jnp.maximum(jnp.max(s_loc, axis=-1, keepdims=True), jnp.max(s_ctx, axis=-1, keepdims=True))
        e_loc = jnp.exp2(s_loc - m)
        e_ctx = jnp.exp2(s_ctx - m)
        l = jnp.sum(e_loc, axis=-1, keepdims=True) + jnp.sum(e_ctx, axis=-1, keepdims=True)
        return (_dot(e_loc.astype(BF16), vb) + _dot(e_ctx.astype(BF16), vc)) * (1.0 / l)

    o_ref[...] = _na_heads(q, scores, pv).astype(o_ref.dtype)


def _na_bias_table(rpb_l, rows):
    n_ro, n_co = 2 * NA_WIN_H - 1, 2 * NA_WIN_W - 1
    c = np.arange(GRID_W)[:, None]
    kc = np.arange(GRID_W)[None, :]
    cs = np.clip(c - NA_WIN_W // 2, 0, GRID_W - NA_WIN_W)
    col_ok = (kc >= cs) & (kc < cs + NA_WIN_W)
    co = kc - c + (NA_WIN_W - 1)
    pick = ((np.arange(n_co)[:, None, None] == co[None]) & col_ok[None]).astype(np.float32)
    toep = jnp.dot(rpb_l.reshape(NA_HEADS * n_ro, n_co).astype(F32), jnp.asarray(pick.reshape(n_co, -1)),
                   precision=lax.Precision.HIGHEST).reshape(NA_HEADS, n_ro, GRID_W, GRID_W) * LOG2E
    toep = jnp.where(jnp.asarray(col_ok)[None, None], toep, MASKED)
    toep = jnp.concatenate([toep, jnp.full((NA_HEADS, 1, GRID_W, GRID_W), MASKED, F32)], axis=1)
    blk = np.full((3, NA_Q_ROWS, NA_BAND), n_ro, np.int32)
    for v, r0 in enumerate((0, NA_Q_ROWS, rows - NA_Q_ROWS)):
        bs = int(np.clip(r0 - NA_WIN_H // 2, 0, rows - NA_BAND))
        for j in range(NA_Q_ROWS):
            rs = int(np.clip(r0 + j - NA_WIN_H // 2, 0, rows - NA_WIN_H))
            for i in range(NA_BAND):
                if rs <= bs + i < rs + NA_WIN_H:
                    blk[v, j, i] = bs + i - (r0 + j) + (NA_WIN_H - 1)
    tab = toep[:, blk]
    tab = jnp.transpose(tab, (1, 0, 2, 4, 3, 5))
    return tab.reshape(3, NA_HEADS, NA_Q_ROWS * GRID_W, NA_BAND * GRID_W)


def _neighbourhood_attention(p, cache_k, cache_v, bias, out, *, row0, n_seq, T, l):
    C = NA_HEADS * NA_HEAD_DIM
    rows = T // GRID_W
    tq = NA_Q_ROWS * GRID_W
    nq = T // tq
    Lc = cache_k.shape[2]
    qb0 = row0 // tq
    sb0 = row0 // T

    def variant(i):
        r0 = i * NA_Q_ROWS
        return (r0 - jnp.clip(r0 - NA_WIN_H // 2, 0, rows - NA_BAND)) // NA_Q_ROWS

    return pl.pallas_call(
        functools.partial(_na_kernel, rows=rows),
        grid=(n_seq, nq),
        in_specs=[pl.BlockSpec((tq, C), lambda b, i: (qb0 + b * nq + i, 7)),
                  pl.BlockSpec((T, C), lambda b, i: (sb0 + b, 8)),
                  pl.BlockSpec((T, C), lambda b, i: (sb0 + b, 9)),
                  pl.BlockSpec((None, None, Lc, C), lambda b, i: (b, l, 0, 0)),
                  pl.BlockSpec((None, None, Lc, C), lambda b, i: (b, l, 0, 0)),
                  pl.BlockSpec((None, NA_HEADS, tq, NA_BAND * GRID_W), lambda b, i: (variant(i), 0, 0, 0)),
                  pl.BlockSpec(memory_space=pl.ANY)],
        out_specs=pl.BlockSpec((tq, C), lambda b, i: (qb0 + b * nq + i, 0)),
        out_shape=jax.ShapeDtypeStruct(out.shape, out.dtype),
        input_output_aliases={6: 0},
        compiler_params=_params(("arbitrary", "arbitrary")),
        name="neighbourhood_attention",
    )(p, p, p, cache_k, cache_v, bias, out)


def _merge_kernel(pool_ref, dn_ref, na_ref, *refs, D, ctx_blocks):
    mod_ref, g1_ref, g2_ref, w_ref, rw_ref, rb_ref, x1_ref, h2_ref, route_ref, cnt_ref = refs[-10:]
    x_rows = _token_rows(refs[:-10], ctx_blocks)
    parts = [slice(j * TOK_CHUNK, (j + 1) * TOK_CHUNK) for j in range(x1_ref.shape[0] // TOK_CHUNK)]
    c0 = pool_ref.shape[1]
    c1 = c0 + dn_ref.shape[1]
    mix = [_dot(pool_ref[p, :].astype(BF16), w_ref[0:c0, :])
           + _dot(dn_ref[p, :].astype(BF16), w_ref[c0:c1, :])
           + _dot(na_ref[p, :].astype(BF16), w_ref[c1:, :]) for p in parts]
    x1 = [x_rows(p) + mod_ref[:, 2 * D:3 * D] * (_rms(m) * g1_ref[...]) for p, m in zip(parts, mix)]
    for p, v in zip(parts, x1):
        x1_ref[p, :] = v
    h2 = [(_rms(v) * g2_ref[...]) * (1.0 + mod_ref[:, 4 * D:5 * D]) + mod_ref[:, 3 * D:4 * D] for v in x1]
    for p, v in zip(parts, h2):
        h2_ref[p, :] = v.astype(h2_ref.dtype)

    rw_hi, rw_lo = _split(rw_ref[...])
    h2_split = [_split(v) for v in h2]
    logits = [_dot(hi, rw_hi) + _dot(lo, rw_hi) + _dot(hi, rw_lo) + rb_ref[...] for hi, lo in h2_split]
    lane = lax.broadcasted_iota(I32, (TOK_CHUNK, LANES), 1)
    lane_f = lane.astype(F32)
    work = logits
    vals, hots = [], []
    for _ in range(TOP_K):
        mx = [jnp.max(w, axis=-1, keepdims=True) for w in work]
        idx = [jnp.min(jnp.where(w == m, lane_f, float(LANES)), axis=-1, keepdims=True)
               for w, m in zip(work, mx)]
        hot = [lane_f == i for i in idx]
        vals.append(mx)
        hots.append(hot)
        work = [jnp.where(h, -jnp.inf, w) for h, w in zip(hot, work)]

    r = lax.broadcasted_iota(I32, (TOK_CHUNK, TOK_CHUNK), 0)
    c = lax.broadcasted_iota(I32, (TOK_CHUNK, TOK_CHUNK), 1)
    before = jnp.where(c < r, 1.0, 0.0).astype(BF16)
    er = lax.broadcasted_iota(I32, (LANES, LANES), 0)
    ec = lax.broadcasted_iota(I32, (LANES, LANES), 1)
    earlier = jnp.where(er < ec, 1.0, 0.0).astype(BF16)
    for j, p in enumerate(parts):
        hot_j = [hots[k][j] for k in range(TOP_K)]
        es = [jnp.exp(vals[k][j] - vals[0][j]) for k in range(TOP_K)]
        inv = 1.0 / (es[0] + es[1] + es[2] + es[3])
        sel = jnp.zeros((TOK_CHUNK, LANES), F32)
        for hot in hot_j:
            sel = jnp.where(hot, 1.0, sel)
        rank = _dot(before, sel.astype(BF16))
        cnt = jnp.sum(sel, axis=0, keepdims=True)
        cnt_ref[j] = cnt
        run = jnp.floor((cnt + (RUN_ALIGN - 1)) * (1.0 / RUN_ALIGN)) * RUN_ALIGN
        run_start = _dot(jnp.broadcast_to(run, (8, LANES)).astype(BF16), earlier)[0:1, :]
        pos = rank + run_start
        route = jnp.zeros((TOK_CHUNK, LANES), F32)
        for k in range(TOP_K):
            e_k = jnp.sum(jnp.where(hot_j[k], lane_f, 0.0), axis=-1, keepdims=True)
            p_k = jnp.sum(jnp.where(hot_j[k], pos, 0.0), axis=-1, keepdims=True)
            route = jnp.where(lane == k, e_k, route)
            route = jnp.where(lane == TOP_K + k, p_k, route)
            route = jnp.where(lane == 2 * TOP_K + k, es[k] * inv, route)
        route_ref[p, :] = route


def _merge_route(pool_o, dn, na_o, x, mod4, g1, g2, w_out_bf, rw_pad, rb_pad, *, tm, ctx_blocks,
                 blocks_per_seq, l):
    xs, x_specs = _token_specs(x, tm, ctx_blocks)
    N, D = sum(a.shape[0] for a in xs), xs[0].shape[1]
    per_step = tm // TOK_CHUNK

    def grp(i):
        return jnp.where(i < ctx_blocks, 0, 1 + (i - ctx_blocks) // blocks_per_seq)

    row = lambda w: pl.BlockSpec((tm, w), lambda i: (i, 0))
    full = lambda a: pl.BlockSpec(a.shape, lambda i: (0,) * a.ndim)
    return pl.pallas_call(
        functools.partial(_merge_kernel, D=D, ctx_blocks=ctx_blocks),
        grid=(N // tm,),
        in_specs=[row(pool_o.shape[1]), row(dn.shape[1]), row(na_o.shape[1])] + x_specs + [
                  pl.BlockSpec((None, None, 1, N_MOD * D), lambda i: (l, grp(i), 0, 0)),
                  full(g1), full(g2), full(w_out_bf), full(rw_pad), full(rb_pad)],
        out_specs=[row(D), row(D), row(LANES), pl.BlockSpec((per_step, 1, LANES), lambda i: (i, 0, 0))],
        out_shape=[jax.ShapeDtypeStruct((N, D), F32), jax.ShapeDtypeStruct((N, D), BF16),
                   jax.ShapeDtypeStruct((N, LANES), F32), jax.ShapeDtypeStruct((N // TOK_CHUNK, 1, LANES), F32)],
        compiler_params=_params(("arbitrary",)),
        name="merge_route",
    )(pool_o, dn, na_o, *xs, mod4, g1, g2, w_out_bf, rw_pad, rb_pad)


def _pack_pairs(x):
    C = x.shape[1] // 2
    bits = lax.bitcast_convert_type(x.astype(BF16).astype(F32), jnp.uint32)
    return bits[:, C:] | (bits[:, :C] >> 16)


def _unpack_pairs(w):
    lo = lax.bitcast_convert_type(w << 16, F32)
    hi = lax.bitcast_convert_type(w & jnp.uint32(0xFFFF0000), F32)
    return jnp.concatenate([lo, hi], axis=1).astype(BF16)


def _run_sizes(lo, hi):
    return [1 << k for k in range(hi.bit_length() - 1, lo.bit_length() - 2, -1)]


def _run_copies(n, src_at, dst_at, sem, wait, sizes):
    for size in sizes:
        @pl.when((n & size) != 0)
        def _(size=size):
            off = n & -(2 * size)
            cp = pltpu.make_async_copy(src_at(off, size), dst_at(off, size), sem)
            cp.wait() if wait else cp.start()


def _block_runs(step, n_exp, run_ref, src_ref, dst_ref, large_ref, hbm_ref, buf, sem, *, to_hbm, wait):
    def each_run(sizes):
        def body(e, carry):
            j = step * n_exp + e
            so = src_ref[j]
            do = dst_ref[j]
            in_buf = lambda o, s: buf.at[pl.ds(pl.multiple_of(so + o, RUN_ALIGN), s)]
            in_hbm = lambda o, s: hbm_ref.at[pl.ds(pl.multiple_of(do + o, RUN_ALIGN), s)]
            if to_hbm:
                _run_copies(run_ref[j], in_buf, in_hbm, sem, wait, sizes)
            else:
                _run_copies(run_ref[j], in_hbm, in_buf, sem, wait, sizes)
            return carry
        lax.fori_loop(0, n_exp, body, 0, unroll=RUN_UNROLL)

    each_run(_run_sizes(RUN_ALIGN, RUN_LARGE // 2))

    @pl.when(large_ref[step] != 0)
    def _():
        each_run(_run_sizes(RUN_LARGE, TOK_CHUNK))


def _dispatch_kernel(run_ref, src_ref, dst_ref, large_ref, tail_ref, taildst_ref, nu_ref, route_ref, h_ref, xs_ref,
                     sorted_buf, zero_buf, sems, *, n_exp):
    b = pl.program_id(0)
    nb = pl.num_programs(0)
    slot = b % 2
    runs = functools.partial(_block_runs, n_exp=n_exp, run_ref=run_ref, src_ref=src_ref, dst_ref=dst_ref,
                             large_ref=large_ref, hbm_ref=xs_ref, to_hbm=True)

    @pl.when(b >= 2)
    def _():
        runs(b - 2, buf=sorted_buf.at[slot], sem=sems.at[slot], wait=True)

    route = route_ref[...]
    col = lax.broadcasted_iota(I32, (TOK_CHUNK, sorted_buf.shape[1]), 1).astype(F32)
    place = jnp.zeros(col.shape, F32)
    for k in range(TOP_K):
        place = jnp.where(col == route[:, TOP_K + k:TOP_K + k + 1], 1.0, place)
    srt = lax.dot_general(place.astype(BF16), h_ref[...].astype(BF16), (((0,), (0,)), ((), ())),
                          preferred_element_type=F32)
    sorted_buf[slot] = _pack_pairs(srt)
    runs(b, buf=sorted_buf.at[slot], sem=sems.at[slot], wait=False)

    @pl.when(b == nb - 1)
    def _():
        @pl.when(b >= 1)
        def _():
            runs(b - 1, buf=sorted_buf.at[1 - slot], sem=sems.at[1 - slot], wait=True)
        runs(b, buf=sorted_buf.at[slot], sem=sems.at[slot], wait=True)

        zero_buf[...] = jnp.zeros(zero_buf.shape, zero_buf.dtype)
        sem = sems.at[0]

        def each_tail(wait):
            def body(e, carry):
                do = taildst_ref[e]
                _run_copies(tail_ref[e],
                            lambda o, s: zero_buf.at[pl.ds(0, s)],
                            lambda o, s: xs_ref.at[pl.ds(pl.multiple_of(do + o, RUN_ALIGN), s)], sem, wait,
                            _run_sizes(RUN_ALIGN, MOE_ROWS // 2))
                return carry
            lax.fori_loop(0, n_exp, body, 0)

        def spare_block(wait):
            def body(i, carry):
                cp = pltpu.make_async_copy(zero_buf.at[pl.ds(0, MOE_ROWS)],
                                           xs_ref.at[pl.ds(pl.multiple_of(i * MOE_ROWS, MOE_ROWS), MOE_ROWS)], sem)
                cp.wait() if wait else cp.start()
                return carry
            lax.fori_loop(nu_ref[0], xs_ref.shape[0] // MOE_ROWS, body, 0)

        each_tail(False)
        spare_block(False)
        each_tail(True)
        spare_block(True)


def _dispatch(plan, n_used, route, h2, n_rows, n_exp):
    N, D = h2.shape
    C = D // 2
    grid_spec = pltpu.PrefetchScalarGridSpec(
        num_scalar_prefetch=7,
        grid=(N // TOK_CHUNK,),
        in_specs=[pl.BlockSpec((TOK_CHUNK, LANES), lambda i, *_: (i, 0)),
                  pl.BlockSpec((TOK_CHUNK, D), lambda i, *_: (i, 0))],
        out_specs=pl.BlockSpec(memory_space=pl.ANY),
        scratch_shapes=[pltpu.VMEM((2, _sorted_rows(n_exp), C), jnp.uint32),
                        pltpu.VMEM((max(TOK_CHUNK, MOE_ROWS), C), jnp.uint32),
                        pltpu.SemaphoreType.DMA((2,))],
    )
    return pl.pallas_call(
        functools.partial(_dispatch_kernel, n_exp=n_exp),
        grid_spec=grid_spec,
        out_shape=jax.ShapeDtypeStruct((n_rows, C), jnp.uint32),
        compiler_params=_params(("arbitrary",)),
        name="moe_dispatch",
    )(plan["run"], plan["src"], plan["dst"], plan["large"], plan["tail"], plan["tail_dst"], n_used, route, h2)


def _expert_kernel(be_ref, nu_ref, rows_ref, xs_ref, w1_ref, b1_ref, w2_ref, b2_ref, o_ref, w1b, w2b, *, F):
    del nu_ref
    i = pl.program_id(0)
    e = be_ref[i]
    prev = be_ref[jnp.maximum(i - 1, 0)]
    rows = rows_ref[i]
    half = MOE_ROWS // 2

    @pl.when((i == 0) | (e != prev))
    def _():
        w1b[...] = w1_ref[...].astype(BF16)
        w2b[...] = w2_ref[...].astype(BF16)

    def ffn(xw):
        hh = _dot(_unpack_pairs(xw), w1b[...]) + b1_ref[...]
        g = jnp.minimum(hh[:, 0:F], SWIGLU_LIMIT)
        u = jnp.clip(hh[:, F:2 * F], -SWIGLU_LIMIT, SWIGLU_LIMIT)
        a = (g * (1.0 / (1.0 + jnp.exp(-SWIGLU_ALPHA * g)))) * (u + 1.0)
        return _pack_pairs(_dot(a.astype(BF16), w2b[...]) + b2_ref[...])

    @pl.when(rows > half)
    def _():
        o_ref[...] = ffn(xs_ref[...])

    @pl.when((rows > 0) & (rows <= half))
    def _():
        o_ref[0:half, :] = ffn(xs_ref[0:half, :])
        o_ref[half:MOE_ROWS, :] = jnp.zeros((MOE_ROWS - half, o_ref.shape[1]), o_ref.dtype)

    @pl.when(rows == 0)
    def _():
        o_ref[...] = jnp.zeros(o_ref.shape, o_ref.dtype)


def _experts(block_e, n_used, block_rows, xs, w1, b1, w2, b2, *, l):
    R, C = xs.shape
    L, E, D, F2 = w1.shape
    F = F2 // 2
    nblk = R // MOE_ROWS
    grid_spec = pltpu.PrefetchScalarGridSpec(
        num_scalar_prefetch=3,
        grid=(nblk,),
        in_specs=[pl.BlockSpec((MOE_ROWS, C), lambda i, be, nu, br: (jnp.minimum(i, nu[0] - 1), 0)),
                  pl.BlockSpec((None, None, D, F2), lambda i, be, nu, br: (l, be[i], 0, 0)),
                  pl.BlockSpec((None, None, 1, F2), lambda i, be, nu, br: (l, be[i], 0, 0)),
                  pl.BlockSpec((None, None, F, D), lambda i, be, nu, br: (l, be[i], 0, 0)),
                  pl.BlockSpec((None, None, 1, D), lambda i, be, nu, br: (l, be[i], 0, 0))],
        out_specs=pl.BlockSpec((MOE_ROWS, C), lambda i, be, nu, br: (i, 0)),
        scratch_shapes=[pltpu.VMEM((D, F2), BF16), pltpu.VMEM((F, D), BF16)],
    )
    return pl.pallas_call(
        functools.partial(_expert_kernel, F=F),
        grid_spec=grid_spec,
        out_shape=jax.ShapeDtypeStruct((R, C), jnp.uint32),
        compiler_params=_params(("arbitrary",)),
        name="moe_experts",
    )(block_e, n_used, block_rows, xs, w1, b1.reshape(L, E, 1, F2), w2, b2.reshape(L, E, 1, D))


def _combine_kernel(run_ref, src_ref, dst_ref, large_ref, yb_ref, route_ref, x1_ref, mod_ref, g_ref, *rest, D,
                    n_exp, ctx_blocks, split):
    if split:
        ctx_ref, lat_ref, sorted_buf, sems = rest
    else:
        o_ref, sorted_buf, sems = rest
    b = pl.program_id(0)
    nb = pl.num_programs(0)
    slot = b % 2
    runs = functools.partial(_block_runs, n_exp=n_exp, run_ref=run_ref, src_ref=src_ref, dst_ref=dst_ref,
                             large_ref=large_ref, hbm_ref=yb_ref, to_hbm=False)

    @pl.when(b == 0)
    def _():
        sorted_buf[...] = jnp.zeros(sorted_buf.shape, sorted_buf.dtype)
        runs(b, buf=sorted_buf.at[slot], sem=sems.at[slot], wait=False)

    @pl.when(b + 1 < nb)
    def _():
        runs(b + 1, buf=sorted_buf.at[1 - slot], sem=sems.at[1 - slot], wait=False)

    runs(b, buf=sorted_buf.at[slot], sem=sems.at[slot], wait=True)
    route = route_ref[...]
    col = lax.broadcasted_iota(I32, (TOK_CHUNK, sorted_buf.shape[1]), 1).astype(F32)
    gate = jnp.zeros(col.shape, F32)
    for k in range(TOP_K):
        gate = jnp.where(col == route[:, TOP_K + k:TOP_K + k + 1], route[:, 2 * TOP_K + k:2 * TOP_K + k + 1], gate)
    g_hi, g_lo = _split(gate)
    yb = _unpack_pairs(sorted_buf[slot])
    y = _dot(g_hi, yb) + _dot(g_lo, yb)
    x2 = x1_ref[...] + mod_ref[:, 5 * D:6 * D] * (_rms(y) * g_ref[...])
    if split:
        @pl.when(b < ctx_blocks)
        def _():
            ctx_ref[...] = x2

        @pl.when(b >= ctx_blocks)
        def _():
            lat_ref[...] = x2
    else:
        o_ref[...] = x2


def _combine(plan, yb, route, x1, mod4, g3, *, n_exp, ctx_blocks, blocks_per_seq, l, split):
    N, D = x1.shape
    row = lambda f: pl.BlockSpec((TOK_CHUNK, D), lambda i, *_: (f(i), 0))
    if split:
        out_specs = [row(lambda i: jnp.minimum(i, ctx_blocks - 1)), row(lambda i: jnp.maximum(i - ctx_blocks, 0))]
        out_shape = [jax.ShapeDtypeStruct((ctx_blocks * TOK_CHUNK, D), F32),
                     jax.ShapeDtypeStruct((N - ctx_blocks * TOK_CHUNK, D), F32)]
    else:
        out_specs = row(lambda i: i)
        out_shape = jax.ShapeDtypeStruct((N, D), F32)

    def grp(i):
        return jnp.where(i < ctx_blocks, 0, 1 + (i - ctx_blocks) // blocks_per_seq)

    grid_spec = pltpu.PrefetchScalarGridSpec(
        num_scalar_prefetch=4,
        grid=(N // TOK_CHUNK,),
        in_specs=[pl.BlockSpec(memory_space=pl.ANY),
                  pl.BlockSpec((TOK_CHUNK, LANES), lambda i, *_: (i, 0)),
                  pl.BlockSpec((TOK_CHUNK, D), lambda i, *_: (i, 0)),
                  pl.BlockSpec((None, None, 1, N_MOD * D), lambda i, *_: (l, grp(i), 0, 0)),
                  pl.BlockSpec((1, D), lambda i, *_: (0, 0))],
        out_specs=out_specs,
        scratch_shapes=[pltpu.VMEM((2, _sorted_rows(n_exp), yb.shape[1]), jnp.uint32),
                        pltpu.SemaphoreType.DMA((2,))],
    )
    return pl.pallas_call(
        functools.partial(_combine_kernel, D=D, n_exp=n_exp, ctx_blocks=ctx_blocks, split=split),
        grid_spec=grid_spec,
        out_shape=out_shape,
        compiler_params=_params(("arbitrary",)),
        name="moe_combine",
    )(plan["run"], plan["src"], plan["dst"], plan["large"], yb, route, x1, mod4, g3)


def _rope_tables(Ts, tm):
    nf = DIFF_QK_DIM // 4
    inv = ROPE_BASE ** (-jnp.arange(nf, dtype=F32) / nf)
    t = jnp.arange(Ts)
    pos = jnp.stack([(t // GRID_W).astype(F32), (t % GRID_W).astype(F32)], axis=1)
    ang = pos[:, :, None] * inv[None, None, :]
    cos = jnp.repeat(jnp.cos(ang)[:, :, None, :], 2, axis=2).reshape(Ts, DIFF_QK_DIM)
    sin = jnp.sin(ang)
    sin = jnp.stack([-sin, sin], axis=2).reshape(Ts, DIFF_QK_DIM)
    reps = LANES // DIFF_QK_DIM
    cos = jnp.concatenate([jnp.ones((tm, LANES), F32), jnp.tile(cos, (1, reps))], axis=0)
    sin = jnp.concatenate([jnp.zeros((tm, LANES), F32), jnp.tile(sin, (1, reps))], axis=0)
    return cos, sin


def _block_diag(w):
    G, a, b = w.shape
    out = jnp.zeros((G * a, G * b), w.dtype)
    for g in range(G):
        out = out.at[g * a:(g + 1) * a, g * b:(g + 1) * b].set(w[g])
    return out


def _route_plan(counts, n_exp, nblk):
    cnt = counts[:, 0, :n_exp].astype(I32)
    run = (cnt + RUN_ALIGN - 1) // RUN_ALIGN * RUN_ALIGN
    src = jnp.cumsum(run, axis=1) - run
    tot = jnp.sum(run, axis=0)
    region = (tot + MOE_ROWS - 1) // MOE_ROWS * MOE_ROWS
    region_end = jnp.cumsum(region)
    region_start = region_end - region
    dst = region_start[None, :] + jnp.cumsum(run, axis=0) - run
    n_used = (region_end[-1] // MOE_ROWS).astype(I32)
    blk = jnp.arange(nblk, dtype=I32) * MOE_ROWS
    block_e = jnp.minimum(jnp.sum((blk[:, None] >= region_end[None, :]).astype(I32), axis=1), n_exp - 1)
    last = jnp.sum(jnp.where(jnp.arange(nblk) == n_used - 1, block_e, 0))
    used = jnp.arange(nblk) < n_used
    block_rows = jnp.where(used, jnp.clip((region_start + tot)[block_e] - blk, 0, MOE_ROWS), 0).astype(I32)
    block_e = jnp.where(used, block_e, last).astype(I32)
    plan = dict(run=run.reshape(-1).astype(I32), src=src.reshape(-1).astype(I32), dst=dst.reshape(-1).astype(I32),
                large=jnp.any(run >= RUN_LARGE, axis=1).astype(I32),
                tail=(region - tot).astype(I32), tail_dst=(region_start + tot).astype(I32))
    return plan, block_e, n_used.reshape(1), block_rows


def kernel(x_prompt, x_sample, cache_diff_k, cache_diff_v, cache_na_k, cache_na_v, c, c_ctx, w_ada, b_ada,
           norm_gain, w_in, w_out, pool_w, pool_scale, diff_lambda, diff_subln, na_rpb, router_w, router_b,
           moe_w1, moe_b1, moe_w2, moe_b2):
    Bp, Tp, D = x_prompt.shape
    Bs, Ts, _ = x_sample.shape
    L = w_ada.shape[0]
    E = router_w.shape[-1]
    Np, Ns = Bp * Tp, Bs * Ts
    N = Np + Ns
    tm = TOK_CHUNK
    assert Np % Ts == 0 or Bs == 0, "context rows must be a whole number of latent-sequence blocks"
    assert Np % tm == 0 and Ts % tm == 0 and Ts % (NA_Q_ROWS * GRID_W) == 0
    assert Ts // GRID_W >= NA_BAND + NA_Q_ROWS
    ctx_blocks, blocks_per_seq = Np // tm, Ts // tm

    G = 16
    cvec = jnp.zeros((G, D), F32).at[0].set(c_ctx).at[1:1 + Bs].set(c)
    mod4 = _modulation(cvec, w_ada, b_ada).reshape(L, G, 1, N_MOD * D)
    tp = PROJ_ROWS
    assert Np % tp == 0 and Ts % tp == 0 and Np % MERGE_ROWS == 0 and Ts % MERGE_ROWS == 0
    cos_t, sin_t = _rope_tables(Ts, tp)
    w_in_bf = w_in.astype(BF16)
    w_out_bf = w_out.astype(BF16)
    rw_pad = jnp.zeros((L, D, LANES), F32).at[:, :, :E].set(router_w)
    rb_pad = jnp.full((L, 1, LANES), MASKED, F32).at[:, 0, :E].set(router_b)
    ck = cache_diff_k.reshape(Bs, L, -1, DIFF_HEADS * 2 * DIFF_QK_DIM)
    cv = cache_diff_v.reshape(Bs, L, -1, DIFF_HEADS * DIFF_V_DIM)
    nk = cache_na_k.reshape(Bs, L, -1, NA_HEADS * NA_HEAD_DIM)
    nv = cache_na_v.reshape(Bs, L, -1, NA_HEADS * NA_HEAD_DIM)
    nb = N // TOK_CHUNK
    nblk = -(-(N * TOP_K + nb * E * (RUN_ALIGN - 1) + E * (MOE_ROWS - 1)) // MOE_ROWS)
    blocks = dict(ctx_blocks=ctx_blocks, blocks_per_seq=blocks_per_seq)

    x = (x_prompt.reshape(Np, D), x_sample.reshape(Ns, D))
    new_dk, new_dv, new_nk, new_nv = [], [], [], []
    for l in range(L):
        lam_init = 0.8 - 0.6 * math.exp(-0.3 * l)
        g = norm_gain[l]
        p, dk_l, dv_l, nk_l, nv_l = _in_projection(x, mod4, g[0:1], w_in_bf[l], cos_t, sin_t, tm=tp, l=l,
                                                   ctx_blocks=Np // tp, blocks_per_seq=Ts // tp)

        pw = _block_diag(pool_w[l]).astype(BF16)
        ps = pool_scale[l].reshape(1, -1)
        pool_o = jnp.zeros((N, pw.shape[0]), BF16)
        pool_o = _pool(p, pw, ps, row0=0, n_seq=Bp, T=Tp, out=pool_o)
        pool_o = _pool(p, pw, ps, row0=Np, n_seq=Bs, T=Ts, out=pool_o)

        sub = diff_subln[l].reshape(1, -1)
        dn = jnp.zeros((N, DIFF_HEADS * DIFF_V_DIM), BF16)
        dn = _diff_attention_ctx(p, diff_lambda[l], sub, dn, n_seq=Bp, T=Tp, lam_init=lam_init)
        dn = _diff_attention(p, diff_lambda[l], sub, dn, row0=Np, n_seq=Bs, T=Ts, tq=DIFF_Q_ROWS, lam_init=lam_init,
                             cache_k=ck, cache_v=cv, l=l)

        na_o = jnp.zeros((N, NA_HEADS * NA_HEAD_DIM), BF16)
        na_o = _dense_attention(p, na_o, n_seq=Bp, T=Tp)
        bias = _na_bias_table(na_rpb[l], Ts // GRID_W)
        na_o = _neighbourhood_attention(p, nk, nv, bias, na_o, row0=Np, n_seq=Bs, T=Ts, l=l)

        x1, h2, route, counts = _merge_route(pool_o, dn, na_o, x, mod4, g[1:2], g[2:3], w_out_bf[l],
                                             rw_pad[l], rb_pad[l], tm=MERGE_ROWS, l=l,
                                             ctx_blocks=Np // MERGE_ROWS, blocks_per_seq=Ts // MERGE_ROWS)
        plan, block_e, n_used, block_rows = _route_plan(counts, E, nblk)
        xs = _dispatch(plan, n_used, route, h2, nblk * MOE_ROWS, E)
        yb = _experts(block_e, n_used, block_rows, xs, moe_w1, moe_b1, moe_w2, moe_b2, l=l)
        x = _combine(plan, yb, route, x1, mod4, g[3:4], n_exp=E, l=l, split=l == L - 1, **blocks)

        new_dk.append(dk_l.reshape(Bp, Tp, DIFF_HEADS, 2 * DIFF_QK_DIM))
        new_dv.append(dv_l.reshape(Bp, Tp, DIFF_HEADS, DIFF_V_DIM))
        new_nk.append(nk_l.reshape(Bp, Tp, NA_HEADS, NA_HEAD_DIM))
        new_nv.append(nv_l.reshape(Bp, Tp, NA_HEADS, NA_HEAD_DIM))

    return (x[0].reshape(Bp, Tp, D), x[1].reshape(Bs, Ts, D),
            jnp.stack(new_dk, axis=1), jnp.stack(new_dv, axis=1),
            jnp.stack(new_nk, axis=1), jnp.stack(new_nv, axis=1))
```

```python
import functools
import math

import numpy as np
import jax
import jax.numpy as jnp
from jax import lax
from jax.experimental import pallas as pl
from jax.experimental.pallas import tpu as pltpu

F32 = jnp.float32
BF16 = jnp.bfloat16
I32 = jnp.int32

GRID_W = 64
POOL_GROUPS = 4
POOL_MAX_HALF = 8
DIFF_HEADS = 4
DIFF_V_DIM = 128
DIFF_QK_DIM = 64
NA_HEADS = 4
NA_HEAD_DIM = 64
NA_WIN_H = 8
NA_WIN_W = 16
NA_Q_ROWS = 4
NA_BAND = 12
TOP_K = 4
SWIGLU_ALPHA = 1.702
SWIGLU_LIMIT = 7.0
ROPE_BASE = 10000.0
NORM_EPS = 1e-6
N_MOD = 6

LOG2E = 1.4426950408889634
MASKED = -1e30
LANES = 128
MXU_TILE = 256
MOE_ROWS = 512
COL_DQ, COL_DK, COL_DV = 256, 768, 1280
COL_NQ, COL_NK, COL_NV = 1792, 2048, 2304
DIFF_KEYS = 256
PROJ_ROWS = 512
MERGE_ROWS = 1024
DIFF_Q_ROWS = 256
TOK_CHUNK = 256
RUN_ALIGN = 8
RUN_LARGE = 128
RUN_UNROLL = 4
VMEM_LIMIT = 56 * 1024 * 1024


def _sorted_rows(n_exp):
    rows = TOK_CHUNK * TOP_K + n_exp * (RUN_ALIGN - 1)
    return -(-rows // MXU_TILE) * MXU_TILE


def _params(sem, vmem=VMEM_LIMIT):
    return pltpu.CompilerParams(dimension_semantics=sem, vmem_limit_bytes=vmem)


def _dot(a, b):
    return jnp.dot(a, b, preferred_element_type=F32)


def _dot_nt(a, b):
    return lax.dot_general(a, b, (((1,), (1,)), ((), ())), preferred_element_type=F32)


def _split(x):
    hi = x.astype(BF16)
    return hi, (x - hi.astype(F32)).astype(BF16)


def _dot3(a, b):
    ah, al = _split(a)
    bh, bl = _split(b)
    return _dot(ah, bh) + _dot(al, bh) + _dot(ah, bl)


def _rms(x):
    return x * lax.rsqrt(jnp.mean(x * x, axis=-1, keepdims=True) + NORM_EPS)


def _mod_kernel(c_ref, w_ref, b_ref, o_ref):
    c = c_ref[...]
    a = c * (1.0 / (1.0 + jnp.exp(-c)))
    o_ref[...] = _dot3(a, w_ref[...]) + b_ref[...]


def _modulation(cvec, w_ada, b_ada):
    L, D, W = w_ada.shape
    G = cvec.shape[0]
    return pl.pallas_call(
        _mod_kernel,
        grid=(L, W // D),
        in_specs=[pl.BlockSpec((G, D), lambda l, j: (0, 0)),
                  pl.BlockSpec((None, D, D), lambda l, j: (l, 0, j)),
                  pl.BlockSpec((None, 1, D), lambda l, j: (l, 0, j))],
        out_specs=pl.BlockSpec((None, G, D), lambda l, j: (l, 0, j)),
        out_shape=jax.ShapeDtypeStruct((L, G, W), F32),
        compiler_params=_params(("arbitrary", "arbitrary")),
        name="ada_modulation",
    )(cvec, w_ada, b_ada.reshape(L, 1, W))


def _token_rows(x_refs, ctx_blocks):
    if len(x_refs) == 1:
        return lambda rows=slice(None): x_refs[0][rows, :]
    is_ctx = pl.program_id(0) < ctx_blocks
    return lambda rows=slice(None): jnp.where(is_ctx, x_refs[0][rows, :], x_refs[1][rows, :])


def _token_specs(x, tm, ctx_blocks):
    if not isinstance(x, tuple):
        return [x], [pl.BlockSpec((tm, x.shape[1]), lambda i: (i, 0))]
    D = x[0].shape[1]
    return list(x), [pl.BlockSpec((tm, D), lambda i: (jnp.minimum(i, ctx_blocks - 1), 0)),
                     pl.BlockSpec((tm, D), lambda i: (jnp.maximum(i - ctx_blocks, 0), 0))]


def _inproj_kernel(*refs, D, rope_lo, rope_hi, ctx_blocks):
    mod_ref, g_ref, w_ref, cos_ref, sin_ref, o_ref, dk_ref, dv_ref, nk_ref, nv_ref = refs[-10:]
    h = _rms(_token_rows(refs[:-10], ctx_blocks)()) * g_ref[...]
    h = h * (1.0 + mod_ref[:, D:2 * D]) + mod_ref[:, 0:D]
    p = _dot(h.astype(BF16), w_ref[...])
    W = p.shape[1]
    o_ref[:, 0:rope_lo] = p[:, 0:rope_lo].astype(o_ref.dtype)
    o_ref[:, rope_hi:W] = p[:, rope_hi:W].astype(o_ref.dtype)
    cos = cos_ref[...]
    sin = sin_ref[...]
    lane = lax.broadcasted_iota(I32, cos.shape, 1)
    first = (lane % 32) < 16
    for c0 in range(rope_lo, rope_hi, LANES):
        xc = p[:, c0:c0 + LANES]
        partner = jnp.where(first, pltpu.roll(xc, LANES - 16, 1), pltpu.roll(xc, 16, 1))
        o_ref[:, c0:c0 + LANES] = (xc * cos + partner * sin).astype(o_ref.dtype)

    @pl.when(pl.program_id(0) < ctx_blocks)
    def _():
        dk_ref[...] = p[:, COL_DK:COL_DV]
        dv_ref[...] = p[:, COL_DV:COL_NQ]
        nk_ref[...] = p[:, COL_NK:COL_NV]
        nv_ref[...] = p[:, COL_NV:W]


def _in_projection(x, mod4, gain, w_bf, cos_t, sin_t, *, tm, ctx_blocks, blocks_per_seq, l):
    xs, x_specs = _token_specs(x, tm, ctx_blocks)
    N, D = sum(a.shape[0] for a in xs), xs[0].shape[1]
    W = w_bf.shape[1]

    def grp(i):
        return jnp.where(i < ctx_blocks, 0, 1 + (i - ctx_blocks) // blocks_per_seq)

    def rope_blk(i):
        return jnp.where(i < ctx_blocks, 0, 1 + (i - ctx_blocks) % blocks_per_seq)

    kern = functools.partial(_inproj_kernel, D=D, rope_lo=COL_DQ, rope_hi=COL_DV, ctx_blocks=ctx_blocks)
    ctx_rows = ctx_blocks * tm
    ctx_out = lambda w: pl.BlockSpec((tm, w), lambda i: (jnp.minimum(i, ctx_blocks - 1), 0))
    widths = (COL_DV - COL_DK, COL_NQ - COL_DV, COL_NV - COL_NK, W - COL_NV)
    return pl.pallas_call(
        kern,
        grid=(N // tm,),
        in_specs=x_specs + [
                  pl.BlockSpec((None, None, 1, N_MOD * D), lambda i: (l, grp(i), 0, 0)),
                  pl.BlockSpec((1, D), lambda i: (0, 0)),
                  pl.BlockSpec((D, W), lambda i: (0, 0)),
                  pl.BlockSpec((tm, LANES), lambda i: (rope_blk(i), 0)),
                  pl.BlockSpec((tm, LANES), lambda i: (rope_blk(i), 0))],
        out_specs=[pl.BlockSpec((tm, W), lambda i: (i, 0))] + [ctx_out(w) for w in widths],
        out_shape=[jax.ShapeDtypeStruct((N, W), BF16)] + [jax.ShapeDtypeStruct((ctx_rows, w), F32) for w in widths],
        compiler_params=_params(("arbitrary",)),
        name="in_projection",
    )(*xs, mod4, gain, w_bf, cos_t, sin_t)


def _pool_kernel(u_ref, w_ref, sc_ref, o_ref, pad_ref, *, T, CH):
    H = 2 * POOL_MAX_HALF
    zeros = jnp.zeros((H, pad_ref.shape[1]), F32)
    pad_ref[0:H, :] = zeros
    pad_ref[H + T:2 * H + T, :] = zeros
    pad_ref[H:H + T, :] = u_ref[...].astype(F32)
    C = pad_ref.shape[1]
    lane = lax.broadcasted_iota(I32, (CH, C), 1)
    group = lane // (C // POOL_GROUPS)
    half = jnp.left_shift(1, group)
    row = lax.broadcasted_iota(I32, (CH, C), 0)
    R = CH + 2 * H

    def body(ci, carry):
        base = pl.multiple_of(ci * CH, CH)
        win = pad_ref[pl.ds(base, R), :]
        run = win
        acc = None
        for g in range(POOL_GROUPS):
            run = run + pltpu.roll(run, R - (1 << g), 0)
            start = H - (1 << g)
            part = run[start:start + CH, :]
            acc = part if acc is None else jnp.where(group == g, part, acc)
        t = row + base
        cnt = jnp.minimum(t + half, T) - jnp.maximum(t - half, 0)
        d = acc / cnt.astype(F32) - win[H:H + CH, :]
        o_ref[pl.ds(base, CH), :] = (_dot(d.astype(BF16), w_ref[...]) * sc_ref[...]).astype(o_ref.dtype)
        return carry

    lax.fori_loop(0, T // CH, body, 0)


def _pool(p, w_bd, scale, *, row0, n_seq, T, out):
    C = w_bd.shape[0]
    CH = min(T, 256)
    blk0 = row0 // T

    def kern(u_ref, w_ref, sc_ref, prev_ref, o_ref, pad_ref):
        del prev_ref
        _pool_kernel(u_ref, w_ref, sc_ref, o_ref, pad_ref, T=T, CH=CH)

    return pl.pallas_call(
        kern,
        grid=(n_seq,),
        in_specs=[pl.BlockSpec((T, C), lambda s: (blk0 + s, 0)),
                  pl.BlockSpec((C, C), lambda s: (0, 0)),
                  pl.BlockSpec((1, C), lambda s: (0, 0)),
                  pl.BlockSpec(memory_space=pl.ANY)],
        out_specs=pl.BlockSpec((T, C), lambda s: (blk0 + s, 0)),
        out_shape=jax.ShapeDtypeStruct(out.shape, out.dtype),
        scratch_shapes=[pltpu.VMEM((T + 4 * POOL_MAX_HALF, C), F32)],
        input_output_aliases={3: 0},
        compiler_params=_params(("arbitrary",)),
        name="pool_mixer",
    )(p, w_bd, scale, out)


def _softmax_pv(s, v):
    m = jnp.max(s, axis=-1, keepdims=True)
    e = jnp.exp2(s - m)
    l = jnp.sum(e, axis=-1, keepdims=True)
    return _dot(e.astype(BF16), v) * (1.0 / l)


def _diff_kernel(*refs, Ts, Lc, lam_init, nq, n_units):
    if Lc:
        lam_ref, sub_ref, q_ref, ks_ref, vs_ref, kc_ref, vc_ref, prev_ref, o_ref, kb, vt, s_a, s_b, m_a, m_b = refs
    else:
        lam_ref, sub_ref, q_ref, ks_ref, vs_ref, prev_ref, o_ref, kb, vt, s_a, s_b, m_a, m_b = refs
    del prev_ref
    t = pl.program_id(0)
    S = Ts + Lc
    dv = DIFF_V_DIM
    head_new = jnp.minimum(t, n_units - 1) // nq
    slot_new = head_new % 2
    slot_old = (jnp.maximum(t - 1, 0) // nq) % 2

    @pl.when((t % nq == 0) & (t < n_units))
    def _():
        kb[0:Ts, :] = ks_ref[...].astype(BF16)
        vt[slot_new, 0:dv, 0:Ts] = vs_ref[...].astype(F32).T.astype(BF16)
        if Lc:
            kb[Ts:S, :] = kc_ref[...].astype(BF16)
            vt[slot_new, 0:dv, Ts:S] = vc_ref[...].T.astype(BF16)
        vt[slot_new, dv:, :] = jnp.ones((vt.shape[1] - dv, S), BF16)

    @pl.when(t == 0)
    def _():
        s_b[...] = jnp.zeros(s_b.shape, F32)
        m_b[...] = jnp.zeros(m_b.shape, F32)

    lm = lam_ref[...]
    lam = (jnp.exp(jnp.sum(lm[0:1, :] * lm[1:2, :], axis=-1, keepdims=True))
           - jnp.exp(jnp.sum(lm[2:3, :] * lm[3:4, :], axis=-1, keepdims=True)) + lam_init)

    def stage(s_new, m_new, s_old, m_old):
        q = q_ref[...].astype(F32) * (DIFF_QK_DIM ** -0.5 * LOG2E)
        tq = q.shape[0]
        part = lax.broadcasted_iota(I32, q.shape, 1) // DIFF_QK_DIM
        qm = [jnp.where(part == u, q, 0.0).astype(BF16) for u in range(2)]
        top = [m_old[u][0:1, :] for u in range(2)]
        acc = [jnp.zeros((vt.shape[1], tq), F32) for _ in range(2)]
        run = [jnp.full((8, tq), -jnp.inf, F32) for _ in range(2)]
        for c0 in range(0, S, DIFF_KEYS):
            keys = slice(c0, c0 + DIFF_KEYS)
            for u in range(2):
                sc = _dot_nt(kb[keys, :], qm[u])
                s_new[u, keys, :] = sc
                for j in range(0, DIFF_KEYS, 8):
                    run[u] = jnp.maximum(run[u], sc[j:j + 8, :])
                e = jnp.exp2(s_old[u, keys, :] - top[u])
                acc[u] = acc[u] + _dot(vt[slot_old, :, keys], e.astype(BF16))
        for u in range(2):
            m_new[u] = jnp.broadcast_to(jnp.max(run[u], axis=0, keepdims=True), m_new.shape[1:])
        o = (acc[0][0:dv, :] * (1.0 / acc[0][dv:dv + 1, :])
             - lam * (acc[1][0:dv, :] * (1.0 / acc[1][dv:dv + 1, :])))
        o_ref[...] = (_rms(o.T) * sub_ref[...] * (1.0 - lam_init)).astype(o_ref.dtype)

    @pl.when(t % 2 == 0)
    def _():
        stage(s_a, m_a, s_b, m_b)

    @pl.when(t % 2 == 1)
    def _():
        stage(s_b, m_b, s_a, m_a)


def _diff_ctx_kernel(lam_ref, sub_ref, q_ref, k_ref, v_ref, prev_ref, o_ref, *, lam_init, heads):
    del prev_ref
    lm = lam_ref[...]
    lam = (jnp.exp(jnp.sum(lm[0:1, :] * lm[1:2, :], axis=-1, keepdims=True))
           - jnp.exp(jnp.sum(lm[2:3, :] * lm[3:4, :], axis=-1, keepdims=True)) + lam_init)
    q = q_ref[...].astype(F32) * (DIFF_QK_DIM ** -0.5 * LOG2E)
    part = lax.broadcasted_iota(I32, q.shape, 1) // DIFF_QK_DIM
    k = k_ref[...].astype(BF16)
    v = v_ref[...].astype(BF16)
    for hh in range(heads):
        o1 = _softmax_pv(_dot_nt(jnp.where(part == 2 * hh, q, 0.0).astype(BF16), k), v)
        o2 = _softmax_pv(_dot_nt(jnp.where(part == 2 * hh + 1, q, 0.0).astype(BF16), k), v)
        cols = slice(hh * DIFF_V_DIM, (hh + 1) * DIFF_V_DIM)
        o = o1[:, cols] - lam * o2[:, cols]
        o_ref[:, cols] = (_rms(o) * sub_ref[...] * (1.0 - lam_init)).astype(o_ref.dtype)


def _diff_attention_ctx(p, lam_l, subln, out, *, n_seq, T, lam_init):
    heads = 2
    W = heads * DIFF_V_DIM
    spec = lambda col0: pl.BlockSpec((T, W), lambda b, h: (b, col0 // W + h))
    return pl.pallas_call(
        functools.partial(_diff_ctx_kernel, lam_init=lam_init, heads=heads),
        grid=(n_seq, DIFF_HEADS // heads),
        in_specs=[pl.BlockSpec((4, DIFF_QK_DIM), lambda b, h: (0, 0)),
                  pl.BlockSpec((1, DIFF_V_DIM), lambda b, h: (0, 0)),
                  spec(COL_DQ), spec(COL_DK), spec(COL_DV),
                  pl.BlockSpec(memory_space=pl.ANY)],
        out_specs=pl.BlockSpec((T, W), lambda b, h: (b, h)),
        out_shape=jax.ShapeDtypeStruct(out.shape, out.dtype),
        input_output_aliases={5: 0},
        compiler_params=_params(("arbitrary", "arbitrary")),
        name="diff_attention_ctx",
    )(lam_l, subln, p, p, p, out)


def _diff_attention(p, lam_l, subln, out, *, row0, n_seq, T, tq, lam_init, cache_k=None, cache_v=None, l=0):
    W = DIFF_V_DIM
    assert 2 * DIFF_QK_DIM == W == LANES
    Lc = 0 if cache_k is None else cache_k.shape[2]
    S = T + Lc
    assert S % DIFF_KEYS == 0
    nq = T // tq
    qb0 = row0 // tq
    sb0 = row0 // T
    n_units = n_seq * DIFF_HEADS * nq

    def unit(t):
        return t // (DIFF_HEADS * nq), (t // nq) % DIFF_HEADS, t % nq

    def scored(t, col0):
        b, h, i = unit(jnp.minimum(t, n_units - 1))
        return b, h, i, col0 // W + h

    in_specs = [pl.BlockSpec((4, DIFF_QK_DIM), lambda t: (0, 0)),
                pl.BlockSpec((1, W), lambda t: (0, 0)),
                pl.BlockSpec((tq, W), lambda t: (qb0 + scored(t, 0)[0] * nq + scored(t, 0)[2], scored(t, COL_DQ)[3])),
                pl.BlockSpec((T, W), lambda t: (sb0 + scored(t, 0)[0], scored(t, COL_DK)[3])),
                pl.BlockSpec((T, W), lambda t: (sb0 + scored(t, 0)[0], scored(t, COL_DV)[3]))]
    args = [lam_l, subln, p, p, p]
    if Lc:
        in_specs += [pl.BlockSpec((None, None, Lc, W), lambda t: (scored(t, 0)[0], l, 0, scored(t, 0)[1])),
                     pl.BlockSpec((None, None, Lc, W), lambda t: (scored(t, 0)[0], l, 0, scored(t, 0)[1]))]
        args += [cache_k, cache_v]
    in_specs.append(pl.BlockSpec(memory_space=pl.ANY))
    args.append(out)
    scores = pltpu.VMEM((2, S, tq), F32)
    row_max = pltpu.VMEM((2, 8, tq), F32)
    vt_rows = W + 16

    def finished(t):
        b, h, i = unit(jnp.maximum(t - 1, 0))
        return qb0 + b * nq + i, h

    return pl.pallas_call(
        functools.partial(_diff_kernel, Ts=T, Lc=Lc, lam_init=lam_init, nq=nq, n_units=n_units),
        grid=(n_units + 1,),
        in_specs=in_specs,
        out_specs=pl.BlockSpec((tq, W), finished),
        out_shape=jax.ShapeDtypeStruct(out.shape, out.dtype),
        scratch_shapes=[pltpu.VMEM((S, W), BF16), pltpu.VMEM((2, vt_rows, S), BF16),
                        scores, scores, row_max, row_max],
        input_output_aliases={len(args) - 1: 0},
        compiler_params=_params(("arbitrary",)),
        name="diff_attention",
    )(*args)


def _na_heads(q, score_fn, pv_fn):
    lane = lax.broadcasted_iota(I32, q.shape, 1) // NA_HEAD_DIM
    out = jnp.zeros(q.shape, F32)
    for h in range(NA_HEADS):
        qh = jnp.where(lane == h, q, 0.0).astype(BF16)
        out = jnp.where(lane == h, pv_fn(score_fn(qh, h)), out)
    return out


def _dense_kernel(q_ref, k_ref, v_ref, prev_ref, o_ref):
    del prev_ref
    q = q_ref[...].astype(F32) * (NA_HEAD_DIM ** -0.5 * LOG2E)
    k = k_ref[...].astype(BF16)
    v = v_ref[...].astype(BF16)
    o_ref[...] = _na_heads(q, lambda qh, h: _dot_nt(qh, k), lambda s: _softmax_pv(s, v)).astype(o_ref.dtype)


def _dense_attention(p, out, *, n_seq, T):
    C = NA_HEADS * NA_HEAD_DIM
    return pl.pallas_call(
        _dense_kernel,
        grid=(n_seq,),
        in_specs=[pl.BlockSpec((T, C), lambda b: (b, 7)),
                  pl.BlockSpec((T, C), lambda b: (b, 8)),
                  pl.BlockSpec((T, C), lambda b: (b, 9)),
                  pl.BlockSpec(memory_space=pl.ANY)],
        out_specs=pl.BlockSpec((T, C), lambda b: (b, 0)),
        out_shape=jax.ShapeDtypeStruct(out.shape, out.dtype),
        input_output_aliases={3: 0},
        compiler_params=_params(("arbitrary",)),
        name="dense_attention",
    )(p, p, p, out)


def _na_kernel(q_ref, ks_ref, vs_ref, kc_ref, vc_ref, bias_ref, prev_ref, o_ref, *, rows):
    del prev_ref
    r0 = pl.program_id(1) * NA_Q_ROWS
    bs = jnp.clip(r0 - NA_WIN_H // 2, 0, rows - NA_BAND)
    start = pl.multiple_of(bs * GRID_W, GRID_W)
    nb = NA_BAND * GRID_W
    kb = ks_ref[pl.ds(start, nb), :].astype(BF16)
    vb = vs_ref[pl.ds(start, nb), :].astype(BF16)
    kc = kc_ref[...].astype(BF16)
    vc = vc_ref[...].astype(BF16)
    q = q_ref[...].astype(F32) * (NA_HEAD_DIM ** -0.5 * LOG2E)

    def scores(qh, h):
        return _dot_nt(qh, kb) + bias_ref[h], _dot_nt(qh, kc)

    def pv(s):
        s_loc, s_ctx = s
        m = jnp.maximum(jnp.max(s_loc, axis=-1, keepdims=True), jnp.max(s_ctx, axis=-1, keepdims=True))
        e_loc = jnp.exp2(s_loc - m)
        e_ctx = jnp.exp2(s_ctx - m)
        l = jnp.sum(e_loc, axis=-1, keepdims=True) + jnp.sum(e_ctx, axis=-1, keepdims=True)
        return (_dot(e_loc.astype(BF16), vb) + _dot(e_ctx.astype(BF16), vc)) * (1.0 / l)

    o_ref[...] = _na_heads(q, scores, pv).astype(o_ref.dtype)


def _na_bias_table(rpb_l, rows):
    n_ro, n_co = 2 * NA_WIN_H - 1, 2 * NA_WIN_W - 1
    c = np.arange(GRID_W)[:, None]
    kc = np.arange(GRID_W)[None, :]
    cs = np.clip(c - NA_WIN_W // 2, 0, GRID_W - NA_WIN_W)
    col_ok = (kc >= cs) & (kc < cs + NA_WIN_W)
    co = kc - c + (NA_WIN_W - 1)
    pick = ((np.arange(n_co)[:, None, None] == co[None]) & col_ok[None]).astype(np.float32)
    toep = jnp.dot(rpb_l.reshape(NA_HEADS * n_ro, n_co).astype(F32), jnp.asarray(pick.reshape(n_co, -1)),
                   precision=lax.Precision.HIGHEST).reshape(NA_HEADS, n_ro, GRID_W, GRID_W) * LOG2E
    toep = jnp.where(jnp.asarray(col_ok)[None, None], toep, MASKED)
    toep = jnp.concatenate([toep, jnp.full((NA_HEADS, 1, GRID_W, GRID_W), MASKED, F32)], axis=1)
    blk = np.full((3, NA_Q_ROWS, NA_BAND), n_ro, np.int32)
    for v, r0 in enumerate((0, NA_Q_ROWS, rows - NA_Q_ROWS)):
        bs = int(np.clip(r0 - NA_WIN_H // 2, 0, rows - NA_BAND))
        for j in range(NA_Q_ROWS):
            rs = int(np.clip(r0 + j - NA_WIN_H // 2, 0, rows - NA_WIN_H))
            for i in range(NA_BAND):
                if rs <= bs + i < rs + NA_WIN_H:
                    blk[v, j, i] = bs + i - (r0 + j) + (NA_WIN_H - 1)
    tab = toep[:, blk]
    tab = jnp.transpose(tab, (1, 0, 2, 4, 3, 5))
    return tab.reshape(3, NA_HEADS, NA_Q_ROWS * GRID_W, NA_BAND * GRID_W)


def _neighbourhood_attention(p, cache_k, cache_v, bias, out, *, row0, n_seq, T, l):
    C = NA_HEADS * NA_HEAD_DIM
    rows = T // GRID_W
    tq = NA_Q_ROWS * GRID_W
    nq = T // tq
    Lc = cache_k.shape[2]
    qb0 = row0 // tq
    sb0 = row0 // T

    def variant(i):
        r0 = i * NA_Q_ROWS
        return (r0 - jnp.clip(r0 - NA_WIN_H // 2, 0, rows - NA_BAND)) // NA_Q_ROWS

    return pl.pallas_call(
        functools.partial(_na_kernel, rows=rows),
        grid=(n_seq, nq),
        in_specs=[pl.BlockSpec((tq, C), lambda b, i: (qb0 + b * nq + i, 7)),
                  pl.BlockSpec((T, C), lambda b, i: (sb0 + b, 8)),
                  pl.BlockSpec((T, C), lambda b, i: (sb0 + b, 9)),
                  pl.BlockSpec((None, None, Lc, C), lambda b, i: (b, l, 0, 0)),
                  pl.BlockSpec((None, None, Lc, C), lambda b, i: (b, l, 0, 0)),
                  pl.BlockSpec((None, NA_HEADS, tq, NA_BAND * GRID_W), lambda b, i: (variant(i), 0, 0, 0)),
                  pl.BlockSpec(memory_space=pl.ANY)],
        out_specs=pl.BlockSpec((tq, C), lambda b, i: (qb0 + b * nq + i, 0)),
        out_shape=jax.ShapeDtypeStruct(out.shape, out.dtype),
        input_output_aliases={6: 0},
        compiler_params=_params(("arbitrary", "arbitrary")),
        name="neighbourhood_attention",
    )(p, p, p, cache_k, cache_v, bias, out)


def _merge_kernel(pool_ref, dn_ref, na_ref, *refs, D, ctx_blocks):
    mod_ref, g1_ref, g2_ref, w_ref, rw_ref, rb_ref, x1_ref, h2_ref, route_ref, cnt_ref = refs[-10:]
    x_rows = _token_rows(refs[:-10], ctx_blocks)
    parts = [slice(j * TOK_CHUNK, (j + 1) * TOK_CHUNK) for j in range(x1_ref.shape[0] // TOK_CHUNK)]
    c0 = pool_ref.shape[1]
    c1 = c0 + dn_ref.shape[1]
    mix = [_dot(pool_ref[p, :].astype(BF16), w_ref[0:c0, :])
           + _dot(dn_ref[p, :].astype(BF16), w_ref[c0:c1, :])
           + _dot(na_ref[p, :].astype(BF16), w_ref[c1:, :]) for p in parts]
    x1 = [x_rows(p) + mod_ref[:, 2 * D:3 * D] * (_rms(m) * g1_ref[...]) for p, m in zip(parts, mix)]
    for p, v in zip(parts, x1):
        x1_ref[p, :] = v
    h2 = [(_rms(v) * g2_ref[...]) * (1.0 + mod_ref[:, 4 * D:5 * D]) + mod_ref[:, 3 * D:4 * D] for v in x1]
    for p, v in zip(parts, h2):
        h2_ref[p, :] = v.astype(h2_ref.dtype)

    rw_hi, rw_lo = _split(rw_ref[...])
    h2_split = [_split(v) for v in h2]
    logits = [_dot(hi, rw_hi) + _dot(lo, rw_hi) + _dot(hi, rw_lo) + rb_ref[...] for hi, lo in h2_split]
    lane = lax.broadcasted_iota(I32, (TOK_CHUNK, LANES), 1)
    lane_f = lane.astype(F32)
    work = logits
    vals, hots = [], []
    for _ in range(TOP_K):
        mx = [jnp.max(w, axis=-1, keepdims=True) for w in work]
        idx = [jnp.min(jnp.where(w == m, lane_f, float(LANES)), axis=-1, keepdims=True)
               for w, m in zip(work, mx)]
        hot = [lane_f == i for i in idx]
        vals.append(mx)
        hots.append(hot)
        work = [jnp.where(h, -jnp.inf, w) for h, w in zip(hot, work)]

    r = lax.broadcasted_iota(I32, (TOK_CHUNK, TOK_CHUNK), 0)
    c = lax.broadcasted_iota(I32, (TOK_CHUNK, TOK_CHUNK), 1)
    before = jnp.where(c < r, 1.0, 0.0).astype(BF16)
    er = lax.broadcasted_iota(I32, (LANES, LANES), 0)
    ec = lax.broadcasted_iota(I32, (LANES, LANES), 1)
    earlier = jnp.where(er < ec, 1.0, 0.0).astype(BF16)
    for j, p in enumerate(parts):
        hot_j = [hots[k][j] for k in range(TOP_K)]
        es = [jnp.exp(vals[k][j] - vals[0][j]) for k in range(TOP_K)]
        inv = 1.0 / (es[0] + es[1] + es[2] + es[3])
        sel = jnp.zeros((TOK_CHUNK, LANES), F32)
        for hot in hot_j:
            sel = jnp.where(hot, 1.0, sel)
        rank = _dot(before, sel.astype(BF16))
        cnt = jnp.sum(sel, axis=0, keepdims=True)
        cnt_ref[j] = cnt
        run = jnp.floor((cnt + (RUN_ALIGN - 1)) * (1.0 / RUN_ALIGN)) * RUN_ALIGN
        run_start = _dot(jnp.broadcast_to(run, (8, LANES)).astype(BF16), earlier)[0:1, :]
        pos = rank + run_start
        route = jnp.zeros((TOK_CHUNK, LANES), F32)
        for k in range(TOP_K):
            e_k = jnp.sum(jnp.where(hot_j[k], lane_f, 0.0), axis=-1, keepdims=True)
            p_k = jnp.sum(jnp.where(hot_j[k], pos, 0.0), axis=-1, keepdims=True)
            route = jnp.where(lane == k, e_k, route)
            route = jnp.where(lane == TOP_K + k, p_k, route)
            route = jnp.where(lane == 2 * TOP_K + k, es[k] * inv, route)
        route_ref[p, :] = route


def _merge_route(pool_o, dn, na_o, x, mod4, g1, g2, w_out_bf, rw_pad, rb_pad, *, tm, ctx_blocks,
                 blocks_per_seq, l):
    xs, x_specs = _token_specs(x, tm, ctx_blocks)
    N, D = sum(a.shape[0] for a in xs), xs[0].shape[1]
    per_step = tm // TOK_CHUNK

    def grp(i):
        return jnp.where(i < ctx_blocks, 0, 1 + (i - ctx_blocks) // blocks_per_seq)

    row = lambda w: pl.BlockSpec((tm, w), lambda i: (i, 0))
    full = lambda a: pl.BlockSpec(a.shape, lambda i: (0,) * a.ndim)
    return pl.pallas_call(
        functools.partial(_merge_kernel, D=D, ctx_blocks=ctx_blocks),
        grid=(N // tm,),
        in_specs=[row(pool_o.shape[1]), row(dn.shape[1]), row(na_o.shape[1])] + x_specs + [
                  pl.BlockSpec((None, None, 1, N_MOD * D), lambda i: (l, grp(i), 0, 0)),
                  full(g1), full(g2), full(w_out_bf), full(rw_pad), full(rb_pad)],
        out_specs=[row(D), row(D), row(LANES), pl.BlockSpec((per_step, 1, LANES), lambda i: (i, 0, 0))],
        out_shape=[jax.ShapeDtypeStruct((N, D), F32), jax.ShapeDtypeStruct((N, D), BF16),
                   jax.ShapeDtypeStruct((N, LANES), F32), jax.ShapeDtypeStruct((N // TOK_CHUNK, 1, LANES), F32)],
        compiler_params=_params(("arbitrary",)),
        name="merge_route",
    )(pool_o, dn, na_o, *xs, mod4, g1, g2, w_out_bf, rw_pad, rb_pad)


def _pack_pairs(x):
    C = x.shape[1] // 2
    bits = lax.bitcast_convert_type(x.astype(BF16).astype(F32), jnp.uint32)
    return bits[:, C:] | (bits[:, :C] >> 16)


def _unpack_pairs(w):
    lo = lax.bitcast_convert_type(w << 16, F32)
    hi = lax.bitcast_convert_type(w & jnp.uint32(0xFFFF0000), F32)
    return jnp.concatenate([lo, hi], axis=1).astype(BF16)


def _run_sizes(lo, hi):
    return [1 << k for k in range(hi.bit_length() - 1, lo.bit_length() - 2, -1)]


def _run_copies(n, src_at, dst_at, sem, wait, sizes):
    for size in sizes:
        @pl.when((n & size) != 0)
        def _(size=size):
            off = n & -(2 * size)
            cp = pltpu.make_async_copy(src_at(off, size), dst_at(off, size), sem)
            cp.wait() if wait else cp.start()


def _block_runs(step, n_exp, run_ref, src_ref, dst_ref, large_ref, hbm_ref, buf, sem, *, to_hbm, wait):
    def each_run(sizes):
        def body(e, carry):
            j = step * n_exp + e
            so = src_ref[j]
            do = dst_ref[j]
            in_buf = lambda o, s: buf.at[pl.ds(pl.multiple_of(so + o, RUN_ALIGN), s)]
            in_hbm = lambda o, s: hbm_ref.at[pl.ds(pl.multiple_of(do + o, RUN_ALIGN), s)]
            if to_hbm:
                _run_copies(run_ref[j], in_buf, in_hbm, sem, wait, sizes)
            else:
                _run_copies(run_ref[j], in_hbm, in_buf, sem, wait, sizes)
            return carry
        lax.fori_loop(0, n_exp, body, 0, unroll=RUN_UNROLL)

    each_run(_run_sizes(RUN_ALIGN, RUN_LARGE // 2))

    @pl.when(large_ref[step] != 0)
    def _():
        each_run(_run_sizes(RUN_LARGE, TOK_CHUNK))


def _dispatch_kernel(run_ref, src_ref, dst_ref, large_ref, tail_ref, taildst_ref, nu_ref, route_ref, h_ref, xs_ref,
                     sorted_buf, zero_buf, sems, *, n_exp):
    b = pl.program_id(0)
    nb = pl.num_programs(0)
    slot = b % 2
    runs = functools.partial(_block_runs, n_exp=n_exp, run_ref=run_ref, src_ref=src_ref, dst_ref=dst_ref,
                             large_ref=large_ref, hbm_ref=xs_ref, to_hbm=True)

    @pl.when(b >= 2)
    def _():
        runs(b - 2, buf=sorted_buf.at[slot], sem=sems.at[slot], wait=True)

    route = route_ref[...]
    col = lax.broadcasted_iota(I32, (TOK_CHUNK, sorted_buf.shape[1]), 1).astype(F32)
    place = jnp.zeros(col.shape, F32)
    for k in range(TOP_K):
        place = jnp.where(col == route[:, TOP_K + k:TOP_K + k + 1], 1.0, place)
    srt = lax.dot_general(place.astype(BF16), h_ref[...].astype(BF16), (((0,), (0,)), ((), ())),
                          preferred_element_type=F32)
    sorted_buf[slot] = _pack_pairs(srt)
    runs(b, buf=sorted_buf.at[slot], sem=sems.at[slot], wait=False)

    @pl.when(b == nb - 1)
    def _():
        @pl.when(b >= 1)
        def _():
            runs(b - 1, buf=sorted_buf.at[1 - slot], sem=sems.at[1 - slot], wait=True)
        runs(b, buf=sorted_buf.at[slot], sem=sems.at[slot], wait=True)

        zero_buf[...] = jnp.zeros(zero_buf.shape, zero_buf.dtype)
        sem = sems.at[0]

        def each_tail(wait):
            def body(e, carry):
                do = taildst_ref[e]
                _run_copies(tail_ref[e],
                            lambda o, s: zero_buf.at[pl.ds(0, s)],
                            lambda o, s: xs_ref.at[pl.ds(pl.multiple_of(do + o, RUN_ALIGN), s)], sem, wait,
                            _run_sizes(RUN_ALIGN, MOE_ROWS // 2))
                return carry
            lax.fori_loop(0, n_exp, body, 0)

        def spare_block(wait):
            def body(i, carry):
                cp = pltpu.make_async_copy(zero_buf.at[pl.ds(0, MOE_ROWS)],
                                           xs_ref.at[pl.ds(pl.multiple_of(i * MOE_ROWS, MOE_ROWS), MOE_ROWS)], sem)
                cp.wait() if wait else cp.start()
                return carry
            lax.fori_loop(nu_ref[0], xs_ref.shape[0] // MOE_ROWS, body, 0)

        each_tail(False)
        spare_block(False)
        each_tail(True)
        spare_block(True)


def _dispatch(plan, n_used, route, h2, n_rows, n_exp):
    N, D = h2.shape
    C = D // 2
    grid_spec = pltpu.PrefetchScalarGridSpec(
        num_scalar_prefetch=7,
        grid=(N // TOK_CHUNK,),
        in_specs=[pl.BlockSpec((TOK_CHUNK, LANES), lambda i, *_: (i, 0)),
                  pl.BlockSpec((TOK_CHUNK, D), lambda i, *_: (i, 0))],
        out_specs=pl.BlockSpec(memory_space=pl.ANY),
        scratch_shapes=[pltpu.VMEM((2, _sorted_rows(n_exp), C), jnp.uint32),
                        pltpu.VMEM((max(TOK_CHUNK, MOE_ROWS), C), jnp.uint32),
                        pltpu.SemaphoreType.DMA((2,))],
    )
    return pl.pallas_call(
        functools.partial(_dispatch_kernel, n_exp=n_exp),
        grid_spec=grid_spec,
        out_shape=jax.ShapeDtypeStruct((n_rows, C), jnp.uint32),
        compiler_params=_params(("arbitrary",)),
        name="moe_dispatch",
    )(plan["run"], plan["src"], plan["dst"], plan["large"], plan["tail"], plan["tail_dst"], n_used, route, h2)


def _expert_kernel(be_ref, nu_ref, rows_ref, xs_ref, w1_ref, b1_ref, w2_ref, b2_ref, o_ref, w1b, w2b, *, F):
    del nu_ref
    i = pl.program_id(0)
    e = be_ref[i]
    prev = be_ref[jnp.maximum(i - 1, 0)]
    rows = rows_ref[i]
    half = MOE_ROWS // 2

    first = (i == 0) | (e != prev)

    def ffn(xw, fresh):
        if fresh:
            w1 = w1_ref[...].astype(BF16)
            w2 = w2_ref[...].astype(BF16)
            w1b[...] = w1
            w2b[...] = w2
        else:
            w1 = w1b[...]
            w2 = w2b[...]
        hh = _dot(_unpack_pairs(xw), w1) + b1_ref[...]
        g = jnp.minimum(hh[:, 0:F], SWIGLU_LIMIT)
        u = jnp.clip(hh[:, F:2 * F], -SWIGLU_LIMIT, SWIGLU_LIMIT)
        a = (g * (1.0 / (1.0 + jnp.exp(-SWIGLU_ALPHA * g)))) * (u + 1.0)
        return _pack_pairs(_dot(a.astype(BF16), w2) + b2_ref[...])

    for fresh in (True, False):
        new_expert = first if fresh else jnp.logical_not(first)

        @pl.when(new_expert & (rows > half))
        def _(fresh=fresh):
            o_ref[...] = ffn(xs_ref[...], fresh)

        @pl.when(new_expert & (rows > 0) & (rows <= half))
        def _(fresh=fresh):
            o_ref[0:half, :] = ffn(xs_ref[0:half, :], fresh)
            o_ref[half:MOE_ROWS, :] = jnp.zeros((MOE_ROWS - half, o_ref.shape[1]), o_ref.dtype)

    @pl.when(rows == 0)
    def _():
        o_ref[...] = jnp.zeros(o_ref.shape, o_ref.dtype)


def _experts(block_e, n_used, block_rows, xs, w1, b1, w2, b2, *, l):
    R, C = xs.shape
    L, E, D, F2 = w1.shape
    F = F2 // 2
    nblk = R // MOE_ROWS
    grid_spec = pltpu.PrefetchScalarGridSpec(
        num_scalar_prefetch=3,
        grid=(nblk,),
        in_specs=[pl.BlockSpec((MOE_ROWS, C), lambda i, be, nu, br: (jnp.minimum(i, nu[0] - 1), 0)),
                  pl.BlockSpec((None, None, D, F2), lambda i, be, nu, br: (l, be[i], 0, 0)),
                  pl.BlockSpec((None, None, 1, F2), lambda i, be, nu, br: (l, be[i], 0, 0)),
                  pl.BlockSpec((None, None, F, D), lambda i, be, nu, br: (l, be[i], 0, 0)),
                  pl.BlockSpec((None, None, 1, D), lambda i, be, nu, br: (l, be[i], 0, 0))],
        out_specs=pl.BlockSpec((MOE_ROWS, C), lambda i, be, nu, br: (i, 0)),
        scratch_shapes=[pltpu.VMEM((D, F2), BF16), pltpu.VMEM((F, D), BF16)],
    )
    return pl.pallas_call(
        functools.partial(_expert_kernel, F=F),
        grid_spec=grid_spec,
        out_shape=jax.ShapeDtypeStruct((R, C), jnp.uint32),
        compiler_params=_params(("arbitrary",)),
        name="moe_experts",
    )(block_e, n_used, block_rows, xs, w1, b1.reshape(L, E, 1, F2), w2, b2.reshape(L, E, 1, D))


def _combine_kernel(run_ref, src_ref, dst_ref, large_ref, yb_ref, route_ref, x1_ref, mod_ref, g_ref, *rest, D,
                    n_exp, ctx_blocks, split):
    if split:
        ctx_ref, lat_ref, sorted_buf, sems = rest
    else:
        o_ref, sorted_buf, sems = rest
    b = pl.program_id(0)
    nb = pl.num_programs(0)
    slot = b % 2
    runs = functools.partial(_block_runs, n_exp=n_exp, run_ref=run_ref, src_ref=src_ref, dst_ref=dst_ref,
                             large_ref=large_ref, hbm_ref=yb_ref, to_hbm=False)

    @pl.when(b == 0)
    def _():
        sorted_buf[...] = jnp.zeros(sorted_buf.shape, sorted_buf.dtype)
        runs(b, buf=sorted_buf.at[slot], sem=sems.at[slot], wait=False)

    @pl.when(b + 1 < nb)
    def _():
        runs(b + 1, buf=sorted_buf.at[1 - slot], sem=sems.at[1 - slot], wait=False)

    runs(b, buf=sorted_buf.at[slot], sem=sems.at[slot], wait=True)
    route = route_ref[...]
    col = lax.broadcasted_iota(I32, (TOK_CHUNK, sorted_buf.shape[1]), 1).astype(F32)
    gate = jnp.zeros(col.shape, F32)
    for k in range(TOP_K):
        gate = jnp.where(col == route[:, TOP_K + k:TOP_K + k + 1], route[:, 2 * TOP_K + k:2 * TOP_K + k + 1], gate)
    g_hi, g_lo = _split(gate)
    yb = _unpack_pairs(sorted_buf[slot])
    y = _dot(g_hi, yb) + _dot(g_lo, yb)
    x2 = x1_ref[...] + mod_ref[:, 5 * D:6 * D] * (_rms(y) * g_ref[...])
    if split:
        @pl.when(b < ctx_blocks)
        def _():
            ctx_ref[...] = x2

        @pl.when(b >= ctx_blocks)
        def _():
            lat_ref[...] = x2
    else:
        o_ref[...] = x2


def _combine(plan, yb, route, x1, mod4, g3, *, n_exp, ctx_blocks, blocks_per_seq, l, split):
    N, D = x1.shape
    row = lambda f: pl.BlockSpec((TOK_CHUNK, D), lambda i, *_: (f(i), 0))
    if split:
        out_specs = [row(lambda i: jnp.minimum(i, ctx_blocks - 1)), row(lambda i: jnp.maximum(i - ctx_blocks, 0))]
        out_shape = [jax.ShapeDtypeStruct((ctx_blocks * TOK_CHUNK, D), F32),
                     jax.ShapeDtypeStruct((N - ctx_blocks * TOK_CHUNK, D), F32)]
    else:
        out_specs = row(lambda i: i)
        out_shape = jax.ShapeDtypeStruct((N, D), F32)

    def grp(i):
        return jnp.where(i < ctx_blocks, 0, 1 + (i - ctx_blocks) // blocks_per_seq)

    grid_spec = pltpu.PrefetchScalarGridSpec(
        num_scalar_prefetch=4,
        grid=(N // TOK_CHUNK,),
        in_specs=[pl.BlockSpec(memory_space=pl.ANY),
                  pl.BlockSpec((TOK_CHUNK, LANES), lambda i, *_: (i, 0)),
                  pl.BlockSpec((TOK_CHUNK, D), lambda i, *_: (i, 0)),
                  pl.BlockSpec((None, None, 1, N_MOD * D), lambda i, *_: (l, grp(i), 0, 0)),
                  pl.BlockSpec((1, D), lambda i, *_: (0, 0))],
        out_specs=out_specs,
        scratch_shapes=[pltpu.VMEM((2, _sorted_rows(n_exp), yb.shape[1]), jnp.uint32),
                        pltpu.SemaphoreType.DMA((2,))],
    )
    return pl.pallas_call(
        functools.partial(_combine_kernel, D=D, n_exp=n_exp, ctx_blocks=ctx_blocks, split=split),
        grid_spec=grid_spec,
        out_shape=out_shape,
        compiler_params=_params(("arbitrary",)),
        name="moe_combine",
    )(plan["run"], plan["src"], plan["dst"], plan["large"], yb, route, x1, mod4, g3)


def _rope_tables(Ts, tm):
    nf = DIFF_QK_DIM // 4
    inv = ROPE_BASE ** (-jnp.arange(nf, dtype=F32) / nf)
    t = jnp.arange(Ts)
    pos = jnp.stack([(t // GRID_W).astype(F32), (t % GRID_W).astype(F32)], axis=1)
    ang = pos[:, :, None] * inv[None, None, :]
    cos = jnp.repeat(jnp.cos(ang)[:, :, None, :], 2, axis=2).reshape(Ts, DIFF_QK_DIM)
    sin = jnp.sin(ang)
    sin = jnp.stack([-sin, sin], axis=2).reshape(Ts, DIFF_QK_DIM)
    reps = LANES // DIFF_QK_DIM
    cos = jnp.concatenate([jnp.ones((tm, LANES), F32), jnp.tile(cos, (1, reps))], axis=0)
    sin = jnp.concatenate([jnp.zeros((tm, LANES), F32), jnp.tile(sin, (1, reps))], axis=0)
    return cos, sin


def _block_diag(w):
    G, a, b = w.shape
    out = jnp.zeros((G * a, G * b), w.dtype)
    for g in range(G):
        out = out.at[g * a:(g + 1) * a, g * b:(g + 1) * b].set(w[g])
    return out


def _route_plan(counts, n_exp, nblk):
    cnt = counts[:, 0, :n_exp].astype(I32)
    run = (cnt + RUN_ALIGN - 1) // RUN_ALIGN * RUN_ALIGN
    src = jnp.cumsum(run, axis=1) - run
    tot = jnp.sum(run, axis=0)
    region = (tot + MOE_ROWS - 1) // MOE_ROWS * MOE_ROWS
    region_end = jnp.cumsum(region)
    region_start = region_end - region
    dst = region_start[None, :] + jnp.cumsum(run, axis=0) - run
    n_used = (region_end[-1] // MOE_ROWS).astype(I32)
    blk = jnp.arange(nblk, dtype=I32) * MOE_ROWS
    block_e = jnp.minimum(jnp.sum((blk[:, None] >= region_end[None, :]).astype(I32), axis=1), n_exp - 1)
    last = jnp.sum(jnp.where(jnp.arange(nblk) == n_used - 1, block_e, 0))
    used = jnp.arange(nblk) < n_used
    block_rows = jnp.where(used, jnp.clip((region_start + tot)[block_e] - blk, 0, MOE_ROWS), 0).astype(I32)
    block_e = jnp.where(used, block_e, last).astype(I32)
    plan = dict(run=run.reshape(-1).astype(I32), src=src.reshape(-1).astype(I32), dst=dst.reshape(-1).astype(I32),
                large=jnp.any(run >= RUN_LARGE, axis=1).astype(I32),
                tail=(region - tot).astype(I32), tail_dst=(region_start + tot).astype(I32))
    return plan, block_e, n_used.reshape(1), block_rows


def kernel(x_prompt, x_sample, cache_diff_k, cache_diff_v, cache_na_k, cache_na_v, c, c_ctx, w_ada, b_ada,
           norm_gain, w_in, w_out, pool_w, pool_scale, diff_lambda, diff_subln, na_rpb, router_w, router_b,
           moe_w1, moe_b1, moe_w2, moe_b2):
    Bp, Tp, D = x_prompt.shape
    Bs, Ts, _ = x_sample.shape
    L = w_ada.shape[0]
    E = router_w.shape[-1]
    Np, Ns = Bp * Tp, Bs * Ts
    N = Np + Ns
    tm = TOK_CHUNK
    assert Np % Ts == 0 or Bs == 0, "context rows must be a whole number of latent-sequence blocks"
    assert Np % tm == 0 and Ts % tm == 0 and Ts % (NA_Q_ROWS * GRID_W) == 0
    assert Ts // GRID_W >= NA_BAND + NA_Q_ROWS
    ctx_blocks, blocks_per_seq = Np // tm, Ts // tm

    G = 16
    cvec = jnp.zeros((G, D), F32).at[0].set(c_ctx).at[1:1 + Bs].set(c)
    mod4 = _modulation(cvec, w_ada, b_ada).reshape(L, G, 1, N_MOD * D)
    tp = PROJ_ROWS
    assert Np % tp == 0 and Ts % tp == 0 and Np % MERGE_ROWS == 0 and Ts % MERGE_ROWS == 0
    cos_t, sin_t = _rope_tables(Ts, tp)
    w_in_bf = w_in.astype(BF16)
    w_out_bf = w_out.astype(BF16)
    rw_pad = jnp.zeros((L, D, LANES), F32).at[:, :, :E].set(router_w)
    rb_pad = jnp.full((L, 1, LANES), MASKED, F32).at[:, 0, :E].set(router_b)
    ck = cache_diff_k.reshape(Bs, L, -1, DIFF_HEADS * 2 * DIFF_QK_DIM)
    cv = cache_diff_v.reshape(Bs, L, -1, DIFF_HEADS * DIFF_V_DIM)
    nk = cache_na_k.reshape(Bs, L, -1, NA_HEADS * NA_HEAD_DIM)
    nv = cache_na_v.reshape(Bs, L, -1, NA_HEADS * NA_HEAD_DIM)
    nb = N // TOK_CHUNK
    nblk = -(-(N * TOP_K + nb * E * (RUN_ALIGN - 1) + E * (MOE_ROWS - 1)) // MOE_ROWS)
    blocks = dict(ctx_blocks=ctx_blocks, blocks_per_seq=blocks_per_seq)

    x = (x_prompt.reshape(Np, D), x_sample.reshape(Ns, D))
    new_dk, new_dv, new_nk, new_nv = [], [], [], []
    for l in range(L):
        lam_init = 0.8 - 0.6 * math.exp(-0.3 * l)
        g = norm_gain[l]
        p, dk_l, dv_l, nk_l, nv_l = _in_projection(x, mod4, g[0:1], w_in_bf[l], cos_t, sin_t, tm=tp, l=l,
                                                   ctx_blocks=Np // tp, blocks_per_seq=Ts // tp)

        pw = _block_diag(pool_w[l]).astype(BF16)
        ps = pool_scale[l].reshape(1, -1)
        pool_o = jnp.zeros((N, pw.shape[0]), BF16)
        pool_o = _pool(p, pw, ps, row0=0, n_seq=Bp, T=Tp, out=pool_o)
        pool_o = _pool(p, pw, ps, row0=Np, n_seq=Bs, T=Ts, out=pool_o)

        sub = diff_subln[l].reshape(1, -1)
        dn = jnp.zeros((N, DIFF_HEADS * DIFF_V_DIM), BF16)
        dn = _diff_attention_ctx(p, diff_lambda[l], sub, dn, n_seq=Bp, T=Tp, lam_init=lam_init)
        dn = _diff_attention(p, diff_lambda[l], sub, dn, row0=Np, n_seq=Bs, T=Ts, tq=DIFF_Q_ROWS, lam_init=lam_init,
                             cache_k=ck, cache_v=cv, l=l)

        na_o = jnp.zeros((N, NA_HEADS * NA_HEAD_DIM), BF16)
        na_o = _dense_attention(p, na_o, n_seq=Bp, T=Tp)
        bias = _na_bias_table(na_rpb[l], Ts // GRID_W)
        na_o = _neighbourhood_attention(p, nk, nv, bias, na_o, row0=Np, n_seq=Bs, T=Ts, l=l)

        x1, h2, route, counts = _merge_route(pool_o, dn, na_o, x, mod4, g[1:2], g[2:3], w_out_bf[l],
                                             rw_pad[l], rb_pad[l], tm=MERGE_ROWS, l=l,
                                             ctx_blocks=Np // MERGE_ROWS, blocks_per_seq=Ts // MERGE_ROWS)
        plan, block_e, n_used, block_rows = _route_plan(counts, E, nblk)
        xs = _dispatch(plan, n_used, route, h2, nblk * MOE_ROWS, E)
        yb = _experts(block_e, n_used, block_rows, xs, moe_w1, moe_b1, moe_w2, moe_b2, l=l)
        x = _combine(plan, yb, route, x1, mod4, g[3:4], n_exp=E, l=l, split=l == L - 1, **blocks)

        new_dk.append(dk_l.reshape(Bp, Tp, DIFF_HEADS, 2 * DIFF_QK_DIM))
        new_dv.append(dv_l.reshape(Bp, Tp, DIFF_HEADS, DIFF_V_DIM))
        new_nk.append(nk_l.reshape(Bp, Tp, NA_HEADS, NA_HEAD_DIM))
        new_nv.append(nv_l.reshape(Bp, Tp, NA_HEADS, NA_HEAD_DIM))

    return (x[0].reshape(Bp, Tp, D), x[1].reshape(Bs, Ts, D),
            jnp.stack(new_dk, axis=1), jnp.stack(new_dv, axis=1),
            jnp.stack(new_nk, axis=1), jnp.stack(new_nv, axis=1))
```

```python
import functools
import math

import numpy as np
import jax
import jax.numpy as jnp
from jax import lax
from jax.experimental import pallas as pl
from jax.experimental.pallas import tpu as pltpu

F32 = jnp.float32
BF16 = jnp.bfloat16
I32 = jnp.int32

GRID_W = 64
POOL_GROUPS = 4
POOL_MAX_HALF = 8
DIFF_HEADS = 4
DIFF_V_DIM = 128
DIFF_QK_DIM = 64
NA_HEADS = 4
NA_HEAD_DIM = 64
NA_WIN_H = 8
NA_WIN_W = 16
NA_Q_ROWS = 4
NA_BAND = 12
TOP_K = 4
SWIGLU_ALPHA = 1.702
SWIGLU_LIMIT = 7.0
ROPE_BASE = 10000.0
NORM_EPS = 1e-6
N_MOD = 6

LOG2E = 1.4426950408889634
MASKED = -1e30
LANES = 128
SUBLANES = 8
BF16_ROWS = 16
MXU_TILE = 256
MOE_ROWS = 512
COL_DQ, COL_DK, COL_DV = 256, 768, 1280
COL_NQ, COL_NK, COL_NV = 1792, 2048, 2304
DIFF_KEYS = 256
PROJ_ROWS = 512
MERGE_ROWS = 1024
DIFF_Q_ROWS = 256
TOK_CHUNK = 256
RUN_ALIGN = SUBLANES
RUN_LARGE = 128
RUN_UNROLL = 4
VMEM_LIMIT = 56 * 1024 * 1024


def _sorted_rows(n_exp):
    rows = TOK_CHUNK * TOP_K + n_exp * (RUN_ALIGN - 1)
    return -(-rows // MXU_TILE) * MXU_TILE


def _params(sem, vmem=VMEM_LIMIT):
    return pltpu.CompilerParams(dimension_semantics=sem, vmem_limit_bytes=vmem)


def _dot(a, b):
    return jnp.dot(a, b, preferred_element_type=F32)


def _dot_nt(a, b):
    return lax.dot_general(a, b, (((1,), (1,)), ((), ())), preferred_element_type=F32)


def _split(x):
    hi = x.astype(BF16)
    return hi, (x - hi.astype(F32)).astype(BF16)


def _dot3(a, b):
    ah, al = _split(a)
    bh, bl = _split(b)
    return _dot(ah, bh) + _dot(al, bh) + _dot(ah, bl)


def _rms(x):
    return x * lax.rsqrt(jnp.mean(x * x, axis=-1, keepdims=True) + NORM_EPS)


def _mod_kernel(c_ref, w_ref, b_ref, o_ref):
    c = c_ref[...]
    a = c * (1.0 / (1.0 + jnp.exp(-c)))
    o_ref[...] = _dot3(a, w_ref[...]) + b_ref[...]


def _modulation(cvec, w_ada, b_ada):
    L, D, W = w_ada.shape
    G = cvec.shape[0]
    return pl.pallas_call(
        _mod_kernel,
        grid=(L, W // D),
        in_specs=[pl.BlockSpec((G, D), lambda l, j: (0, 0)),
                  pl.BlockSpec((None, D, D), lambda l, j: (l, 0, j)),
                  pl.BlockSpec((None, 1, D), lambda l, j: (l, 0, j))],
        out_specs=pl.BlockSpec((None, G, D), lambda l, j: (l, 0, j)),
        out_shape=jax.ShapeDtypeStruct((L, G, W), F32),
        compiler_params=_params(("arbitrary", "arbitrary")),
        name="ada_modulation",
    )(cvec, w_ada, b_ada.reshape(L, 1, W))


def _token_rows(x_refs, ctx_blocks):
    if len(x_refs) == 1:
        return lambda rows=slice(None): x_refs[0][rows, :]
    is_ctx = pl.program_id(0) < ctx_blocks
    return lambda rows=slice(None): jnp.where(is_ctx, x_refs[0][rows, :], x_refs[1][rows, :])


def _token_specs(x, tm, ctx_blocks):
    if not isinstance(x, tuple):
        return [x], [pl.BlockSpec((tm, x.shape[1]), lambda i: (i, 0))]
    D = x[0].shape[1]
    return list(x), [pl.BlockSpec((tm, D), lambda i: (jnp.minimum(i, ctx_blocks - 1), 0)),
                     pl.BlockSpec((tm, D), lambda i: (jnp.maximum(i - ctx_blocks, 0), 0))]


def _inproj_kernel(*refs, D, rope_lo, rope_hi, ctx_blocks):
    mod_ref, g_ref, w_ref, cos_ref, sin_ref, o_ref, dk_ref, dv_ref, nk_ref, nv_ref = refs[-10:]
    h = _rms(_token_rows(refs[:-10], ctx_blocks)()) * g_ref[...]
    h = h * (1.0 + mod_ref[:, D:2 * D]) + mod_ref[:, 0:D]
    p = _dot(h.astype(BF16), w_ref[...])
    W = p.shape[1]
    o_ref[:, 0:rope_lo] = p[:, 0:rope_lo].astype(o_ref.dtype)
    o_ref[:, rope_hi:W] = p[:, rope_hi:W].astype(o_ref.dtype)
    cos = cos_ref[...]
    sin = sin_ref[...]
    lane = lax.broadcasted_iota(I32, cos.shape, 1)
    rot = DIFF_QK_DIM // 4
    first = (lane % (2 * rot)) < rot
    for c0 in range(rope_lo, rope_hi, LANES):
        xc = p[:, c0:c0 + LANES]
        partner = jnp.where(first, pltpu.roll(xc, LANES - rot, 1), pltpu.roll(xc, rot, 1))
        o_ref[:, c0:c0 + LANES] = (xc * cos + partner * sin).astype(o_ref.dtype)

    @pl.when(pl.program_id(0) < ctx_blocks)
    def _():
        dk_ref[...] = p[:, COL_DK:COL_DV]
        dv_ref[...] = p[:, COL_DV:COL_NQ]
        nk_ref[...] = p[:, COL_NK:COL_NV]
        nv_ref[...] = p[:, COL_NV:W]


def _in_projection(x, mod4, gain, w_bf, cos_t, sin_t, *, tm, ctx_blocks, blocks_per_seq, l):
    xs, x_specs = _token_specs(x, tm, ctx_blocks)
    N, D = sum(a.shape[0] for a in xs), xs[0].shape[1]
    W = w_bf.shape[1]

    def grp(i):
        return jnp.where(i < ctx_blocks, 0, 1 + (i - ctx_blocks) // blocks_per_seq)

    def rope_blk(i):
        return jnp.where(i < ctx_blocks, 0, 1 + (i - ctx_blocks) % blocks_per_seq)

    kern = functools.partial(_inproj_kernel, D=D, rope_lo=COL_DQ, rope_hi=COL_DV, ctx_blocks=ctx_blocks)
    ctx_rows = ctx_blocks * tm
    ctx_out = lambda w: pl.BlockSpec((tm, w), lambda i: (jnp.minimum(i, ctx_blocks - 1), 0))
    widths = (COL_DV - COL_DK, COL_NQ - COL_DV, COL_NV - COL_NK, W - COL_NV)
    return pl.pallas_call(
        kern,
        grid=(N // tm,),
        in_specs=x_specs + [
                  pl.BlockSpec((None, None, 1, N_MOD * D), lambda i: (l, grp(i), 0, 0)),
                  pl.BlockSpec((1, D), lambda i: (0, 0)),
                  pl.BlockSpec((D, W), lambda i: (0, 0)),
                  pl.BlockSpec((tm, LANES), lambda i: (rope_blk(i), 0)),
                  pl.BlockSpec((tm, LANES), lambda i: (rope_blk(i), 0))],
        out_specs=[pl.BlockSpec((tm, W), lambda i: (i, 0))] + [ctx_out(w) for w in widths],
        out_shape=[jax.ShapeDtypeStruct((N, W), BF16)] + [jax.ShapeDtypeStruct((ctx_rows, w), F32) for w in widths],
        compiler_params=_params(("arbitrary",)),
        name="in_projection",
    )(*xs, mod4, gain, w_bf, cos_t, sin_t)


def _pool_kernel(u_ref, w_ref, sc_ref, o_ref, pad_ref, *, T, CH):
    H = 2 * POOL_MAX_HALF
    zeros = jnp.zeros((H, pad_ref.shape[1]), F32)
    pad_ref[0:H, :] = zeros
    pad_ref[H + T:2 * H + T, :] = zeros
    pad_ref[H:H + T, :] = u_ref[...].astype(F32)
    C = pad_ref.shape[1]
    lane = lax.broadcasted_iota(I32, (CH, C), 1)
    group = lane // (C // POOL_GROUPS)
    half = jnp.left_shift(1, group)
    row = lax.broadcasted_iota(I32, (CH, C), 0)
    R = CH + 2 * H

    def body(ci, carry):
        base = pl.multiple_of(ci * CH, CH)
        win = pad_ref[pl.ds(base, R), :]
        run = win
        acc = None
        for g in range(POOL_GROUPS):
            run = run + pltpu.roll(run, R - (1 << g), 0)
            start = H - (1 << g)
            part = run[start:start + CH, :]
            acc = part if acc is None else jnp.where(group == g, part, acc)
        t = row + base
        cnt = jnp.minimum(t + half, T) - jnp.maximum(t - half, 0)
        d = acc / cnt.astype(F32) - win[H:H + CH, :]
        o_ref[pl.ds(base, CH), :] = (_dot(d.astype(BF16), w_ref[...]) * sc_ref[...]).astype(o_ref.dtype)
        return carry

    lax.fori_loop(0, T // CH, body, 0)


def _pool(p, w_bd, scale, *, row0, n_seq, T, out):
    C = w_bd.shape[0]
    CH = min(T, 256)
    blk0 = row0 // T

    def kern(u_ref, w_ref, sc_ref, prev_ref, o_ref, pad_ref):
        del prev_ref
        _pool_kernel(u_ref, w_ref, sc_ref, o_ref, pad_ref, T=T, CH=CH)

    return pl.pallas_call(
        kern,
        grid=(n_seq,),
        in_specs=[pl.BlockSpec((T, C), lambda s: (blk0 + s, 0)),
                  pl.BlockSpec((C, C), lambda s: (0, 0)),
                  pl.BlockSpec((1, C), lambda s: (0, 0)),
                  pl.BlockSpec(memory_space=pl.ANY)],
        out_specs=pl.BlockSpec((T, C), lambda s: (blk0 + s, 0)),
        out_shape=jax.ShapeDtypeStruct(out.shape, out.dtype),
        scratch_shapes=[pltpu.VMEM((T + 4 * POOL_MAX_HALF, C), F32)],
        input_output_aliases={3: 0},
        compiler_params=_params(("arbitrary",)),
        name="pool_mixer",
    )(p, w_bd, scale, out)


def _softmax_pv(s, v):
    m = jnp.max(s, axis=-1, keepdims=True)
    e = jnp.exp2(s - m)
    l = jnp.sum(e, axis=-1, keepdims=True)
    return _dot(e.astype(BF16), v) * (1.0 / l)


def _diff_kernel(*refs, Ts, Lc, lam_init, nq, n_units):
    if Lc:
        lam_ref, sub_ref, q_ref, ks_ref, vs_ref, kc_ref, vc_ref, prev_ref, o_ref, kb, vt, s_a, s_b, m_a, m_b = refs
    else:
        lam_ref, sub_ref, q_ref, ks_ref, vs_ref, prev_ref, o_ref, kb, vt, s_a, s_b, m_a, m_b = refs
    del prev_ref
    t = pl.program_id(0)
    S = Ts + Lc
    dv = DIFF_V_DIM
    head_new = jnp.minimum(t, n_units - 1) // nq
    slot_new = head_new % 2
    slot_old = (jnp.maximum(t - 1, 0) // nq) % 2

    @pl.when((t % nq == 0) & (t < n_units))
    def _():
        kb[0:Ts, :] = ks_ref[...].astype(BF16)
        vt[slot_new, 0:dv, 0:Ts] = vs_ref[...].astype(F32).T.astype(BF16)
        if Lc:
            kb[Ts:S, :] = kc_ref[...].astype(BF16)
            vt[slot_new, 0:dv, Ts:S] = vc_ref[...].T.astype(BF16)
        vt[slot_new, dv:, :] = jnp.ones((vt.shape[1] - dv, S), BF16)

    @pl.when(t == 0)
    def _():
        s_b[...] = jnp.zeros(s_b.shape, F32)
        m_b[...] = jnp.zeros(m_b.shape, F32)

    lm = lam_ref[...]
    lam = (jnp.exp(jnp.sum(lm[0:1, :] * lm[1:2, :], axis=-1, keepdims=True))
           - jnp.exp(jnp.sum(lm[2:3, :] * lm[3:4, :], axis=-1, keepdims=True)) + lam_init)

    def stage(s_new, m_new, s_old, m_old):
        q = q_ref[...].astype(F32) * (DIFF_QK_DIM ** -0.5 * LOG2E)
        tq = q.shape[0]
        part = lax.broadcasted_iota(I32, q.shape, 1) // DIFF_QK_DIM
        qm = [jnp.where(part == u, q, 0.0).astype(BF16) for u in range(2)]
        top = [m_old[u][0:1, :] for u in range(2)]
        acc = [jnp.zeros((vt.shape[1], tq), F32) for _ in range(2)]
        run = [jnp.full((SUBLANES, tq), -jnp.inf, F32) for _ in range(2)]
        for c0 in range(0, S, DIFF_KEYS):
            keys = slice(c0, c0 + DIFF_KEYS)
            for u in range(2):
                sc = _dot_nt(kb[keys, :], qm[u])
                s_new[u, keys, :] = sc
                for j in range(0, DIFF_KEYS, SUBLANES):
                    run[u] = jnp.maximum(run[u], sc[j:j + SUBLANES, :])
                e = jnp.exp2(s_old[u, keys, :] - top[u])
                acc[u] = acc[u] + _dot(vt[slot_old, :, keys], e.astype(BF16))
        for u in range(2):
            m_new[u] = jnp.broadcast_to(jnp.max(run[u], axis=0, keepdims=True), m_new.shape[1:])
        o = (acc[0][0:dv, :] * (1.0 / acc[0][dv:dv + 1, :])
             - lam * (acc[1][0:dv, :] * (1.0 / acc[1][dv:dv + 1, :])))
        o_ref[...] = (_rms(o.T) * sub_ref[...] * (1.0 - lam_init)).astype(o_ref.dtype)

    @pl.when(t % 2 == 0)
    def _():
        stage(s_a, m_a, s_b, m_b)

    @pl.when(t % 2 == 1)
    def _():
        stage(s_b, m_b, s_a, m_a)


def _diff_ctx_kernel(lam_ref, sub_ref, q_ref, k_ref, v_ref, prev_ref, o_ref, *, lam_init, heads):
    del prev_ref
    lm = lam_ref[...]
    lam = (jnp.exp(jnp.sum(lm[0:1, :] * lm[1:2, :], axis=-1, keepdims=True))
           - jnp.exp(jnp.sum(lm[2:3, :] * lm[3:4, :], axis=-1, keepdims=True)) + lam_init)
    q = q_ref[...].astype(F32) * (DIFF_QK_DIM ** -0.5 * LOG2E)
    part = lax.broadcasted_iota(I32, q.shape, 1) // DIFF_QK_DIM
    k = k_ref[...].astype(BF16)
    v = v_ref[...].astype(BF16)
    for hh in range(heads):
        o1 = _softmax_pv(_dot_nt(jnp.where(part == 2 * hh, q, 0.0).astype(BF16), k), v)
        o2 = _softmax_pv(_dot_nt(jnp.where(part == 2 * hh + 1, q, 0.0).astype(BF16), k), v)
        cols = slice(hh * DIFF_V_DIM, (hh + 1) * DIFF_V_DIM)
        o = o1[:, cols] - lam * o2[:, cols]
        o_ref[:, cols] = (_rms(o) * sub_ref[...] * (1.0 - lam_init)).astype(o_ref.dtype)


def _diff_attention_ctx(p, lam_l, subln, out, *, n_seq, T, lam_init):
    heads = 2
    W = heads * DIFF_V_DIM
    spec = lambda col0: pl.BlockSpec((T, W), lambda b, h: (b, col0 // W + h))
    return pl.pallas_call(
        functools.partial(_diff_ctx_kernel, lam_init=lam_init, heads=heads),
        grid=(n_seq, DIFF_HEADS // heads),
        in_specs=[pl.BlockSpec((4, DIFF_QK_DIM), lambda b, h: (0, 0)),
                  pl.BlockSpec((1, DIFF_V_DIM), lambda b, h: (0, 0)),
                  spec(COL_DQ), spec(COL_DK), spec(COL_DV),
                  pl.BlockSpec(memory_space=pl.ANY)],
        out_specs=pl.BlockSpec((T, W), lambda b, h: (b, h)),
        out_shape=jax.ShapeDtypeStruct(out.shape, out.dtype),
        input_output_aliases={5: 0},
        compiler_params=_params(("arbitrary", "arbitrary")),
        name="diff_attention_ctx",
    )(lam_l, subln, p, p, p, out)


def _diff_attention(p, lam_l, subln, out, *, row0, n_seq, T, tq, lam_init, cache_k=None, cache_v=None, l=0):
    W = DIFF_V_DIM
    assert 2 * DIFF_QK_DIM == W == LANES
    Lc = 0 if cache_k is None else cache_k.shape[2]
    S = T + Lc
    assert S % DIFF_KEYS == 0
    nq = T // tq
    qb0 = row0 // tq
    sb0 = row0 // T
    n_units = n_seq * DIFF_HEADS * nq

    def unit(t):
        return t // (DIFF_HEADS * nq), (t // nq) % DIFF_HEADS, t % nq

    def scored(t, col0):
        b, h, i = unit(jnp.minimum(t, n_units - 1))
        return b, h, i, col0 // W + h

    in_specs = [pl.BlockSpec((4, DIFF_QK_DIM), lambda t: (0, 0)),
                pl.BlockSpec((1, W), lambda t: (0, 0)),
                pl.BlockSpec((tq, W), lambda t: (qb0 + scored(t, 0)[0] * nq + scored(t, 0)[2], scored(t, COL_DQ)[3])),
                pl.BlockSpec((T, W), lambda t: (sb0 + scored(t, 0)[0], scored(t, COL_DK)[3])),
                pl.BlockSpec((T, W), lambda t: (sb0 + scored(t, 0)[0], scored(t, COL_DV)[3]))]
    args = [lam_l, subln, p, p, p]
    if Lc:
        in_specs += [pl.BlockSpec((None, None, Lc, W), lambda t: (scored(t, 0)[0], l, 0, scored(t, 0)[1])),
                     pl.BlockSpec((None, None, Lc, W), lambda t: (scored(t, 0)[0], l, 0, scored(t, 0)[1]))]
        args += [cache_k, cache_v]
    in_specs.append(pl.BlockSpec(memory_space=pl.ANY))
    args.append(out)
    scores = pltpu.VMEM((2, S, tq), F32)
    row_max = pltpu.VMEM((2, SUBLANES, tq), F32)
    vt_rows = W + BF16_ROWS

    def finished(t):
        b, h, i = unit(jnp.maximum(t - 1, 0))
        return qb0 + b * nq + i, h

    return pl.pallas_call(
        functools.partial(_diff_kernel, Ts=T, Lc=Lc, lam_init=lam_init, nq=nq, n_units=n_units),
        grid=(n_units + 1,),
        in_specs=in_specs,
        out_specs=pl.BlockSpec((tq, W), finished),
        out_shape=jax.ShapeDtypeStruct(out.shape, out.dtype),
        scratch_shapes=[pltpu.VMEM((S, W), BF16), pltpu.VMEM((2, vt_rows, S), BF16),
                        scores, scores, row_max, row_max],
        input_output_aliases={len(args) - 1: 0},
        compiler_params=_params(("arbitrary",)),
        name="diff_attention",
    )(*args)


def _na_heads(q, score_fn, pv_fn):
    lane = lax.broadcasted_iota(I32, q.shape, 1) // NA_HEAD_DIM
    out = jnp.zeros(q.shape, F32)
    for h in range(NA_HEADS):
        qh = jnp.where(lane == h, q, 0.0).astype(BF16)
        out = jnp.where(lane == h, pv_fn(score_fn(qh, h)), out)
    return out


def _dense_kernel(q_ref, k_ref, v_ref, prev_ref, o_ref):
    del prev_ref
    q = q_ref[...].astype(F32) * (NA_HEAD_DIM ** -0.5 * LOG2E)
    k = k_ref[...].astype(BF16)
    v = v_ref[...].astype(BF16)
    o_ref[...] = _na_heads(q, lambda qh, h: _dot_nt(qh, k), lambda s: _softmax_pv(s, v)).astype(o_ref.dtype)


def _dense_attention(p, out, *, n_seq, T):
    C = NA_HEADS * NA_HEAD_DIM
    return pl.pallas_call(
        _dense_kernel,
        grid=(n_seq,),
        in_specs=[pl.BlockSpec((T, C), lambda b: (b, COL_NQ // C)),
                  pl.BlockSpec((T, C), lambda b: (b, COL_NK // C)),
                  pl.BlockSpec((T, C), lambda b: (b, COL_NV // C)),
                  pl.BlockSpec(memory_space=pl.ANY)],
        out_specs=pl.BlockSpec((T, C), lambda b: (b, 0)),
        out_shape=jax.ShapeDtypeStruct(out.shape, out.dtype),
        input_output_aliases={3: 0},
        compiler_params=_params(("arbitrary",)),
        name="dense_attention",
    )(p, p, p, out)


def _na_kernel(q_ref, ks_ref, vs_ref, kc_ref, vc_ref, bias_ref, prev_ref, o_ref, *, rows):
    del prev_ref
    r0 = pl.program_id(1) * NA_Q_ROWS
    bs = jnp.clip(r0 - NA_WIN_H // 2, 0, rows - NA_BAND)
    start = pl.multiple_of(bs * GRID_W, GRID_W)
    nb = NA_BAND * GRID_W
    kb = ks_ref[pl.ds(start, nb), :].astype(BF16)
    vb = vs_ref[pl.ds(start, nb), :].astype(BF16)
    kc = kc_ref[...].astype(BF16)
    vc = vc_ref[...].astype(BF16)
    q = q_ref[...].astype(F32) * (NA_HEAD_DIM ** -0.5 * LOG2E)

    def scores(qh, h):
        return _dot_nt(qh, kb) + bias_ref[h], _dot_nt(qh, kc)

    def pv(s):
        s_loc, s_ctx = s
        m = jnp.maximum(jnp.max(s_loc, axis=-1, keepdims=True), jnp.max(s_ctx, axis=-1, keepdims=True))
        e_loc = jnp.exp2(s_loc - m)
        e_ctx = jnp.exp2(s_ctx - m)
        l = jnp.sum(e_loc, axis=-1, keepdims=True) + jnp.sum(e_ctx, axis=-1, keepdims=True)
        return (_dot(e_loc.astype(BF16), vb) + _dot(e_ctx.astype(BF16), vc)) * (1.0 / l)

    o_ref[...] = _na_heads(q, scores, pv).astype(o_ref.dtype)


def _na_bias_table(rpb_l, rows):
    n_ro, n_co = 2 * NA_WIN_H - 1, 2 * NA_WIN_W - 1
    c = np.arange(GRID_W)[:, None]
    kc = np.arange(GRID_W)[None, :]
    cs = np.clip(c - NA_WIN_W // 2, 0, GRID_W - NA_WIN_W)
    col_ok = (kc >= cs) & (kc < cs + NA_WIN_W)
    co = kc - c + (NA_WIN_W - 1)
    pick = ((np.arange(n_co)[:, None, None] == co[None]) & col_ok[None]).astype(np.float32)
    toep = jnp.dot(rpb_l.reshape(NA_HEADS * n_ro, n_co).astype(F32), jnp.asarray(pick.reshape(n_co, -1)),
                   precision=lax.Precision.HIGHEST).reshape(NA_HEADS, n_ro, GRID_W, GRID_W) * LOG2E
    toep = jnp.where(jnp.asarray(col_ok)[None, None], toep, MASKED)
    toep = jnp.concatenate([toep, jnp.full((NA_HEADS, 1, GRID_W, GRID_W), MASKED, F32)], axis=1)
    blk = np.full((3, NA_Q_ROWS, NA_BAND), n_ro, np.int32)
    for v, r0 in enumerate((0, NA_Q_ROWS, rows - NA_Q_ROWS)):
        bs = int(np.clip(r0 - NA_WIN_H // 2, 0, rows - NA_BAND))
        for j in range(NA_Q_ROWS):
            rs = int(np.clip(r0 + j - NA_WIN_H // 2, 0, rows - NA_WIN_H))
            for i in range(NA_BAND):
                if rs <= bs + i < rs + NA_WIN_H:
                    blk[v, j, i] = bs + i - (r0 + j) + (NA_WIN_H - 1)
    tab = toep[:, blk]
    tab = jnp.transpose(tab, (1, 0, 2, 4, 3, 5))
    return tab.reshape(3, NA_HEADS, NA_Q_ROWS * GRID_W, NA_BAND * GRID_W)


def _neighbourhood_attention(p, cache_k, cache_v, bias, out, *, row0, n_seq, T, l):
    C = NA_HEADS * NA_HEAD_DIM
    rows = T // GRID_W
    tq = NA_Q_ROWS * GRID_W
    nq = T // tq
    Lc = cache_k.shape[2]
    qb0 = row0 // tq
    sb0 = row0 // T

    def variant(i):
        r0 = i * NA_Q_ROWS
        return (r0 - jnp.clip(r0 - NA_WIN_H // 2, 0, rows - NA_BAND)) // NA_Q_ROWS

    return pl.pallas_call(
        functools.partial(_na_kernel, rows=rows),
        grid=(n_seq, nq),
        in_specs=[pl.BlockSpec((tq, C), lambda b, i: (qb0 + b * nq + i, COL_NQ // C)),
                  pl.BlockSpec((T, C), lambda b, i: (sb0 + b, COL_NK // C)),
                  pl.BlockSpec((T, C), lambda b, i: (sb0 + b, COL_NV // C)),
                  pl.BlockSpec((None, None, Lc, C), lambda b, i: (b, l, 0, 0)),
                  pl.BlockSpec((None, None, Lc, C), lambda b, i: (b, l, 0, 0)),
                  pl.BlockSpec((None, NA_HEADS, tq, NA_BAND * GRID_W), lambda b, i: (variant(i), 0, 0, 0)),
                  pl.BlockSpec(memory_space=pl.ANY)],
        out_specs=pl.BlockSpec((tq, C), lambda b, i: (qb0 + b * nq + i, 0)),
        out_shape=jax.ShapeDtypeStruct(out.shape, out.dtype),
        input_output_aliases={6: 0},
        compiler_params=_params(("arbitrary", "arbitrary")),
        name="neighbourhood_attention",
    )(p, p, p, cache_k, cache_v, bias, out)


def _merge_kernel(pool_ref, dn_ref, na_ref, *refs, D, ctx_blocks):
    mod_ref, g1_ref, g2_ref, w_ref, rw_ref, rb_ref, x1_ref, h2_ref, route_ref, cnt_ref = refs[-10:]
    x_rows = _token_rows(refs[:-10], ctx_blocks)
    parts = [slice(j * TOK_CHUNK, (j + 1) * TOK_CHUNK) for j in range(x1_ref.shape[0] // TOK_CHUNK)]
    c0 = pool_ref.shape[1]
    c1 = c0 + dn_ref.shape[1]
    mix = [_dot(pool_ref[p, :].astype(BF16), w_ref[0:c0, :])
           + _dot(dn_ref[p, :].astype(BF16), w_ref[c0:c1, :])
           + _dot(na_ref[p, :].astype(BF16), w_ref[c1:, :]) for p in parts]
    x1 = [x_rows(p) + mod_ref[:, 2 * D:3 * D] * (_rms(m) * g1_ref[...]) for p, m in zip(parts, mix)]
    for p, v in zip(parts, x1):
        x1_ref[p, :] = v
    h2 = [(_rms(v) * g2_ref[...]) * (1.0 + mod_ref[:, 4 * D:5 * D]) + mod_ref[:, 3 * D:4 * D] for v in x1]
    for p, v in zip(parts, h2):
        h2_ref[p, :] = v.astype(h2_ref.dtype)

    rw_hi, rw_lo = _split(rw_ref[...])
    h2_split = [_split(v) for v in h2]
    logits = [_dot(hi, rw_hi) + _dot(lo, rw_hi) + _dot(hi, rw_lo) + rb_ref[...] for hi, lo in h2_split]
    lane = lax.broadcasted_iota(I32, (TOK_CHUNK, LANES), 1)
    lane_f = lane.astype(F32)
    work = logits
    vals, hots = [], []
    for _ in range(TOP_K):
        mx = [jnp.max(w, axis=-1, keepdims=True) for w in work]
        idx = [jnp.min(jnp.where(w == m, lane_f, float(LANES)), axis=-1, keepdims=True)
               for w, m in zip(work, mx)]
        hot = [lane_f == i for i in idx]
        vals.append(mx)
        hots.append(hot)
        work = [jnp.where(h, -jnp.inf, w) for h, w in zip(hot, work)]

    r = lax.broadcasted_iota(I32, (TOK_CHUNK, TOK_CHUNK), 0)
    c = lax.broadcasted_iota(I32, (TOK_CHUNK, TOK_CHUNK), 1)
    before = jnp.where(c < r, 1.0, 0.0).astype(BF16)
    er = lax.broadcasted_iota(I32, (LANES, LANES), 0)
    ec = lax.broadcasted_iota(I32, (LANES, LANES), 1)
    earlier = jnp.where(er < ec, 1.0, 0.0).astype(BF16)
    for j, p in enumerate(parts):
        hot_j = [hots[k][j] for k in range(TOP_K)]
        es = [jnp.exp(vals[k][j] - vals[0][j]) for k in range(TOP_K)]
        inv = 1.0 / (es[0] + es[1] + es[2] + es[3])
        sel = jnp.zeros((TOK_CHUNK, LANES), F32)
        for hot in hot_j:
            sel = jnp.where(hot, 1.0, sel)
        rank = _dot(before, sel.astype(BF16))
        cnt = jnp.sum(sel, axis=0, keepdims=True)
        cnt_ref[j] = cnt
        run = jnp.floor((cnt + (RUN_ALIGN - 1)) * (1.0 / RUN_ALIGN)) * RUN_ALIGN
        run_start = _dot(jnp.broadcast_to(run, (SUBLANES, LANES)).astype(BF16), earlier)[0:1, :]
        pos = rank + run_start
        route = jnp.zeros((TOK_CHUNK, LANES), F32)
        for k in range(TOP_K):
            e_k = jnp.sum(jnp.where(hot_j[k], lane_f, 0.0), axis=-1, keepdims=True)
            p_k = jnp.sum(jnp.where(hot_j[k], pos, 0.0), axis=-1, keepdims=True)
            route = jnp.where(lane == k, e_k, route)
            route = jnp.where(lane == TOP_K + k, p_k, route)
            route = jnp.where(lane == 2 * TOP_K + k, es[k] * inv, route)
        route_ref[p, :] = route


def _merge_route(pool_o, dn, na_o, x, mod4, g1, g2, w_out_bf, rw_pad, rb_pad, *, tm, ctx_blocks,
                 blocks_per_seq, l):
    xs, x_specs = _token_specs(x, tm, ctx_blocks)
    N, D = sum(a.shape[0] for a in xs), xs[0].shape[1]
    per_step = tm // TOK_CHUNK

    def grp(i):
        return jnp.where(i < ctx_blocks, 0, 1 + (i - ctx_blocks) // blocks_per_seq)

    row = lambda w: pl.BlockSpec((tm, w), lambda i: (i, 0))
    full = lambda a: pl.BlockSpec(a.shape, lambda i: (0,) * a.ndim)
    return pl.pallas_call(
        functools.partial(_merge_kernel, D=D, ctx_blocks=ctx_blocks),
        grid=(N // tm,),
        in_specs=[row(pool_o.shape[1]), row(dn.shape[1]), row(na_o.shape[1])] + x_specs + [
                  pl.BlockSpec((None, None, 1, N_MOD * D), lambda i: (l, grp(i), 0, 0)),
                  full(g1), full(g2), full(w_out_bf), full(rw_pad), full(rb_pad)],
        out_specs=[row(D), row(D), row(LANES), pl.BlockSpec((per_step, 1, LANES), lambda i: (i, 0, 0))],
        out_shape=[jax.ShapeDtypeStruct((N, D), F32), jax.ShapeDtypeStruct((N, D), BF16),
                   jax.ShapeDtypeStruct((N, LANES), F32), jax.ShapeDtypeStruct((N // TOK_CHUNK, 1, LANES), F32)],
        compiler_params=_params(("arbitrary",)),
        name="merge_route",
    )(pool_o, dn, na_o, *xs, mod4, g1, g2, w_out_bf, rw_pad, rb_pad)


def _pack_pairs(x):
    C = x.shape[1] // 2
    bits = lax.bitcast_convert_type(x.astype(BF16).astype(F32), jnp.uint32)
    return bits[:, C:] | (bits[:, :C] >> 16)


def _unpack_pairs(w):
    lo = lax.bitcast_convert_type(w << 16, F32)
    hi = lax.bitcast_convert_type(w & jnp.uint32(0xFFFF0000), F32)
    return jnp.concatenate([lo, hi], axis=1).astype(BF16)


def _run_sizes(lo, hi):
    return [1 << k for k in range(hi.bit_length() - 1, lo.bit_length() - 2, -1)]


def _run_copies(n, src_at, dst_at, sem, wait, sizes):
    for size in sizes:
        @pl.when((n & size) != 0)
        def _(size=size):
            off = n & -(2 * size)
            cp = pltpu.make_async_copy(src_at(off, size), dst_at(off, size), sem)
            cp.wait() if wait else cp.start()


def _block_runs(step, n_exp, run_ref, src_ref, dst_ref, large_ref, hbm_ref, buf, sem, *, to_hbm, wait):
    def each_run(sizes):
        def body(e, carry):
            j = step * n_exp + e
            so = src_ref[j]
            do = dst_ref[j]
            in_buf = lambda o, s: buf.at[pl.ds(pl.multiple_of(so + o, RUN_ALIGN), s)]
            in_hbm = lambda o, s: hbm_ref.at[pl.ds(pl.multiple_of(do + o, RUN_ALIGN), s)]
            if to_hbm:
                _run_copies(run_ref[j], in_buf, in_hbm, sem, wait, sizes)
            else:
                _run_copies(run_ref[j], in_hbm, in_buf, sem, wait, sizes)
            return carry
        lax.fori_loop(0, n_exp, body, 0, unroll=RUN_UNROLL)

    each_run(_run_sizes(RUN_ALIGN, RUN_LARGE // 2))

    @pl.when(large_ref[step] != 0)
    def _():
        each_run(_run_sizes(RUN_LARGE, TOK_CHUNK))


def _dispatch_kernel(run_ref, src_ref, dst_ref, large_ref, tail_ref, taildst_ref, nu_ref, route_ref, h_ref, xs_ref,
                     sorted_buf, zero_buf, sems, *, n_exp):
    b = pl.program_id(0)
    nb = pl.num_programs(0)
    slot = b % 2
    runs = functools.partial(_block_runs, n_exp=n_exp, run_ref=run_ref, src_ref=src_ref, dst_ref=dst_ref,
                             large_ref=large_ref, hbm_ref=xs_ref, to_hbm=True)

    @pl.when(b >= 2)
    def _():
        runs(b - 2, buf=sorted_buf.at[slot], sem=sems.at[slot], wait=True)

    route = route_ref[...]
    col = lax.broadcasted_iota(I32, (TOK_CHUNK, sorted_buf.shape[1]), 1).astype(F32)
    place = jnp.zeros(col.shape, F32)
    for k in range(TOP_K):
        place = jnp.where(col == route[:, TOP_K + k:TOP_K + k + 1], 1.0, place)
    srt = lax.dot_general(place.astype(BF16), h_ref[...].astype(BF16), (((0,), (0,)), ((), ())),
                          preferred_element_type=F32)
    sorted_buf[slot] = _pack_pairs(srt)
    runs(b, buf=sorted_buf.at[slot], sem=sems.at[slot], wait=False)

    @pl.when(b == nb - 1)
    def _():
        @pl.when(b >= 1)
        def _():
            runs(b - 1, buf=sorted_buf.at[1 - slot], sem=sems.at[1 - slot], wait=True)
        runs(b, buf=sorted_buf.at[slot], sem=sems.at[slot], wait=True)

        zero_buf[...] = jnp.zeros(zero_buf.shape, zero_buf.dtype)
        sem = sems.at[0]

        def each_tail(wait):
            def body(e, carry):
                do = taildst_ref[e]
                _run_copies(tail_ref[e],
                            lambda o, s: zero_buf.at[pl.ds(0, s)],
                            lambda o, s: xs_ref.at[pl.ds(pl.multiple_of(do + o, RUN_ALIGN), s)], sem, wait,
                            _run_sizes(RUN_ALIGN, MOE_ROWS // 2))
                return carry
            lax.fori_loop(0, n_exp, body, 0)

        def spare_block(wait):
            def body(i, carry):
                cp = pltpu.make_async_copy(zero_buf.at[pl.ds(0, MOE_ROWS)],
                                           xs_ref.at[pl.ds(pl.multiple_of(i * MOE_ROWS, MOE_ROWS), MOE_ROWS)], sem)
                cp.wait() if wait else cp.start()
                return carry
            lax.fori_loop(nu_ref[0], xs_ref.shape[0] // MOE_ROWS, body, 0)

        each_tail(False)
        spare_block(False)
        each_tail(True)
        spare_block(True)


def _dispatch(plan, n_used, route, h2, n_rows, n_exp):
    N, D = h2.shape
    C = D // 2
    grid_spec = pltpu.PrefetchScalarGridSpec(
        num_scalar_prefetch=7,
        grid=(N // TOK_CHUNK,),
        in_specs=[pl.BlockSpec((TOK_CHUNK, LANES), lambda i, *_: (i, 0)),
                  pl.BlockSpec((TOK_CHUNK, D), lambda i, *_: (i, 0))],
        out_specs=pl.BlockSpec(memory_space=pl.ANY),
        scratch_shapes=[pltpu.VMEM((2, _sorted_rows(n_exp), C), jnp.uint32),
                        pltpu.VMEM((max(TOK_CHUNK, MOE_ROWS), C), jnp.uint32),
                        pltpu.SemaphoreType.DMA((2,))],
    )
    return pl.pallas_call(
        functools.partial(_dispatch_kernel, n_exp=n_exp),
        grid_spec=grid_spec,
        out_shape=jax.ShapeDtypeStruct((n_rows, C), jnp.uint32),
        compiler_params=_params(("arbitrary",)),
        name="moe_dispatch",
    )(plan["run"], plan["src"], plan["dst"], plan["large"], plan["tail"], plan["tail_dst"], n_used, route, h2)


def _expert_kernel(be_ref, nu_ref, rows_ref, xs_ref, w1_ref, b1_ref, w2_ref, b2_ref, o_ref, w1b, w2b, *, F):
    del nu_ref
    i = pl.program_id(0)
    e = be_ref[i]
    prev = be_ref[jnp.maximum(i - 1, 0)]
    rows = rows_ref[i]
    half = MOE_ROWS // 2

    first = (i == 0) | (e != prev)

    def ffn(xw, fresh):
        if fresh:
            w1 = w1_ref[...].astype(BF16)
            w2 = w2_ref[...].astype(BF16)
            w1b[...] = w1
            w2b[...] = w2
        else:
            w1 = w1b[...]
            w2 = w2b[...]
        hh = _dot(_unpack_pairs(xw), w1) + b1_ref[...]
        g = jnp.minimum(hh[:, 0:F], SWIGLU_LIMIT)
        u = jnp.clip(hh[:, F:2 * F], -SWIGLU_LIMIT, SWIGLU_LIMIT)
        a = (g * (1.0 / (1.0 + jnp.exp(-SWIGLU_ALPHA * g)))) * (u + 1.0)
        return _pack_pairs(_dot(a.astype(BF16), w2) + b2_ref[...])

    for fresh in (True, False):
        new_expert = first if fresh else jnp.logical_not(first)

        @pl.when(new_expert & (rows > half))
        def _(fresh=fresh):
            o_ref[...] = ffn(xs_ref[...], fresh)

        @pl.when(new_expert & (rows > 0) & (rows <= half))
        def _(fresh=fresh):
            o_ref[0:half, :] = ffn(xs_ref[0:half, :], fresh)
            o_ref[half:MOE_ROWS, :] = jnp.zeros((MOE_ROWS - half, o_ref.shape[1]), o_ref.dtype)

    @pl.when(rows == 0)
    def _():
        o_ref[...] = jnp.zeros(o_ref.shape, o_ref.dtype)


def _experts(block_e, n_used, block_rows, xs, w1, b1, w2, b2, *, l):
    R, C = xs.shape
    L, E, D, F2 = w1.shape
    F = F2 // 2
    nblk = R // MOE_ROWS
    grid_spec = pltpu.PrefetchScalarGridSpec(
        num_scalar_prefetch=3,
        grid=(nblk,),
        in_specs=[pl.BlockSpec((MOE_ROWS, C), lambda i, be, nu, br: (jnp.minimum(i, nu[0] - 1), 0)),
                  pl.BlockSpec((None, None, D, F2), lambda i, be, nu, br: (l, be[i], 0, 0)),
                  pl.BlockSpec((None, None, 1, F2), lambda i, be, nu, br: (l, be[i], 0, 0)),
                  pl.BlockSpec((None, None, F, D), lambda i, be, nu, br: (l, be[i], 0, 0)),
                  pl.BlockSpec((None, None, 1, D), lambda i, be, nu, br: (l, be[i], 0, 0))],
        out_specs=pl.BlockSpec((MOE_ROWS, C), lambda i, be, nu, br: (i, 0)),
        scratch_shapes=[pltpu.VMEM((D, F2), BF16), pltpu.VMEM((F, D), BF16)],
    )
    return pl.pallas_call(
        functools.partial(_expert_kernel, F=F),
        grid_spec=grid_spec,
        out_shape=jax.ShapeDtypeStruct((R, C), jnp.uint32),
        compiler_params=_params(("arbitrary",)),
        name="moe_experts",
    )(block_e, n_used, block_rows, xs, w1, b1.reshape(L, E, 1, F2), w2, b2.reshape(L, E, 1, D))


def _combine_kernel(run_ref, src_ref, dst_ref, large_ref, yb_ref, route_ref, x1_ref, mod_ref, g_ref, *rest, D,
                    n_exp, ctx_blocks, split):
    if split:
        ctx_ref, lat_ref, sorted_buf, sems = rest
    else:
        o_ref, sorted_buf, sems = rest
    b = pl.program_id(0)
    nb = pl.num_programs(0)
    slot = b % 2
    runs = functools.partial(_block_runs, n_exp=n_exp, run_ref=run_ref, src_ref=src_ref, dst_ref=dst_ref,
                             large_ref=large_ref, hbm_ref=yb_ref, to_hbm=False)

    @pl.when(b == 0)
    def _():
        sorted_buf[...] = jnp.zeros(sorted_buf.shape, sorted_buf.dtype)
        runs(b, buf=sorted_buf.at[slot], sem=sems.at[slot], wait=False)

    @pl.when(b + 1 < nb)
    def _():
        runs(b + 1, buf=sorted_buf.at[1 - slot], sem=sems.at[1 - slot], wait=False)

    runs(b, buf=sorted_buf.at[slot], sem=sems.at[slot], wait=True)
    route = route_ref[...]
    col = lax.broadcasted_iota(I32, (TOK_CHUNK, sorted_buf.shape[1]), 1).astype(F32)
    gate = jnp.zeros(col.shape, F32)
    for k in range(TOP_K):
        gate = jnp.where(col == route[:, TOP_K + k:TOP_K + k + 1], route[:, 2 * TOP_K + k:2 * TOP_K + k + 1], gate)
    g_hi, g_lo = _split(gate)
    yb = _unpack_pairs(sorted_buf[slot])
    y = _dot(g_hi, yb) + _dot(g_lo, yb)
    x2 = x1_ref[...] + mod_ref[:, 5 * D:6 * D] * (_rms(y) * g_ref[...])
    if split:
        @pl.when(b < ctx_blocks)
        def _():
            ctx_ref[...] = x2

        @pl.when(b >= ctx_blocks)
        def _():
            lat_ref[...] = x2
    else:
        o_ref[...] = x2


def _combine(plan, yb, route, x1, mod4, g3, *, n_exp, ctx_blocks, blocks_per_seq, l, split):
    N, D = x1.shape
    row = lambda f: pl.BlockSpec((TOK_CHUNK, D), lambda i, *_: (f(i), 0))
    if split:
        out_specs = [row(lambda i: jnp.minimum(i, ctx_blocks - 1)), row(lambda i: jnp.maximum(i - ctx_blocks, 0))]
        out_shape = [jax.ShapeDtypeStruct((ctx_blocks * TOK_CHUNK, D), F32),
                     jax.ShapeDtypeStruct((N - ctx_blocks * TOK_CHUNK, D), F32)]
    else:
        out_specs = row(lambda i: i)
        out_shape = jax.ShapeDtypeStruct((N, D), F32)

    def grp(i):
        return jnp.where(i < ctx_blocks, 0, 1 + (i - ctx_blocks) // blocks_per_seq)

    grid_spec = pltpu.PrefetchScalarGridSpec(
        num_scalar_prefetch=4,
        grid=(N // TOK_CHUNK,),
        in_specs=[pl.BlockSpec(memory_space=pl.ANY),
                  pl.BlockSpec((TOK_CHUNK, LANES), lambda i, *_: (i, 0)),
                  pl.BlockSpec((TOK_CHUNK, D), lambda i, *_: (i, 0)),
                  pl.BlockSpec((None, None, 1, N_MOD * D), lambda i, *_: (l, grp(i), 0, 0)),
                  pl.BlockSpec((1, D), lambda i, *_: (0, 0))],
        out_specs=out_specs,
        scratch_shapes=[pltpu.VMEM((2, _sorted_rows(n_exp), yb.shape[1]), jnp.uint32),
                        pltpu.SemaphoreType.DMA((2,))],
    )
    return pl.pallas_call(
        functools.partial(_combine_kernel, D=D, n_exp=n_exp, ctx_blocks=ctx_blocks, split=split),
        grid_spec=grid_spec,
        out_shape=out_shape,
        compiler_params=_params(("arbitrary",)),
        name="moe_combine",
    )(plan["run"], plan["src"], plan["dst"], plan["large"], yb, route, x1, mod4, g3)


def _rope_tables(Ts, tm):
    nf = DIFF_QK_DIM // 4
    inv = ROPE_BASE ** (-jnp.arange(nf, dtype=F32) / nf)
    t = jnp.arange(Ts)
    pos = jnp.stack([(t // GRID_W).astype(F32), (t % GRID_W).astype(F32)], axis=1)
    ang = pos[:, :, None] * inv[None, None, :]
    cos = jnp.repeat(jnp.cos(ang)[:, :, None, :], 2, axis=2).reshape(Ts, DIFF_QK_DIM)
    sin = jnp.sin(ang)
    sin = jnp.stack([-sin, sin], axis=2).reshape(Ts, DIFF_QK_DIM)
    reps = LANES // DIFF_QK_DIM
    cos = jnp.concatenate([jnp.ones((tm, LANES), F32), jnp.tile(cos, (1, reps))], axis=0)
    sin = jnp.concatenate([jnp.zeros((tm, LANES), F32), jnp.tile(sin, (1, reps))], axis=0)
    return cos, sin


def _block_diag(w):
    G, a, b = w.shape
    out = jnp.zeros((G * a, G * b), w.dtype)
    for g in range(G):
        out = out.at[g * a:(g + 1) * a, g * b:(g + 1) * b].set(w[g])
    return out


def _route_plan(counts, n_exp, nblk):
    cnt = counts[:, 0, :n_exp].astype(I32)
    run = (cnt + RUN_ALIGN - 1) // RUN_ALIGN * RUN_ALIGN
    src = jnp.cumsum(run, axis=1) - run
    tot = jnp.sum(run, axis=0)
    region = (tot + MOE_ROWS - 1) // MOE_ROWS * MOE_ROWS
    region_end = jnp.cumsum(region)
    region_start = region_end - region
    dst = region_start[None, :] + jnp.cumsum(run, axis=0) - run
    n_used = (region_end[-1] // MOE_ROWS).astype(I32)
    blk = jnp.arange(nblk, dtype=I32) * MOE_ROWS
    block_e = jnp.minimum(jnp.sum((blk[:, None] >= region_end[None, :]).astype(I32), axis=1), n_exp - 1)
    last = jnp.sum(jnp.where(jnp.arange(nblk) == n_used - 1, block_e, 0))
    used = jnp.arange(nblk) < n_used
    block_rows = jnp.where(used, jnp.clip((region_start + tot)[block_e] - blk, 0, MOE_ROWS), 0).astype(I32)
    block_e = jnp.where(used, block_e, last).astype(I32)
    plan = dict(run=run.reshape(-1).astype(I32), src=src.reshape(-1).astype(I32), dst=dst.reshape(-1).astype(I32),
                large=jnp.any(run >= RUN_LARGE, axis=1).astype(I32),
                tail=(region - tot).astype(I32), tail_dst=(region_start + tot).astype(I32))
    return plan, block_e, n_used.reshape(1), block_rows


def kernel(x_prompt, x_sample, cache_diff_k, cache_diff_v, cache_na_k, cache_na_v, c, c_ctx, w_ada, b_ada,
           norm_gain, w_in, w_out, pool_w, pool_scale, diff_lambda, diff_subln, na_rpb, router_w, router_b,
           moe_w1, moe_b1, moe_w2, moe_b2):
    Bp, Tp, D = x_prompt.shape
    Bs, Ts, _ = x_sample.shape
    L = w_ada.shape[0]
    E = router_w.shape[-1]
    Np, Ns = Bp * Tp, Bs * Ts
    N = Np + Ns
    tm = TOK_CHUNK
    assert Np % Ts == 0 or Bs == 0, "context rows must be a whole number of latent-sequence blocks"
    assert Np % tm == 0 and Ts % tm == 0 and Ts % (NA_Q_ROWS * GRID_W) == 0
    assert Ts // GRID_W >= NA_BAND + NA_Q_ROWS
    ctx_blocks, blocks_per_seq = Np // tm, Ts // tm

    G = BF16_ROWS
    assert 1 + Bs <= G
    cvec = jnp.zeros((G, D), F32).at[0].set(c_ctx).at[1:1 + Bs].set(c)
    mod4 = _modulation(cvec, w_ada, b_ada).reshape(L, G, 1, N_MOD * D)
    tp = PROJ_ROWS
    assert Np % tp == 0 and Ts % tp == 0 and Np % MERGE_ROWS == 0 and Ts % MERGE_ROWS == 0
    cos_t, sin_t = _rope_tables(Ts, tp)
    w_in_bf = w_in.astype(BF16)
    w_out_bf = w_out.astype(BF16)
    rw_pad = jnp.zeros((L, D, LANES), F32).at[:, :, :E].set(router_w)
    rb_pad = jnp.full((L, 1, LANES), MASKED, F32).at[:, 0, :E].set(router_b)
    ck = cache_diff_k.reshape(Bs, L, -1, DIFF_HEADS * 2 * DIFF_QK_DIM)
    cv = cache_diff_v.reshape(Bs, L, -1, DIFF_HEADS * DIFF_V_DIM)
    nk = cache_na_k.reshape(Bs, L, -1, NA_HEADS * NA_HEAD_DIM)
    nv = cache_na_v.reshape(Bs, L, -1, NA_HEADS * NA_HEAD_DIM)
    nb = N // TOK_CHUNK
    nblk = -(-(N * TOP_K + nb * E * (RUN_ALIGN - 1) + E * (MOE_ROWS - 1)) // MOE_ROWS)
    blocks = dict(ctx_blocks=ctx_blocks, blocks_per_seq=blocks_per_seq)

    x = (x_prompt.reshape(Np, D), x_sample.reshape(Ns, D))
    new_dk, new_dv, new_nk, new_nv = [], [], [], []
    for l in range(L):
        lam_init = 0.8 - 0.6 * math.exp(-0.3 * l)
        g = norm_gain[l]
        p, dk_l, dv_l, nk_l, nv_l = _in_projection(x, mod4, g[0:1], w_in_bf[l], cos_t, sin_t, tm=tp, l=l,
                                                   ctx_blocks=Np // tp, blocks_per_seq=Ts // tp)

        pw = _block_diag(pool_w[l]).astype(BF16)
        ps = pool_scale[l].reshape(1, -1)
        pool_o = jnp.zeros((N, pw.shape[0]), BF16)
        pool_o = _pool(p, pw, ps, row0=0, n_seq=Bp, T=Tp, out=pool_o)
        pool_o = _pool(p, pw, ps, row0=Np, n_seq=Bs, T=Ts, out=pool_o)

        sub = diff_subln[l].reshape(1, -1)
        dn = jnp.zeros((N, DIFF_HEADS * DIFF_V_DIM), BF16)
        dn = _diff_attention_ctx(p, diff_lambda[l], sub, dn, n_seq=Bp, T=Tp, lam_init=lam_init)
        dn = _diff_attention(p, diff_lambda[l], sub, dn, row0=Np, n_seq=Bs, T=Ts, tq=DIFF_Q_ROWS, lam_init=lam_init,
                             cache_k=ck, cache_v=cv, l=l)

        na_o = jnp.zeros((N, NA_HEADS * NA_HEAD_DIM), BF16)
        na_o = _dense_attention(p, na_o, n_seq=Bp, T=Tp)
        bias = _na_bias_table(na_rpb[l], Ts // GRID_W)
        na_o = _neighbourhood_attention(p, nk, nv, bias, na_o, row0=Np, n_seq=Bs, T=Ts, l=l)

        x1, h2, route, counts = _merge_route(pool_o, dn, na_o, x, mod4, g[1:2], g[2:3], w_out_bf[l],
                                             rw_pad[l], rb_pad[l], tm=MERGE_ROWS, l=l,
                                             ctx_blocks=Np // MERGE_ROWS, blocks_per_seq=Ts // MERGE_ROWS)
        plan, block_e, n_used, block_rows = _route_plan(counts, E, nblk)
        xs = _dispatch(plan, n_used, route, h2, nblk * MOE_ROWS, E)
        yb = _experts(block_e, n_used, block_rows, xs, moe_w1, moe_b1, moe_w2, moe_b2, l=l)
        x = _combine(plan, yb, route, x1, mod4, g[3:4], n_exp=E, l=l, split=l == L - 1, **blocks)

        new_dk.append(dk_l.reshape(Bp, Tp, DIFF_HEADS, 2 * DIFF_QK_DIM))
        new_dv.append(dv_l.reshape(Bp, Tp, DIFF_HEADS, DIFF_V_DIM))
        new_nk.append(nk_l.reshape(Bp, Tp, NA_HEADS, NA_HEAD_DIM))
        new_nv.append(nv_l.reshape(Bp, Tp, NA_HEADS, NA_HEAD_DIM))

    return (x[0].reshape(Bp, Tp, D), x[1].reshape(Bs, Ts, D),
            jnp.stack(new_dk, axis=1), jnp.stack(new_dv, axis=1),
            jnp.stack(new_nk, axis=1), jnp.stack(new_nv, axis=1))
```

```python
import functools
import math

import numpy as np
import jax
import jax.numpy as jnp
from jax import lax
from jax.experimental import pallas as pl
from jax.experimental.pallas import tpu as pltpu

F32 = jnp.float32
BF16 = jnp.bfloat16
I32 = jnp.int32

GRID_W = 64
POOL_GROUPS = 4
POOL_MAX_HALF = 8
DIFF_HEADS = 4
DIFF_V_DIM = 128
DIFF_QK_DIM = 64
NA_HEADS = 4
NA_HEAD_DIM = 64
NA_WIN_H = 8
NA_WIN_W = 16
NA_Q_ROWS = 4
NA_BAND = 12
TOP_K = 4
SWIGLU_ALPHA = 1.702
SWIGLU_LIMIT = 7.0
ROPE_BASE = 10000.0
NORM_EPS = 1e-6
N_MOD = 6

LOG2E = 1.4426950408889634
MASKED = -1e30
LANES = 128
SUBLANES = 8
BF16_ROWS = 16
MXU_TILE = 256
MOE_ROWS = 512
COL_DQ, COL_DK, COL_DV = 256, 768, 1280
COL_NQ, COL_NK, COL_NV = 1792, 2048, 2304
DIFF_KEYS = 256
PROJ_ROWS = 512
MERGE_ROWS = 1024
DIFF_Q_ROWS = 256
TOK_CHUNK = 256
RUN_ALIGN = SUBLANES
RUN_LARGE = 128
RUN_UNROLL = 4
VMEM_LIMIT = 56 * 1024 * 1024


def _sorted_rows(n_exp):
    rows = TOK_CHUNK * TOP_K + n_exp * (RUN_ALIGN - 1)
    return -(-rows // MXU_TILE) * MXU_TILE


def _params(sem, vmem=VMEM_LIMIT):
    return pltpu.CompilerParams(dimension_semantics=sem, vmem_limit_bytes=vmem)


def _dot(a, b):
    return jnp.dot(a, b, preferred_element_type=F32)


def _dot_nt(a, b):
    return lax.dot_general(a, b, (((1,), (1,)), ((), ())), preferred_element_type=F32)


def _split(x):
    hi = x.astype(BF16)
    return hi, (x - hi.astype(F32)).astype(BF16)


def _dot3(a, b):
    ah, al = _split(a)
    bh, bl = _split(b)
    return _dot(ah, bh) + _dot(al, bh) + _dot(ah, bl)


def _rms(x):
    return x * lax.rsqrt(jnp.mean(x * x, axis=-1, keepdims=True) + NORM_EPS)


def _mod_kernel(c_ref, w_ref, b_ref, o_ref):
    c = c_ref[...]
    a = c * (1.0 / (1.0 + jnp.exp(-c)))
    o_ref[...] = _dot3(a, w_ref[...]) + b_ref[...]


def _modulation(cvec, w_ada, b_ada):
    L, D, W = w_ada.shape
    G = cvec.shape[0]
    return pl.pallas_call(
        _mod_kernel,
        grid=(L, W // D),
        in_specs=[pl.BlockSpec((G, D), lambda l, j: (0, 0)),
                  pl.BlockSpec((None, D, D), lambda l, j: (l, 0, j)),
                  pl.BlockSpec((None, 1, D), lambda l, j: (l, 0, j))],
        out_specs=pl.BlockSpec((None, G, D), lambda l, j: (l, 0, j)),
        out_shape=jax.ShapeDtypeStruct((L, G, W), F32),
        compiler_params=_params(("arbitrary", "arbitrary")),
        name="ada_modulation",
    )(cvec, w_ada, b_ada.reshape(L, 1, W))


def _token_rows(x_refs, ctx_blocks):
    if len(x_refs) == 1:
        return lambda rows=slice(None): x_refs[0][rows, :]
    is_ctx = pl.program_id(0) < ctx_blocks
    return lambda rows=slice(None): jnp.where(is_ctx, x_refs[0][rows, :], x_refs[1][rows, :])


def _token_specs(x, tm, ctx_blocks):
    if not isinstance(x, tuple):
        return [x], [pl.BlockSpec((tm, x.shape[1]), lambda i: (i, 0))]
    D = x[0].shape[1]
    return list(x), [pl.BlockSpec((tm, D), lambda i: (jnp.minimum(i, ctx_blocks - 1), 0)),
                     pl.BlockSpec((tm, D), lambda i: (jnp.maximum(i - ctx_blocks, 0), 0))]


def _inproj_kernel(*refs, D, rope_lo, rope_hi, ctx_blocks):
    mod_ref, g_ref, w_ref, cos_ref, sin_ref, o_ref, dk_ref, dv_ref, nk_ref, nv_ref = refs[-10:]
    h = _rms(_token_rows(refs[:-10], ctx_blocks)()) * g_ref[...]
    h = h * (1.0 + mod_ref[:, D:2 * D]) + mod_ref[:, 0:D]
    p = _dot(h.astype(BF16), w_ref[...])
    W = p.shape[1]
    o_ref[:, 0:rope_lo] = p[:, 0:rope_lo].astype(o_ref.dtype)
    o_ref[:, rope_hi:W] = p[:, rope_hi:W].astype(o_ref.dtype)
    cos = cos_ref[...]
    sin = sin_ref[...]
    lane = lax.broadcasted_iota(I32, cos.shape, 1)
    rot = DIFF_QK_DIM // 4
    first = (lane % (2 * rot)) < rot
    for c0 in range(rope_lo, rope_hi, LANES):
        xc = p[:, c0:c0 + LANES]
        partner = jnp.where(first, pltpu.roll(xc, LANES - rot, 1), pltpu.roll(xc, rot, 1))
        o_ref[:, c0:c0 + LANES] = (xc * cos + partner * sin).astype(o_ref.dtype)

    @pl.when(pl.program_id(0) < ctx_blocks)
    def _():
        dk_ref[...] = p[:, COL_DK:COL_DV]
        dv_ref[...] = p[:, COL_DV:COL_NQ]
        nk_ref[...] = p[:, COL_NK:COL_NV]
        nv_ref[...] = p[:, COL_NV:W]


def _in_projection(x, mod4, gain, w_bf, cos_t, sin_t, *, tm, ctx_blocks, blocks_per_seq, l):
    xs, x_specs = _token_specs(x, tm, ctx_blocks)
    N, D = sum(a.shape[0] for a in xs), xs[0].shape[1]
    W = w_bf.shape[1]

    def grp(i):
        return jnp.where(i < ctx_blocks, 0, 1 + (i - ctx_blocks) // blocks_per_seq)

    def rope_blk(i):
        return jnp.where(i < ctx_blocks, 0, 1 + (i - ctx_blocks) % blocks_per_seq)

    kern = functools.partial(_inproj_kernel, D=D, rope_lo=COL_DQ, rope_hi=COL_DV, ctx_blocks=ctx_blocks)
    ctx_rows = ctx_blocks * tm
    ctx_out = lambda w: pl.BlockSpec((tm, w), lambda i: (jnp.minimum(i, ctx_blocks - 1), 0))
    widths = (COL_DV - COL_DK, COL_NQ - COL_DV, COL_NV - COL_NK, W - COL_NV)
    return pl.pallas_call(
        kern,
        grid=(N // tm,),
        in_specs=x_specs + [
                  pl.BlockSpec((None, None, 1, N_MOD * D), lambda i: (l, grp(i), 0, 0)),
                  pl.BlockSpec((1, D), lambda i: (0, 0)),
                  pl.BlockSpec((D, W), lambda i: (0, 0)),
                  pl.BlockSpec((tm, LANES), lambda i: (rope_blk(i), 0)),
                  pl.BlockSpec((tm, LANES), lambda i: (rope_blk(i), 0))],
        out_specs=[pl.BlockSpec((tm, W), lambda i: (i, 0))] + [ctx_out(w) for w in widths],
        out_shape=[jax.ShapeDtypeStruct((N, W), BF16)] + [jax.ShapeDtypeStruct((ctx_rows, w), F32) for w in widths],
        compiler_params=_params(("arbitrary",)),
        name="in_projection",
    )(*xs, mod4, gain, w_bf, cos_t, sin_t)


def _pool_kernel(u_ref, w_ref, sc_ref, o_ref, pad_ref, *, T, CH):
    H = 2 * POOL_MAX_HALF
    zeros = jnp.zeros((H, pad_ref.shape[1]), F32)
    pad_ref[0:H, :] = zeros
    pad_ref[H + T:2 * H + T, :] = zeros
    pad_ref[H:H + T, :] = u_ref[...].astype(F32)
    C = pad_ref.shape[1]
    lane = lax.broadcasted_iota(I32, (CH, C), 1)
    group = lane // (C // POOL_GROUPS)
    half = jnp.left_shift(1, group)
    row = lax.broadcasted_iota(I32, (CH, C), 0)
    R = CH + 2 * H

    def body(ci, carry):
        base = pl.multiple_of(ci * CH, CH)
        win = pad_ref[pl.ds(base, R), :]
        run = win
        acc = None
        for g in range(POOL_GROUPS):
            run = run + pltpu.roll(run, R - (1 << g), 0)
            start = H - (1 << g)
            part = run[start:start + CH, :]
            acc = part if acc is None else jnp.where(group == g, part, acc)
        t = row + base
        cnt = jnp.minimum(t + half, T) - jnp.maximum(t - half, 0)
        d = acc / cnt.astype(F32) - win[H:H + CH, :]
        o_ref[pl.ds(base, CH), :] = (_dot(d.astype(BF16), w_ref[...]) * sc_ref[...]).astype(o_ref.dtype)
        return carry

    lax.fori_loop(0, T // CH, body, 0)


def _pool(p, w_bd, scale, *, row0, n_seq, T, out):
    C = w_bd.shape[0]
    CH = min(T, 256)
    blk0 = row0 // T

    def kern(u_ref, w_ref, sc_ref, prev_ref, o_ref, pad_ref):
        del prev_ref
        _pool_kernel(u_ref, w_ref, sc_ref, o_ref, pad_ref, T=T, CH=CH)

    return pl.pallas_call(
        kern,
        grid=(n_seq,),
        in_specs=[pl.BlockSpec((T, C), lambda s: (blk0 + s, 0)),
                  pl.BlockSpec((C, C), lambda s: (0, 0)),
                  pl.BlockSpec((1, C), lambda s: (0, 0)),
                  pl.BlockSpec(memory_space=pl.ANY)],
        out_specs=pl.BlockSpec((T, C), lambda s: (blk0 + s, 0)),
        out_shape=jax.ShapeDtypeStruct(out.shape, out.dtype),
        scratch_shapes=[pltpu.VMEM((T + 4 * POOL_MAX_HALF, C), F32)],
        input_output_aliases={3: 0},
        compiler_params=_params(("arbitrary",)),
        name="pool_mixer",
    )(p, w_bd, scale, out)


def _softmax_pv(s, v):
    m = jnp.max(s, axis=-1, keepdims=True)
    e = jnp.exp2(s - m)
    l = jnp.sum(e, axis=-1, keepdims=True)
    return _dot(e.astype(BF16), v) * (1.0 / l)


def _diff_kernel(*refs, Ts, Lc, lam_init, nq, n_units):
    if Lc:
        lam_ref, sub_ref, q_ref, ks_ref, vs_ref, kc_ref, vc_ref, prev_ref, o_ref, kb, vt, s_a, s_b, m_a, m_b = refs
    else:
        lam_ref, sub_ref, q_ref, ks_ref, vs_ref, prev_ref, o_ref, kb, vt, s_a, s_b, m_a, m_b = refs
    del prev_ref
    t = pl.program_id(0)
    S = Ts + Lc
    dv = DIFF_V_DIM
    head_new = jnp.minimum(t, n_units - 1) // nq
    slot_new = head_new % 2
    slot_old = (jnp.maximum(t - 1, 0) // nq) % 2

    @pl.when((t % nq == 0) & (t < n_units))
    def _():
        kb[0:Ts, :] = ks_ref[...].astype(BF16)
        vt[slot_new, 0:dv, 0:Ts] = vs_ref[...].astype(F32).T.astype(BF16)
        if Lc:
            kb[Ts:S, :] = kc_ref[...].astype(BF16)
            vt[slot_new, 0:dv, Ts:S] = vc_ref[...].T.astype(BF16)
        vt[slot_new, dv:, :] = jnp.ones((vt.shape[1] - dv, S), BF16)

    @pl.when(t == 0)
    def _():
        s_b[...] = jnp.zeros(s_b.shape, F32)
        m_b[...] = jnp.zeros(m_b.shape, F32)

    lm = lam_ref[...]
    lam = (jnp.exp(jnp.sum(lm[0:1, :] * lm[1:2, :], axis=-1, keepdims=True))
           - jnp.exp(jnp.sum(lm[2:3, :] * lm[3:4, :], axis=-1, keepdims=True)) + lam_init)

    def stage(s_new, m_new, s_old, m_old):
        q = q_ref[...].astype(F32) * (DIFF_QK_DIM ** -0.5 * LOG2E)
        tq = q.shape[0]
        part = lax.broadcasted_iota(I32, q.shape, 1) // DIFF_QK_DIM
        qm = [jnp.where(part == u, q, 0.0).astype(BF16) for u in range(2)]
        top = [m_old[u][0:1, :] for u in range(2)]
        acc = [jnp.zeros((vt.shape[1], tq), F32) for _ in range(2)]
        run = [jnp.full((SUBLANES, tq), -jnp.inf, F32) for _ in range(2)]
        for c0 in range(0, S, DIFF_KEYS):
            keys = slice(c0, c0 + DIFF_KEYS)
            for u in range(2):
                sc = _dot_nt(kb[keys, :], qm[u])
                s_new[u, keys, :] = sc
                for j in range(0, DIFF_KEYS, SUBLANES):
                    run[u] = jnp.maximum(run[u], sc[j:j + SUBLANES, :])
                e = jnp.exp2(s_old[u, keys, :] - top[u])
                acc[u] = acc[u] + _dot(vt[slot_old, :, keys], e.astype(BF16))
        for u in range(2):
            m_new[u] = jnp.broadcast_to(jnp.max(run[u], axis=0, keepdims=True), m_new.shape[1:])
        o = (acc[0][0:dv, :] * (1.0 / acc[0][dv:dv + 1, :])
             - lam * (acc[1][0:dv, :] * (1.0 / acc[1][dv:dv + 1, :])))
        o_ref[...] = (_rms(o.T) * sub_ref[...] * (1.0 - lam_init)).astype(o_ref.dtype)

    @pl.when(t % 2 == 0)
    def _():
        stage(s_a, m_a, s_b, m_b)

    @pl.when(t % 2 == 1)
    def _():
        stage(s_b, m_b, s_a, m_a)


def _diff_ctx_kernel(lam_ref, sub_ref, q_ref, k_ref, v_ref, prev_ref, o_ref, *, lam_init, heads):
    del prev_ref
    lm = lam_ref[...]
    lam = (jnp.exp(jnp.sum(lm[0:1, :] * lm[1:2, :], axis=-1, keepdims=True))
           - jnp.exp(jnp.sum(lm[2:3, :] * lm[3:4, :], axis=-1, keepdims=True)) + lam_init)
    q = q_ref[...].astype(F32) * (DIFF_QK_DIM ** -0.5 * LOG2E)
    part = lax.broadcasted_iota(I32, q.shape, 1) // DIFF_QK_DIM
    k = k_ref[...].astype(BF16)
    v = v_ref[...].astype(BF16)
    for hh in range(heads):
        o1 = _softmax_pv(_dot_nt(jnp.where(part == 2 * hh, q, 0.0).astype(BF16), k), v)
        o2 = _softmax_pv(_dot_nt(jnp.where(part == 2 * hh + 1, q, 0.0).astype(BF16), k), v)
        cols = slice(hh * DIFF_V_DIM, (hh + 1) * DIFF_V_DIM)
        o = o1[:, cols] - lam * o2[:, cols]
        o_ref[:, cols] = (_rms(o) * sub_ref[...] * (1.0 - lam_init)).astype(o_ref.dtype)


def _diff_attention_ctx(p, lam_l, subln, out, *, n_seq, T, lam_init):
    heads = 2
    W = heads * DIFF_V_DIM
    spec = lambda col0: pl.BlockSpec((T, W), lambda b, h: (b, col0 // W + h))
    return pl.pallas_call(
        functools.partial(_diff_ctx_kernel, lam_init=lam_init, heads=heads),
        grid=(n_seq, DIFF_HEADS // heads),
        in_specs=[pl.BlockSpec((4, DIFF_QK_DIM), lambda b, h: (0, 0)),
                  pl.BlockSpec((1, DIFF_V_DIM), lambda b, h: (0, 0)),
                  spec(COL_DQ), spec(COL_DK), spec(COL_DV),
                  pl.BlockSpec(memory_space=pl.ANY)],
        out_specs=pl.BlockSpec((T, W), lambda b, h: (b, h)),
        out_shape=jax.ShapeDtypeStruct(out.shape, out.dtype),
        input_output_aliases={5: 0},
        compiler_params=_params(("arbitrary", "arbitrary")),
        name="diff_attention_ctx",
    )(lam_l, subln, p, p, p, out)


def _diff_attention(p, lam_l, subln, out, *, row0, n_seq, T, tq, lam_init, cache_k=None, cache_v=None, l=0):
    W = DIFF_V_DIM
    assert 2 * DIFF_QK_DIM == W == LANES
    Lc = 0 if cache_k is None else cache_k.shape[2]
    S = T + Lc
    assert S % DIFF_KEYS == 0
    nq = T // tq
    qb0 = row0 // tq
    sb0 = row0 // T
    n_units = n_seq * DIFF_HEADS * nq

    def unit(t):
        return t // (DIFF_HEADS * nq), (t // nq) % DIFF_HEADS, t % nq

    def scored(t, col0):
        b, h, i = unit(jnp.minimum(t, n_units - 1))
        return b, h, i, col0 // W + h

    in_specs = [pl.BlockSpec((4, DIFF_QK_DIM), lambda t: (0, 0)),
                pl.BlockSpec((1, W), lambda t: (0, 0)),
                pl.BlockSpec((tq, W), lambda t: (qb0 + scored(t, 0)[0] * nq + scored(t, 0)[2], scored(t, COL_DQ)[3])),
                pl.BlockSpec((T, W), lambda t: (sb0 + scored(t, 0)[0], scored(t, COL_DK)[3])),
                pl.BlockSpec((T, W), lambda t: (sb0 + scored(t, 0)[0], scored(t, COL_DV)[3]))]
    args = [lam_l, subln, p, p, p]
    if Lc:
        in_specs += [pl.BlockSpec((None, None, Lc, W), lambda t: (scored(t, 0)[0], l, 0, scored(t, 0)[1])),
                     pl.BlockSpec((None, None, Lc, W), lambda t: (scored(t, 0)[0], l, 0, scored(t, 0)[1]))]
        args += [cache_k, cache_v]
    in_specs.append(pl.BlockSpec(memory_space=pl.ANY))
    args.append(out)
    scores = pltpu.VMEM((2, S, tq), F32)
    row_max = pltpu.VMEM((2, SUBLANES, tq), F32)
    vt_rows = W + BF16_ROWS

    def finished(t):
        b, h, i = unit(jnp.maximum(t - 1, 0))
        return qb0 + b * nq + i, h

    return pl.pallas_call(
        functools.partial(_diff_kernel, Ts=T, Lc=Lc, lam_init=lam_init, nq=nq, n_units=n_units),
        grid=(n_units + 1,),
        in_specs=in_specs,
        out_specs=pl.BlockSpec((tq, W), finished),
        out_shape=jax.ShapeDtypeStruct(out.shape, out.dtype),
        scratch_shapes=[pltpu.VMEM((S, W), BF16), pltpu.VMEM((2, vt_rows, S), BF16),
                        scores, scores, row_max, row_max],
        input_output_aliases={len(args) - 1: 0},
        compiler_params=_params(("arbitrary",)),
        name="diff_attention",
    )(*args)


def _na_heads(q, score_fn, pv_fn):
    lane = lax.broadcasted_iota(I32, q.shape, 1) // NA_HEAD_DIM
    out = jnp.zeros(q.shape, F32)
    for h in range(NA_HEADS):
        qh = jnp.where(lane == h, q, 0.0).astype(BF16)
        out = jnp.where(lane == h, pv_fn(score_fn(qh, h)), out)
    return out


def _dense_kernel(q_ref, k_ref, v_ref, prev_ref, o_ref):
    del prev_ref
    q = q_ref[...].astype(F32) * (NA_HEAD_DIM ** -0.5 * LOG2E)
    k = k_ref[...].astype(BF16)
    v = v_ref[...].astype(BF16)
    o_ref[...] = _na_heads(q, lambda qh, h: _dot_nt(qh, k), lambda s: _softmax_pv(s, v)).astype(o_ref.dtype)


def _dense_attention(p, out, *, n_seq, T):
    C = NA_HEADS * NA_HEAD_DIM
    return pl.pallas_call(
        _dense_kernel,
        grid=(n_seq,),
        in_specs=[pl.BlockSpec((T, C), lambda b: (b, COL_NQ // C)),
                  pl.BlockSpec((T, C), lambda b: (b, COL_NK // C)),
                  pl.BlockSpec((T, C), lambda b: (b, COL_NV // C)),
                  pl.BlockSpec(memory_space=pl.ANY)],
        out_specs=pl.BlockSpec((T, C), lambda b: (b, 0)),
        out_shape=jax.ShapeDtypeStruct(out.shape, out.dtype),
        input_output_aliases={3: 0},
        compiler_params=_params(("arbitrary",)),
        name="dense_attention",
    )(p, p, p, out)


def _na_kernel(q_ref, ks_ref, vs_ref, kc_ref, vc_ref, bias_ref, prev_ref, o_ref, *, rows):
    del prev_ref
    r0 = pl.program_id(1) * NA_Q_ROWS
    bs = jnp.clip(r0 - NA_WIN_H // 2, 0, rows - NA_BAND)
    start = pl.multiple_of(bs * GRID_W, GRID_W)
    nb = NA_BAND * GRID_W
    kb = ks_ref[pl.ds(start, nb), :].astype(BF16)
    vb = vs_ref[pl.ds(start, nb), :].astype(BF16)
    kc = kc_ref[...].astype(BF16)
    vc = vc_ref[...].astype(BF16)
    q = q_ref[...].astype(F32) * (NA_HEAD_DIM ** -0.5 * LOG2E)

    def scores(qh, h):
        return _dot_nt(qh, kb) + bias_ref[h], _dot_nt(qh, kc)

    def pv(s):
        s_loc, s_ctx = s
        m = jnp.maximum(jnp.max(s_loc, axis=-1, keepdims=True), jnp.max(s_ctx, axis=-1, keepdims=True))
        e_loc = jnp.exp2(s_loc - m)
        e_ctx = jnp.exp2(s_ctx - m)
        l = jnp.sum(e_loc, axis=-1, keepdims=True) + jnp.sum(e_ctx, axis=-1, keepdims=True)
        return (_dot(e_loc.astype(BF16), vb) + _dot(e_ctx.astype(BF16), vc)) * (1.0 / l)

    o_ref[...] = _na_heads(q, scores, pv).astype(o_ref.dtype)


def _na_bias_table(rpb_l, rows):
    n_ro, n_co = 2 * NA_WIN_H - 1, 2 * NA_WIN_W - 1
    c = np.arange(GRID_W)[:, None]
    kc = np.arange(GRID_W)[None, :]
    cs = np.clip(c - NA_WIN_W // 2, 0, GRID_W - NA_WIN_W)
    col_ok = (kc >= cs) & (kc < cs + NA_WIN_W)
    co = kc - c + (NA_WIN_W - 1)
    pick = ((np.arange(n_co)[:, None, None] == co[None]) & col_ok[None]).astype(np.float32)
    toep = jnp.dot(rpb_l.reshape(NA_HEADS * n_ro, n_co).astype(F32), jnp.asarray(pick.reshape(n_co, -1)),
                   precision=lax.Precision.HIGHEST).reshape(NA_HEADS, n_ro, GRID_W, GRID_W) * LOG2E
    toep = jnp.where(jnp.asarray(col_ok)[None, None], toep, MASKED)
    toep = jnp.concatenate([toep, jnp.full((NA_HEADS, 1, GRID_W, GRID_W), MASKED, F32)], axis=1)
    blk = np.full((3, NA_Q_ROWS, NA_BAND), n_ro, np.int32)
    for v, r0 in enumerate((0, NA_Q_ROWS, rows - NA_Q_ROWS)):
        bs = int(np.clip(r0 - NA_WIN_H // 2, 0, rows - NA_BAND))
        for j in range(NA_Q_ROWS):
            rs = int(np.clip(r0 + j - NA_WIN_H // 2, 0, rows - NA_WIN_H))
            for i in range(NA_BAND):
                if rs <= bs + i < rs + NA_WIN_H:
                    blk[v, j, i] = bs + i - (r0 + j) + (NA_WIN_H - 1)
    tab = toep[:, blk]
    tab = jnp.transpose(tab, (1, 0, 2, 4, 3, 5))
    return tab.reshape(3, NA_HEADS, NA_Q_ROWS * GRID_W, NA_BAND * GRID_W)


def _neighbourhood_attention(p, cache_k, cache_v, bias, out, *, row0, n_seq, T, l):
    C = NA_HEADS * NA_HEAD_DIM
    rows = T // GRID_W
    tq = NA_Q_ROWS * GRID_W
    nq = T // tq
    Lc = cache_k.shape[2]
    qb0 = row0 // tq
    sb0 = row0 // T

    def variant(i):
        r0 = i * NA_Q_ROWS
        return (r0 - jnp.clip(r0 - NA_WIN_H // 2, 0, rows - NA_BAND)) // NA_Q_ROWS

    return pl.pallas_call(
        functools.partial(_na_kernel, rows=rows),
        grid=(n_seq, nq),
        in_specs=[pl.BlockSpec((tq, C), lambda b, i: (qb0 + b * nq + i, COL_NQ // C)),
                  pl.BlockSpec((T, C), lambda b, i: (sb0 + b, COL_NK // C)),
                  pl.BlockSpec((T, C), lambda b, i: (sb0 + b, COL_NV // C)),
                  pl.BlockSpec((None, None, Lc, C), lambda b, i: (b, l, 0, 0)),
                  pl.BlockSpec((None, None, Lc, C), lambda b, i: (b, l, 0, 0)),
                  pl.BlockSpec((None, NA_HEADS, tq, NA_BAND * GRID_W), lambda b, i: (variant(i), 0, 0, 0)),
                  pl.BlockSpec(memory_space=pl.ANY)],
        out_specs=pl.BlockSpec((tq, C), lambda b, i: (qb0 + b * nq + i, 0)),
        out_shape=jax.ShapeDtypeStruct(out.shape, out.dtype),
        input_output_aliases={6: 0},
        compiler_params=_params(("arbitrary", "arbitrary")),
        name="neighbourhood_attention",
    )(p, p, p, cache_k, cache_v, bias, out)


def _merge_kernel(pool_ref, dn_ref, na_ref, *refs, D, ctx_blocks):
    mod_ref, g1_ref, g2_ref, w_ref, rw_ref, rb_ref, x1_ref, h2_ref, route_ref, cnt_ref = refs[-10:]
    x_rows = _token_rows(refs[:-10], ctx_blocks)
    parts = [slice(j * TOK_CHUNK, (j + 1) * TOK_CHUNK) for j in range(x1_ref.shape[0] // TOK_CHUNK)]
    c0 = pool_ref.shape[1]
    c1 = c0 + dn_ref.shape[1]
    mix = [_dot(pool_ref[p, :].astype(BF16), w_ref[0:c0, :])
           + _dot(dn_ref[p, :].astype(BF16), w_ref[c0:c1, :])
           + _dot(na_ref[p, :].astype(BF16), w_ref[c1:, :]) for p in parts]
    x1 = [x_rows(p) + mod_ref[:, 2 * D:3 * D] * (_rms(m) * g1_ref[...]) for p, m in zip(parts, mix)]
    for p, v in zip(parts, x1):
        x1_ref[p, :] = v
    h2 = [(_rms(v) * g2_ref[...]) * (1.0 + mod_ref[:, 4 * D:5 * D]) + mod_ref[:, 3 * D:4 * D] for v in x1]
    for p, v in zip(parts, h2):
        h2_ref[p, :] = v.astype(h2_ref.dtype)

    rw_hi, rw_lo = _split(rw_ref[...])
    h2_split = [_split(v) for v in h2]
    logits = [_dot(hi, rw_hi) + _dot(lo, rw_hi) + _dot(hi, rw_lo) + rb_ref[...] for hi, lo in h2_split]
    lane = lax.broadcasted_iota(I32, (TOK_CHUNK, LANES), 1)
    lane_f = lane.astype(F32)
    work = logits
    vals, hots = [], []
    for _ in range(TOP_K):
        mx = [jnp.max(w, axis=-1, keepdims=True) for w in work]
        idx = [jnp.min(jnp.where(w == m, lane_f, float(LANES)), axis=-1, keepdims=True)
               for w, m in zip(work, mx)]
        hot = [lane_f == i for i in idx]
        vals.append(mx)
        hots.append(hot)
        work = [jnp.where(h, -jnp.inf, w) for h, w in zip(hot, work)]

    r = lax.broadcasted_iota(I32, (TOK_CHUNK, TOK_CHUNK), 0)
    c = lax.broadcasted_iota(I32, (TOK_CHUNK, TOK_CHUNK), 1)
    before = jnp.where(c < r, 1.0, 0.0).astype(BF16)
    er = lax.broadcasted_iota(I32, (LANES, LANES), 0)
    ec = lax.broadcasted_iota(I32, (LANES, LANES), 1)
    earlier = jnp.where(er < ec, 1.0, 0.0).astype(BF16)
    for j, p in enumerate(parts):
        hot_j = [hots[k][j] for k in range(TOP_K)]
        es = [jnp.exp(vals[k][j] - vals[0][j]) for k in range(TOP_K)]
        inv = 1.0 / (es[0] + es[1] + es[2] + es[3])
        sel = jnp.zeros((TOK_CHUNK, LANES), F32)
        for hot in hot_j:
            sel = jnp.where(hot, 1.0, sel)
        rank = _dot(before, sel.astype(BF16))
        cnt = jnp.sum(sel, axis=0, keepdims=True)
        cnt_ref[j] = cnt
        run = jnp.floor((cnt + (RUN_ALIGN - 1)) * (1.0 / RUN_ALIGN)) * RUN_ALIGN
        run_start = _dot(jnp.broadcast_to(run, (SUBLANES, LANES)).astype(BF16), earlier)[0:1, :]
        pos = rank + run_start
        route = jnp.zeros((TOK_CHUNK, LANES), F32)
        for k in range(TOP_K):
            e_k = jnp.sum(jnp.where(hot_j[k], lane_f, 0.0), axis=-1, keepdims=True)
            p_k = jnp.sum(jnp.where(hot_j[k], pos, 0.0), axis=-1, keepdims=True)
            route = jnp.where(lane == k, e_k, route)
            route = jnp.where(lane == TOP_K + k, p_k, route)
            route = jnp.where(lane == 2 * TOP_K + k, es[k] * inv, route)
        route_ref[p, :] = route


def _merge_route(pool_o, dn, na_o, x, mod4, g1, g2, w_out_bf, rw_pad, rb_pad, *, tm, ctx_blocks,
                 blocks_per_seq, l):
    xs, x_specs = _token_specs(x, tm, ctx_blocks)
    N, D = sum(a.shape[0] for a in xs), xs[0].shape[1]
    per_step = tm // TOK_CHUNK

    def grp(i):
        return jnp.where(i < ctx_blocks, 0, 1 + (i - ctx_blocks) // blocks_per_seq)

    row = lambda w: pl.BlockSpec((tm, w), lambda i: (i, 0))
    full = lambda a: pl.BlockSpec(a.shape, lambda i: (0,) * a.ndim)
    return pl.pallas_call(
        functools.partial(_merge_kernel, D=D, ctx_blocks=ctx_blocks),
        grid=(N // tm,),
        in_specs=[row(pool_o.shape[1]), row(dn.shape[1]), row(na_o.shape[1])] + x_specs + [
                  pl.BlockSpec((None, None, 1, N_MOD * D), lambda i: (l, grp(i), 0, 0)),
                  full(g1), full(g2), full(w_out_bf), full(rw_pad), full(rb_pad)],
        out_specs=[row(D), row(D), row(LANES), pl.BlockSpec((per_step, 1, LANES), lambda i: (i, 0, 0))],
        out_shape=[jax.ShapeDtypeStruct((N, D), F32), jax.ShapeDtypeStruct((N, D), BF16),
                   jax.ShapeDtypeStruct((N, LANES), F32), jax.ShapeDtypeStruct((N // TOK_CHUNK, 1, LANES), F32)],
        compiler_params=_params(("arbitrary",)),
        name="merge_route",
    )(pool_o, dn, na_o, *xs, mod4, g1, g2, w_out_bf, rw_pad, rb_pad)


def _pack_pairs(x):
    C = x.shape[1] // 2
    bits = lax.bitcast_convert_type(x.astype(BF16).astype(F32), jnp.uint32)
    return bits[:, C:] | (bits[:, :C] >> 16)


def _unpack_pairs(w):
    lo = lax.bitcast_convert_type(w << 16, F32)
    hi = lax.bitcast_convert_type(w & jnp.uint32(0xFFFF0000), F32)
    return jnp.concatenate([lo, hi], axis=1).astype(BF16)


def _run_sizes(lo, hi):
    return [1 << k for k in range(hi.bit_length() - 1, lo.bit_length() - 2, -1)]


def _run_copies(n, src_at, dst_at, sem, wait, sizes):
    for size in sizes:
        @pl.when((n & size) != 0)
        def _(size=size):
            off = n & -(2 * size)
            cp = pltpu.make_async_copy(src_at(off, size), dst_at(off, size), sem)
            cp.wait() if wait else cp.start(priority=int(size < 4 * RUN_ALIGN))


def _block_runs(step, n_exp, run_ref, src_ref, dst_ref, large_ref, hbm_ref, buf, sem, *, to_hbm, wait):
    def each_run(sizes):
        def body(e, carry):
            j = step * n_exp + e
            so = src_ref[j]
            do = dst_ref[j]
            in_buf = lambda o, s: buf.at[pl.ds(pl.multiple_of(so + o, RUN_ALIGN), s)]
            in_hbm = lambda o, s: hbm_ref.at[pl.ds(pl.multiple_of(do + o, RUN_ALIGN), s)]
            if to_hbm:
                _run_copies(run_ref[j], in_buf, in_hbm, sem, wait, sizes)
            else:
                _run_copies(run_ref[j], in_hbm, in_buf, sem, wait, sizes)
            return carry
        lax.fori_loop(0, n_exp, body, 0, unroll=RUN_UNROLL)

    each_run(_run_sizes(RUN_ALIGN, RUN_LARGE // 2))

    @pl.when(large_ref[step] != 0)
    def _():
        each_run(_run_sizes(RUN_LARGE, TOK_CHUNK))


def _dispatch_kernel(run_ref, src_ref, dst_ref, large_ref, tail_ref, taildst_ref, nu_ref, route_ref, h_ref, xs_ref,
                     sorted_buf, zero_buf, sems, *, n_exp):
    b = pl.program_id(0)
    nb = pl.num_programs(0)
    slot = b % 2
    runs = functools.partial(_block_runs, n_exp=n_exp, run_ref=run_ref, src_ref=src_ref, dst_ref=dst_ref,
                             large_ref=large_ref, hbm_ref=xs_ref, to_hbm=True)

    @pl.when(b >= 2)
    def _():
        runs(b - 2, buf=sorted_buf.at[slot], sem=sems.at[slot], wait=True)

    route = route_ref[...]
    col = lax.broadcasted_iota(I32, (TOK_CHUNK, sorted_buf.shape[1]), 1).astype(F32)
    place = jnp.zeros(col.shape, F32)
    for k in range(TOP_K):
        place = jnp.where(col == route[:, TOP_K + k:TOP_K + k + 1], 1.0, place)
    srt = lax.dot_general(place.astype(BF16), h_ref[...].astype(BF16), (((0,), (0,)), ((), ())),
                          preferred_element_type=F32)
    sorted_buf[slot] = _pack_pairs(srt)
    runs(b, buf=sorted_buf.at[slot], sem=sems.at[slot], wait=False)

    @pl.when(b == nb - 1)
    def _():
        @pl.when(b >= 1)
        def _():
            runs(b - 1, buf=sorted_buf.at[1 - slot], sem=sems.at[1 - slot], wait=True)
        runs(b, buf=sorted_buf.at[slot], sem=sems.at[slot], wait=True)

        zero_buf[...] = jnp.zeros(zero_buf.shape, zero_buf.dtype)
        sem = sems.at[0]

        def each_tail(wait):
            def body(e, carry):
                do = taildst_ref[e]
                _run_copies(tail_ref[e],
                            lambda o, s: zero_buf.at[pl.ds(0, s)],
                            lambda o, s: xs_ref.at[pl.ds(pl.multiple_of(do + o, RUN_ALIGN), s)], sem, wait,
                            _run_sizes(RUN_ALIGN, MOE_ROWS // 2))
                return carry
            lax.fori_loop(0, n_exp, body, 0)

        def spare_block(wait):
            def body(i, carry):
                cp = pltpu.make_async_copy(zero_buf.at[pl.ds(0, MOE_ROWS)],
                                           xs_ref.at[pl.ds(pl.multiple_of(i * MOE_ROWS, MOE_ROWS), MOE_ROWS)], sem)
                cp.wait() if wait else cp.start()
                return carry
            lax.fori_loop(nu_ref[0], xs_ref.shape[0] // MOE_ROWS, body, 0)

        each_tail(False)
        spare_block(False)
        each_tail(True)
        spare_block(True)


def _dispatch(plan, n_used, route, h2, n_rows, n_exp):
    N, D = h2.shape
    C = D // 2
    grid_spec = pltpu.PrefetchScalarGridSpec(
        num_scalar_prefetch=7,
        grid=(N // TOK_CHUNK,),
        in_specs=[pl.BlockSpec((TOK_CHUNK, LANES), lambda i, *_: (i, 0)),
                  pl.BlockSpec((TOK_CHUNK, D), lambda i, *_: (i, 0))],
        out_specs=pl.BlockSpec(memory_space=pl.ANY),
        scratch_shapes=[pltpu.VMEM((2, _sorted_rows(n_exp), C), jnp.uint32),
                        pltpu.VMEM((max(TOK_CHUNK, MOE_ROWS), C), jnp.uint32),
                        pltpu.SemaphoreType.DMA((2,))],
    )
    return pl.pallas_call(
        functools.partial(_dispatch_kernel, n_exp=n_exp),
        grid_spec=grid_spec,
        out_shape=jax.ShapeDtypeStruct((n_rows, C), jnp.uint32),
        compiler_params=_params(("arbitrary",)),
        name="moe_dispatch",
    )(plan["run"], plan["src"], plan["dst"], plan["large"], plan["tail"], plan["tail_dst"], n_used, route, h2)


def _expert_kernel(be_ref, nu_ref, rows_ref, xs_ref, w1_ref, b1_ref, w2_ref, b2_ref, o_ref, w1b, w2b, *, F):
    del nu_ref
    i = pl.program_id(0)
    e = be_ref[i]
    prev = be_ref[jnp.maximum(i - 1, 0)]
    rows = rows_ref[i]
    half = MOE_ROWS // 2

    first = (i == 0) | (e != prev)

    def ffn(xw, fresh):
        if fresh:
            w1 = w1_ref[...].astype(BF16)
            w2 = w2_ref[...].astype(BF16)
            w1b[...] = w1
            w2b[...] = w2
        else:
            w1 = w1b[...]
            w2 = w2b[...]
        hh = _dot(_unpack_pairs(xw), w1) + b1_ref[...]
        g = jnp.minimum(hh[:, 0:F], SWIGLU_LIMIT)
        u = jnp.clip(hh[:, F:2 * F], -SWIGLU_LIMIT, SWIGLU_LIMIT)
        a = (g * (1.0 / (1.0 + jnp.exp(-SWIGLU_ALPHA * g)))) * (u + 1.0)
        return _pack_pairs(_dot(a.astype(BF16), w2) + b2_ref[...])

    for fresh in (True, False):
        new_expert = first if fresh else jnp.logical_not(first)

        @pl.when(new_expert & (rows > half))
        def _(fresh=fresh):
            o_ref[...] = ffn(xs_ref[...], fresh)

        @pl.when(new_expert & (rows > 0) & (rows <= half))
        def _(fresh=fresh):
            o_ref[0:half, :] = ffn(xs_ref[0:half, :], fresh)
            o_ref[half:MOE_ROWS, :] = jnp.zeros((MOE_ROWS - half, o_ref.shape[1]), o_ref.dtype)

    @pl.when(rows == 0)
    def _():
        o_ref[...] = jnp.zeros(o_ref.shape, o_ref.dtype)


def _experts(block_e, n_used, block_rows, xs, w1, b1, w2, b2, *, l):
    R, C = xs.shape
    L, E, D, F2 = w1.shape
    F = F2 // 2
    nblk = R // MOE_ROWS
    grid_spec = pltpu.PrefetchScalarGridSpec(
        num_scalar_prefetch=3,
        grid=(nblk,),
        in_specs=[pl.BlockSpec((MOE_ROWS, C), lambda i, be, nu, br: (jnp.minimum(i, nu[0] - 1), 0)),
                  pl.BlockSpec((None, None, D, F2), lambda i, be, nu, br: (l, be[i], 0, 0)),
                  pl.BlockSpec((None, None, 1, F2), lambda i, be, nu, br: (l, be[i], 0, 0)),
                  pl.BlockSpec((None, None, F, D), lambda i, be, nu, br: (l, be[i], 0, 0)),
                  pl.BlockSpec((None, None, 1, D), lambda i, be, nu, br: (l, be[i], 0, 0))],
        out_specs=pl.BlockSpec((MOE_ROWS, C), lambda i, be, nu, br: (i, 0)),
        scratch_shapes=[pltpu.VMEM((D, F2), BF16), pltpu.VMEM((F, D), BF16)],
    )
    return pl.pallas_call(
        functools.partial(_expert_kernel, F=F),
        grid_spec=grid_spec,
        out_shape=jax.ShapeDtypeStruct((R, C), jnp.uint32),
        compiler_params=_params(("arbitrary",)),
        name="moe_experts",
    )(block_e, n_used, block_rows, xs, w1, b1.reshape(L, E, 1, F2), w2, b2.reshape(L, E, 1, D))


def _combine_kernel(run_ref, src_ref, dst_ref, large_ref, yb_ref, route_ref, x1_ref, mod_ref, g_ref, *rest, D,
                    n_exp, ctx_blocks, split):
    if split:
        ctx_ref, lat_ref, sorted_buf, sems = rest
    else:
        o_ref, sorted_buf, sems = rest
    b = pl.program_id(0)
    nb = pl.num_programs(0)
    slot = b % 2
    runs = functools.partial(_block_runs, n_exp=n_exp, run_ref=run_ref, src_ref=src_ref, dst_ref=dst_ref,
                             large_ref=large_ref, hbm_ref=yb_ref, to_hbm=False)

    @pl.when(b == 0)
    def _():
        sorted_buf[...] = jnp.zeros(sorted_buf.shape, sorted_buf.dtype)
        runs(b, buf=sorted_buf.at[slot], sem=sems.at[slot], wait=False)

    @pl.when(b + 1 < nb)
    def _():
        runs(b + 1, buf=sorted_buf.at[1 - slot], sem=sems.at[1 - slot], wait=False)

    runs(b, buf=sorted_buf.at[slot], sem=sems.at[slot], wait=True)
    route = route_ref[...]
    col = lax.broadcasted_iota(I32, (TOK_CHUNK, sorted_buf.shape[1]), 1).astype(F32)
    gate = jnp.zeros(col.shape, F32)
    for k in range(TOP_K):
        gate = jnp.where(col == route[:, TOP_K + k:TOP_K + k + 1], route[:, 2 * TOP_K + k:2 * TOP_K + k + 1], gate)
    g_hi, g_lo = _split(gate)
    yb = _unpack_pairs(sorted_buf[slot])
    y = _dot(g_hi, yb) + _dot(g_lo, yb)
    x2 = x1_ref[...] + mod_ref[:, 5 * D:6 * D] * (_rms(y) * g_ref[...])
    if split:
        @pl.when(b < ctx_blocks)
        def _():
            ctx_ref[...] = x2

        @pl.when(b >= ctx_blocks)
        def _():
            lat_ref[...] = x2
    else:
        o_ref[...] = x2


def _combine(plan, yb, route, x1, mod4, g3, *, n_exp, ctx_blocks, blocks_per_seq, l, split):
    N, D = x1.shape
    row = lambda f: pl.BlockSpec((TOK_CHUNK, D), lambda i, *_: (f(i), 0))
    if split:
        out_specs = [row(lambda i: jnp.minimum(i, ctx_blocks - 1)), row(lambda i: jnp.maximum(i - ctx_blocks, 0))]
        out_shape = [jax.ShapeDtypeStruct((ctx_blocks * TOK_CHUNK, D), F32),
                     jax.ShapeDtypeStruct((N - ctx_blocks * TOK_CHUNK, D), F32)]
    else:
        out_specs = row(lambda i: i)
        out_shape = jax.ShapeDtypeStruct((N, D), F32)

    def grp(i):
        return jnp.where(i < ctx_blocks, 0, 1 + (i - ctx_blocks) // blocks_per_seq)

    grid_spec = pltpu.PrefetchScalarGridSpec(
        num_scalar_prefetch=4,
        grid=(N // TOK_CHUNK,),
        in_specs=[pl.BlockSpec(memory_space=pl.ANY),
                  pl.BlockSpec((TOK_CHUNK, LANES), lambda i, *_: (i, 0)),
                  pl.BlockSpec((TOK_CHUNK, D), lambda i, *_: (i, 0)),
                  pl.BlockSpec((None, None, 1, N_MOD * D), lambda i, *_: (l, grp(i), 0, 0)),
                  pl.BlockSpec((1, D), lambda i, *_: (0, 0))],
        out_specs=out_specs,
        scratch_shapes=[pltpu.VMEM((2, _sorted_rows(n_exp), yb.shape[1]), jnp.uint32),
                        pltpu.SemaphoreType.DMA((2,))],
    )
    return pl.pallas_call(
        functools.partial(_combine_kernel, D=D, n_exp=n_exp, ctx_blocks=ctx_blocks, split=split),
        grid_spec=grid_spec,
        out_shape=out_shape,
        compiler_params=_params(("arbitrary",)),
        name="moe_combine",
    )(plan["run"], plan["src"], plan["dst"], plan["large"], yb, route, x1, mod4, g3)


def _rope_tables(Ts, tm):
    nf = DIFF_QK_DIM // 4
    inv = ROPE_BASE ** (-jnp.arange(nf, dtype=F32) / nf)
    t = jnp.arange(Ts)
    pos = jnp.stack([(t // GRID_W).astype(F32), (t % GRID_W).astype(F32)], axis=1)
    ang = pos[:, :, None] * inv[None, None, :]
    cos = jnp.repeat(jnp.cos(ang)[:, :, None, :], 2, axis=2).reshape(Ts, DIFF_QK_DIM)
    sin = jnp.sin(ang)
    sin = jnp.stack([-sin, sin], axis=2).reshape(Ts, DIFF_QK_DIM)
    reps = LANES // DIFF_QK_DIM
    cos = jnp.concatenate([jnp.ones((tm, LANES), F32), jnp.tile(cos, (1, reps))], axis=0)
    sin = jnp.concatenate([jnp.zeros((tm, LANES), F32), jnp.tile(sin, (1, reps))], axis=0)
    return cos, sin


def _block_diag(w):
    G, a, b = w.shape
    out = jnp.zeros((G * a, G * b), w.dtype)
    for g in range(G):
        out = out.at[g * a:(g + 1) * a, g * b:(g + 1) * b].set(w[g])
    return out


def _route_plan(counts, n_exp, nblk):
    cnt = counts[:, 0, :n_exp].astype(I32)
    run = (cnt + RUN_ALIGN - 1) // RUN_ALIGN * RUN_ALIGN
    src = jnp.cumsum(run, axis=1) - run
    tot = jnp.sum(run, axis=0)
    region = (tot + MOE_ROWS - 1) // MOE_ROWS * MOE_ROWS
    region_end = jnp.cumsum(region)
    region_start = region_end - region
    dst = region_start[None, :] + jnp.cumsum(run, axis=0) - run
    n_used = (region_end[-1] // MOE_ROWS).astype(I32)
    blk = jnp.arange(nblk, dtype=I32) * MOE_ROWS
    block_e = jnp.minimum(jnp.sum((blk[:, None] >= region_end[None, :]).astype(I32), axis=1), n_exp - 1)
    last = jnp.sum(jnp.where(jnp.arange(nblk) == n_used - 1, block_e, 0))
    used = jnp.arange(nblk) < n_used
    block_rows = jnp.where(used, jnp.clip((region_start + tot)[block_e] - blk, 0, MOE_ROWS), 0).astype(I32)
    block_e = jnp.where(used, block_e, last).astype(I32)
    plan = dict(run=run.reshape(-1).astype(I32), src=src.reshape(-1).astype(I32), dst=dst.reshape(-1).astype(I32),
                large=jnp.any(run >= RUN_LARGE, axis=1).astype(I32),
                tail=(region - tot).astype(I32), tail_dst=(region_start + tot).astype(I32))
    return plan, block_e, n_used.reshape(1), block_rows


def kernel(x_prompt, x_sample, cache_diff_k, cache_diff_v, cache_na_k, cache_na_v, c, c_ctx, w_ada, b_ada,
           norm_gain, w_in, w_out, pool_w, pool_scale, diff_lambda, diff_subln, na_rpb, router_w, router_b,
           moe_w1, moe_b1, moe_w2, moe_b2):
    Bp, Tp, D = x_prompt.shape
    Bs, Ts, _ = x_sample.shape
    L = w_ada.shape[0]
    E = router_w.shape[-1]
    Np, Ns = Bp * Tp, Bs * Ts
    N = Np + Ns
    tm = TOK_CHUNK
    assert Np % Ts == 0 or Bs == 0, "context rows must be a whole number of latent-sequence blocks"
    assert Np % tm == 0 and Ts % tm == 0 and Ts % (NA_Q_ROWS * GRID_W) == 0
    assert Ts // GRID_W >= NA_BAND + NA_Q_ROWS
    ctx_blocks, blocks_per_seq = Np // tm, Ts // tm

    G = BF16_ROWS
    assert 1 + Bs <= G
    cvec = jnp.zeros((G, D), F32).at[0].set(c_ctx).at[1:1 + Bs].set(c)
    mod4 = _modulation(cvec, w_ada, b_ada).reshape(L, G, 1, N_MOD * D)
    tp = PROJ_ROWS
    assert Np % tp == 0 and Ts % tp == 0 and Np % MERGE_ROWS == 0 and Ts % MERGE_ROWS == 0
    cos_t, sin_t = _rope_tables(Ts, tp)
    w_in_bf = w_in.astype(BF16)
    w_out_bf = w_out.astype(BF16)
    rw_pad = jnp.zeros((L, D, LANES), F32).at[:, :, :E].set(router_w)
    rb_pad = jnp.full((L, 1, LANES), MASKED, F32).at[:, 0, :E].set(router_b)
    ck = cache_diff_k.reshape(Bs, L, -1, DIFF_HEADS * 2 * DIFF_QK_DIM)
    cv = cache_diff_v.reshape(Bs, L, -1, DIFF_HEADS * DIFF_V_DIM)
    nk = cache_na_k.reshape(Bs, L, -1, NA_HEADS * NA_HEAD_DIM)
    nv = cache_na_v.reshape(Bs, L, -1, NA_HEADS * NA_HEAD_DIM)
    nb = N // TOK_CHUNK
    nblk = -(-(N * TOP_K + nb * E * (RUN_ALIGN - 1) + E * (MOE_ROWS - 1)) // MOE_ROWS)
    blocks = dict(ctx_blocks=ctx_blocks, blocks_per_seq=blocks_per_seq)

    x = (x_prompt.reshape(Np, D), x_sample.reshape(Ns, D))
    new_dk, new_dv, new_nk, new_nv = [], [], [], []
    for l in range(L):
        lam_init = 0.8 - 0.6 * math.exp(-0.3 * l)
        g = norm_gain[l]
        p, dk_l, dv_l, nk_l, nv_l = _in_projection(x, mod4, g[0:1], w_in_bf[l], cos_t, sin_t, tm=tp, l=l,
                                                   ctx_blocks=Np // tp, blocks_per_seq=Ts // tp)

        pw = _block_diag(pool_w[l]).astype(BF16)
        ps = pool_scale[l].reshape(1, -1)
        pool_o = jnp.zeros((N, pw.shape[0]), BF16)
        pool_o = _pool(p, pw, ps, row0=0, n_seq=Bp, T=Tp, out=pool_o)
        pool_o = _pool(p, pw, ps, row0=Np, n_seq=Bs, T=Ts, out=pool_o)

        sub = diff_subln[l].reshape(1, -1)
        dn = jnp.zeros((N, DIFF_HEADS * DIFF_V_DIM), BF16)
        dn = _diff_attention_ctx(p, diff_lambda[l], sub, dn, n_seq=Bp, T=Tp, lam_init=lam_init)
        dn = _diff_attention(p, diff_lambda[l], sub, dn, row0=Np, n_seq=Bs, T=Ts, tq=DIFF_Q_ROWS, lam_init=lam_init,
                             cache_k=ck, cache_v=cv, l=l)

        na_o = jnp.zeros((N, NA_HEADS * NA_HEAD_DIM), BF16)
        na_o = _dense_attention(p, na_o, n_seq=Bp, T=Tp)
        bias = _na_bias_table(na_rpb[l], Ts // GRID_W)
        na_o = _neighbourhood_attention(p, nk, nv, bias, na_o, row0=Np, n_seq=Bs, T=Ts, l=l)

        x1, h2, route, counts = _merge_route(pool_o, dn, na_o, x, mod4, g[1:2], g[2:3], w_out_bf[l],
                                             rw_pad[l], rb_pad[l], tm=MERGE_ROWS, l=l,
                                             ctx_blocks=Np // MERGE_ROWS, blocks_per_seq=Ts // MERGE_ROWS)
        plan, block_e, n_used, block_rows = _route_plan(counts, E, nblk)
        xs = _dispatch(plan, n_used, route, h2, nblk * MOE_ROWS, E)
        yb = _experts(block_e, n_used, block_rows, xs, moe_w1, moe_b1, moe_w2, moe_b2, l=l)
        x = _combine(plan, yb, route, x1, mod4, g[3:4], n_exp=E, l=l, split=l == L - 1, **blocks)

        new_dk.append(dk_l.reshape(Bp, Tp, DIFF_HEADS, 2 * DIFF_QK_DIM))
        new_dv.append(dv_l.reshape(Bp, Tp, DIFF_HEADS, DIFF_V_DIM))
        new_nk.append(nk_l.reshape(Bp, Tp, NA_HEADS, NA_HEAD_DIM))
        new_nv.append(nv_l.reshape(Bp, Tp, NA_HEADS, NA_HEAD_DIM))

    return (x[0].reshape(Bp, Tp, D), x[1].reshape(Bs, Ts, D),
            jnp.stack(new_dk, axis=1), jnp.stack(new_dv, axis=1),
            jnp.stack(new_nk, axis=1), jnp.stack(new_nv, axis=1))
```
